```python
import jax, jax.numpy as jnp
from jax import lax
import numpy as np

D_MODEL = 1024
BATCH = 4
SEQ = 4096
DEPTH = 4
DEC_BATCH = 2
DEC_SEQ = 8192
PAST_LEN = 128

HEAD_DIM = 64
GRID_W = 64
Q_BLOCK = 128
ROT_DIM = HEAD_DIM // 4
ROPE_THETA = 500000.0
AXIAL_THETA = 10000.0
A_PATTERNS = ((128, 1), (512, 4), (2048, 16))
A_HEADS_PER_GROUP = 4
A_GROUPS = len(A_PATTERNS)
A_WIDTH = A_GROUPS * A_HEADS_PER_GROUP * HEAD_DIM
A_OUT = A_HEADS_PER_GROUP * HEAD_DIM
B_Q_HEADS = 12
B_KV_HEADS = 4
B_Q_WIDTH = B_Q_HEADS * HEAD_DIM
B_KV_WIDTH = B_KV_HEADS * HEAD_DIM
AB_IN = 3 * A_WIDTH + B_Q_WIDTH + 2 * B_KV_WIDTH
AB_OUT = A_OUT + B_Q_WIDTH
C_Q_HEADS = 16
C_KV_HEADS = 4
C_RADIUS = 128
C_Q_WIDTH = C_Q_HEADS * HEAD_DIM
C_KV_WIDTH = C_KV_HEADS * HEAD_DIM
C_IN = C_Q_WIDTH + 2 * C_KV_WIDTH
N_EXPERTS = 16
EXPERT_FF = 1024
CAPACITY_FACTOR = 2
N_EVEN = (DEPTH + 1) // 2
N_ODD = DEPTH // 2
NEG_INF = -1e30
EPS = 1e-6

kernel_name = "hybrid_dilated_axial_sink_ec_encoder"


def rms_norm(x, g):
    xf = x.astype(jnp.float32)
    y = xf * lax.rsqrt(jnp.mean(xf * xf, axis=-1, keepdims=True) + EPS)
    return (y * g.astype(jnp.float32)).astype(x.dtype)


def rope_angles(pos, dim, theta):
    exps = jnp.arange(0, dim, 2, dtype=jnp.float32) / dim
    inv = jnp.power(jnp.float32(theta), -exps)
    ang = pos.astype(jnp.float32)[:, None] * inv[None, :]
    return jnp.cos(ang), jnp.sin(ang)


def apply_rope(x, cos, sin):
    shape = (cos.shape[0],) + (1,) * (x.ndim - 3) + (cos.shape[-1],)
    c = cos.reshape(shape).astype(x.dtype)
    s = sin.reshape(shape).astype(x.dtype)
    half = x.shape[-1] // 2
    x1, x2 = x[..., :half], x[..., half:]
    return jnp.concatenate([x1 * c - x2 * s, x2 * c + x1 * s], axis=-1)


def partial_rope(x, cos, sin):
    return jnp.concatenate([apply_rope(x[..., :ROT_DIM], cos, sin), x[..., ROT_DIM:]], axis=-1)


def banded_attention(q, k, v, radius, sink=None):
    b, n, hkv, g, dh = q.shape
    r = radius
    nb = -(-n // r)
    npad = nb * r
    qp = jnp.pad(q, ((0, 0), (0, npad - n), (0, 0), (0, 0), (0, 0))).reshape(b, nb, r, hkv, g, dh)
    pad_kv = ((0, 0), (r, npad - n + r), (0, 0), (0, 0))
    kp = jnp.pad(k, pad_kv).reshape(b, nb + 2, r, hkv, dh)
    vp = jnp.pad(v, pad_kv).reshape(b, nb + 2, r, hkv, dh)
    kw = jnp.concatenate([kp[:, :-2], kp[:, 1:-1], kp[:, 2:]], axis=2)
    vw = jnp.concatenate([vp[:, :-2], vp[:, 1:-1], vp[:, 2:]], axis=2)
    s = jnp.einsum('bnqhgd,bnkhd->bnhgqk', qp, kw,
                   preferred_element_type=jnp.float32) * (dh ** -0.5)
    blk = jnp.arange(nb)[:, None]
    qpos = blk * r + jnp.arange(r)[None, :]
    kpos = (blk - 1) * r + jnp.arange(3 * r)[None, :]
    kq = kpos[:, None, :]
    valid = (jnp.abs(kq - qpos[:, :, None]) <= r) & (kq >= 0) & (kq < n)
    s = jnp.where(valid[None, :, None, None], s, NEG_INF)
    m = jnp.max(s, axis=-1, keepdims=True)
    if sink is not None:
        sk = sink.astype(jnp.float32)[None, None, :, :, None, None]
        m = jnp.maximum(m, sk)
    e = jnp.exp(s - m)
    den = jnp.sum(e, axis=-1, keepdims=True)
    if sink is not None:
        den = den + jnp.exp(sk - m)
    o = jnp.einsum('bnhgqk,bnkhd->bnqhgd', (e / den).astype(v.dtype), vw)
    lse = (m + jnp.log(den))[..., 0]
    o = o.reshape(b, npad, hkv, g, dh)[:, :n]
    lse = lse.transpose(0, 1, 4, 2, 3).reshape(b, npad, hkv, g)[:, :n]
    return o, lse


def dilated_window_attn(q, k, v, window, dilation):
    b, l, h, dh = q.shape
    n = l // dilation

    def to_sub(t):
        return t.reshape(b, n, dilation, h, dh).transpose(0, 2, 1, 3, 4).reshape(b * dilation, n, h, dh)

    radius = window // (2 * dilation)
    o, lse = banded_attention(to_sub(q)[:, :, :, None, :], to_sub(k), to_sub(v), radius)
    o = o[:, :, :, 0].reshape(b, dilation, n, h, dh).transpose(0, 2, 1, 3, 4).reshape(b, l, h, dh)
    lse = lse[..., 0].reshape(b, dilation, n, h).transpose(0, 2, 1, 3).reshape(b, l, h)
    return o, lse


def dense_block_attention(q, k, v):
    b, l, hkv, g, dh = q.shape
    qb = q.reshape(b, l // Q_BLOCK, Q_BLOCK, hkv, g, dh).transpose(1, 0, 2, 3, 4, 5)
    scale = dh ** -0.5

    def one_block(qi):
        s = jnp.einsum('bqhgd,bkhd->bhgqk', qi, k, preferred_element_type=jnp.float32) * scale
        p = jax.nn.softmax(s, axis=-1)
        return jnp.einsum('bhgqk,bkhd->bqhgd', p.astype(v.dtype), v)

    o = lax.map(one_block, qb)
    return o.transpose(1, 0, 2, 3, 4, 5).reshape(b, l, hkv * g * dh)


def mixer_ab(h, w_in, qn_a, kn_a, qn_b, kn_b, w_out):
    b, l, _ = h.shape
    proj = h @ w_in
    splits = [A_WIDTH, 2 * A_WIDTH, 3 * A_WIDTH, 3 * A_WIDTH + B_Q_WIDTH,
              3 * A_WIDTH + B_Q_WIDTH + B_KV_WIDTH]
    qa, ka, va, qb, kb, vb = jnp.split(proj, splits, axis=-1)
    a_shape = (b, l, A_GROUPS, A_HEADS_PER_GROUP, HEAD_DIM)
    qa = rms_norm(qa.reshape(a_shape), qn_a[:, None, :])
    ka = rms_norm(ka.reshape(a_shape), kn_a[:, None, :])
    va = va.reshape(a_shape)
    cos, sin = rope_angles(jnp.arange(l), ROT_DIM, ROPE_THETA)
    qa = partial_rope(qa, cos, sin)
    ka = partial_rope(ka, cos, sin)
    outs, lses = [], []
    for gi, (window, dil) in enumerate(A_PATTERNS):
        o, lse = dilated_window_attn(qa[:, :, gi], ka[:, :, gi], va[:, :, gi], window, dil)
        outs.append(o)
        lses.append(lse)
    wgt = jax.nn.softmax(jnp.stack(lses, axis=0), axis=0)
    oa = jnp.einsum('gblh,gblhd->blhd', wgt.astype(h.dtype), jnp.stack(outs, axis=0)).reshape(b, l, A_OUT)
    g_b = B_Q_HEADS // B_KV_HEADS
    qb = rms_norm(qb.reshape(b, l, B_KV_HEADS, g_b, HEAD_DIM), qn_b)
    kb = rms_norm(kb.reshape(b, l, B_KV_HEADS, HEAD_DIM), kn_b)
    vb = vb.reshape(b, l, B_KV_HEADS, HEAD_DIM)
    rows = l // GRID_W
    t = jnp.arange(rows * GRID_W)
    half = HEAD_DIM // 2
    cr, sr = rope_angles(t // GRID_W, half, AXIAL_THETA)
    cc, sc = rope_angles(t % GRID_W, half, AXIAL_THETA)

    def axial(x):
        return jnp.concatenate([apply_rope(x[..., :half], cr, sr),
                                apply_rope(x[..., half:], cc, sc)], axis=-1)

    ob = dense_block_attention(axial(qb), axial(kb), vb)
    return jnp.concatenate([oa, ob], axis=-1) @ w_out


def mixer_c(h, w_in, qn, kn, sink, w_out):
    b, l, _ = h.shape
    proj = h @ w_in
    q, k, v = jnp.split(proj, [C_Q_WIDTH, C_Q_WIDTH + C_KV_WIDTH], axis=-1)
    g_c = C_Q_HEADS // C_KV_HEADS
    q = rms_norm(q.reshape(b, l, C_KV_HEADS, g_c, HEAD_DIM), qn)
    k = rms_norm(k.reshape(b, l, C_KV_HEADS, HEAD_DIM), kn)
    v = v.reshape(b, l, C_KV_HEADS, HEAD_DIM)
    cos, sin = rope_angles(jnp.arange(l), ROT_DIM, ROPE_THETA)
    q = partial_rope(q, cos, sin)
    k = partial_rope(k, cos, sin)
    o, _ = banded_attention(q, k, v, C_RADIUS, sink.reshape(C_KV_HEADS, g_c))
    return o.reshape(b, l, C_Q_WIDTH) @ w_out


def expert_choice_moe(h, w_router, w_gate, w_up, w_down):
    b, l, d = h.shape
    t = b * l
    xt = h.reshape(t, d)
    aff = jax.nn.softmax(jnp.matmul(xt, w_router, preferred_element_type=jnp.float32), axis=-1)
    cap = CAPACITY_FACTOR * t // N_EXPERTS
    gates, idx = lax.top_k(aff.T, cap)
    xe = xt[idx]
    hid = jax.nn.silu(jnp.einsum('ecd,edf->ecf', xe, w_gate)) * jnp.einsum('ecd,edf->ecf', xe, w_up)
    ye = jnp.einsum('ecf,efd->ecd', hid, w_down)
    contrib = (gates[..., None] * ye.astype(jnp.float32)).astype(h.dtype).reshape(-1, d)
    y = jnp.zeros_like(xt).at[idx.reshape(-1)].add(contrib)
    return y.reshape(b, l, d)


def encode(x, norm_mix, norm_ffn, w_in_ab, qn_a, kn_a, qn_b, kn_b, w_out_ab,
           w_in_c, qn_c, kn_c, sink_c, w_out_c, w_router, w_gate, w_up, w_down):
    for layer in range(DEPTH):
        h = rms_norm(x, norm_mix[layer])
        j = layer // 2
        if layer % 2 == 0:
            x = x + mixer_ab(h, w_in_ab[j], qn_a[j], kn_a[j], qn_b[j], kn_b[j], w_out_ab[j])
        else:
            x = x + mixer_c(h, w_in_c[j], qn_c[j], kn_c[j], sink_c[j], w_out_c[j])
        h = rms_norm(x, norm_ffn[layer])
        x = x + expert_choice_moe(h, w_router[layer], w_gate[layer], w_up[layer], w_down[layer])
    return x


def setup_inputs(seed: int = 0) -> dict:
    key = jax.random.key(seed)
    ks = jax.random.split(key, 20)
    f32 = jnp.float32

    def nrm(k, shape, scale):
        return jax.random.normal(k, shape, f32) * scale

    def gain(k, shape):
        return 1.0 + 0.05 * jax.random.normal(k, shape, f32)

    return {
        "x_prompt": nrm(ks[0], (BATCH, SEQ, D_MODEL), 1.0),
        "x_sample": nrm(ks[1], (DEC_BATCH, DEC_SEQ, D_MODEL), 1.0),
        "norm_mix": gain(ks[2], (DEPTH, D_MODEL)),
        "norm_ffn": gain(ks[3], (DEPTH, D_MODEL)),
        "w_in_ab": nrm(ks[4], (N_EVEN, D_MODEL, AB_IN), D_MODEL ** -0.5),
        "qn_a": gain(ks[5], (N_EVEN, A_GROUPS, HEAD_DIM)),
        "kn_a": gain(ks[6], (N_EVEN, A_GROUPS, HEAD_DIM)),
        "qn_b": gain(ks[7], (N_EVEN, HEAD_DIM)),
        "kn_b": gain(ks[8], (N_EVEN, HEAD_DIM)),
        "w_out_ab": nrm(ks[9], (N_EVEN, AB_OUT, D_MODEL), AB_OUT ** -0.5),
        "w_in_c": nrm(ks[10], (N_ODD, D_MODEL, C_IN), D_MODEL ** -0.5),
        "qn_c": gain(ks[11], (N_ODD, HEAD_DIM)),
        "kn_c": gain(ks[12], (N_ODD, HEAD_DIM)),
        "sink_c": nrm(ks[13], (N_ODD, C_Q_HEADS), 0.5),
        "w_out_c": nrm(ks[14], (N_ODD, C_Q_WIDTH, D_MODEL), C_Q_WIDTH ** -0.5),
        "w_router": nrm(ks[15], (DEPTH, D_MODEL, N_EXPERTS), D_MODEL ** -0.5),
        "w_gate": nrm(ks[16], (DEPTH, N_EXPERTS, D_MODEL, EXPERT_FF), D_MODEL ** -0.5),
        "w_up": nrm(ks[17], (DEPTH, N_EXPERTS, D_MODEL, EXPERT_FF), D_MODEL ** -0.5),
        "w_down": nrm(ks[18], (DEPTH, N_EXPERTS, EXPERT_FF, D_MODEL), EXPERT_FF ** -0.5),
    }


def reference(x_prompt, x_sample, norm_mix, norm_ffn, w_in_ab, qn_a, kn_a, qn_b, kn_b, w_out_ab,
              w_in_c, qn_c, kn_c, sink_c, w_out_c, w_router, w_gate, w_up, w_down):
    y_prompt = encode(x_prompt, norm_mix, norm_ffn, w_in_ab, qn_a, kn_a, qn_b, kn_b, w_out_ab,
                      w_in_c, qn_c, kn_c, sink_c, w_out_c, w_router, w_gate, w_up, w_down)
    y_sample = encode(x_sample, norm_mix, norm_ffn, w_in_ab, qn_a, kn_a, qn_b, kn_b, w_out_ab,
                      w_in_c, qn_c, kn_c, sink_c, w_out_c, w_router, w_gate, w_up, w_down)
    return (y_prompt, y_sample)
```

```python
import functools
import math

import jax
import jax.numpy as jnp
import numpy as np
from jax import lax
from jax.experimental import pallas as pl
from jax.experimental.pallas import tpu as pltpu

F32 = jnp.float32
BF16 = jnp.bfloat16
I32 = jnp.int32
U32 = jnp.uint32

D_MODEL = 1024
HEAD_DIM = 64
LANES = 128
MXU_COLS = 256
GRID_W = 64
ROT_DIM = HEAD_DIM // 4
ROPE_THETA = 500000.0
AXIAL_THETA = 10000.0
A_PATTERNS = ((128, 1), (512, 4), (2048, 16))
A_HEADS = 4
A_GROUPS = len(A_PATTERNS)
A_WIDTH = A_GROUPS * A_HEADS * HEAD_DIM
B_Q_HEADS = 12
B_KV_HEADS = 4
C_Q_HEADS = 16
C_KV_HEADS = 4
C_RADIUS = 128
N_EXPERTS = 16
EXPERT_FF = 1024
CAPACITY_FACTOR = 2
NEG_INF = -1e30
EPS = 1e-6

VMEM_LIMIT = 56 * 1024 * 1024

PROJ_TM = 512
ATTN_TQ = 128
DENSE_TK = 512
MOE_TB = 256
MOE_W = 64
FFN_TM = 256


def _cparams(sem):
    return pltpu.CompilerParams(dimension_semantics=sem, vmem_limit_bytes=VMEM_LIMIT)


def _head_block_diag():
    idx = np.arange(MXU_COLS) // HEAD_DIM
    return jnp.asarray((idx[:, None] == idx[None, :]).astype(np.float32), dtype=BF16)


def _rope_tables(max_len):
    pos = jnp.arange(max_len, dtype=F32)
    j = np.arange(LANES) % HEAD_DIM

    def angles(p, dim, theta):
        exps = jnp.arange(0, dim, 2, dtype=F32) / dim
        inv = jnp.power(jnp.float32(theta), -exps)
        return p[:, None] * inv[None, :]

    half = ROT_DIM // 2
    ang = angles(pos, ROT_DIM, ROPE_THETA)
    cos, sin = jnp.cos(ang), jnp.sin(ang)
    fa = np.where(j < half, j, np.where(j < ROT_DIM, j - half, 0))
    cos_l, sin_l = cos[:, fa], sin[:, fa]
    lo = jnp.asarray(j < half)[None, :]
    hi = jnp.asarray((j >= half) & (j < ROT_DIM))[None, :]
    tab_a = jnp.stack([jnp.where(lo | hi, cos_l, 1.0),
                       jnp.where(lo, -sin_l, 0.0),
                       jnp.where(hi, sin_l, 0.0)])
    hb = HEAD_DIM // 2
    qb = hb // 2
    t = jnp.arange(max_len)
    ang_r = angles((t // GRID_W).astype(F32), hb, AXIAL_THETA)
    ang_c = angles((t % GRID_W).astype(F32), hb, AXIAL_THETA)
    fb = j % qb
    is_col = jnp.asarray(j >= hb)[None, :]
    ang_l = jnp.where(is_col, ang_c[:, fb], ang_r[:, fb])
    cos_b, sin_b = jnp.cos(ang_l), jnp.sin(ang_l)
    first = jnp.asarray((j % hb) < qb)[None, :]
    tab_b = jnp.stack([cos_b, jnp.where(first, -sin_b, 0.0), jnp.where(first, 0.0, sin_b)])
    return tab_a.astype(F32), tab_b.astype(F32)


def _pair_layout_perm(n_kv, group):
    cols = []
    for p in range(n_kv // 2):
        for g in range(group):
            for par in range(2):
                h = (2 * p + par) * group + g
                cols.extend(range(h * HEAD_DIM, (h + 1) * HEAD_DIM))
    return np.asarray(cols, dtype=np.int32)


def _proj_kernel(x_ref, g_ref, w_ref, s_ref, gain_ref, *rest, plan, n_tabs):
    tab_refs = rest[:n_tabs]
    out_refs = rest[n_tabs:]
    x = x_ref[...]
    ms = jnp.mean(x * x, axis=1, keepdims=True)
    xn = ((x * lax.rsqrt(ms + EPS)) * g_ref[...]).astype(BF16)
    tm = x.shape[0]
    ones = jnp.ones((tm, LANES), BF16)
    for c, (kind, out_i, out_col, tab_i, shift) in enumerate(plan):
        acc = jnp.dot(xn, w_ref[:, c * MXU_COLS:(c + 1) * MXU_COLS], preferred_element_type=F32)
        o_ref = out_refs[out_i]
        if kind == "v":
            for p in range(MXU_COLS // LANES):
                base = out_col + 2 * p * LANES
                o_ref[:, base:base + LANES] = acc[:, p * LANES:(p + 1) * LANES].astype(BF16)
                o_ref[:, base + LANES:base + 2 * LANES] = ones
            continue
        ss = jnp.dot((acc * acc).astype(BF16), s_ref[...], preferred_element_type=F32)
        y = (acc * lax.rsqrt(ss * (1.0 / HEAD_DIM) + EPS)) * gain_ref[:, c * MXU_COLS:(c + 1) * MXU_COLS]
        tab = tab_refs[tab_i]
        t0 = jnp.concatenate([tab[0], tab[0]], axis=1)
        t1 = jnp.concatenate([tab[1], tab[1]], axis=1)
        t2 = jnp.concatenate([tab[2], tab[2]], axis=1)
        y = y * t0 + pltpu.roll(y, MXU_COLS - shift, 1) * t1 + pltpu.roll(y, shift, 1) * t2
        o_ref[:, out_col:out_col + MXU_COLS] = y.astype(BF16)


def _projection(x, norm_g, w, gains, tabs, plan, out_widths, seq_len):
    t_tokens = x.shape[0]
    tm = PROJ_TM
    n_in = w.shape[1]
    blocks_per_seq = seq_len // tm
    in_specs = [
        pl.BlockSpec((tm, D_MODEL), lambda i: (i, 0)),
        pl.BlockSpec((1, D_MODEL), lambda i: (0, 0)),
        pl.BlockSpec((D_MODEL, n_in), lambda i: (0, 0)),
        pl.BlockSpec((MXU_COLS, MXU_COLS), lambda i: (0, 0)),
        pl.BlockSpec((1, n_in), lambda i: (0, 0)),
    ] + [pl.BlockSpec((3, tm, LANES), lambda i: (0, i % blocks_per_seq, 0)) for _ in tabs]
    out_specs = [pl.BlockSpec((tm, wd), lambda i: (i, 0)) for wd in out_widths]
    out_shape = [jax.ShapeDtypeStruct((t_tokens, wd), BF16) for wd in out_widths]
    return pl.pallas_call(
        functools.partial(_proj_kernel, plan=tuple(plan), n_tabs=len(tabs)),
        grid=(t_tokens // tm,),
        in_specs=in_specs, out_specs=out_specs, out_shape=out_shape,
        compiler_params=_cparams(("parallel",)),
        name="projection",
    )(x, norm_g, w, _head_block_diag(), gains, *tabs)


def _attn_kernel(*refs, group, n_pairs, tq, tk, n_keys, band_r, has_sink, want_lse):
    refs = list(refs)
    sink_ref = refs.pop(0) if has_sink else None
    q_ref, k_ref, v_ref, o_ref = refs[:4]
    refs = refs[4:]
    lse_ref = refs.pop(0) if want_lse else None
    qm_ref, acc_ref = refs
    m_rows = group * tq
    n_chain = 2 * n_pairs
    i = pl.program_id(2)
    lane = lax.broadcasted_iota(I32, (1, LANES), 1)
    low = lane < HEAD_DIM
    half_mask = [jnp.where(low, 1.0, 0.0).astype(BF16), jnp.where(low, 0.0, 1.0).astype(BF16)]

    for p in range(n_pairs):
        q_p = jnp.concatenate(
            [q_ref[:, (p * group + g) * LANES:(p * group + g + 1) * LANES] for g in range(group)], axis=0)
        for par in range(2):
            qm_ref[2 * p + par] = q_p * half_mask[par]

    lane2 = lax.broadcasted_iota(I32, (m_rows, 2 * LANES), 1)
    if has_sink:
        acc0 = jnp.where(lane2 >= LANES, 1.0, 0.0).astype(F32)
        m0 = []
        for c in range(n_chain):
            m0.append(jnp.concatenate(
                [jnp.full((tq, 1), sink_ref[c * group + g], F32) for g in range(group)], axis=0))
    else:
        acc0 = jnp.zeros((m_rows, 2 * LANES), F32)
        m0 = [jnp.full((m_rows, 1), NEG_INF, F32) for _ in range(n_chain)]
    for c in range(n_chain):
        acc_ref[c] = acc0

    def step(k0, width, m_prev, mask):
        m_next = []
        for p in range(n_pairs):
            kt = k_ref[pl.ds(k0, width), p * LANES:(p + 1) * LANES]
            vt = v_ref[pl.ds(k0, width), 2 * p * LANES:2 * (p + 1) * LANES]
            for par in range(2):
                c = 2 * p + par
                s = lax.dot_general(qm_ref[c], kt, (((1,), (1,)), ((), ())), preferred_element_type=F32)
                if mask is not None:
                    s = jnp.where(mask, s, NEG_INF)
                m_new = jnp.maximum(m_prev[c], jnp.max(s, axis=1, keepdims=True))
                alpha = jnp.exp(m_prev[c] - m_new)
                pmat = jnp.exp(s - m_new).astype(BF16)
                pv = jnp.dot(pmat, vt, preferred_element_type=F32)
                acc_ref[c] = alpha * acc_ref[c] + pv
                m_next.append(m_new)
        return m_next

    if band_r is None:
        def body(kb, carry):
            return tuple(step(pl.multiple_of(kb * tk, tk), tk, list(carry), None))
        m_fin = lax.fori_loop(0, n_keys // tk, body, tuple(m0))
    else:
        width = min(tq + 2 * band_r, n_keys)
        ws = jnp.clip(i * tq - band_r, 0, n_keys - width)
        ws = pl.multiple_of(ws, 16)
        kpos = ws + lax.broadcasted_iota(I32, (1, width), 1)
        row = lax.broadcasted_iota(I32, (tq, 1), 0)
        qpos = i * tq + jnp.concatenate([row] * group, axis=0)
        mask = jnp.abs(kpos - qpos) <= band_r
        m_fin = step(ws, width, m0, mask)

    for p in range(n_pairs):
        for g in range(group):
            rows = slice(g * tq, (g + 1) * tq)
            a_even = acc_ref[2 * p][rows]
            a_odd = acc_ref[2 * p + 1][rows]
            num = jnp.where(low, a_even[:, :LANES], a_odd[:, :LANES])
            den = jnp.where(low, a_even[:, LANES:], a_odd[:, LANES:])
            cols = slice((p * group + g) * LANES, (p * group + g + 1) * LANES)
            o_ref[:, cols] = (num / den).astype(o_ref.dtype)
            if want_lse:
                m_sel = jnp.where(low, m_fin[2 * p][rows], m_fin[2 * p + 1][rows])
                lse_ref[:, cols] = m_sel + jnp.log(den)


def _attention(q, k, v, *, batch, n_keys, residues, col_group, n_col_groups, group, n_pairs,
               band_r, sink=None, want_lse=False):
    tq = ATTN_TQ
    wq = n_pairs * group * LANES
    wk = n_pairs * LANES
    wv = 2 * n_pairs * LANES
    q3 = q.reshape(batch, n_keys, -1)
    k3 = k.reshape(batch, n_keys, -1)
    v3 = v.reshape(batch, n_keys, -1)

    def col(r):
        return r * n_col_groups + col_group

    in_specs = []
    args = []
    if sink is not None:
        in_specs.append(pl.BlockSpec(memory_space=pltpu.SMEM))
        args.append(sink)
    in_specs += [
        pl.BlockSpec((None, tq, wq), lambda b, r, i: (b, i, col(r))),
        pl.BlockSpec((None, n_keys, wk), lambda b, r, i: (b, 0, col(r))),
        pl.BlockSpec((None, n_keys, wv), lambda b, r, i: (b, 0, col(r))),
    ]
    args += [q3, k3, v3]
    out_specs = [pl.BlockSpec((None, tq, wq), lambda b, r, i: (b, i, r))]
    out_shape = [jax.ShapeDtypeStruct((batch, n_keys, residues * wq), BF16)]
    if want_lse:
        out_specs.append(pl.BlockSpec((None, tq, wq), lambda b, r, i: (b, i, r)))
        out_shape.append(jax.ShapeDtypeStruct((batch, n_keys, residues * wq), F32))
    outs = pl.pallas_call(
        functools.partial(_attn_kernel, group=group, n_pairs=n_pairs, tq=tq, tk=DENSE_TK, n_keys=n_keys,
                          band_r=band_r, has_sink=sink is not None, want_lse=want_lse),
        grid=(batch, residues, n_keys // tq),
        in_specs=in_specs, out_specs=out_specs, out_shape=out_shape,
        scratch_shapes=[pltpu.VMEM((2 * n_pairs, group * tq, LANES), BF16),
                        pltpu.VMEM((2 * n_pairs, group * tq, 2 * LANES), F32)],
        compiler_params=_cparams(("parallel", "parallel", "arbitrary")),
        name="attention",
    )(*args)
    return [o.reshape(batch * n_keys * residues, wq) for o in outs]


def _outproj_kernel(*refs, n_merge):
    refs = list(refs)
    if n_merge:
        o_parts = refs[:n_merge]
        lse_parts = refs[n_merge:2 * n_merge]
        refs = refs[2 * n_merge:]
    o_rest, x_ref, w_ref, g_ref, wrh_ref, wrl_ref, xo_ref, h_ref, aff_ref = refs
    acc = x_ref[...]
    k0 = 0
    if n_merge:
        lses = [r[...] for r in lse_parts]
        m = functools.reduce(jnp.maximum, lses)
        ws = [jnp.exp(l - m) for l in lses]
        num = sum(wgt * r[...].astype(F32) for wgt, r in zip(ws, o_parts))
        oa = (num / sum(ws)).astype(BF16)
        k0 = oa.shape[1]
        acc = acc + jnp.dot(oa, w_ref[:k0, :], preferred_element_type=F32)
    acc = acc + jnp.dot(o_rest[...], w_ref[k0:, :], preferred_element_type=F32)
    xo_ref[...] = acc
    ms = jnp.mean(acc * acc, axis=1, keepdims=True)
    h = (acc * lax.rsqrt(ms + EPS)) * g_ref[...]
    h_hi = h.astype(BF16)
    h_ref[...] = h_hi
    h_lo = (h - h_hi.astype(F32)).astype(BF16)
    nt = (((1,), (1,)), ((), ()))
    logits = (lax.dot_general(wrh_ref[...], h_hi, nt, preferred_element_type=F32)
              + lax.dot_general(wrl_ref[...], h_hi, nt, preferred_element_type=F32)
              + lax.dot_general(wrh_ref[...], h_lo, nt, preferred_element_type=F32))
    mx = jnp.max(logits, axis=0, keepdims=True)
    e = jnp.exp(logits - mx)
    aff_ref[...] = e / jnp.sum(e, axis=0, keepdims=True)


def _out_projection(o_merge, lse_merge, o_rest, x, w_out, norm_g, wr_hi, wr_lo):
    t_tokens = x.shape[0]
    tm = PROJ_TM
    n_merge = len(o_merge)
    row = lambda i: (i, 0)
    fixed = lambda i: (0, 0)
    in_specs = ([pl.BlockSpec((tm, o.shape[1]), row) for o in o_merge]
                + [pl.BlockSpec((tm, l.shape[1]), row) for l in lse_merge]
                + [pl.BlockSpec((tm, o_rest.shape[1]), row),
                   pl.BlockSpec((tm, D_MODEL), row),
                   pl.BlockSpec(w_out.shape, fixed),
                   pl.BlockSpec((1, D_MODEL), fixed),
                   pl.BlockSpec(wr_hi.shape, fixed),
                   pl.BlockSpec(wr_lo.shape, fixed)])
    out_specs = [pl.BlockSpec((tm, D_MODEL), row),
                 pl.BlockSpec((tm, D_MODEL), row),
                 pl.BlockSpec((N_EXPERTS, tm), lambda i: (0, i))]
    out_shape = [jax.ShapeDtypeStruct((t_tokens, D_MODEL), F32),
                 jax.ShapeDtypeStruct((t_tokens, D_MODEL), BF16),
                 jax.ShapeDtypeStruct((N_EXPERTS, t_tokens), F32)]
    return pl.pallas_call(
        functools.partial(_outproj_kernel, n_merge=n_merge),
        grid=(t_tokens // tm,),
        in_specs=in_specs, out_specs=out_specs, out_shape=out_shape,
        compiler_params=_cparams(("parallel",)),
        name="out_projection",
    )(*o_merge, *lse_merge, o_rest, x, w_out, norm_g, wr_hi, wr_lo)


def _select_kernel(aff_ref, ind_ref, tri_ref, sel_ref, off_ref, *, cap):
    aff = aff_ref[...]
    n_tok = aff.shape[1]
    bits = pltpu.bitcast(aff, I32)

    def count(pred):
        return jnp.sum(jnp.where(pred, 1.0, 0.0), axis=1, keepdims=True)

    def value_step(j, thr):
        cand = thr | lax.shift_left(jnp.int32(1), 30 - j)
        return jnp.where(count(bits >= cand) >= cap, cand, thr)

    thr = lax.fori_loop(0, 31, value_step, jnp.zeros((N_EXPERTS, 1), I32))
    gt = bits > thr
    eq = bits == thr
    need = cap - count(gt)
    idx = lax.broadcasted_iota(I32, aff.shape, 1)
    idx_bits = int(math.log2(n_tok))

    def index_step(j, bound):
        cand = bound | lax.shift_left(jnp.int32(1), idx_bits - j)
        return jnp.where(count(eq & (idx < cand)) <= need, cand, bound)

    bound = lax.fori_loop(0, idx_bits + 1, index_step, jnp.zeros((N_EXPERTS, 1), I32))
    sel = jnp.where(gt | (eq & (idx < bound)), 1.0, 0.0).astype(BF16)
    sel_ref[...] = sel
    counts = jnp.dot(sel, ind_ref[...], preferred_element_type=F32)
    offs = jnp.dot(counts.astype(BF16), tri_ref[...], preferred_element_type=F32)
    off_ref[...] = offs.astype(I32)


def _strict_upper(n):
    return jnp.asarray(np.triu(np.ones((n, n), np.float32), k=1), dtype=BF16)


def _select(aff_t, cap):
    n_tok = aff_t.shape[1]
    ind = np.zeros((n_tok, LANES), np.float32)
    ind[np.arange(n_tok), np.arange(n_tok) // MOE_TB] = 1.0
    return pl.pallas_call(
        functools.partial(_select_kernel, cap=cap),
        out_shape=[jax.ShapeDtypeStruct((N_EXPERTS, n_tok), BF16),
                   jax.ShapeDtypeStruct((N_EXPERTS, LANES), I32)],
        compiler_params=pltpu.CompilerParams(vmem_limit_bytes=VMEM_LIMIT),
        name="select",
    )(aff_t, jnp.asarray(ind, dtype=BF16), _strict_upper(LANES))


def _pack_rows(y):
    half = y.shape[1] // 2
    yb = y.astype(BF16).astype(F32)
    hi = pltpu.bitcast(yb[:, :half], U32)
    lo = pltpu.bitcast(yb[:, half:], U32)
    return hi | lax.shift_right_logical(lo, jnp.uint32(16))


def _unpack_rows(w):
    hi = pltpu.bitcast(w & jnp.uint32(0xFFFF0000), F32).astype(BF16)
    lo = pltpu.bitcast(lax.shift_left(w, jnp.uint32(16)), F32).astype(BF16)
    return jnp.concatenate([hi, lo], axis=1)


SUBLANES = 8
NOT_SELECTED = -1e6


class _BlockLayout:
    def __init__(self, off_ref, b):
        self.start = [off_ref[e, b] for e in range(N_EXPERTS)]
        self.end = [off_ref[e, b + 1] for e in range(N_EXPERTS)]
        self.base = [lax.shift_left(lax.shift_right_logical(s, 3), 3) for s in self.start]
        self.shift = [s - a for s, a in zip(self.start, self.base)]
        span = [en - a for en, a in zip(self.end, self.base)]
        self.n_chunks = functools.reduce(jnp.maximum, [lax.shift_right_logical(sp, 6) for sp in span]) + 1
        tail = [lax.shift_left(lax.shift_right_logical(sp, 3), 3) for sp in span]
        self.tail_chunk = [lax.shift_right_logical(t, 6) for t in tail]
        self.tail_row = [pl.multiple_of(t & (MOE_W - 1), SUBLANES) for t in tail]

    def window_row(self, e, chunk):
        return pl.multiple_of(self.base[e] + chunk * MOE_W, SUBLANES)


def _slot_onehots(pos, weight, chunk):
    n_tok = pos.shape[1]
    slot = (lax.broadcasted_iota(I32, (MOE_W, n_tok), 0) + chunk * MOE_W).astype(F32)
    parts = []
    for e in range(N_EXPERTS):
        hit = pos[e:e + 1, :] == slot
        val = 1.0 if weight is None else weight[e:e + 1, :]
        parts.append(jnp.where(hit, val, 0.0).astype(BF16))
    return jnp.concatenate(parts, axis=0)


def _window_positions(sel_ref, tri_ref, layout):
    sel = sel_ref[...]
    rank = jnp.dot(sel, tri_ref[...], preferred_element_type=F32)
    shift = jnp.concatenate([jnp.full((1, 1), s, I32) for s in layout.shift], axis=0).astype(F32)
    return jnp.where(sel > 0, rank + shift, NOT_SELECTED)


def _dispatch_kernel(off_ref, sel_ref, h_ref, tri_ref, xe_ref, stage_ref, extra_ref, tail_ref, zero_ref, sem,
                     sem_extra):
    b = pl.program_id(0)
    nb = pl.num_programs(0)
    slot = b % 2
    layout = _BlockLayout(off_ref, b)
    pos = _window_positions(sel_ref, tri_ref, layout)
    h = h_ref[...]

    def rows_for(chunk):
        return _pack_rows(jnp.dot(_slot_onehots(pos, None, chunk), h, preferred_element_type=F32))

    def copy(src, e, row0, s):
        return pltpu.make_async_copy(src.at[e], xe_ref.at[e, pl.ds(row0, MOE_W)], s)

    @pl.when(b == 0)
    def _():
        tail_ref[...] = jnp.zeros(tail_ref.shape, U32)
        zero_ref[...] = jnp.zeros(zero_ref.shape, U32)
        pad = zero_ref.shape[0]
        fills = [pltpu.make_async_copy(zero_ref, xe_ref.at[e, pl.ds(xe_ref.shape[1] - pad, pad)], sem_extra.at[0])
                 for e in range(N_EXPERTS)]
        for f in fills:
            f.start()
        for f in fills:
            f.wait()

    packed = rows_for(0)

    @pl.when(b > 0)
    def _():
        for e in range(N_EXPERTS):
            copy(stage_ref.at[1 - slot], e, 0, sem.at[1 - slot]).wait()

    for e in range(N_EXPERTS):
        r0 = e * MOE_W
        stage_ref[slot, e, :SUBLANES] = packed[r0:r0 + SUBLANES] | tail_ref[e]
        stage_ref[slot, e, SUBLANES:] = packed[r0 + SUBLANES:r0 + MOE_W]
    for e in range(N_EXPERTS):
        copy(stage_ref.at[slot], e, layout.window_row(e, 0), sem.at[slot]).start()
    for e in range(N_EXPERTS):
        @pl.when(layout.tail_chunk[e] == 0)
        def _():
            tail_ref[e] = stage_ref[slot, e, pl.ds(layout.tail_row[e], SUBLANES), :]

    def overflow(chunk, carry):
        more = rows_for(chunk)
        for e in range(N_EXPERTS):
            extra_ref[e] = more[e * MOE_W:(e + 1) * MOE_W]
        for e in range(N_EXPERTS):
            copy(extra_ref, e, layout.window_row(e, chunk), sem_extra.at[0]).start()
        for e in range(N_EXPERTS):
            @pl.when(layout.tail_chunk[e] == chunk)
            def _():
                tail_ref[e] = extra_ref[e, pl.ds(layout.tail_row[e], SUBLANES), :]
        for e in range(N_EXPERTS):
            copy(extra_ref, e, 0, sem_extra.at[0]).wait()
        return carry

    lax.fori_loop(1, layout.n_chunks, overflow, 0)

    @pl.when(b == nb - 1)
    def _():
        for e in range(N_EXPERTS):
            copy(stage_ref.at[slot], e, 0, sem.at[slot]).wait()


def _slot_rows(cap):
    pad = ((SUBLANES + MOE_TB) // MOE_W + 1) * MOE_W
    return cap + -(-pad // FFN_TM) * FFN_TM


def _dispatch(offs, sel, h, cap):
    n_tok = h.shape[0]
    nb = n_tok // MOE_TB
    rows = _slot_rows(cap)
    half = D_MODEL // 2
    grid_spec = pltpu.PrefetchScalarGridSpec(
        num_scalar_prefetch=1,
        grid=(nb,),
        in_specs=[pl.BlockSpec((N_EXPERTS, MOE_TB), lambda b, off: (0, b)),
                  pl.BlockSpec((MOE_TB, D_MODEL), lambda b, off: (b, 0)),
                  pl.BlockSpec((MOE_TB, MOE_TB), lambda b, off: (0, 0))],
        out_specs=pl.BlockSpec(memory_space=pl.ANY),
        scratch_shapes=[pltpu.VMEM((2, N_EXPERTS, MOE_W, half), U32),
                        pltpu.VMEM((N_EXPERTS, MOE_W, half), U32),
                        pltpu.VMEM((N_EXPERTS, SUBLANES, half), U32),
                        pltpu.VMEM((rows - cap, half), U32),
                        pltpu.SemaphoreType.DMA((2,)),
                        pltpu.SemaphoreType.DMA((1,))],
    )
    return pl.pallas_call(
        _dispatch_kernel,
        grid_spec=grid_spec,
        out_shape=jax.ShapeDtypeStruct((N_EXPERTS, rows, half), U32),
        compiler_params=_cparams(("arbitrary",)),
        name="dispatch",
    )(offs, sel, h, _strict_upper(MOE_TB))


def _ffn_kernel(xe_ref, wg_ref, wu_ref, wd_ref, ye_ref, *, n_real):
    j = pl.program_id(1)

    @pl.when(j < n_real)
    def _():
        x = _unpack_rows(xe_ref[...])
        gate = jnp.dot(x, wg_ref[...], preferred_element_type=F32)
        up = jnp.dot(x, wu_ref[...], preferred_element_type=F32)
        hid = (gate * jax.nn.sigmoid(gate) * up).astype(BF16)
        ye_ref[...] = _pack_rows(jnp.dot(hid, wd_ref[...], preferred_element_type=F32))

    @pl.when(j >= n_real)
    def _():
        ye_ref[...] = jnp.zeros(ye_ref.shape, U32)


def _expert_ffn(xe, w_gate, w_up, w_down, cap):
    rows = xe.shape[1]
    half = D_MODEL // 2
    n_real = cap // FFN_TM
    n_tiles = rows // FFN_TM
    wspec = lambda shape: pl.BlockSpec((None,) + shape, lambda e, j: (e, 0, 0))
    return pl.pallas_call(
        functools.partial(_ffn_kernel, n_real=n_real),
        grid=(N_EXPERTS, n_tiles),
        in_specs=[pl.BlockSpec((None, FFN_TM, half), lambda e, j: (e, j, 0)),
                  wspec((D_MODEL, EXPERT_FF)), wspec((D_MODEL, EXPERT_FF)), wspec((EXPERT_FF, D_MODEL))],
        out_specs=pl.BlockSpec((None, FFN_TM, half), lambda e, j: (e, j, 0)),
        out_shape=jax.ShapeDtypeStruct((N_EXPERTS, rows, half), U32),
        compiler_params=_cparams(("parallel", "arbitrary")),
        name="expert_ffn",
    )(xe, w_gate, w_up, w_down)


def _combine_kernel(off_ref, sel_ref, aff_ref, tri_ref, x_ref, ye_ref, out_ref, win_ref, extra_ref, sem, sem_extra,
                    *, rows):
    b = pl.program_id(0)
    nb = pl.num_programs(0)
    slot = b % 2

    def copy(dst, e, row0, s):
        return pltpu.make_async_copy(ye_ref.at[e, pl.ds(row0, MOE_W)], dst.at[e], s)

    def fetch(blk, dst_slot):
        ahead = _BlockLayout(off_ref, blk)
        for e in range(N_EXPERTS):
            copy(win_ref.at[dst_slot], e, ahead.window_row(e, 0), sem.at[dst_slot]).start()

    @pl.when(b == 0)
    def _():
        fetch(0, 0)

    @pl.when(b + 1 < nb)
    def _():
        fetch(b + 1, 1 - slot)

    layout = _BlockLayout(off_ref, b)
    pos = _window_positions(sel_ref, tri_ref, layout)
    aff = aff_ref[...]
    tn = (((0,), (0,)), ((), ()))

    def contribution(chunk, window):
        gates = _slot_onehots(pos, aff, chunk)
        vals = _unpack_rows(window.reshape(N_EXPERTS * MOE_W, window.shape[-1]))
        return lax.dot_general(gates, vals, tn, preferred_element_type=F32)

    for e in range(N_EXPERTS):
        copy(win_ref.at[slot], e, 0, sem.at[slot]).wait()
    out_ref[...] = x_ref[...] + contribution(0, win_ref[slot])

    def overflow(chunk, carry):
        for e in range(N_EXPERTS):
            row0 = pl.multiple_of(jnp.minimum(layout.window_row(e, chunk), rows - MOE_W), SUBLANES)
            copy(extra_ref, e, row0, sem_extra.at[0]).start()
        for e in range(N_EXPERTS):
            copy(extra_ref, e, 0, sem_extra.at[0]).wait()
        out_ref[...] += contribution(chunk, extra_ref[...])
        return carry

    lax.fori_loop(1, layout.n_chunks, overflow, 0)


def _combine(offs, sel, aff_t, x, ye):
    n_tok = x.shape[0]
    nb = n_tok // MOE_TB
    rows = ye.shape[1]
    half = D_MODEL // 2
    grid_spec = pltpu.PrefetchScalarGridSpec(
        num_scalar_prefetch=1,
        grid=(nb,),
        in_specs=[pl.BlockSpec((N_EXPERTS, MOE_TB), lambda b, off: (0, b)),
                  pl.BlockSpec((N_EXPERTS, MOE_TB), lambda b, off: (0, b)),
                  pl.BlockSpec((MOE_TB, MOE_TB), lambda b, off: (0, 0)),
                  pl.BlockSpec((MOE_TB, D_MODEL), lambda b, off: (b, 0)),
                  pl.BlockSpec(memory_space=pl.ANY)],
        out_specs=pl.BlockSpec((MOE_TB, D_MODEL), lambda b, off: (b, 0)),
        scratch_shapes=[pltpu.VMEM((2, N_EXPERTS, MOE_W, half), U32),
                        pltpu.VMEM((N_EXPERTS, MOE_W, half), U32),
                        pltpu.SemaphoreType.DMA((2,)),
                        pltpu.SemaphoreType.DMA((1,))],
    )
    return pl.pallas_call(
        functools.partial(_combine_kernel, rows=rows),
        grid_spec=grid_spec,
        out_shape=jax.ShapeDtypeStruct((n_tok, D_MODEL), F32),
        compiler_params=_cparams(("arbitrary",)),
        name="combine",
    )(offs, sel, aff_t, _strict_upper(MOE_TB), x, ye)


def _moe(x, h, aff_t, w_gate, w_up, w_down):
    cap = CAPACITY_FACTOR * x.shape[0] // N_EXPERTS
    sel, offs = _select(aff_t, cap)
    xe = _dispatch(offs, sel, h, cap)
    ye = _expert_ffn(xe, w_gate, w_up, w_down, cap)
    return _combine(offs, sel, aff_t, x, ye)


def _row(v):
    return v.reshape(1, -1).astype(F32)


def _router_split(w_router):
    wt = w_router.T.astype(F32)
    hi = wt.astype(BF16)
    lo = (wt - hi.astype(F32)).astype(BF16)
    return hi, lo


def _layer_ab(x, batch, seq, tabs, norm_g, w_in, qn_a, kn_a, qn_b, kn_b, w_out, norm_ffn, w_router):
    tab_a, tab_b = tabs
    scale = HEAD_DIM ** -0.5
    perm_b = _pair_layout_perm(B_KV_HEADS, B_Q_HEADS // B_KV_HEADS)
    a3 = 3 * A_WIDTH
    bq = B_Q_HEADS * HEAD_DIM
    w_cols = np.concatenate([np.arange(a3), a3 + perm_b, np.arange(a3 + bq, w_in.shape[1])])
    w = w_in[:, w_cols].astype(BF16)
    gains = jnp.concatenate([
        jnp.repeat(qn_a, A_HEADS, axis=0).reshape(-1) * scale,
        jnp.repeat(kn_a, A_HEADS, axis=0).reshape(-1),
        jnp.ones((A_WIDTH,), F32),
        jnp.tile(qn_b, B_Q_HEADS) * scale,
        jnp.tile(kn_b, B_KV_HEADS),
        jnp.ones((B_KV_HEADS * HEAD_DIM,), F32)])
    nq = A_WIDTH // MXU_COLS
    plan = ([("qk", 0, c * MXU_COLS, 0, ROT_DIM // 2) for c in range(nq)]
            + [("qk", 1, c * MXU_COLS, 0, ROT_DIM // 2) for c in range(nq)]
            + [("v", 2, 2 * c * MXU_COLS, 0, 0) for c in range(nq)]
            + [("qk", 3, c * MXU_COLS, 1, HEAD_DIM // 4) for c in range(bq // MXU_COLS)]
            + [("qk", 4, 0, 1, HEAD_DIM // 4), ("v", 5, 0, 0, 0)])
    qa, ka, va, qb, kb, vb = _projection(
        x, _row(norm_g), w, _row(gains), (tab_a, tab_b), plan,
        (A_WIDTH, A_WIDTH, 2 * A_WIDTH, bq, B_KV_HEADS * HEAD_DIM, 2 * B_KV_HEADS * HEAD_DIM), seq)
    o_parts, lse_parts = [], []
    for gi, (window, dil) in enumerate(A_PATTERNS):
        o, lse = _attention(qa, ka, va, batch=batch, n_keys=seq // dil, residues=dil, col_group=gi,
                            n_col_groups=A_GROUPS, group=1, n_pairs=A_HEADS // 2,
                            band_r=window // (2 * dil), want_lse=True)
        o_parts.append(o)
        lse_parts.append(lse)
    (ob,) = _attention(qb, kb, vb, batch=batch, n_keys=seq, residues=1, col_group=0, n_col_groups=1,
                       group=B_Q_HEADS // B_KV_HEADS, n_pairs=B_KV_HEADS // 2, band_r=None)
    a_out = A_HEADS * HEAD_DIM
    w_o = w_out[np.concatenate([np.arange(a_out), a_out + perm_b])].astype(BF16)
    wr_hi, wr_lo = _router_split(w_router)
    return _out_projection(o_parts, lse_parts, ob, x, w_o, _row(norm_ffn), wr_hi, wr_lo)


def _layer_c(x, batch, seq, tabs, norm_g, w_in, qn, kn, sink, w_out, norm_ffn, w_router):
    tab_a, _ = tabs
    scale = HEAD_DIM ** -0.5
    group = C_Q_HEADS // C_KV_HEADS
    perm = _pair_layout_perm(C_KV_HEADS, group)
    cq = C_Q_HEADS * HEAD_DIM
    w_cols = np.concatenate([perm, np.arange(cq, w_in.shape[1])])
    w = w_in[:, w_cols].astype(BF16)
    gains = jnp.concatenate([jnp.tile(qn, C_Q_HEADS) * scale, jnp.tile(kn, C_KV_HEADS),
                             jnp.ones((C_KV_HEADS * HEAD_DIM,), F32)])
    plan = ([("qk", 0, c * MXU_COLS, 0, ROT_DIM // 2) for c in range(cq // MXU_COLS)]
            + [("qk", 1, 0, 0, ROT_DIM // 2), ("v", 2, 0, 0, 0)])
    q, k, v = _projection(x, _row(norm_g), w, _row(gains), (tab_a,), plan,
                          (cq, C_KV_HEADS * HEAD_DIM, 2 * C_KV_HEADS * HEAD_DIM), seq)
    (o,) = _attention(q, k, v, batch=batch, n_keys=seq, residues=1, col_group=0, n_col_groups=1,
                      group=group, n_pairs=C_KV_HEADS // 2, band_r=C_RADIUS, sink=sink.astype(F32))
    wr_hi, wr_lo = _router_split(w_router)
    return _out_projection([], [], o, x, w_out[perm].astype(BF16), _row(norm_ffn), wr_hi, wr_lo)


def _encode(x, params):
    (norm_mix, norm_ffn, w_in_ab, qn_a, kn_a, qn_b, kn_b, w_out_ab, w_in_c, qn_c, kn_c, sink_c, w_out_c,
     w_router, w_gate, w_up, w_down) = params
    batch, seq, _ = x.shape
    tabs = _rope_tables(seq)
    xt = x.reshape(batch * seq, D_MODEL)
    for layer in range(norm_mix.shape[0]):
        j = layer // 2
        if layer % 2 == 0:
            xt, h, aff_t = _layer_ab(xt, batch, seq, tabs, norm_mix[layer], w_in_ab[j], qn_a[j], kn_a[j],
                                     qn_b[j], kn_b[j], w_out_ab[j], norm_ffn[layer], w_router[layer])
        else:
            xt, h, aff_t = _layer_c(xt, batch, seq, tabs, norm_mix[layer], w_in_c[j], qn_c[j], kn_c[j],
                                    sink_c[j], w_out_c[j], norm_ffn[layer], w_router[layer])
        xt = _moe(xt, h, aff_t, w_gate[layer].astype(BF16), w_up[layer].astype(BF16),
                  w_down[layer].astype(BF16))
    return xt.reshape(batch, seq, D_MODEL)


def kernel(x_prompt, x_sample, norm_mix, norm_ffn, w_in_ab, qn_a, kn_a, qn_b, kn_b, w_out_ab, w_in_c, qn_c,
           kn_c, sink_c, w_out_c, w_router, w_gate, w_up, w_down):
    params = (norm_mix, norm_ffn, w_in_ab, qn_a, kn_a, qn_b, kn_b, w_out_ab, w_in_c, qn_c, kn_c, sink_c,
              w_out_c, w_router, w_gate, w_up, w_down)
    return _encode(x_prompt, params), _encode(x_sample, params)
```

```python
import functools
import math

import jax
import jax.numpy as jnp
import numpy as np
from jax import lax
from jax.experimental import pallas as pl
from jax.experimental.pallas import tpu as pltpu

F32 = jnp.float32
BF16 = jnp.bfloat16
I32 = jnp.int32
U32 = jnp.uint32

D_MODEL = 1024
HEAD_DIM = 64
LANES = 128
MXU_COLS = 256
GRID_W = 64
ROT_DIM = HEAD_DIM // 4
ROPE_THETA = 500000.0
AXIAL_THETA = 10000.0
A_PATTERNS = ((128, 1), (512, 4), (2048, 16))
A_HEADS = 4
A_GROUPS = len(A_PATTERNS)
A_WIDTH = A_GROUPS * A_HEADS * HEAD_DIM
B_Q_HEADS = 12
B_KV_HEADS = 4
C_Q_HEADS = 16
C_KV_HEADS = 4
C_RADIUS = 128
N_EXPERTS = 16
EXPERT_FF = 1024
CAPACITY_FACTOR = 2
NEG_INF = -1e30
EPS = 1e-6

VMEM_LIMIT = 56 * 1024 * 1024

PROJ_TM = 512
ATTN_TQ = 128
BAND_TQ = 256
DENSE_TK = 512
MOE_TB = 256
MOE_W = 64
FFN_TM = 256


def _cparams(sem):
    return pltpu.CompilerParams(dimension_semantics=sem, vmem_limit_bytes=VMEM_LIMIT)


def _head_block_diag():
    idx = np.arange(MXU_COLS) // HEAD_DIM
    return jnp.asarray((idx[:, None] == idx[None, :]).astype(np.float32), dtype=BF16)


def _rope_tables(max_len):
    pos = jnp.arange(max_len, dtype=F32)
    j = np.arange(LANES) % HEAD_DIM

    def angles(p, dim, theta):
        exps = jnp.arange(0, dim, 2, dtype=F32) / dim
        inv = jnp.power(jnp.float32(theta), -exps)
        return p[:, None] * inv[None, :]

    half = ROT_DIM // 2
    ang = angles(pos, ROT_DIM, ROPE_THETA)
    cos, sin = jnp.cos(ang), jnp.sin(ang)
    fa = np.where(j < half, j, np.where(j < ROT_DIM, j - half, 0))
    cos_l, sin_l = cos[:, fa], sin[:, fa]
    lo = jnp.asarray(j < half)[None, :]
    hi = jnp.asarray((j >= half) & (j < ROT_DIM))[None, :]
    tab_a = jnp.stack([jnp.where(lo | hi, cos_l, 1.0),
                       jnp.where(lo, -sin_l, 0.0),
                       jnp.where(hi, sin_l, 0.0)])
    hb = HEAD_DIM // 2
    qb = hb // 2
    t = jnp.arange(max_len)
    ang_r = angles((t // GRID_W).astype(F32), hb, AXIAL_THETA)
    ang_c = angles((t % GRID_W).astype(F32), hb, AXIAL_THETA)
    fb = j % qb
    is_col = jnp.asarray(j >= hb)[None, :]
    ang_l = jnp.where(is_col, ang_c[:, fb], ang_r[:, fb])
    cos_b, sin_b = jnp.cos(ang_l), jnp.sin(ang_l)
    first = jnp.asarray((j % hb) < qb)[None, :]
    tab_b = jnp.stack([cos_b, jnp.where(first, -sin_b, 0.0), jnp.where(first, 0.0, sin_b)])
    return tab_a.astype(F32), tab_b.astype(F32)


def _pair_layout_perm(n_kv, group):
    cols = []
    for p in range(n_kv // 2):
        for g in range(group):
            for par in range(2):
                h = (2 * p + par) * group + g
                cols.extend(range(h * HEAD_DIM, (h + 1) * HEAD_DIM))
    return np.asarray(cols, dtype=np.int32)


def _proj_kernel(x_ref, g_ref, w_ref, s_ref, gain_ref, *rest, plan, n_tabs):
    tab_refs = rest[:n_tabs]
    out_refs = rest[n_tabs:]
    x = x_ref[...]
    ms = jnp.mean(x * x, axis=1, keepdims=True)
    xn = ((x * lax.rsqrt(ms + EPS)) * g_ref[...]).astype(BF16)
    tm = x.shape[0]
    ones = jnp.ones((tm, LANES), BF16)
    for c, (kind, out_i, out_col, tab_i, shift) in enumerate(plan):
        acc = jnp.dot(xn, w_ref[:, c * MXU_COLS:(c + 1) * MXU_COLS], preferred_element_type=F32)
        o_ref = out_refs[out_i]
        if kind == "v":
            o_ref[:, out_col:out_col + MXU_COLS] = acc.astype(BF16)
            continue
        if kind == "v_ones":
            for p in range(MXU_COLS // LANES):
                base = out_col + 2 * p * LANES
                o_ref[:, base:base + LANES] = acc[:, p * LANES:(p + 1) * LANES].astype(BF16)
                o_ref[:, base + LANES:base + 2 * LANES] = ones
            continue
        ss = jnp.dot((acc * acc).astype(BF16), s_ref[...], preferred_element_type=F32)
        y = (acc * lax.rsqrt(ss * (1.0 / HEAD_DIM) + EPS)) * gain_ref[:, c * MXU_COLS:(c + 1) * MXU_COLS]
        tab = tab_refs[tab_i]
        t0 = jnp.concatenate([tab[0], tab[0]], axis=1)
        t1 = jnp.concatenate([tab[1], tab[1]], axis=1)
        t2 = jnp.concatenate([tab[2], tab[2]], axis=1)
        y = y * t0 + pltpu.roll(y, MXU_COLS - shift, 1) * t1 + pltpu.roll(y, shift, 1) * t2
        o_ref[:, out_col:out_col + MXU_COLS] = y.astype(BF16)


def _projection(x, norm_g, w, gains, tabs, plan, out_widths, seq_len):
    t_tokens = x.shape[0]
    tm = PROJ_TM
    n_in = w.shape[1]
    blocks_per_seq = seq_len // tm
    in_specs = [
        pl.BlockSpec((tm, D_MODEL), lambda i: (i, 0)),
        pl.BlockSpec((1, D_MODEL), lambda i: (0, 0)),
        pl.BlockSpec((D_MODEL, n_in), lambda i: (0, 0)),
        pl.BlockSpec((MXU_COLS, MXU_COLS), lambda i: (0, 0)),
        pl.BlockSpec((1, n_in), lambda i: (0, 0)),
    ] + [pl.BlockSpec((3, tm, LANES), lambda i: (0, i % blocks_per_seq, 0)) for _ in tabs]
    out_specs = [pl.BlockSpec((tm, wd), lambda i: (i, 0)) for wd in out_widths]
    out_shape = [jax.ShapeDtypeStruct((t_tokens, wd), BF16) for wd in out_widths]
    return pl.pallas_call(
        functools.partial(_proj_kernel, plan=tuple(plan), n_tabs=len(tabs)),
        grid=(t_tokens // tm,),
        in_specs=in_specs, out_specs=out_specs, out_shape=out_shape,
        compiler_params=_cparams(("parallel",)),
        name="projection",
    )(x, norm_g, w, _head_block_diag(), gains, *tabs)


LOG2E = math.log2(math.e)
LN2 = math.log(2.0)
_NT = (((1,), (1,)), ((), ()))


def _attn_kernel(*refs, group, n_pairs, tq, tk, n_keys, band_r, has_sink, want_lse):
    refs = list(refs)
    sink_ref = refs.pop(0) if has_sink else None
    q_ref, k_ref, v_ref, o_ref = refs[:4]
    refs = refs[4:]
    lse_ref = refs.pop(0) if want_lse else None
    qm_ref, acc_ref = refs[:2]
    s_refs = refs[2:]
    m_rows = group * tq
    n_chain = 2 * n_pairs
    i = pl.program_id(2)
    lane = lax.broadcasted_iota(I32, (1, LANES), 1)
    low = lane < HEAD_DIM
    half_mask = [jnp.where(low, 1.0, 0.0).astype(BF16), jnp.where(low, 0.0, 1.0).astype(BF16)]

    for p in range(n_pairs):
        q_p = jnp.concatenate(
            [q_ref[:, (p * group + g) * LANES:(p * group + g + 1) * LANES] for g in range(group)], axis=0)
        for par in range(2):
            qm_ref[2 * p + par] = q_p * half_mask[par]

    lane2 = lax.broadcasted_iota(I32, (m_rows, 2 * LANES), 1)
    if has_sink:
        acc0 = jnp.where(lane2 >= LANES, 1.0, 0.0).astype(F32)
        m0 = []
        for c in range(n_chain):
            m0.append(jnp.concatenate(
                [jnp.full((tq, 1), sink_ref[c * group + g] * LOG2E, F32) for g in range(group)], axis=0))
    else:
        acc0 = jnp.zeros((m_rows, 2 * LANES), F32)
        m0 = [jnp.full((m_rows, 1), NEG_INF, F32) for _ in range(n_chain)]
    for c in range(n_chain):
        acc_ref[c] = acc0

    def softmax_pv(c, s, vt, m_prev):
        m_new = jnp.maximum(m_prev, jnp.max(s, axis=1, keepdims=True))
        alpha = jnp.exp2(m_prev - m_new)
        pmat = jnp.exp2(s - m_new).astype(BF16)
        acc_ref[c] = alpha * acc_ref[c] + jnp.dot(pmat, vt, preferred_element_type=F32)
        return m_new

    if band_r is None:
        s_even, s_odd = s_refs

        def scores(kb, s_ref):
            k0 = pl.multiple_of(kb * tk, tk)
            for p in range(n_pairs):
                kt = k_ref[pl.ds(k0, tk), p * LANES:(p + 1) * LANES]
                for par in range(2):
                    c = 2 * p + par
                    s_ref[c] = lax.dot_general(qm_ref[c], kt, _NT, preferred_element_type=F32)

        def consume(kb, s_ref, m_prev):
            k0 = pl.multiple_of(kb * tk, tk)
            m_next = []
            for p in range(n_pairs):
                vt = v_ref[pl.ds(k0, tk), 2 * p * LANES:2 * (p + 1) * LANES]
                for par in range(2):
                    c = 2 * p + par
                    m_next.append(softmax_pv(c, s_ref[c], vt, m_prev[c]))
            return m_next

        n_blocks = n_keys // tk
        scores(0, s_even)

        def body(j, carry):
            m = list(carry)
            scores(2 * j + 1, s_odd)
            m = consume(2 * j, s_even, m)
            scores(2 * j + 2, s_even)
            m = consume(2 * j + 1, s_odd, m)
            return tuple(m)

        m_fin = list(lax.fori_loop(0, n_blocks // 2 - 1, body, tuple(m0)))
        scores(n_blocks - 1, s_odd)
        m_fin = consume(n_blocks - 2, s_even, m_fin)
        m_fin = consume(n_blocks - 1, s_odd, m_fin)
    else:
        width = min(tq + 2 * band_r, n_keys)
        ws = jnp.clip(i * tq - band_r, 0, n_keys - width)
        ws = pl.multiple_of(ws, 16)
        kpos = ws + lax.broadcasted_iota(I32, (1, width), 1)
        row = lax.broadcasted_iota(I32, (tq, 1), 0)
        qpos = i * tq + jnp.concatenate([row] * group, axis=0)
        mask = jnp.abs(kpos - qpos) <= band_r
        ones = jnp.ones((width, LANES), BF16)
        m_fin = []
        for p in range(n_pairs):
            kt = k_ref[pl.ds(ws, width), p * LANES:(p + 1) * LANES]
            vt = jnp.concatenate([v_ref[pl.ds(ws, width), p * LANES:(p + 1) * LANES], ones], axis=1)
            for par in range(2):
                c = 2 * p + par
                s = lax.dot_general(qm_ref[c], kt, _NT, preferred_element_type=F32)
                m_fin.append(softmax_pv(c, jnp.where(mask, s, NEG_INF), vt, m0[c]))

    for p in range(n_pairs):
        for g in range(group):
            rows = slice(g * tq, (g + 1) * tq)
            a_even = acc_ref[2 * p][rows]
            a_odd = acc_ref[2 * p + 1][rows]
            num = jnp.where(low, a_even[:, :LANES], a_odd[:, :LANES])
            den = jnp.where(low, a_even[:, LANES:], a_odd[:, LANES:])
            cols = slice((p * group + g) * LANES, (p * group + g + 1) * LANES)
            o_ref[:, cols] = (num / den).astype(o_ref.dtype)
            if want_lse:
                m_sel = jnp.where(low, m_fin[2 * p][rows], m_fin[2 * p + 1][rows])
                lse_ref[:, cols] = m_sel * LN2 + jnp.log(den)


def _attention(q, k, v, *, batch, n_keys, residues, group, n_pairs, band_r, sink=None, want_lse=False):
    dense = band_r is None
    tq = ATTN_TQ if dense else BAND_TQ
    wq = n_pairs * group * LANES
    wk = n_pairs * LANES
    wv = 2 * wk if dense else wk
    q3 = q.reshape(batch, n_keys, residues * wq)
    k3 = k.reshape(batch, n_keys, residues * wk)
    v3 = v.reshape(batch, n_keys, residues * wv)

    in_specs = []
    args = []
    if sink is not None:
        in_specs.append(pl.BlockSpec(memory_space=pltpu.SMEM))
        args.append(sink)
    in_specs += [
        pl.BlockSpec((None, tq, wq), lambda b, r, i: (b, i, r)),
        pl.BlockSpec((None, n_keys, wk), lambda b, r, i: (b, 0, r)),
        pl.BlockSpec((None, n_keys, wv), lambda b, r, i: (b, 0, r)),
    ]
    args += [q3, k3, v3]
    out_specs = [pl.BlockSpec((None, tq, wq), lambda b, r, i: (b, i, r))]
    out_shape = [jax.ShapeDtypeStruct((batch, n_keys, residues * wq), BF16)]
    if want_lse:
        out_specs.append(pl.BlockSpec((None, tq, wq), lambda b, r, i: (b, i, r)))
        out_shape.append(jax.ShapeDtypeStruct((batch, n_keys, residues * wq), F32))
    m_rows = group * tq
    scratch = [pltpu.VMEM((2 * n_pairs, m_rows, LANES), BF16),
               pltpu.VMEM((2 * n_pairs, m_rows, 2 * LANES), F32)]
    if dense:
        scratch += [pltpu.VMEM((2 * n_pairs, m_rows, DENSE_TK), F32) for _ in range(2)]
    outs = pl.pallas_call(
        functools.partial(_attn_kernel, group=group, n_pairs=n_pairs, tq=tq, tk=DENSE_TK, n_keys=n_keys,
                          band_r=band_r, has_sink=sink is not None, want_lse=want_lse),
        grid=(batch, residues, n_keys // tq),
        in_specs=in_specs, out_specs=out_specs, out_shape=out_shape,
        scratch_shapes=scratch,
        compiler_params=_cparams(("parallel", "parallel", "arbitrary")),
        name="attention_dense" if dense else "attention_band",
    )(*args)
    return [o.reshape(batch * n_keys * residues, wq) for o in outs]


def _outproj_kernel(*refs, n_merge):
    refs = list(refs)
    if n_merge:
        o_parts = refs[:n_merge]
        lse_parts = refs[n_merge:2 * n_merge]
        refs = refs[2 * n_merge:]
    o_rest, x_ref, w_ref, g_ref, wrh_ref, wrl_ref, xo_ref, h_ref, aff_ref = refs
    acc = x_ref[...]
    k0 = 0
    if n_merge:
        lses = [r[...] for r in lse_parts]
        m = functools.reduce(jnp.maximum, lses)
        ws = [jnp.exp(l - m) for l in lses]
        num = sum(wgt * r[...].astype(F32) for wgt, r in zip(ws, o_parts))
        oa = (num / sum(ws)).astype(BF16)
        k0 = oa.shape[1]
        acc = acc + jnp.dot(oa, w_ref[:k0, :], preferred_element_type=F32)
    acc = acc + jnp.dot(o_rest[...], w_ref[k0:, :], preferred_element_type=F32)
    xo_ref[...] = acc
    ms = jnp.mean(acc * acc, axis=1, keepdims=True)
    h = (acc * lax.rsqrt(ms + EPS)) * g_ref[...]
    h_hi = h.astype(BF16)
    h_ref[...] = h_hi
    h_lo = (h - h_hi.astype(F32)).astype(BF16)
    nt = (((1,), (1,)), ((), ()))
    logits = (lax.dot_general(wrh_ref[...], h_hi, nt, preferred_element_type=F32)
              + lax.dot_general(wrl_ref[...], h_hi, nt, preferred_element_type=F32)
              + lax.dot_general(wrh_ref[...], h_lo, nt, preferred_element_type=F32))
    mx = jnp.max(logits, axis=0, keepdims=True)
    e = jnp.exp(logits - mx)
    aff_ref[...] = e / jnp.sum(e, axis=0, keepdims=True)


def _out_projection(o_merge, lse_merge, o_rest, x, w_out, norm_g, wr_hi, wr_lo):
    t_tokens = x.shape[0]
    tm = PROJ_TM
    n_merge = len(o_merge)
    row = lambda i: (i, 0)
    fixed = lambda i: (0, 0)
    in_specs = ([pl.BlockSpec((tm, o.shape[1]), row) for o in o_merge]
                + [pl.BlockSpec((tm, l.shape[1]), row) for l in lse_merge]
                + [pl.BlockSpec((tm, o_rest.shape[1]), row),
                   pl.BlockSpec((tm, D_MODEL), row),
                   pl.BlockSpec(w_out.shape, fixed),
                   pl.BlockSpec((1, D_MODEL), fixed),
                   pl.BlockSpec(wr_hi.shape, fixed),
                   pl.BlockSpec(wr_lo.shape, fixed)])
    out_specs = [pl.BlockSpec((tm, D_MODEL), row),
                 pl.BlockSpec((tm, D_MODEL), row),
                 pl.BlockSpec((N_EXPERTS, tm), lambda i: (0, i))]
    out_shape = [jax.ShapeDtypeStruct((t_tokens, D_MODEL), F32),
                 jax.ShapeDtypeStruct((t_tokens, D_MODEL), BF16),
                 jax.ShapeDtypeStruct((N_EXPERTS, t_tokens), F32)]
    return pl.pallas_call(
        functools.partial(_outproj_kernel, n_merge=n_merge),
        grid=(t_tokens // tm,),
        in_specs=in_specs, out_specs=out_specs, out_shape=out_shape,
        compiler_params=_cparams(("parallel",)),
        name="out_projection",
    )(*o_merge, *lse_merge, o_rest, x, w_out, norm_g, wr_hi, wr_lo)


def _select_kernel(aff_ref, ind_ref, tri_ref, sel_ref, off_ref, *, cap):
    aff = aff_ref[...]
    n_tok = aff.shape[1]
    bits = pltpu.bitcast(aff, I32)

    def count(pred):
        return jnp.sum(jnp.where(pred, 1.0, 0.0), axis=1, keepdims=True)

    def value_step(j, thr):
        cand = thr | lax.shift_left(jnp.int32(1), 30 - j)
        return jnp.where(count(bits >= cand) >= cap, cand, thr)

    thr = lax.fori_loop(0, 31, value_step, jnp.zeros((N_EXPERTS, 1), I32))
    gt = bits > thr
    eq = bits == thr
    need = cap - count(gt)
    idx = lax.broadcasted_iota(I32, aff.shape, 1)
    idx_bits = int(math.log2(n_tok))

    def index_step(j, bound):
        cand = bound | lax.shift_left(jnp.int32(1), idx_bits - j)
        return jnp.where(count(eq & (idx < cand)) <= need, cand, bound)

    bound = lax.fori_loop(0, idx_bits + 1, index_step, jnp.zeros((N_EXPERTS, 1), I32))
    sel = jnp.where(gt | (eq & (idx < bound)), 1.0, 0.0).astype(BF16)
    sel_ref[...] = sel
    counts = jnp.dot(sel, ind_ref[...], preferred_element_type=F32)
    offs = jnp.dot(counts.astype(BF16), tri_ref[...], preferred_element_type=F32)
    off_ref[...] = offs.astype(I32)


def _strict_upper(n):
    return jnp.asarray(np.triu(np.ones((n, n), np.float32), k=1), dtype=BF16)


def _select(aff_t, cap):
    n_tok = aff_t.shape[1]
    ind = np.zeros((n_tok, LANES), np.float32)
    ind[np.arange(n_tok), np.arange(n_tok) // MOE_TB] = 1.0
    return pl.pallas_call(
        functools.partial(_select_kernel, cap=cap),
        out_shape=[jax.ShapeDtypeStruct((N_EXPERTS, n_tok), BF16),
                   jax.ShapeDtypeStruct((N_EXPERTS, LANES), I32)],
        compiler_params=pltpu.CompilerParams(vmem_limit_bytes=VMEM_LIMIT),
        name="select",
    )(aff_t, jnp.asarray(ind, dtype=BF16), _strict_upper(LANES))


def _pack_rows(y):
    half = y.shape[1] // 2
    yb = y.astype(BF16).astype(F32)
    hi = pltpu.bitcast(yb[:, :half], U32)
    lo = pltpu.bitcast(yb[:, half:], U32)
    return hi | lax.shift_right_logical(lo, jnp.uint32(16))


def _unpack_rows(w):
    hi = pltpu.bitcast(w & jnp.uint32(0xFFFF0000), F32).astype(BF16)
    lo = pltpu.bitcast(lax.shift_left(w, jnp.uint32(16)), F32).astype(BF16)
    return jnp.concatenate([hi, lo], axis=1)


SUBLANES = 8
NOT_SELECTED = -1e6


class _BlockLayout:
    def __init__(self, off_ref, b):
        self.start = [off_ref[e, b] for e in range(N_EXPERTS)]
        self.end = [off_ref[e, b + 1] for e in range(N_EXPERTS)]
        self.base = [lax.shift_left(lax.shift_right_logical(s, 3), 3) for s in self.start]
        self.shift = [s - a for s, a in zip(self.start, self.base)]
        span = [en - a for en, a in zip(self.end, self.base)]
        self.n_chunks = functools.reduce(jnp.maximum, [lax.shift_right_logical(sp, 6) for sp in span]) + 1
        tail = [lax.shift_left(lax.shift_right_logical(sp, 3), 3) for sp in span]
        self.tail_chunk = [lax.shift_right_logical(t, 6) for t in tail]
        self.tail_row = [pl.multiple_of(t & (MOE_W - 1), SUBLANES) for t in tail]

    def window_row(self, e, chunk):
        return pl.multiple_of(self.base[e] + chunk * MOE_W, SUBLANES)


def _slot_onehots(pos, weight, chunk):
    n_tok = pos.shape[1]
    slot = (lax.broadcasted_iota(I32, (MOE_W, n_tok), 0) + chunk * MOE_W).astype(F32)
    parts = []
    for e in range(N_EXPERTS):
        hit = pos[e:e + 1, :] == slot
        val = 1.0 if weight is None else weight[e:e + 1, :]
        parts.append(jnp.where(hit, val, 0.0).astype(BF16))
    return jnp.concatenate(parts, axis=0)


def _window_positions(sel_ref, tri_ref, layout):
    sel = sel_ref[...]
    rank = jnp.dot(sel, tri_ref[...], preferred_element_type=F32)
    shift = jnp.concatenate([jnp.full((1, 1), s, I32) for s in layout.shift], axis=0).astype(F32)
    return jnp.where(sel > 0, rank + shift, NOT_SELECTED)


def _dispatch_kernel(off_ref, sel_ref, h_ref, tri_ref, xe_ref, stage_ref, extra_ref, tail_ref, zero_ref, sem,
                     sem_extra):
    b = pl.program_id(0)
    nb = pl.num_programs(0)
    slot = b % 2
    layout = _BlockLayout(off_ref, b)
    pos = _window_positions(sel_ref, tri_ref, layout)
    h = h_ref[...]

    def rows_for(chunk):
        return _pack_rows(jnp.dot(_slot_onehots(pos, None, chunk), h, preferred_element_type=F32))

    def copy(src, e, row0, s):
        return pltpu.make_async_copy(src.at[e], xe_ref.at[e, pl.ds(row0, MOE_W)], s)

    @pl.when(b == 0)
    def _():
        tail_ref[...] = jnp.zeros(tail_ref.shape, U32)
        zero_ref[...] = jnp.zeros(zero_ref.shape, U32)
        pad = zero_ref.shape[0]
        fills = [pltpu.make_async_copy(zero_ref, xe_ref.at[e, pl.ds(xe_ref.shape[1] - pad, pad)], sem_extra.at[0])
                 for e in range(N_EXPERTS)]
        for f in fills:
            f.start()
        for f in fills:
            f.wait()

    packed = rows_for(0)

    @pl.when(b > 0)
    def _():
        for e in range(N_EXPERTS):
            copy(stage_ref.at[1 - slot], e, 0, sem.at[1 - slot]).wait()

    for e in range(N_EXPERTS):
        r0 = e * MOE_W
        stage_ref[slot, e, :SUBLANES] = packed[r0:r0 + SUBLANES] | tail_ref[e]
        stage_ref[slot, e, SUBLANES:] = packed[r0 + SUBLANES:r0 + MOE_W]
    for e in range(N_EXPERTS):
        copy(stage_ref.at[slot], e, layout.window_row(e, 0), sem.at[slot]).start()
    for e in range(N_EXPERTS):
        @pl.when(layout.tail_chunk[e] == 0)
        def _():
            tail_ref[e] = stage_ref[slot, e, pl.ds(layout.tail_row[e], SUBLANES), :]

    def overflow(chunk, carry):
        more = rows_for(chunk)
        for e in range(N_EXPERTS):
            extra_ref[e] = more[e * MOE_W:(e + 1) * MOE_W]
        for e in range(N_EXPERTS):
            copy(extra_ref, e, layout.window_row(e, chunk), sem_extra.at[0]).start()
        for e in range(N_EXPERTS):
            @pl.when(layout.tail_chunk[e] == chunk)
            def _():
                tail_ref[e] = extra_ref[e, pl.ds(layout.tail_row[e], SUBLANES), :]
        for e in range(N_EXPERTS):
            copy(extra_ref, e, 0, sem_extra.at[0]).wait()
        return carry

    lax.fori_loop(1, layout.n_chunks, overflow, 0)

    @pl.when(b == nb - 1)
    def _():
        for e in range(N_EXPERTS):
            copy(stage_ref.at[slot], e, 0, sem.at[slot]).wait()


def _slot_rows(cap):
    pad = ((SUBLANES + MOE_TB) // MOE_W + 1) * MOE_W
    return cap + -(-pad // FFN_TM) * FFN_TM


def _dispatch(offs, sel, h, cap):
    n_tok = h.shape[0]
    nb = n_tok // MOE_TB
    rows = _slot_rows(cap)
    half = D_MODEL // 2
    grid_spec = pltpu.PrefetchScalarGridSpec(
        num_scalar_prefetch=1,
        grid=(nb,),
        in_specs=[pl.BlockSpec((N_EXPERTS, MOE_TB), lambda b, off: (0, b)),
                  pl.BlockSpec((MOE_TB, D_MODEL), lambda b, off: (b, 0)),
                  pl.BlockSpec((MOE_TB, MOE_TB), lambda b, off: (0, 0))],
        out_specs=pl.BlockSpec(memory_space=pl.ANY),
        scratch_shapes=[pltpu.VMEM((2, N_EXPERTS, MOE_W, half), U32),
                        pltpu.VMEM((N_EXPERTS, MOE_W, half), U32),
                        pltpu.VMEM((N_EXPERTS, SUBLANES, half), U32),
                        pltpu.VMEM((rows - cap, half), U32),
                        pltpu.SemaphoreType.DMA((2,)),
                        pltpu.SemaphoreType.DMA((1,))],
    )
    return pl.pallas_call(
        _dispatch_kernel,
        grid_spec=grid_spec,
        out_shape=jax.ShapeDtypeStruct((N_EXPERTS, rows, half), U32),
        compiler_params=_cparams(("arbitrary",)),
        name="dispatch",
    )(offs, sel, h, _strict_upper(MOE_TB))


def _ffn_kernel(xe_ref, wg_ref, wu_ref, wd_ref, ye_ref, *, n_real):
    j = pl.program_id(1)

    @pl.when(j < n_real)
    def _():
        x = _unpack_rows(xe_ref[...])
        gate = jnp.dot(x, wg_ref[...], preferred_element_type=F32)
        up = jnp.dot(x, wu_ref[...], preferred_element_type=F32)
        hid = (gate * jax.nn.sigmoid(gate) * up).astype(BF16)
        ye_ref[...] = _pack_rows(jnp.dot(hid, wd_ref[...], preferred_element_type=F32))

    @pl.when(j >= n_real)
    def _():
        ye_ref[...] = jnp.zeros(ye_ref.shape, U32)


def _expert_ffn(xe, w_gate, w_up, w_down, cap):
    rows = xe.shape[1]
    half = D_MODEL // 2
    n_real = cap // FFN_TM
    n_tiles = rows // FFN_TM
    wspec = lambda shape: pl.BlockSpec((None,) + shape, lambda e, j: (e, 0, 0))
    return pl.pallas_call(
        functools.partial(_ffn_kernel, n_real=n_real),
        grid=(N_EXPERTS, n_tiles),
        in_specs=[pl.BlockSpec((None, FFN_TM, half), lambda e, j: (e, j, 0)),
                  wspec((D_MODEL, EXPERT_FF)), wspec((D_MODEL, EXPERT_FF)), wspec((EXPERT_FF, D_MODEL))],
        out_specs=pl.BlockSpec((None, FFN_TM, half), lambda e, j: (e, j, 0)),
        out_shape=jax.ShapeDtypeStruct((N_EXPERTS, rows, half), U32),
        compiler_params=_cparams(("parallel", "arbitrary")),
        name="expert_ffn",
    )(xe, w_gate, w_up, w_down)


def _combine_kernel(off_ref, sel_ref, aff_ref, tri_ref, x_ref, ye_ref, out_ref, win_ref, extra_ref, sem, sem_extra,
                    *, rows):
    b = pl.program_id(0)
    nb = pl.num_programs(0)
    slot = b % 2

    def copy(dst, e, row0, s):
        return pltpu.make_async_copy(ye_ref.at[e, pl.ds(row0, MOE_W)], dst.at[e], s)

    def fetch(blk, dst_slot):
        ahead = _BlockLayout(off_ref, blk)
        for e in range(N_EXPERTS):
            copy(win_ref.at[dst_slot], e, ahead.window_row(e, 0), sem.at[dst_slot]).start()

    @pl.when(b == 0)
    def _():
        fetch(0, 0)

    @pl.when(b + 1 < nb)
    def _():
        fetch(b + 1, 1 - slot)

    layout = _BlockLayout(off_ref, b)
    pos = _window_positions(sel_ref, tri_ref, layout)
    aff = aff_ref[...]
    tn = (((0,), (0,)), ((), ()))

    def contribution(chunk, window):
        gates = _slot_onehots(pos, aff, chunk)
        vals = _unpack_rows(window.reshape(N_EXPERTS * MOE_W, window.shape[-1]))
        return lax.dot_general(gates, vals, tn, preferred_element_type=F32)

    for e in range(N_EXPERTS):
        copy(win_ref.at[slot], e, 0, sem.at[slot]).wait()
    out_ref[...] = x_ref[...] + contribution(0, win_ref[slot])

    def overflow(chunk, carry):
        for e in range(N_EXPERTS):
            row0 = pl.multiple_of(jnp.minimum(layout.window_row(e, chunk), rows - MOE_W), SUBLANES)
            copy(extra_ref, e, row0, sem_extra.at[0]).start()
        for e in range(N_EXPERTS):
            copy(extra_ref, e, 0, sem_extra.at[0]).wait()
        out_ref[...] += contribution(chunk, extra_ref[...])
        return carry

    lax.fori_loop(1, layout.n_chunks, overflow, 0)


def _combine(offs, sel, aff_t, x, ye):
    n_tok = x.shape[0]
    nb = n_tok // MOE_TB
    rows = ye.shape[1]
    half = D_MODEL // 2
    grid_spec = pltpu.PrefetchScalarGridSpec(
        num_scalar_prefetch=1,
        grid=(nb,),
        in_specs=[pl.BlockSpec((N_EXPERTS, MOE_TB), lambda b, off: (0, b)),
                  pl.BlockSpec((N_EXPERTS, MOE_TB), lambda b, off: (0, b)),
                  pl.BlockSpec((MOE_TB, MOE_TB), lambda b, off: (0, 0)),
                  pl.BlockSpec((MOE_TB, D_MODEL), lambda b, off: (b, 0)),
                  pl.BlockSpec(memory_space=pl.ANY)],
        out_specs=pl.BlockSpec((MOE_TB, D_MODEL), lambda b, off: (b, 0)),
        scratch_shapes=[pltpu.VMEM((2, N_EXPERTS, MOE_W, half), U32),
                        pltpu.VMEM((N_EXPERTS, MOE_W, half), U32),
                        pltpu.SemaphoreType.DMA((2,)),
                        pltpu.SemaphoreType.DMA((1,))],
    )
    return pl.pallas_call(
        functools.partial(_combine_kernel, rows=rows),
        grid_spec=grid_spec,
        out_shape=jax.ShapeDtypeStruct((n_tok, D_MODEL), F32),
        compiler_params=_cparams(("arbitrary",)),
        name="combine",
    )(offs, sel, aff_t, _strict_upper(MOE_TB), x, ye)


def _moe(x, h, aff_t, w_gate, w_up, w_down):
    cap = CAPACITY_FACTOR * x.shape[0] // N_EXPERTS
    sel, offs = _select(aff_t, cap)
    xe = _dispatch(offs, sel, h, cap)
    ye = _expert_ffn(xe, w_gate, w_up, w_down, cap)
    return _combine(offs, sel, aff_t, x, ye)


def _row(v):
    return v.reshape(1, -1).astype(F32)


def _router_split(w_router):
    wt = w_router.T.astype(F32)
    hi = wt.astype(BF16)
    lo = (wt - hi.astype(F32)).astype(BF16)
    return hi, lo


def _layer_ab(x, batch, seq, tabs, norm_g, w_in, qn_a, kn_a, qn_b, kn_b, w_out, norm_ffn, w_router):
    tab_a, tab_b = tabs
    scale = HEAD_DIM ** -0.5 * LOG2E
    perm_b = _pair_layout_perm(B_KV_HEADS, B_Q_HEADS // B_KV_HEADS)
    a3 = 3 * A_WIDTH
    bq = B_Q_HEADS * HEAD_DIM
    w_cols = np.concatenate([np.arange(a3), a3 + perm_b, np.arange(a3 + bq, w_in.shape[1])])
    w = w_in[:, w_cols].astype(BF16)
    gains = jnp.concatenate([
        jnp.repeat(qn_a, A_HEADS, axis=0).reshape(-1) * scale,
        jnp.repeat(kn_a, A_HEADS, axis=0).reshape(-1),
        jnp.ones((A_WIDTH,), F32),
        jnp.tile(qn_b, B_Q_HEADS) * scale,
        jnp.tile(kn_b, B_KV_HEADS),
        jnp.ones((B_KV_HEADS * HEAD_DIM,), F32)])
    n_a = A_GROUPS
    assert A_HEADS * HEAD_DIM == MXU_COLS
    plan = ([("qk", g, 0, 0, ROT_DIM // 2) for g in range(n_a)]
            + [("qk", n_a + g, 0, 0, ROT_DIM // 2) for g in range(n_a)]
            + [("v", 2 * n_a + g, 0, 0, 0) for g in range(n_a)]
            + [("qk", 3 * n_a, c * MXU_COLS, 1, HEAD_DIM // 4) for c in range(bq // MXU_COLS)]
            + [("qk", 3 * n_a + 1, 0, 1, HEAD_DIM // 4), ("v_ones", 3 * n_a + 2, 0, 0, 0)])
    outs = _projection(
        x, _row(norm_g), w, _row(gains), (tab_a, tab_b), plan,
        (MXU_COLS,) * (3 * n_a) + (bq, B_KV_HEADS * HEAD_DIM, 2 * B_KV_HEADS * HEAD_DIM), seq)
    qa, ka, va = outs[:n_a], outs[n_a:2 * n_a], outs[2 * n_a:3 * n_a]
    qb, kb, vb = outs[3 * n_a:]
    o_parts, lse_parts = [], []
    for gi, (window, dil) in enumerate(A_PATTERNS):
        o, lse = _attention(qa[gi], ka[gi], va[gi], batch=batch, n_keys=seq // dil, residues=dil, group=1,
                            n_pairs=A_HEADS // 2, band_r=window // (2 * dil), want_lse=True)
        o_parts.append(o)
        lse_parts.append(lse)
    (ob,) = _attention(qb, kb, vb, batch=batch, n_keys=seq, residues=1,
                       group=B_Q_HEADS // B_KV_HEADS, n_pairs=B_KV_HEADS // 2, band_r=None)
    a_out = A_HEADS * HEAD_DIM
    w_o = w_out[np.concatenate([np.arange(a_out), a_out + perm_b])].astype(BF16)
    wr_hi, wr_lo = _router_split(w_router)
    return _out_projection(o_parts, lse_parts, ob, x, w_o, _row(norm_ffn), wr_hi, wr_lo)


def _layer_c(x, batch, seq, tabs, norm_g, w_in, qn, kn, sink, w_out, norm_ffn, w_router):
    tab_a, _ = tabs
    scale = HEAD_DIM ** -0.5 * LOG2E
    group = C_Q_HEADS // C_KV_HEADS
    perm = _pair_layout_perm(C_KV_HEADS, group)
    cq = C_Q_HEADS * HEAD_DIM
    w_cols = np.concatenate([perm, np.arange(cq, w_in.shape[1])])
    w = w_in[:, w_cols].astype(BF16)
    gains = jnp.concatenate([jnp.tile(qn, C_Q_HEADS) * scale, jnp.tile(kn, C_KV_HEADS),
                             jnp.ones((C_KV_HEADS * HEAD_DIM,), F32)])
    plan = ([("qk", 0, c * MXU_COLS, 0, ROT_DIM // 2) for c in range(cq // MXU_COLS)]
            + [("qk", 1, 0, 0, ROT_DIM // 2), ("v", 2, 0, 0, 0)])
    q, k, v = _projection(x, _row(norm_g), w, _row(gains), (tab_a,), plan,
                          (cq, C_KV_HEADS * HEAD_DIM, C_KV_HEADS * HEAD_DIM), seq)
    (o,) = _attention(q, k, v, batch=batch, n_keys=seq, residues=1,
                      group=group, n_pairs=C_KV_HEADS // 2, band_r=C_RADIUS, sink=sink.astype(F32))
    wr_hi, wr_lo = _router_split(w_router)
    return _out_projection([], [], o, x, w_out[perm].astype(BF16), _row(norm_ffn), wr_hi, wr_lo)


def _encode(x, params):
    (norm_mix, norm_ffn, w_in_ab, qn_a, kn_a, qn_b, kn_b, w_out_ab, w_in_c, qn_c, kn_c, sink_c, w_out_c,
     w_router, w_gate, w_up, w_down) = params
    batch, seq, _ = x.shape
    tabs = _rope_tables(seq)
    xt = x.reshape(batch * seq, D_MODEL)
    for layer in range(norm_mix.shape[0]):
        j = layer // 2
        if layer % 2 == 0:
            xt, h, aff_t = _layer_ab(xt, batch, seq, tabs, norm_mix[layer], w_in_ab[j], qn_a[j], kn_a[j],
                                     qn_b[j], kn_b[j], w_out_ab[j], norm_ffn[layer], w_router[layer])
        else:
            xt, h, aff_t = _layer_c(xt, batch, seq, tabs, norm_mix[layer], w_in_c[j], qn_c[j], kn_c[j],
                                    sink_c[j], w_out_c[j], norm_ffn[layer], w_router[layer])
        xt = _moe(xt, h, aff_t, w_gate[layer].astype(BF16), w_up[layer].astype(BF16),
                  w_down[layer].astype(BF16))
    return xt.reshape(batch, seq, D_MODEL)


def kernel(x_prompt, x_sample, norm_mix, norm_ffn, w_in_ab, qn_a, kn_a, qn_b, kn_b, w_out_ab, w_in_c, qn_c,
           kn_c, sink_c, w_out_c, w_router, w_gate, w_up, w_down):
    params = (norm_mix, norm_ffn, w_in_ab, qn_a, kn_a, qn_b, kn_b, w_out_ab, w_in_c, qn_c, kn_c, sink_c,
              w_out_c, w_router, w_gate, w_up, w_down)
    return _encode(x_prompt, params), _encode(x_sample, params)
```

```python
import functools
import math

import jax
import jax.numpy as jnp
import numpy as np
from jax import lax
from jax.experimental import pallas as pl
from jax.experimental.pallas import tpu as pltpu

F32 = jnp.float32
BF16 = jnp.bfloat16
I32 = jnp.int32

D_MODEL = 1024
HEAD_DIM = 64
LANES = 128
MXU_COLS = 256
GRID_W = 64
ROT_DIM = HEAD_DIM // 4
ROPE_THETA = 500000.0
AXIAL_THETA = 10000.0
A_PATTERNS = ((128, 1), (512, 4), (2048, 16))
A_HEADS = 4
A_GROUPS = len(A_PATTERNS)
A_WIDTH = A_GROUPS * A_HEADS * HEAD_DIM
B_Q_HEADS = 12
B_KV_HEADS = 4
C_Q_HEADS = 16
C_KV_HEADS = 4
C_RADIUS = 128
N_EXPERTS = 16
EXPERT_FF = 1024
CAPACITY_FACTOR = 2
NEG_INF = -1e30
EPS = 1e-6

VMEM_LIMIT = 56 * 1024 * 1024

PROJ_TM = 512
ATTN_TQ = 256
ACC_ROWS = LANES + 16
DENSE_TK = 512
MOE_TB = 256
MOE_W = 64
FFN_TM = 256


def _cparams(sem):
    return pltpu.CompilerParams(dimension_semantics=sem, vmem_limit_bytes=VMEM_LIMIT)


def _head_block_diag():
    idx = np.arange(MXU_COLS) // HEAD_DIM
    return jnp.asarray((idx[:, None] == idx[None, :]).astype(np.float32), dtype=BF16)


def _rope_tables(max_len):
    pos = jnp.arange(max_len, dtype=F32)
    j = np.arange(LANES) % HEAD_DIM

    def angles(p, dim, theta):
        exps = jnp.arange(0, dim, 2, dtype=F32) / dim
        inv = jnp.power(jnp.float32(theta), -exps)
        return p[:, None] * inv[None, :]

    half = ROT_DIM // 2
    ang = angles(pos, ROT_DIM, ROPE_THETA)
    cos, sin = jnp.cos(ang), jnp.sin(ang)
    fa = np.where(j < half, j, np.where(j < ROT_DIM, j - half, 0))
    cos_l, sin_l = cos[:, fa], sin[:, fa]
    lo = jnp.asarray(j < half)[None, :]
    hi = jnp.asarray((j >= half) & (j < ROT_DIM))[None, :]
    tab_a = jnp.stack([jnp.where(lo | hi, cos_l, 1.0),
                       jnp.where(lo, -sin_l, 0.0),
                       jnp.where(hi, sin_l, 0.0)])
    hb = HEAD_DIM // 2
    qb = hb // 2
    t = jnp.arange(max_len)
    ang_r = angles((t // GRID_W).astype(F32), hb, AXIAL_THETA)
    ang_c = angles((t % GRID_W).astype(F32), hb, AXIAL_THETA)
    fb = j % qb
    is_col = jnp.asarray(j >= hb)[None, :]
    ang_l = jnp.where(is_col, ang_c[:, fb], ang_r[:, fb])
    cos_b, sin_b = jnp.cos(ang_l), jnp.sin(ang_l)
    first = jnp.asarray((j % hb) < qb)[None, :]
    tab_b = jnp.stack([cos_b, jnp.where(first, -sin_b, 0.0), jnp.where(first, 0.0, sin_b)])
    return tab_a.astype(F32), tab_b.astype(F32)


def _pair_layout_perm(n_kv, group):
    cols = []
    for p in range(n_kv // 2):
        for g in range(group):
            for par in range(2):
                h = (2 * p + par) * group + g
                cols.extend(range(h * HEAD_DIM, (h + 1) * HEAD_DIM))
    return np.asarray(cols, dtype=np.int32)


def _proj_kernel(x_ref, g_ref, w_ref, s_ref, gain_ref, *rest, plan, n_tabs):
    tab_refs = rest[:n_tabs]
    out_refs = rest[n_tabs:]
    x = x_ref[...]
    ms = jnp.mean(x * x, axis=1, keepdims=True)
    xn = ((x * lax.rsqrt(ms + EPS)) * g_ref[...]).astype(BF16)
    for c, (kind, out_i, out_col, tab_i, shift) in enumerate(plan):
        acc = jnp.dot(xn, w_ref[:, c * MXU_COLS:(c + 1) * MXU_COLS], preferred_element_type=F32)
        o_ref = out_refs[out_i]
        if kind == "v":
            o_ref[...] = acc.T.astype(BF16)
            continue
        ss = jnp.dot((acc * acc).astype(BF16), s_ref[...], preferred_element_type=F32)
        y = (acc * lax.rsqrt(ss * (1.0 / HEAD_DIM) + EPS)) * gain_ref[:, c * MXU_COLS:(c + 1) * MXU_COLS]
        tab = tab_refs[tab_i]
        t0 = jnp.concatenate([tab[0], tab[0]], axis=1)
        t1 = jnp.concatenate([tab[1], tab[1]], axis=1)
        t2 = jnp.concatenate([tab[2], tab[2]], axis=1)
        y = y * t0 + pltpu.roll(y, MXU_COLS - shift, 1) * t1 + pltpu.roll(y, shift, 1) * t2
        o_ref[:, out_col:out_col + MXU_COLS] = y.astype(BF16)


def _projection(x, norm_g, w, gains, tabs, plan, out_widths, seq_len):
    t_tokens = x.shape[0]
    tm = PROJ_TM
    n_in = w.shape[1]
    blocks_per_seq = seq_len // tm
    in_specs = [
        pl.BlockSpec((tm, D_MODEL), lambda i: (i, 0)),
        pl.BlockSpec((1, D_MODEL), lambda i: (0, 0)),
        pl.BlockSpec((D_MODEL, n_in), lambda i: (0, 0)),
        pl.BlockSpec((MXU_COLS, MXU_COLS), lambda i: (0, 0)),
        pl.BlockSpec((1, n_in), lambda i: (0, 0)),
    ] + [pl.BlockSpec((3, tm, LANES), lambda i: (0, i % blocks_per_seq, 0)) for _ in tabs]
    out_specs, out_shape = [], []
    for wd in out_widths:
        if wd is None:
            out_specs.append(pl.BlockSpec((None, MXU_COLS, tm),
                                          lambda i: (i // blocks_per_seq, 0, i % blocks_per_seq)))
            out_shape.append(jax.ShapeDtypeStruct((t_tokens // seq_len, MXU_COLS, seq_len), BF16))
        else:
            out_specs.append(pl.BlockSpec((tm, wd), lambda i: (i, 0)))
            out_shape.append(jax.ShapeDtypeStruct((t_tokens, wd), BF16))
    return pl.pallas_call(
        functools.partial(_proj_kernel, plan=tuple(plan), n_tabs=len(tabs)),
        grid=(t_tokens // tm,),
        in_specs=in_specs, out_specs=out_specs, out_shape=out_shape,
        compiler_params=_cparams(("parallel",)),
        name="projection",
    )(x, norm_g, w, _head_block_diag(), gains, *tabs)


LOG2E = math.log2(math.e)
LN2 = math.log(2.0)
_NT = (((1,), (1,)), ((), ()))


def _attn_kernel(*refs, group, n_pairs, tq, tk, n_keys, band_r, has_sink, want_lse):
    refs = list(refs)
    sink_ref = refs.pop(0) if has_sink else None
    q_ref, k_ref, v_ref, o_ref = refs[:4]
    refs = refs[4:]
    lse_ref = refs.pop(0) if want_lse else None
    scratch = refs
    m_cols = group * tq
    n_chain = 2 * n_pairs
    i = pl.program_id(2)
    lane = lax.broadcasted_iota(I32, (1, LANES), 1)
    low = lane < HEAD_DIM
    half_mask = [jnp.where(low, 1.0, 0.0).astype(BF16), jnp.where(low, 0.0, 1.0).astype(BF16)]

    q_masked = []
    for p in range(n_pairs):
        q_p = jnp.concatenate(
            [q_ref[:, (p * group + g) * LANES:(p * group + g + 1) * LANES] for g in range(group)], axis=0)
        q_masked += [q_p * half_mask[par] for par in range(2)]

    sub = lax.broadcasted_iota(I32, (ACC_ROWS, m_cols), 0)
    if has_sink:
        acc0 = jnp.where(sub >= LANES, 1.0, 0.0).astype(F32)
        m0 = []
        for c in range(n_chain):
            m0.append(jnp.concatenate(
                [jnp.full((1, tq), sink_ref[c * group + g] * LOG2E, F32) for g in range(group)], axis=1))
    else:
        acc0 = jnp.zeros((ACC_ROWS, m_cols), F32)
        m0 = [jnp.full((1, m_cols), NEG_INF, F32) for _ in range(n_chain)]

    def values_t(p, k0, width):
        ones = jnp.ones((ACC_ROWS - LANES, width), BF16)
        return jnp.concatenate([v_ref[p * LANES:(p + 1) * LANES, pl.ds(k0, width)], ones], axis=0)

    def softmax_pv(s_t, v_t, m_prev, acc_prev):
        m_new = jnp.maximum(m_prev, jnp.max(s_t, axis=0, keepdims=True))
        alpha = jnp.exp2(m_prev - m_new)
        p_t = jnp.exp2(s_t - m_new).astype(BF16)
        return m_new, alpha * acc_prev + jnp.dot(v_t, p_t, preferred_element_type=F32)

    if band_r is None:
        qm_ref, acc_ref, s_even, s_odd = scratch
        for c in range(n_chain):
            qm_ref[c] = q_masked[c]
            acc_ref[c] = acc0

        def scores(kb, s_ref):
            k0 = pl.multiple_of(kb * tk, tk)
            for p in range(n_pairs):
                kt = k_ref[pl.ds(k0, tk), p * LANES:(p + 1) * LANES]
                for par in range(2):
                    c = 2 * p + par
                    s_ref[c] = lax.dot_general(kt, qm_ref[c], _NT, preferred_element_type=F32)

        def consume(kb, s_ref, m_prev):
            k0 = pl.multiple_of(kb * tk, tk)
            m_next = []
            for p in range(n_pairs):
                v_t = values_t(p, k0, tk)
                for par in range(2):
                    c = 2 * p + par
                    m_new, acc_ref[c] = softmax_pv(s_ref[c], v_t, m_prev[c], acc_ref[c])
                    m_next.append(m_new)
            return m_next

        n_blocks = n_keys // tk
        scores(0, s_even)

        def body(j, carry):
            m = list(carry)
            scores(2 * j + 1, s_odd)
            m = consume(2 * j, s_even, m)
            scores(2 * j + 2, s_even)
            m = consume(2 * j + 1, s_odd, m)
            return tuple(m)

        m_fin = list(lax.fori_loop(0, n_blocks // 2 - 1, body, tuple(m0)))
        scores(n_blocks - 1, s_odd)
        m_fin = consume(n_blocks - 2, s_even, m_fin)
        m_fin = consume(n_blocks - 1, s_odd, m_fin)
        acc_fin = [acc_ref[c] for c in range(n_chain)]
    else:
        halo = -(-band_r // LANES) * LANES
        width = min(tq + 2 * halo, n_keys)
        ws = pl.multiple_of(jnp.clip(i * tq - halo, 0, n_keys - width), LANES)
        kpos = ws + lax.broadcasted_iota(I32, (width, 1), 0)
        col = lax.broadcasted_iota(I32, (1, tq), 1)
        qpos = i * tq + jnp.concatenate([col] * group, axis=1)
        mask = jnp.abs(kpos - qpos) <= band_r
        s_all = []
        for p in range(n_pairs):
            kt = k_ref[pl.ds(ws, width), p * LANES:(p + 1) * LANES]
            s_all += [lax.dot_general(kt, q_masked[2 * p + par], _NT, preferred_element_type=F32)
                      for par in range(2)]
        m_fin, acc_fin = [], []
        for c in range(n_chain):
            m_new, acc_new = softmax_pv(jnp.where(mask, s_all[c], NEG_INF), values_t(c // 2, ws, width),
                                        m0[c], acc0)
            m_fin.append(m_new)
            acc_fin.append(acc_new)

    top = lax.broadcasted_iota(I32, (LANES, tq), 0) < HEAD_DIM
    for p in range(n_pairs):
        even, odd = acc_fin[2 * p], acc_fin[2 * p + 1]
        for g in range(group):
            cols = slice(g * tq, (g + 1) * tq)
            num = jnp.where(top, even[:LANES, cols], odd[:LANES, cols])
            den = jnp.where(top, even[LANES:LANES + 1, cols], odd[LANES:LANES + 1, cols])
            out_cols = slice((p * group + g) * LANES, (p * group + g + 1) * LANES)
            o_ref[:, out_cols] = (num / den).T.astype(o_ref.dtype)
            if want_lse:
                m_sel = jnp.where(top, m_fin[2 * p][:, cols], m_fin[2 * p + 1][:, cols])
                lse_ref[:, out_cols] = (m_sel * LN2 + jnp.log(den)).T


def _attention(q, k, v_t, *, batch, n_keys, residues, group, n_pairs, band_r, sink=None, want_lse=False):
    dense = band_r is None
    tq = ATTN_TQ
    wq = n_pairs * group * LANES
    wk = n_pairs * LANES
    q3 = q.reshape(batch, n_keys, residues * wq)
    k3 = k.reshape(batch, n_keys, residues * wk)
    v4 = v_t.reshape(batch, wk, n_keys, residues).transpose(0, 3, 1, 2)

    in_specs = []
    args = []
    if sink is not None:
        in_specs.append(pl.BlockSpec(memory_space=pltpu.SMEM))
        args.append(sink)
    in_specs += [
        pl.BlockSpec((None, tq, wq), lambda b, r, i: (b, i, r)),
        pl.BlockSpec((None, n_keys, wk), lambda b, r, i: (b, 0, r)),
        pl.BlockSpec((None, None, wk, n_keys), lambda b, r, i: (b, r, 0, 0)),
    ]
    args += [q3, k3, v4]
    out_specs = [pl.BlockSpec((None, tq, wq), lambda b, r, i: (b, i, r))]
    out_shape = [jax.ShapeDtypeStruct((batch, n_keys, residues * wq), BF16)]
    if want_lse:
        out_specs.append(pl.BlockSpec((None, tq, wq), lambda b, r, i: (b, i, r)))
        out_shape.append(jax.ShapeDtypeStruct((batch, n_keys, residues * wq), F32))
    m_cols = group * tq
    scratch = []
    if dense:
        scratch = [pltpu.VMEM((2 * n_pairs, m_cols, LANES), BF16),
                   pltpu.VMEM((2 * n_pairs, ACC_ROWS, m_cols), F32)]
        scratch += [pltpu.VMEM((2 * n_pairs, DENSE_TK, m_cols), F32) for _ in range(2)]
    outs = pl.pallas_call(
        functools.partial(_attn_kernel, group=group, n_pairs=n_pairs, tq=tq, tk=DENSE_TK, n_keys=n_keys,
                          band_r=band_r, has_sink=sink is not None, want_lse=want_lse),
        grid=(batch, residues, n_keys // tq),
        in_specs=in_specs, out_specs=out_specs, out_shape=out_shape,
        scratch_shapes=scratch,
        compiler_params=_cparams(("parallel", "parallel", "arbitrary")),
        name="attention_dense" if dense else "attention_band",
    )(*args)
    return [o.reshape(batch * n_keys * residues, wq) for o in outs]


def _outproj_kernel(*refs, n_merge):
    refs = list(refs)
    if n_merge:
        o_parts = refs[:n_merge]
        lse_parts = refs[n_merge:2 * n_merge]
        refs = refs[2 * n_merge:]
    o_rest, x_ref, w_ref, g_ref, wrh_ref, wrl_ref, xo_ref, h_ref, aff_ref = refs
    acc = x_ref[...]
    k0 = 0
    if n_merge:
        lses = [r[...] for r in lse_parts]
        m = functools.reduce(jnp.maximum, lses)
        ws = [jnp.exp(l - m) for l in lses]
        num = sum(wgt * r[...].astype(F32) for wgt, r in zip(ws, o_parts))
        oa = (num / sum(ws)).astype(BF16)
        k0 = oa.shape[1]
        acc = acc + jnp.dot(oa, w_ref[:k0, :], preferred_element_type=F32)
    acc = acc + jnp.dot(o_rest[...], w_ref[k0:, :], preferred_element_type=F32)
    xo_ref[...] = acc
    ms = jnp.mean(acc * acc, axis=1, keepdims=True)
    h = (acc * lax.rsqrt(ms + EPS)) * g_ref[...]
    h_hi = h.astype(BF16)
    h_ref[...] = h_hi
    h_lo = (h - h_hi.astype(F32)).astype(BF16)
    nt = (((1,), (1,)), ((), ()))
    logits = (lax.dot_general(wrh_ref[...], h_hi, nt, preferred_element_type=F32)
              + lax.dot_general(wrl_ref[...], h_hi, nt, preferred_element_type=F32)
              + lax.dot_general(wrh_ref[...], h_lo, nt, preferred_element_type=F32))
    mx = jnp.max(logits, axis=0, keepdims=True)
    e = jnp.exp(logits - mx)
    aff_ref[...] = e / jnp.sum(e, axis=0, keepdims=True)


def _out_projection(o_merge, lse_merge, o_rest, x, w_out, norm_g, wr_hi, wr_lo):
    t_tokens = x.shape[0]
    tm = PROJ_TM
    n_merge = len(o_merge)
    row = lambda i: (i, 0)
    fixed = lambda i: (0, 0)
    in_specs = ([pl.BlockSpec((tm, o.shape[1]), row) for o in o_merge]
                + [pl.BlockSpec((tm, l.shape[1]), row) for l in lse_merge]
                + [pl.BlockSpec((tm, o_rest.shape[1]), row),
                   pl.BlockSpec((tm, D_MODEL), row),
                   pl.BlockSpec(w_out.shape, fixed),
                   pl.BlockSpec((1, D_MODEL), fixed),
                   pl.BlockSpec(wr_hi.shape, fixed),
                   pl.BlockSpec(wr_lo.shape, fixed)])
    out_specs = [pl.BlockSpec((tm, D_MODEL), row),
                 pl.BlockSpec((tm, D_MODEL), row),
                 pl.BlockSpec((N_EXPERTS, tm), lambda i: (0, i))]
    out_shape = [jax.ShapeDtypeStruct((t_tokens, D_MODEL), F32),
                 jax.ShapeDtypeStruct((t_tokens, D_MODEL), BF16),
                 jax.ShapeDtypeStruct((N_EXPERTS, t_tokens), F32)]
    return pl.pallas_call(
        functools.partial(_outproj_kernel, n_merge=n_merge),
        grid=(t_tokens // tm,),
        in_specs=in_specs, out_specs=out_specs, out_shape=out_shape,
        compiler_params=_cparams(("parallel",)),
        name="out_projection",
    )(*o_merge, *lse_merge, o_rest, x, w_out, norm_g, wr_hi, wr_lo)


def _select_kernel(aff_ref, ind_ref, tri_ref, sel_ref, off_ref, *, cap):
    aff = aff_ref[...]
    n_tok = aff.shape[1]
    bits = pltpu.bitcast(aff, I32)

    def count(pred):
        return jnp.sum(jnp.where(pred, 1.0, 0.0), axis=1, keepdims=True)

    def value_step(j, thr):
        cand = thr | lax.shift_left(jnp.int32(1), 30 - j)
        return jnp.where(count(bits >= cand) >= cap, cand, thr)

    thr = lax.fori_loop(0, 31, value_step, jnp.zeros((N_EXPERTS, 1), I32))
    gt = bits > thr
    eq = bits == thr
    need = cap - count(gt)
    idx = lax.broadcasted_iota(I32, aff.shape, 1)
    idx_bits = int(math.log2(n_tok))

    def index_step(j, bound):
        cand = bound | lax.shift_left(jnp.int32(1), idx_bits - j)
        return jnp.where(count(eq & (idx < cand)) <= need, cand, bound)

    bound = lax.fori_loop(0, idx_bits + 1, index_step, jnp.zeros((N_EXPERTS, 1), I32))
    sel = jnp.where(gt | (eq & (idx < bound)), 1.0, 0.0).astype(BF16)
    sel_ref[...] = sel
    counts = jnp.dot(sel, ind_ref[...], preferred_element_type=F32)
    offs = jnp.dot(counts.astype(BF16), tri_ref[...], preferred_element_type=F32)
    off_ref[...] = offs.astype(I32)


def _strict_upper(n):
    return jnp.asarray(np.triu(np.ones((n, n), np.float32), k=1), dtype=BF16)


def _select(aff_t, cap):
    n_tok = aff_t.shape[1]
    ind = np.zeros((n_tok, LANES), np.float32)
    ind[np.arange(n_tok), np.arange(n_tok) // MOE_TB] = 1.0
    return pl.pallas_call(
        functools.partial(_select_kernel, cap=cap),
        out_shape=[jax.ShapeDtypeStruct((N_EXPERTS, n_tok), BF16),
                   jax.ShapeDtypeStruct((N_EXPERTS, LANES), I32)],
        compiler_params=pltpu.CompilerParams(vmem_limit_bytes=VMEM_LIMIT),
        name="select",
    )(aff_t, jnp.asarray(ind, dtype=BF16), _strict_upper(LANES))


SLOT_ALIGN = 16
ALIGN_BITS = 4
CHUNK_BITS = 6
NOT_SELECTED = -1e6


class _BlockLayout:
    def __init__(self, off_ref, b):
        self.start = [off_ref[e, b] for e in range(N_EXPERTS)]
        self.end = [off_ref[e, b + 1] for e in range(N_EXPERTS)]
        def floor_tile(v):
            return lax.shift_left(lax.shift_right_logical(v, ALIGN_BITS), ALIGN_BITS)

        self.base = [floor_tile(s) for s in self.start]
        self.shift = [s - a for s, a in zip(self.start, self.base)]
        span = [en - a for en, a in zip(self.end, self.base)]
        self.n_chunks = functools.reduce(
            jnp.maximum, [lax.shift_right_logical(sp, CHUNK_BITS) for sp in span]) + 1
        tail = [floor_tile(sp) for sp in span]
        self.tail_chunk = [lax.shift_right_logical(t, CHUNK_BITS) for t in tail]
        self.tail_row = [pl.multiple_of(t & (MOE_W - 1), SLOT_ALIGN) for t in tail]

    def window_row(self, e, chunk):
        return pl.multiple_of(self.base[e] + chunk * MOE_W, SLOT_ALIGN)


def _slot_onehots(pos, weight, chunk):
    n_tok = pos.shape[1]
    slot = (lax.broadcasted_iota(I32, (MOE_W, n_tok), 0) + chunk * MOE_W).astype(F32)
    parts = []
    for e in range(N_EXPERTS):
        hit = pos[e:e + 1, :] == slot
        val = 1.0 if weight is None else weight[e:e + 1, :]
        parts.append(jnp.where(hit, val, 0.0).astype(BF16))
    return jnp.concatenate(parts, axis=0)


def _window_positions(sel_ref, tri_ref, layout):
    sel = sel_ref[...]
    rank = jnp.dot(sel, tri_ref[...], preferred_element_type=F32)
    shift = jnp.concatenate([jnp.full((1, 1), s, I32) for s in layout.shift], axis=0).astype(F32)
    return jnp.where(sel > 0, rank + shift, NOT_SELECTED)


def _dispatch_kernel(off_ref, sel_ref, h_ref, tri_ref, xe_ref, stage_ref, extra_ref, tail_ref, zero_ref, sem,
                     sem_extra):
    b = pl.program_id(0)
    nb = pl.num_programs(0)
    slot = b % 2
    layout = _BlockLayout(off_ref, b)
    pos = _window_positions(sel_ref, tri_ref, layout)
    h = h_ref[...]

    def rows_for(chunk):
        return jnp.dot(_slot_onehots(pos, None, chunk), h, preferred_element_type=F32)

    def copy(src, e, row0, s):
        return pltpu.make_async_copy(src.at[e], xe_ref.at[e, pl.ds(row0, MOE_W)], s)

    @pl.when(b == 0)
    def _():
        tail_ref[...] = jnp.zeros(tail_ref.shape, BF16)
        zero_ref[...] = jnp.zeros(zero_ref.shape, BF16)
        pad = zero_ref.shape[0]
        fills = [pltpu.make_async_copy(zero_ref, xe_ref.at[e, pl.ds(xe_ref.shape[1] - pad, pad)], sem_extra.at[0])
                 for e in range(N_EXPERTS)]
        for f in fills:
            f.start()
        for f in fills:
            f.wait()

    rows0 = rows_for(0)

    @pl.when(b > 0)
    def _():
        for e in range(N_EXPERTS):
            copy(stage_ref.at[1 - slot], e, 0, sem.at[1 - slot]).wait()

    for e in range(N_EXPERTS):
        r0 = e * MOE_W
        first = rows0[r0:r0 + SLOT_ALIGN] + tail_ref[e].astype(F32)
        stage_ref[slot, e, :SLOT_ALIGN] = first.astype(BF16)
        stage_ref[slot, e, SLOT_ALIGN:] = rows0[r0 + SLOT_ALIGN:r0 + MOE_W].astype(BF16)
    for e in range(N_EXPERTS):
        copy(stage_ref.at[slot], e, layout.window_row(e, 0), sem.at[slot]).start()
    for e in range(N_EXPERTS):
        @pl.when(layout.tail_chunk[e] == 0)
        def _():
            tail_ref[e] = stage_ref[slot, e, pl.ds(layout.tail_row[e], SLOT_ALIGN), :]

    def overflow(chunk, carry):
        more = rows_for(chunk).astype(BF16)
        for e in range(N_EXPERTS):
            extra_ref[e] = more[e * MOE_W:(e + 1) * MOE_W]
        for e in range(N_EXPERTS):
            copy(extra_ref, e, layout.window_row(e, chunk), sem_extra.at[0]).start()
        for e in range(N_EXPERTS):
            @pl.when(layout.tail_chunk[e] == chunk)
            def _():
                tail_ref[e] = extra_ref[e, pl.ds(layout.tail_row[e], SLOT_ALIGN), :]
        for e in range(N_EXPERTS):
            copy(extra_ref, e, 0, sem_extra.at[0]).wait()
        return carry

    lax.fori_loop(1, layout.n_chunks, overflow, 0)

    @pl.when(b == nb - 1)
    def _():
        for e in range(N_EXPERTS):
            copy(stage_ref.at[slot], e, 0, sem.at[slot]).wait()


def _slot_rows(cap):
    pad = ((SLOT_ALIGN + MOE_TB) // MOE_W + 1) * MOE_W
    return cap + -(-pad // FFN_TM) * FFN_TM


def _dispatch(offs, sel, h, cap):
    n_tok = h.shape[0]
    nb = n_tok // MOE_TB
    rows = _slot_rows(cap)
    grid_spec = pltpu.PrefetchScalarGridSpec(
        num_scalar_prefetch=1,
        grid=(nb,),
        in_specs=[pl.BlockSpec((N_EXPERTS, MOE_TB), lambda b, off: (0, b)),
                  pl.BlockSpec((MOE_TB, D_MODEL), lambda b, off: (b, 0)),
                  pl.BlockSpec((MOE_TB, MOE_TB), lambda b, off: (0, 0))],
        out_specs=pl.BlockSpec(memory_space=pl.ANY),
        scratch_shapes=[pltpu.VMEM((2, N_EXPERTS, MOE_W, D_MODEL), BF16),
                        pltpu.VMEM((N_EXPERTS, MOE_W, D_MODEL), BF16),
                        pltpu.VMEM((N_EXPERTS, SLOT_ALIGN, D_MODEL), BF16),
                        pltpu.VMEM((rows - cap, D_MODEL), BF16),
                        pltpu.SemaphoreType.DMA((2,)),
                        pltpu.SemaphoreType.DMA((1,))],
    )
    return pl.pallas_call(
        _dispatch_kernel,
        grid_spec=grid_spec,
        out_shape=jax.ShapeDtypeStruct((N_EXPERTS, rows, D_MODEL), BF16),
        compiler_params=_cparams(("arbitrary",)),
        name="dispatch",
    )(offs, sel, h, _strict_upper(MOE_TB))


def _ffn_kernel(xe_ref, wg_ref, wu_ref, wd_ref, ye_ref, wg_bf, wu_bf, wd_bf, *, n_real):
    j = pl.program_id(1)

    @pl.when(j == 0)
    def _():
        wg_bf[...] = wg_ref[...].astype(BF16)
        wu_bf[...] = wu_ref[...].astype(BF16)
        wd_bf[...] = wd_ref[...].astype(BF16)

    @pl.when(j < n_real)
    def _():
        x = xe_ref[...]
        gate = jnp.dot(x, wg_bf[...], preferred_element_type=F32)
        up = jnp.dot(x, wu_bf[...], preferred_element_type=F32)
        hid = (gate * jax.nn.sigmoid(gate) * up).astype(BF16)
        ye_ref[...] = jnp.dot(hid, wd_bf[...], preferred_element_type=F32).astype(BF16)

    @pl.when(j >= n_real)
    def _():
        ye_ref[...] = jnp.zeros(ye_ref.shape, BF16)


def _expert_ffn(xe, w_gate, w_up, w_down, layer, cap):
    rows = xe.shape[1]
    n_real = cap // FFN_TM
    n_tiles = rows // FFN_TM
    wspec = lambda shape: pl.BlockSpec((None, None) + shape, lambda e, j: (layer, e, 0, 0))
    return pl.pallas_call(
        functools.partial(_ffn_kernel, n_real=n_real),
        grid=(N_EXPERTS, n_tiles),
        in_specs=[pl.BlockSpec((None, FFN_TM, D_MODEL), lambda e, j: (e, j, 0)),
                  wspec((D_MODEL, EXPERT_FF)), wspec((D_MODEL, EXPERT_FF)), wspec((EXPERT_FF, D_MODEL))],
        out_specs=pl.BlockSpec((None, FFN_TM, D_MODEL), lambda e, j: (e, j, 0)),
        out_shape=jax.ShapeDtypeStruct((N_EXPERTS, rows, D_MODEL), BF16),
        scratch_shapes=[pltpu.VMEM((D_MODEL, EXPERT_FF), BF16), pltpu.VMEM((D_MODEL, EXPERT_FF), BF16),
                        pltpu.VMEM((EXPERT_FF, D_MODEL), BF16)],
        compiler_params=_cparams(("parallel", "arbitrary")),
        name="expert_ffn",
    )(xe, w_gate, w_up, w_down)


def _combine_kernel(off_ref, sel_ref, aff_ref, tri_ref, x_ref, ye_ref, out_ref, win_ref, extra_ref, sem, sem_extra,
                    *, rows):
    b = pl.program_id(0)
    nb = pl.num_programs(0)
    slot = b % 2

    def copy(dst, e, row0, s):
        return pltpu.make_async_copy(ye_ref.at[e, pl.ds(row0, MOE_W)], dst.at[e], s)

    def fetch(blk, dst_slot):
        ahead = _BlockLayout(off_ref, blk)
        for e in range(N_EXPERTS):
            copy(win_ref.at[dst_slot], e, ahead.window_row(e, 0), sem.at[dst_slot]).start()

    @pl.when(b == 0)
    def _():
        fetch(0, 0)

    @pl.when(b + 1 < nb)
    def _():
        fetch(b + 1, 1 - slot)

    layout = _BlockLayout(off_ref, b)
    pos = _window_positions(sel_ref, tri_ref, layout)
    aff = aff_ref[...]
    tn = (((0,), (0,)), ((), ()))

    def contribution(chunk, window):
        gates = _slot_onehots(pos, aff, chunk)
        vals = window.reshape(N_EXPERTS * MOE_W, window.shape[-1])
        return lax.dot_general(gates, vals, tn, preferred_element_type=F32)

    for e in range(N_EXPERTS):
        copy(win_ref.at[slot], e, 0, sem.at[slot]).wait()
    out_ref[...] = x_ref[...] + contribution(0, win_ref[slot])

    def overflow(chunk, carry):
        for e in range(N_EXPERTS):
            row0 = pl.multiple_of(jnp.minimum(layout.window_row(e, chunk), rows - MOE_W), SLOT_ALIGN)
            copy(extra_ref, e, row0, sem_extra.at[0]).start()
        for e in range(N_EXPERTS):
            copy(extra_ref, e, 0, sem_extra.at[0]).wait()
        out_ref[...] += contribution(chunk, extra_ref[...])
        return carry

    lax.fori_loop(1, layout.n_chunks, overflow, 0)


def _combine(offs, sel, aff_t, x, ye):
    n_tok = x.shape[0]
    nb = n_tok // MOE_TB
    rows = ye.shape[1]
    grid_spec = pltpu.PrefetchScalarGridSpec(
        num_scalar_prefetch=1,
        grid=(nb,),
        in_specs=[pl.BlockSpec((N_EXPERTS, MOE_TB), lambda b, off: (0, b)),
                  pl.BlockSpec((N_EXPERTS, MOE_TB), lambda b, off: (0, b)),
                  pl.BlockSpec((MOE_TB, MOE_TB), lambda b, off: (0, 0)),
                  pl.BlockSpec((MOE_TB, D_MODEL), lambda b, off: (b, 0)),
                  pl.BlockSpec(memory_space=pl.ANY)],
        out_specs=pl.BlockSpec((MOE_TB, D_MODEL), lambda b, off: (b, 0)),
        scratch_shapes=[pltpu.VMEM((2, N_EXPERTS, MOE_W, D_MODEL), BF16),
                        pltpu.VMEM((N_EXPERTS, MOE_W, D_MODEL), BF16),
                        pltpu.SemaphoreType.DMA((2,)),
                        pltpu.SemaphoreType.DMA((1,))],
    )
    return pl.pallas_call(
        functools.partial(_combine_kernel, rows=rows),
        grid_spec=grid_spec,
        out_shape=jax.ShapeDtypeStruct((n_tok, D_MODEL), F32),
        compiler_params=_cparams(("arbitrary",)),
        name="combine",
    )(offs, sel, aff_t, _strict_upper(MOE_TB), x, ye)


def _moe(x, h, aff_t, w_gate, w_up, w_down, layer):
    cap = CAPACITY_FACTOR * x.shape[0] // N_EXPERTS
    sel, offs = _select(aff_t, cap)
    xe = _dispatch(offs, sel, h, cap)
    ye = _expert_ffn(xe, w_gate, w_up, w_down, layer, cap)
    return _combine(offs, sel, aff_t, x, ye)


def _row(v):
    return v.reshape(1, -1).astype(F32)


def _router_split(w_router):
    wt = w_router.T.astype(F32)
    hi = wt.astype(BF16)
    lo = (wt - hi.astype(F32)).astype(BF16)
    return hi, lo


def _layer_ab(x, batch, seq, tabs, norm_g, w_in, qn_a, kn_a, qn_b, kn_b, w_out, norm_ffn, w_router):
    tab_a, tab_b = tabs
    scale = HEAD_DIM ** -0.5 * LOG2E
    perm_b = _pair_layout_perm(B_KV_HEADS, B_Q_HEADS // B_KV_HEADS)
    a3 = 3 * A_WIDTH
    bq = B_Q_HEADS * HEAD_DIM
    w_cols = np.concatenate([np.arange(a3), a3 + perm_b, np.arange(a3 + bq, w_in.shape[1])])
    w = w_in[:, w_cols].astype(BF16)
    gains = jnp.concatenate([
        jnp.repeat(qn_a, A_HEADS, axis=0).reshape(-1) * scale,
        jnp.repeat(kn_a, A_HEADS, axis=0).reshape(-1),
        jnp.ones((A_WIDTH,), F32),
        jnp.tile(qn_b, B_Q_HEADS) * scale,
        jnp.tile(kn_b, B_KV_HEADS),
        jnp.ones((B_KV_HEADS * HEAD_DIM,), F32)])
    n_a = A_GROUPS
    assert A_HEADS * HEAD_DIM == MXU_COLS
    plan = ([("qk", g, 0, 0, ROT_DIM // 2) for g in range(n_a)]
            + [("qk", n_a + g, 0, 0, ROT_DIM // 2) for g in range(n_a)]
            + [("v", 2 * n_a + g, 0, 0, 0) for g in range(n_a)]
            + [("qk", 3 * n_a, c * MXU_COLS, 1, HEAD_DIM // 4) for c in range(bq // MXU_COLS)]
            + [("qk", 3 * n_a + 1, 0, 1, HEAD_DIM // 4), ("v", 3 * n_a + 2, 0, 0, 0)])
    assert B_KV_HEADS * HEAD_DIM == MXU_COLS
    outs = _projection(
        x, _row(norm_g), w, _row(gains), (tab_a, tab_b), plan,
        (MXU_COLS,) * (2 * n_a) + (None,) * n_a + (bq, MXU_COLS, None), seq)
    qa, ka, va = outs[:n_a], outs[n_a:2 * n_a], outs[2 * n_a:3 * n_a]
    qb, kb, vb = outs[3 * n_a:]
    o_parts, lse_parts = [], []
    for gi, (window, dil) in enumerate(A_PATTERNS):
        o, lse = _attention(qa[gi], ka[gi], va[gi], batch=batch, n_keys=seq // dil, residues=dil, group=1,
                            n_pairs=A_HEADS // 2, band_r=window // (2 * dil), want_lse=True)
        o_parts.append(o)
        lse_parts.append(lse)
    (ob,) = _attention(qb, kb, vb, batch=batch, n_keys=seq, residues=1,
                       group=B_Q_HEADS // B_KV_HEADS, n_pairs=B_KV_HEADS // 2, band_r=None)
    a_out = A_HEADS * HEAD_DIM
    w_o = w_out[np.concatenate([np.arange(a_out), a_out + perm_b])].astype(BF16)
    wr_hi, wr_lo = _router_split(w_router)
    return _out_projection(o_parts, lse_parts, ob, x, w_o, _row(norm_ffn), wr_hi, wr_lo)


def _layer_c(x, batch, seq, tabs, norm_g, w_in, qn, kn, sink, w_out, norm_ffn, w_router):
    tab_a, _ = tabs
    scale = HEAD_DIM ** -0.5 * LOG2E
    group = C_Q_HEADS // C_KV_HEADS
    perm = _pair_layout_perm(C_KV_HEADS, group)
    cq = C_Q_HEADS * HEAD_DIM
    w_cols = np.concatenate([perm, np.arange(cq, w_in.shape[1])])
    w = w_in[:, w_cols].astype(BF16)
    gains = jnp.concatenate([jnp.tile(qn, C_Q_HEADS) * scale, jnp.tile(kn, C_KV_HEADS),
                             jnp.ones((C_KV_HEADS * HEAD_DIM,), F32)])
    plan = ([("qk", 0, c * MXU_COLS, 0, ROT_DIM // 2) for c in range(cq // MXU_COLS)]
            + [("qk", 1, 0, 0, ROT_DIM // 2), ("v", 2, 0, 0, 0)])
    assert C_KV_HEADS * HEAD_DIM == MXU_COLS
    q, k, v = _projection(x, _row(norm_g), w, _row(gains), (tab_a,), plan, (cq, MXU_COLS, None), seq)
    (o,) = _attention(q, k, v, batch=batch, n_keys=seq, residues=1,
                      group=group, n_pairs=C_KV_HEADS // 2, band_r=C_RADIUS, sink=sink.astype(F32))
    wr_hi, wr_lo = _router_split(w_router)
    return _out_projection([], [], o, x, w_out[perm].astype(BF16), _row(norm_ffn), wr_hi, wr_lo)


def _encode(x, params):
    (norm_mix, norm_ffn, w_in_ab, qn_a, kn_a, qn_b, kn_b, w_out_ab, w_in_c, qn_c, kn_c, sink_c, w_out_c,
     w_router, w_gate, w_up, w_down) = params
    batch, seq, _ = x.shape
    tabs = _rope_tables(seq)
    xt = x.reshape(batch * seq, D_MODEL)
    for layer in range(norm_mix.shape[0]):
        j = layer // 2
        if layer % 2 == 0:
            xt, h, aff_t = _layer_ab(xt, batch, seq, tabs, norm_mix[layer], w_in_ab[j], qn_a[j], kn_a[j],
                                     qn_b[j], kn_b[j], w_out_ab[j], norm_ffn[layer], w_router[layer])
        else:
            xt, h, aff_t = _layer_c(xt, batch, seq, tabs, norm_mix[layer], w_in_c[j], qn_c[j], kn_c[j],
                                    sink_c[j], w_out_c[j], norm_ffn[layer], w_router[layer])
        xt = _moe(xt, h, aff_t, w_gate, w_up, w_down, layer)
    return xt.reshape(batch, seq, D_MODEL)


def kernel(x_prompt, x_sample, norm_mix, norm_ffn, w_in_ab, qn_a, kn_a, qn_b, kn_b, w_out_ab, w_in_c, qn_c,
           kn_c, sink_c, w_out_c, w_router, w_gate, w_up, w_down):
    params = (norm_mix, norm_ffn, w_in_ab, qn_a, kn_a, qn_b, kn_b, w_out_ab, w_in_c, qn_c, kn_c, sink_c,
              w_out_c, w_router, w_gate, w_up, w_down)
    return _encode(x_prompt, params), _encode(x_sample, params)
```

```python
import functools
import math

import jax
import jax.numpy as jnp
import numpy as np
from jax import lax
from jax.experimental import pallas as pl
from jax.experimental.pallas import tpu as pltpu

F32 = jnp.float32
BF16 = jnp.bfloat16
I32 = jnp.int32

D_MODEL = 1024
HEAD_DIM = 64
LANES = 128
MXU_COLS = 256
GRID_W = 64
ROT_DIM = HEAD_DIM // 4
ROPE_THETA = 500000.0
AXIAL_THETA = 10000.0
A_PATTERNS = ((128, 1), (512, 4), (2048, 16))
A_HEADS = 4
A_GROUPS = len(A_PATTERNS)
A_WIDTH = A_GROUPS * A_HEADS * HEAD_DIM
B_Q_HEADS = 12
B_KV_HEADS = 4
C_Q_HEADS = 16
C_KV_HEADS = 4
C_RADIUS = 128
N_EXPERTS = 16
EXPERT_FF = 1024
CAPACITY_FACTOR = 2
NEG_INF = -1e30
EPS = 1e-6

VMEM_LIMIT = 56 * 1024 * 1024

PROJ_TM = 512
ATTN_TQ = 256
ACC_ROWS = LANES + 16
DENSE_TK = 512
MOE_TB = 256
MOE_W = 64
FFN_TM = 256


def _cparams(sem):
    return pltpu.CompilerParams(dimension_semantics=sem, vmem_limit_bytes=VMEM_LIMIT)


def _head_block_diag():
    idx = np.arange(MXU_COLS) // HEAD_DIM
    return jnp.asarray((idx[:, None] == idx[None, :]).astype(np.float32), dtype=BF16)


def _rope_tables(max_len):
    pos = jnp.arange(max_len, dtype=F32)
    j = np.arange(LANES) % HEAD_DIM

    def angles(p, dim, theta):
        exps = jnp.arange(0, dim, 2, dtype=F32) / dim
        inv = jnp.power(jnp.float32(theta), -exps)
        return p[:, None] * inv[None, :]

    half = ROT_DIM // 2
    ang = angles(pos, ROT_DIM, ROPE_THETA)
    cos, sin = jnp.cos(ang), jnp.sin(ang)
    fa = np.where(j < half, j, np.where(j < ROT_DIM, j - half, 0))
    cos_l, sin_l = cos[:, fa], sin[:, fa]
    lo = jnp.asarray(j < half)[None, :]
    hi = jnp.asarray((j >= half) & (j < ROT_DIM))[None, :]
    tab_a = jnp.stack([jnp.where(lo | hi, cos_l, 1.0),
                       jnp.where(lo, -sin_l, 0.0),
                       jnp.where(hi, sin_l, 0.0)])
    hb = HEAD_DIM // 2
    qb = hb // 2
    t = jnp.arange(max_len)
    ang_r = angles((t // GRID_W).astype(F32), hb, AXIAL_THETA)
    ang_c = angles((t % GRID_W).astype(F32), hb, AXIAL_THETA)
    fb = j % qb
    is_col = jnp.asarray(j >= hb)[None, :]
    ang_l = jnp.where(is_col, ang_c[:, fb], ang_r[:, fb])
    cos_b, sin_b = jnp.cos(ang_l), jnp.sin(ang_l)
    first = jnp.asarray((j % hb) < qb)[None, :]
    tab_b = jnp.stack([cos_b, jnp.where(first, -sin_b, 0.0), jnp.where(first, 0.0, sin_b)])
    return tab_a.astype(F32), tab_b.astype(F32)


def _pair_layout_perm(n_kv, group):
    cols = []
    for p in range(n_kv // 2):
        for g in range(group):
            for par in range(2):
                h = (2 * p + par) * group + g
                cols.extend(range(h * HEAD_DIM, (h + 1) * HEAD_DIM))
    return np.asarray(cols, dtype=np.int32)


def _proj_kernel(x_ref, g_ref, w_ref, s_ref, gain_ref, *rest, plan, n_tabs):
    tab_refs = rest[:n_tabs]
    out_refs = rest[n_tabs:]
    x = x_ref[...]
    ms = jnp.mean(x * x, axis=1, keepdims=True)
    xn = ((x * lax.rsqrt(ms + EPS)) * g_ref[...]).astype(BF16)
    def project(c):
        return jnp.dot(xn, w_ref[:, c * MXU_COLS:(c + 1) * MXU_COLS], preferred_element_type=F32)

    ahead = project(0)
    for c, (kind, out_i, out_col, tab_i, shift) in enumerate(plan):
        acc = ahead
        if c + 1 < len(plan):
            ahead = project(c + 1)
        o_ref = out_refs[out_i]
        if kind == "v":
            o_ref[...] = acc.T.astype(BF16)
            continue
        ss = jnp.dot((acc * acc).astype(BF16), s_ref[...], preferred_element_type=F32)
        y = (acc * lax.rsqrt(ss * (1.0 / HEAD_DIM) + EPS)) * gain_ref[:, c * MXU_COLS:(c + 1) * MXU_COLS]
        tab = tab_refs[tab_i]
        t0 = jnp.concatenate([tab[0], tab[0]], axis=1)
        t1 = jnp.concatenate([tab[1], tab[1]], axis=1)
        t2 = jnp.concatenate([tab[2], tab[2]], axis=1)
        y = y * t0 + pltpu.roll(y, MXU_COLS - shift, 1) * t1 + pltpu.roll(y, shift, 1) * t2
        o_ref[:, out_col:out_col + MXU_COLS] = y.astype(BF16)


def _projection(x, norm_g, w, gains, tabs, plan, out_widths, seq_len):
    t_tokens = x.shape[0]
    tm = PROJ_TM
    n_in = w.shape[1]
    blocks_per_seq = seq_len // tm
    in_specs = [
        pl.BlockSpec((tm, D_MODEL), lambda i: (i, 0)),
        pl.BlockSpec((1, D_MODEL), lambda i: (0, 0)),
        pl.BlockSpec((D_MODEL, n_in), lambda i: (0, 0)),
        pl.BlockSpec((MXU_COLS, MXU_COLS), lambda i: (0, 0)),
        pl.BlockSpec((1, n_in), lambda i: (0, 0)),
    ] + [pl.BlockSpec((3, tm, LANES), lambda i: (0, i % blocks_per_seq, 0)) for _ in tabs]
    out_specs, out_shape = [], []
    for wd in out_widths:
        if wd is None:
            out_specs.append(pl.BlockSpec((None, MXU_COLS, tm),
                                          lambda i: (i // blocks_per_seq, 0, i % blocks_per_seq)))
            out_shape.append(jax.ShapeDtypeStruct((t_tokens // seq_len, MXU_COLS, seq_len), BF16))
        else:
            out_specs.append(pl.BlockSpec((tm, wd), lambda i: (i, 0)))
            out_shape.append(jax.ShapeDtypeStruct((t_tokens, wd), BF16))
    return pl.pallas_call(
        functools.partial(_proj_kernel, plan=tuple(plan), n_tabs=len(tabs)),
        grid=(t_tokens // tm,),
        in_specs=in_specs, out_specs=out_specs, out_shape=out_shape,
        compiler_params=_cparams(("parallel",)),
        name="projection",
    )(x, norm_g, w, _head_block_diag(), gains, *tabs)


LOG2E = math.log2(math.e)
LN2 = math.log(2.0)
_NT = (((1,), (1,)), ((), ()))


def _attn_kernel(*refs, group, n_pairs, tq, tk, n_keys, band_r, has_sink, want_lse):
    refs = list(refs)
    sink_ref = refs.pop(0) if has_sink else None
    q_ref, k_ref, v_ref, o_ref = refs[:4]
    refs = refs[4:]
    lse_ref = refs.pop(0) if want_lse else None
    scratch = refs
    m_cols = group * tq
    n_chain = 2 * n_pairs
    i = pl.program_id(2)
    lane = lax.broadcasted_iota(I32, (1, LANES), 1)
    low = lane < HEAD_DIM
    half_mask = [jnp.where(low, 1.0, 0.0).astype(BF16), jnp.where(low, 0.0, 1.0).astype(BF16)]

    q_masked = []
    for p in range(n_pairs):
        q_p = jnp.concatenate(
            [q_ref[:, (p * group + g) * LANES:(p * group + g + 1) * LANES] for g in range(group)], axis=0)
        q_masked += [q_p * half_mask[par] for par in range(2)]

    sub = lax.broadcasted_iota(I32, (ACC_ROWS, m_cols), 0)
    if has_sink:
        acc0 = jnp.where(sub >= LANES, 1.0, 0.0).astype(F32)
        m0 = []
        for c in range(n_chain):
            m0.append(jnp.concatenate(
                [jnp.full((1, tq), sink_ref[c * group + g] * LOG2E, F32) for g in range(group)], axis=1))
    else:
        acc0 = jnp.zeros((ACC_ROWS, m_cols), F32)
        m0 = [jnp.full((1, m_cols), NEG_INF, F32) for _ in range(n_chain)]

    def values_t(p, k0, width):
        ones = jnp.ones((ACC_ROWS - LANES, width), BF16)
        return jnp.concatenate([v_ref[p * LANES:(p + 1) * LANES, pl.ds(k0, width)], ones], axis=0)

    def softmax_pv(s_t, v_t, m_prev, acc_prev):
        m_new = jnp.maximum(m_prev, jnp.max(s_t, axis=0, keepdims=True))
        alpha = jnp.exp2(m_prev - m_new)
        p_t = jnp.exp2(s_t - m_new).astype(BF16)
        return m_new, alpha * acc_prev + jnp.dot(v_t, p_t, preferred_element_type=F32)

    if band_r is None:
        qm_ref, acc_ref, s_even, s_odd = scratch
        for c in range(n_chain):
            qm_ref[c] = q_masked[c]
            acc_ref[c] = acc0

        def scores(kb, s_ref):
            k0 = pl.multiple_of(kb * tk, tk)
            for p in range(n_pairs):
                kt = k_ref[pl.ds(k0, tk), p * LANES:(p + 1) * LANES]
                for par in range(2):
                    c = 2 * p + par
                    s_ref[c] = lax.dot_general(kt, qm_ref[c], _NT, preferred_element_type=F32)

        def consume(kb, s_ref, m_prev):
            k0 = pl.multiple_of(kb * tk, tk)
            m_next = []
            for p in range(n_pairs):
                v_t = values_t(p, k0, tk)
                for par in range(2):
                    c = 2 * p + par
                    m_new, acc_ref[c] = softmax_pv(s_ref[c], v_t, m_prev[c], acc_ref[c])
                    m_next.append(m_new)
            return m_next

        n_blocks = n_keys // tk
        scores(0, s_even)

        def body(j, carry):
            m = list(carry)
            scores(2 * j + 1, s_odd)
            m = consume(2 * j, s_even, m)
            scores(2 * j + 2, s_even)
            m = consume(2 * j + 1, s_odd, m)
            return tuple(m)

        m_fin = list(lax.fori_loop(0, n_blocks // 2 - 1, body, tuple(m0)))
        scores(n_blocks - 1, s_odd)
        m_fin = consume(n_blocks - 2, s_even, m_fin)
        m_fin = consume(n_blocks - 1, s_odd, m_fin)
        acc_fin = [acc_ref[c] for c in range(n_chain)]
    else:
        halo = -(-band_r // LANES) * LANES
        width = min(tq + 2 * halo, n_keys)
        ws = pl.multiple_of(jnp.clip(i * tq - halo, 0, n_keys - width), LANES)
        kpos = ws + lax.broadcasted_iota(I32, (width, 1), 0)
        col = lax.broadcasted_iota(I32, (1, tq), 1)
        qpos = i * tq + jnp.concatenate([col] * group, axis=1)
        mask = jnp.abs(kpos - qpos) <= band_r
        s_all = []
        for p in range(n_pairs):
            kt = k_ref[pl.ds(ws, width), p * LANES:(p + 1) * LANES]
            s_all += [lax.dot_general(kt, q_masked[2 * p + par], _NT, preferred_element_type=F32)
                      for par in range(2)]
        m_fin, acc_fin = [], []
        for c in range(n_chain):
            m_new, acc_new = softmax_pv(jnp.where(mask, s_all[c], NEG_INF), values_t(c // 2, ws, width),
                                        m0[c], acc0)
            m_fin.append(m_new)
            acc_fin.append(acc_new)

    top = lax.broadcasted_iota(I32, (LANES, tq), 0) < HEAD_DIM
    for p in range(n_pairs):
        even, odd = acc_fin[2 * p], acc_fin[2 * p + 1]
        for g in range(group):
            cols = slice(g * tq, (g + 1) * tq)
            num = jnp.where(top, even[:LANES, cols], odd[:LANES, cols])
            den = jnp.where(top, even[LANES:LANES + 1, cols], odd[LANES:LANES + 1, cols])
            out_cols = slice((p * group + g) * LANES, (p * group + g + 1) * LANES)
            o_ref[:, out_cols] = (num / den).T.astype(o_ref.dtype)
            if want_lse:
                m_sel = jnp.where(top, m_fin[2 * p][:, cols], m_fin[2 * p + 1][:, cols])
                lse_ref[:, out_cols] = (m_sel * LN2 + jnp.log(den)).T


def _attention(q, k, v_t, *, batch, n_keys, residues, group, n_pairs, band_r, sink=None, want_lse=False):
    dense = band_r is None
    tq = ATTN_TQ
    wq = n_pairs * group * LANES
    wk = n_pairs * LANES
    q3 = q.reshape(batch, n_keys, residues * wq)
    k3 = k.reshape(batch, n_keys, residues * wk)
    v4 = v_t.reshape(batch, wk, n_keys, residues).transpose(0, 3, 1, 2)

    in_specs = []
    args = []
    if sink is not None:
        in_specs.append(pl.BlockSpec(memory_space=pltpu.SMEM))
        args.append(sink)
    in_specs += [
        pl.BlockSpec((None, tq, wq), lambda b, r, i: (b, i, r)),
        pl.BlockSpec((None, n_keys, wk), lambda b, r, i: (b, 0, r)),
        pl.BlockSpec((None, None, wk, n_keys), lambda b, r, i: (b, r, 0, 0)),
    ]
    args += [q3, k3, v4]
    out_specs = [pl.BlockSpec((None, tq, wq), lambda b, r, i: (b, i, r))]
    out_shape = [jax.ShapeDtypeStruct((batch, n_keys, residues * wq), BF16)]
    if want_lse:
        out_specs.append(pl.BlockSpec((None, tq, wq), lambda b, r, i: (b, i, r)))
        out_shape.append(jax.ShapeDtypeStruct((batch, n_keys, residues * wq), F32))
    m_cols = group * tq
    scratch = []
    if dense:
        scratch = [pltpu.VMEM((2 * n_pairs, m_cols, LANES), BF16),
                   pltpu.VMEM((2 * n_pairs, ACC_ROWS, m_cols), F32)]
        scratch += [pltpu.VMEM((2 * n_pairs, DENSE_TK, m_cols), F32) for _ in range(2)]
    outs = pl.pallas_call(
        functools.partial(_attn_kernel, group=group, n_pairs=n_pairs, tq=tq, tk=DENSE_TK, n_keys=n_keys,
                          band_r=band_r, has_sink=sink is not None, want_lse=want_lse),
        grid=(batch, residues, n_keys // tq),
        in_specs=in_specs, out_specs=out_specs, out_shape=out_shape,
        scratch_shapes=scratch,
        compiler_params=_cparams(("parallel", "parallel", "arbitrary")),
        name="attention_dense" if dense else "attention_band",
    )(*args)
    return [o.reshape(batch * n_keys * residues, wq) for o in outs]


def _outproj_kernel(*refs, n_merge):
    refs = list(refs)
    if n_merge:
        o_parts = refs[:n_merge]
        lse_parts = refs[n_merge:2 * n_merge]
        refs = refs[2 * n_merge:]
    o_rest, x_ref, w_ref, g_ref, wrh_ref, wrl_ref, xo_ref, h_ref, aff_ref = refs
    tm = x_ref.shape[0]
    halves = [slice(0, tm // 2), slice(tm // 2, tm)]
    accs = []
    for rows in halves:
        acc = x_ref[rows, :]
        k0 = 0
        if n_merge:
            lses = [r[rows, :] for r in lse_parts]
            m = functools.reduce(jnp.maximum, lses)
            ws = [jnp.exp(l - m) for l in lses]
            num = sum(wgt * r[rows, :].astype(F32) for wgt, r in zip(ws, o_parts))
            oa = (num / sum(ws)).astype(BF16)
            k0 = oa.shape[1]
            acc = acc + jnp.dot(oa, w_ref[:k0, :], preferred_element_type=F32)
        accs.append(acc + jnp.dot(o_rest[rows, :], w_ref[k0:, :], preferred_element_type=F32))
    for rows, acc in zip(halves, accs):
        xo_ref[rows, :] = acc
        ms = jnp.mean(acc * acc, axis=1, keepdims=True)
        h = (acc * lax.rsqrt(ms + EPS)) * g_ref[...]
        h_hi = h.astype(BF16)
        h_ref[rows, :] = h_hi
        h_lo = (h - h_hi.astype(F32)).astype(BF16)
        logits = (lax.dot_general(wrh_ref[...], h_hi, _NT, preferred_element_type=F32)
                  + lax.dot_general(wrl_ref[...], h_hi, _NT, preferred_element_type=F32)
                  + lax.dot_general(wrh_ref[...], h_lo, _NT, preferred_element_type=F32))
        mx = jnp.max(logits, axis=0, keepdims=True)
        e = jnp.exp(logits - mx)
        aff_ref[:, rows] = e / jnp.sum(e, axis=0, keepdims=True)


def _out_projection(o_merge, lse_merge, o_rest, x, w_out, norm_g, wr_hi, wr_lo):
    t_tokens = x.shape[0]
    tm = PROJ_TM
    n_merge = len(o_merge)
    row = lambda i: (i, 0)
    fixed = lambda i: (0, 0)
    in_specs = ([pl.BlockSpec((tm, o.shape[1]), row) for o in o_merge]
                + [pl.BlockSpec((tm, l.shape[1]), row) for l in lse_merge]
                + [pl.BlockSpec((tm, o_rest.shape[1]), row),
                   pl.BlockSpec((tm, D_MODEL), row),
                   pl.BlockSpec(w_out.shape, fixed),
                   pl.BlockSpec((1, D_MODEL), fixed),
                   pl.BlockSpec(wr_hi.shape, fixed),
                   pl.BlockSpec(wr_lo.shape, fixed)])
    out_specs = [pl.BlockSpec((tm, D_MODEL), row),
                 pl.BlockSpec((tm, D_MODEL), row),
                 pl.BlockSpec((N_EXPERTS, tm), lambda i: (0, i))]
    out_shape = [jax.ShapeDtypeStruct((t_tokens, D_MODEL), F32),
                 jax.ShapeDtypeStruct((t_tokens, D_MODEL), BF16),
                 jax.ShapeDtypeStruct((N_EXPERTS, t_tokens), F32)]
    return pl.pallas_call(
        functools.partial(_outproj_kernel, n_merge=n_merge),
        grid=(t_tokens // tm,),
        in_specs=in_specs, out_specs=out_specs, out_shape=out_shape,
        compiler_params=_cparams(("parallel",)),
        name="out_projection",
    )(*o_merge, *lse_merge, o_rest, x, w_out, norm_g, wr_hi, wr_lo)


def _select_kernel(aff_ref, ind_ref, tri_ref, sel_ref, off_ref, *, cap):
    aff = aff_ref[...]
    n_tok = aff.shape[1]
    bits = pltpu.bitcast(aff, I32)

    def count(pred):
        return jnp.sum(jnp.where(pred, 1.0, 0.0), axis=1, keepdims=True)

    def value_step(j, thr):
        cand = thr | lax.shift_left(jnp.int32(1), 30 - j)
        return jnp.where(count(bits >= cand) >= cap, cand, thr)

    thr = lax.fori_loop(0, 31, value_step, jnp.zeros((N_EXPERTS, 1), I32))
    gt = bits > thr
    eq = bits == thr
    need = cap - count(gt)
    idx = lax.broadcasted_iota(I32, aff.shape, 1)
    idx_bits = int(math.log2(n_tok))

    def index_step(j, bound):
        cand = bound | lax.shift_left(jnp.int32(1), idx_bits - j)
        return jnp.where(count(eq & (idx < cand)) <= need, cand, bound)

    bound = lax.fori_loop(0, idx_bits + 1, index_step, jnp.zeros((N_EXPERTS, 1), I32))
    sel = jnp.where(gt | (eq & (idx < bound)), 1.0, 0.0).astype(BF16)
    sel_ref[...] = sel
    counts = jnp.dot(sel, ind_ref[...], preferred_element_type=F32)
    offs = jnp.dot(counts.astype(BF16), tri_ref[...], preferred_element_type=F32)
    off_ref[...] = offs.astype(I32)


def _strict_upper(n):
    return jnp.asarray(np.triu(np.ones((n, n), np.float32), k=1), dtype=BF16)


def _select(aff_t, cap):
    n_tok = aff_t.shape[1]
    ind = np.zeros((n_tok, LANES), np.float32)
    ind[np.arange(n_tok), np.arange(n_tok) // MOE_TB] = 1.0
    return pl.pallas_call(
        functools.partial(_select_kernel, cap=cap),
        out_shape=[jax.ShapeDtypeStruct((N_EXPERTS, n_tok), BF16),
                   jax.ShapeDtypeStruct((N_EXPERTS, LANES), I32)],
        compiler_params=pltpu.CompilerParams(vmem_limit_bytes=VMEM_LIMIT),
        name="select",
    )(aff_t, jnp.asarray(ind, dtype=BF16), _strict_upper(LANES))


SLOT_ALIGN = 16
ALIGN_BITS = 4
CHUNK_BITS = 6
NOT_SELECTED = -1e6


class _BlockLayout:
    def __init__(self, off_ref, b):
        self.start = [off_ref[e, b] for e in range(N_EXPERTS)]
        self.end = [off_ref[e, b + 1] for e in range(N_EXPERTS)]
        def floor_tile(v):
            return lax.shift_left(lax.shift_right_logical(v, ALIGN_BITS), ALIGN_BITS)

        self.base = [floor_tile(s) for s in self.start]
        self.shift = [s - a for s, a in zip(self.start, self.base)]
        span = [en - a for en, a in zip(self.end, self.base)]
        self.n_chunks = functools.reduce(
            jnp.maximum, [lax.shift_right_logical(sp, CHUNK_BITS) for sp in span]) + 1
        tail = [floor_tile(sp) for sp in span]
        self.tail_chunk = [lax.shift_right_logical(t, CHUNK_BITS) for t in tail]
        self.tail_row = [pl.multiple_of(t & (MOE_W - 1), SLOT_ALIGN) for t in tail]

    def window_row(self, e, chunk):
        return pl.multiple_of(self.base[e] + chunk * MOE_W, SLOT_ALIGN)


def _slot_onehots(pos, weight, chunk):
    n_tok = pos.shape[1]
    slot = (lax.broadcasted_iota(I32, (MOE_W, n_tok), 0) + chunk * MOE_W).astype(F32)
    parts = []
    for e in range(N_EXPERTS):
        hit = pos[e:e + 1, :] == slot
        val = 1.0 if weight is None else weight[e:e + 1, :]
        parts.append(jnp.where(hit, val, 0.0).astype(BF16))
    return jnp.concatenate(parts, axis=0)


def _window_positions(sel_ref, tri_ref, layout):
    sel = sel_ref[...]
    rank = jnp.dot(sel, tri_ref[...], preferred_element_type=F32)
    shift = jnp.concatenate([jnp.full((1, 1), s, I32) for s in layout.shift], axis=0).astype(F32)
    return jnp.where(sel > 0, rank + shift, NOT_SELECTED)


def _dispatch_kernel(off_ref, sel_ref, h_ref, tri_ref, xe_ref, stage_ref, extra_ref, tail_ref, zero_ref, sem,
                     sem_extra):
    b = pl.program_id(0)
    nb = pl.num_programs(0)
    slot = b % 2
    layout = _BlockLayout(off_ref, b)
    pos = _window_positions(sel_ref, tri_ref, layout)
    h = h_ref[...]

    def rows_for(chunk):
        return jnp.dot(_slot_onehots(pos, None, chunk), h, preferred_element_type=F32)

    def copy(src, e, row0, s):
        return pltpu.make_async_copy(src.at[e], xe_ref.at[e, pl.ds(row0, MOE_W)], s)

    @pl.when(b == 0)
    def _():
        tail_ref[...] = jnp.zeros(tail_ref.shape, BF16)
        zero_ref[...] = jnp.zeros(zero_ref.shape, BF16)
        pad = zero_ref.shape[0]
        fills = [pltpu.make_async_copy(zero_ref, xe_ref.at[e, pl.ds(xe_ref.shape[1] - pad, pad)], sem_extra.at[0])
                 for e in range(N_EXPERTS)]
        for f in fills:
            f.start()
        for f in fills:
            f.wait()

    rows0 = rows_for(0)

    @pl.when(b > 0)
    def _():
        for e in range(N_EXPERTS):
            copy(stage_ref.at[1 - slot], e, 0, sem.at[1 - slot]).wait()

    for e in range(N_EXPERTS):
        r0 = e * MOE_W
        first = rows0[r0:r0 + SLOT_ALIGN] + tail_ref[e].astype(F32)
        stage_ref[slot, e, :SLOT_ALIGN] = first.astype(BF16)
        stage_ref[slot, e, SLOT_ALIGN:] = rows0[r0 + SLOT_ALIGN:r0 + MOE_W].astype(BF16)
    for e in range(N_EXPERTS):
        copy(stage_ref.at[slot], e, layout.window_row(e, 0), sem.at[slot]).start()
    for e in range(N_EXPERTS):
        tile = stage_ref[slot, e, pl.ds(layout.tail_row[e], SLOT_ALIGN), :]
        tail_ref[e] = jnp.where(layout.tail_chunk[e] == 0, tile, tail_ref[e])

    def overflow(chunk, carry):
        more = rows_for(chunk).astype(BF16)
        for e in range(N_EXPERTS):
            extra_ref[e] = more[e * MOE_W:(e + 1) * MOE_W]
        for e in range(N_EXPERTS):
            copy(extra_ref, e, layout.window_row(e, chunk), sem_extra.at[0]).start()
        for e in range(N_EXPERTS):
            @pl.when(layout.tail_chunk[e] == chunk)
            def _():
                tail_ref[e] = extra_ref[e, pl.ds(layout.tail_row[e], SLOT_ALIGN), :]
        for e in range(N_EXPERTS):
            copy(extra_ref, e, 0, sem_extra.at[0]).wait()
        return carry

    lax.fori_loop(1, layout.n_chunks, overflow, 0)

    @pl.when(b == nb - 1)
    def _():
        for e in range(N_EXPERTS):
            copy(stage_ref.at[slot], e, 0, sem.at[slot]).wait()


def _slot_rows(cap):
    pad = ((SLOT_ALIGN + MOE_TB) // MOE_W + 1) * MOE_W
    return cap + -(-pad // FFN_TM) * FFN_TM


def _dispatch(offs, sel, h, cap):
    n_tok = h.shape[0]
    nb = n_tok // MOE_TB
    rows = _slot_rows(cap)
    grid_spec = pltpu.PrefetchScalarGridSpec(
        num_scalar_prefetch=1,
        grid=(nb,),
        in_specs=[pl.BlockSpec((N_EXPERTS, MOE_TB), lambda b, off: (0, b)),
                  pl.BlockSpec((MOE_TB, D_MODEL), lambda b, off: (b, 0)),
                  pl.BlockSpec((MOE_TB, MOE_TB), lambda b, off: (0, 0))],
        out_specs=pl.BlockSpec(memory_space=pl.ANY),
        scratch_shapes=[pltpu.VMEM((2, N_EXPERTS, MOE_W, D_MODEL), BF16),
                        pltpu.VMEM((N_EXPERTS, MOE_W, D_MODEL), BF16),
                        pltpu.VMEM((N_EXPERTS, SLOT_ALIGN, D_MODEL), BF16),
                        pltpu.VMEM((rows - cap, D_MODEL), BF16),
                        pltpu.SemaphoreType.DMA((2,)),
                        pltpu.SemaphoreType.DMA((1,))],
    )
    return pl.pallas_call(
        _dispatch_kernel,
        grid_spec=grid_spec,
        out_shape=jax.ShapeDtypeStruct((N_EXPERTS, rows, D_MODEL), BF16),
        compiler_params=_cparams(("arbitrary",)),
        name="dispatch",
    )(offs, sel, h, _strict_upper(MOE_TB))


class _FfnSchedule:
    def __init__(self, caps, rows):
        self.n_real = [cap // FFN_TM for cap in caps]
        self.n_zero = [r // FFN_TM - n for r, n in zip(rows, self.n_real)]
        self.zero_at, step = [], 0
        for n in self.n_zero:
            self.zero_at.append(step)
            step += n
        self.first_real = step
        self.real_at = []
        for n in self.n_real:
            self.real_at.append(step)
            step += n
        self.n_steps = step

    def in_tile(self, g, j):
        return jnp.clip(j - self.real_at[g], 0, self.n_real[g] - 1)

    def out_tile(self, g, j):
        pad = self.n_real[g] + jnp.clip(j - self.zero_at[g], 0, self.n_zero[g] - 1)
        return jnp.where(j < self.real_at[g], pad, self.in_tile(g, j))


def _ffn_kernel(*refs, sched):
    n_groups = len(sched.n_real)
    xe_refs = refs[:n_groups]
    wg_ref, wu_ref, wd_ref = refs[n_groups:n_groups + 3]
    ye_refs = refs[n_groups + 3:2 * n_groups + 3]
    wg_bf, wu_bf, wd_bf = refs[2 * n_groups + 3:]
    j = pl.program_id(1)

    @pl.when(j == 0)
    def _():
        wg_bf[...] = wg_ref[...].astype(BF16)
        wu_bf[...] = wu_ref[...].astype(BF16)
        wd_bf[...] = wd_ref[...].astype(BF16)

    for g in range(n_groups):
        @pl.when((j >= sched.zero_at[g]) & (j < sched.zero_at[g] + sched.n_zero[g]))
        def _():
            ye_refs[g][...] = jnp.zeros(ye_refs[g].shape, BF16)

        @pl.when((j >= sched.real_at[g]) & (j < sched.real_at[g] + sched.n_real[g]))
        def _():
            x = xe_refs[g][...]
            gate = jnp.dot(x, wg_bf[...], preferred_element_type=F32)
            up = jnp.dot(x, wu_bf[...], preferred_element_type=F32)
            hid = (gate * jax.nn.sigmoid(gate) * up).astype(BF16)
            ye_refs[g][...] = jnp.dot(hid, wd_bf[...], preferred_element_type=F32).astype(BF16)


def _expert_ffn(xes, w_gate, w_up, w_down, layer, caps):
    sched = _FfnSchedule(caps, [xe.shape[1] for xe in xes])

    def weight_index(e, j):
        return (layer, jnp.minimum(e + (j >= sched.first_real).astype(I32), N_EXPERTS - 1), 0, 0)

    wspec = lambda shape: pl.BlockSpec((None, None) + shape, weight_index)

    def tile_spec(g, tile_fn):
        return pl.BlockSpec((None, FFN_TM, D_MODEL), lambda e, j: (e, tile_fn(g, j), 0))

    n_groups = len(xes)
    return pl.pallas_call(
        functools.partial(_ffn_kernel, sched=sched),
        grid=(N_EXPERTS, sched.n_steps),
        in_specs=([tile_spec(g, sched.in_tile) for g in range(n_groups)]
                  + [wspec((D_MODEL, EXPERT_FF)), wspec((D_MODEL, EXPERT_FF)), wspec((EXPERT_FF, D_MODEL))]),
        out_specs=[tile_spec(g, sched.out_tile) for g in range(n_groups)],
        out_shape=[jax.ShapeDtypeStruct(xe.shape, BF16) for xe in xes],
        scratch_shapes=[pltpu.VMEM((D_MODEL, EXPERT_FF), BF16), pltpu.VMEM((D_MODEL, EXPERT_FF), BF16),
                        pltpu.VMEM((EXPERT_FF, D_MODEL), BF16)],
        compiler_params=_cparams(("arbitrary", "arbitrary")),
        name="expert_ffn",
    )(*xes, w_gate, w_up, w_down)


def _combine_kernel(off_ref, sel_ref, aff_ref, tri_ref, x_ref, ye_ref, out_ref, win_ref, extra_ref, sem, sem_extra,
                    *, rows):
    b = pl.program_id(0)
    nb = pl.num_programs(0)
    slot = b % 2

    def copy(dst, e, row0, s):
        return pltpu.make_async_copy(ye_ref.at[e, pl.ds(row0, MOE_W)], dst.at[e], s)

    def fetch(blk, dst_slot):
        ahead = _BlockLayout(off_ref, blk)
        for e in range(N_EXPERTS):
            copy(win_ref.at[dst_slot], e, ahead.window_row(e, 0), sem.at[dst_slot]).start()

    @pl.when(b == 0)
    def _():
        fetch(0, 0)

    @pl.when(b + 1 < nb)
    def _():
        fetch(b + 1, 1 - slot)

    layout = _BlockLayout(off_ref, b)
    pos = _window_positions(sel_ref, tri_ref, layout)
    aff = aff_ref[...]
    tn = (((0,), (0,)), ((), ()))

    def contribution(chunk, window):
        gates = _slot_onehots(pos, aff, chunk)
        vals = window.reshape(N_EXPERTS * MOE_W, window.shape[-1])
        return lax.dot_general(gates, vals, tn, preferred_element_type=F32)

    for e in range(N_EXPERTS):
        copy(win_ref.at[slot], e, 0, sem.at[slot]).wait()
    out_ref[...] = x_ref[...] + contribution(0, win_ref[slot])

    def overflow(chunk, carry):
        for e in range(N_EXPERTS):
            row0 = pl.multiple_of(jnp.minimum(layout.window_row(e, chunk), rows - MOE_W), SLOT_ALIGN)
            copy(extra_ref, e, row0, sem_extra.at[0]).start()
        for e in range(N_EXPERTS):
            copy(extra_ref, e, 0, sem_extra.at[0]).wait()
        out_ref[...] += contribution(chunk, extra_ref[...])
        return carry

    lax.fori_loop(1, layout.n_chunks, overflow, 0)


def _combine(offs, sel, aff_t, x, ye):
    n_tok = x.shape[0]
    nb = n_tok // MOE_TB
    rows = ye.shape[1]
    grid_spec = pltpu.PrefetchScalarGridSpec(
        num_scalar_prefetch=1,
        grid=(nb,),
        in_specs=[pl.BlockSpec((N_EXPERTS, MOE_TB), lambda b, off: (0, b)),
                  pl.BlockSpec((N_EXPERTS, MOE_TB), lambda b, off: (0, b)),
                  pl.BlockSpec((MOE_TB, MOE_TB), lambda b, off: (0, 0)),
                  pl.BlockSpec((MOE_TB, D_MODEL), lambda b, off: (b, 0)),
                  pl.BlockSpec(memory_space=pl.ANY)],
        out_specs=pl.BlockSpec((MOE_TB, D_MODEL), lambda b, off: (b, 0)),
        scratch_shapes=[pltpu.VMEM((2, N_EXPERTS, MOE_W, D_MODEL), BF16),
                        pltpu.VMEM((N_EXPERTS, MOE_W, D_MODEL), BF16),
                        pltpu.SemaphoreType.DMA((2,)),
                        pltpu.SemaphoreType.DMA((1,))],
    )
    return pl.pallas_call(
        functools.partial(_combine_kernel, rows=rows),
        grid_spec=grid_spec,
        out_shape=jax.ShapeDtypeStruct((n_tok, D_MODEL), F32),
        compiler_params=_cparams(("arbitrary",)),
        name="combine",
    )(offs, sel, aff_t, _strict_upper(MOE_TB), x, ye)


def _moe(routed, w_gate, w_up, w_down, layer):
    plans = []
    for x, h, aff_t in routed:
        cap = CAPACITY_FACTOR * x.shape[0] // N_EXPERTS
        sel, offs = _select(aff_t, cap)
        plans.append((cap, sel, offs, _dispatch(offs, sel, h, cap)))
    yes = _expert_ffn([p[3] for p in plans], w_gate, w_up, w_down, layer, [p[0] for p in plans])
    return [_combine(offs, sel, aff_t, x, ye)
            for (x, _, aff_t), (_, sel, offs, _), ye in zip(routed, plans, yes)]


def _row(v):
    return v.reshape(1, -1).astype(F32)


def _router_split(w_router):
    wt = w_router.T.astype(F32)
    hi = wt.astype(BF16)
    lo = (wt - hi.astype(F32)).astype(BF16)
    return hi, lo


def _layer_ab(x, batch, seq, tabs, norm_g, w_in, qn_a, kn_a, qn_b, kn_b, w_out, norm_ffn, w_router):
    tab_a, tab_b = tabs
    scale = HEAD_DIM ** -0.5 * LOG2E
    perm_b = _pair_layout_perm(B_KV_HEADS, B_Q_HEADS // B_KV_HEADS)
    a3 = 3 * A_WIDTH
    bq = B_Q_HEADS * HEAD_DIM
    w_cols = np.concatenate([np.arange(a3), a3 + perm_b, np.arange(a3 + bq, w_in.shape[1])])
    w = w_in[:, w_cols].astype(BF16)
    gains = jnp.concatenate([
        jnp.repeat(qn_a, A_HEADS, axis=0).reshape(-1) * scale,
        jnp.repeat(kn_a, A_HEADS, axis=0).reshape(-1),
        jnp.ones((A_WIDTH,), F32),
        jnp.tile(qn_b, B_Q_HEADS) * scale,
        jnp.tile(kn_b, B_KV_HEADS),
        jnp.ones((B_KV_HEADS * HEAD_DIM,), F32)])
    n_a = A_GROUPS
    assert A_HEADS * HEAD_DIM == MXU_COLS
    plan = ([("qk", g, 0, 0, ROT_DIM // 2) for g in range(n_a)]
            + [("qk", n_a + g, 0, 0, ROT_DIM // 2) for g in range(n_a)]
            + [("v", 2 * n_a + g, 0, 0, 0) for g in range(n_a)]
            + [("qk", 3 * n_a, c * MXU_COLS, 1, HEAD_DIM // 4) for c in range(bq // MXU_COLS)]
            + [("qk", 3 * n_a + 1, 0, 1, HEAD_DIM // 4), ("v", 3 * n_a + 2, 0, 0, 0)])
    assert B_KV_HEADS * HEAD_DIM == MXU_COLS
    outs = _projection(
        x, _row(norm_g), w, _row(gains), (tab_a, tab_b), plan,
        (MXU_COLS,) * (2 * n_a) + (None,) * n_a + (bq, MXU_COLS, None), seq)
    qa, ka, va = outs[:n_a], outs[n_a:2 * n_a], outs[2 * n_a:3 * n_a]
    qb, kb, vb = outs[3 * n_a:]
    o_parts, lse_parts = [], []
    for gi, (window, dil) in enumerate(A_PATTERNS):
        o, lse = _attention(qa[gi], ka[gi], va[gi], batch=batch, n_keys=seq // dil, residues=dil, group=1,
                            n_pairs=A_HEADS // 2, band_r=window // (2 * dil), want_lse=True)
        o_parts.append(o)
        lse_parts.append(lse)
    (ob,) = _attention(qb, kb, vb, batch=batch, n_keys=seq, residues=1,
                       group=B_Q_HEADS // B_KV_HEADS, n_pairs=B_KV_HEADS // 2, band_r=None)
    a_out = A_HEADS * HEAD_DIM
    w_o = w_out[np.concatenate([np.arange(a_out), a_out + perm_b])].astype(BF16)
    wr_hi, wr_lo = _router_split(w_router)
    return _out_projection(o_parts, lse_parts, ob, x, w_o, _row(norm_ffn), wr_hi, wr_lo)


def _layer_c(x, batch, seq, tabs, norm_g, w_in, qn, kn, sink, w_out, norm_ffn, w_router):
    tab_a, _ = tabs
    scale = HEAD_DIM ** -0.5 * LOG2E
    group = C_Q_HEADS // C_KV_HEADS
    perm = _pair_layout_perm(C_KV_HEADS, group)
    cq = C_Q_HEADS * HEAD_DIM
    w_cols = np.concatenate([perm, np.arange(cq, w_in.shape[1])])
    w = w_in[:, w_cols].astype(BF16)
    gains = jnp.concatenate([jnp.tile(qn, C_Q_HEADS) * scale, jnp.tile(kn, C_KV_HEADS),
                             jnp.ones((C_KV_HEADS * HEAD_DIM,), F32)])
    plan = ([("qk", 0, c * MXU_COLS, 0, ROT_DIM // 2) for c in range(cq // MXU_COLS)]
            + [("qk", 1, 0, 0, ROT_DIM // 2), ("v", 2, 0, 0, 0)])
    assert C_KV_HEADS * HEAD_DIM == MXU_COLS
    q, k, v = _projection(x, _row(norm_g), w, _row(gains), (tab_a,), plan, (cq, MXU_COLS, None), seq)
    (o,) = _attention(q, k, v, batch=batch, n_keys=seq, residues=1,
                      group=group, n_pairs=C_KV_HEADS // 2, band_r=C_RADIUS, sink=sink.astype(F32))
    wr_hi, wr_lo = _router_split(w_router)
    return _out_projection([], [], o, x, w_out[perm].astype(BF16), _row(norm_ffn), wr_hi, wr_lo)


def _encode(xs, params):
    (norm_mix, norm_ffn, w_in_ab, qn_a, kn_a, qn_b, kn_b, w_out_ab, w_in_c, qn_c, kn_c, sink_c, w_out_c,
     w_router, w_gate, w_up, w_down) = params
    shapes = [x.shape[:2] for x in xs]
    tabs = [_rope_tables(seq) for _, seq in shapes]
    xts = [x.reshape(batch * seq, D_MODEL) for x, (batch, seq) in zip(xs, shapes)]
    for layer in range(norm_mix.shape[0]):
        j = layer // 2
        routed = []
        for xt, (batch, seq), tab in zip(xts, shapes, tabs):
            if layer % 2 == 0:
                routed.append(_layer_ab(xt, batch, seq, tab, norm_mix[layer], w_in_ab[j], qn_a[j], kn_a[j],
                                        qn_b[j], kn_b[j], w_out_ab[j], norm_ffn[layer], w_router[layer]))
            else:
                routed.append(_layer_c(xt, batch, seq, tab, norm_mix[layer], w_in_c[j], qn_c[j], kn_c[j],
                                       sink_c[j], w_out_c[j], norm_ffn[layer], w_router[layer]))
        xts = _moe(routed, w_gate, w_up, w_down, layer)
    return tuple(xt.reshape(batch, seq, D_MODEL) for xt, (batch, seq) in zip(xts, shapes))


def kernel(x_prompt, x_sample, norm_mix, norm_ffn, w_in_ab, qn_a, kn_a, qn_b, kn_b, w_out_ab, w_in_c, qn_c,
           kn_c, sink_c, w_out_c, w_router, w_gate, w_up, w_down):
    params = (norm_mix, norm_ffn, w_in_ab, qn_a, kn_a, qn_b, kn_b, w_out_ab, w_in_c, qn_c, kn_c, sink_c,
              w_out_c, w_router, w_gate, w_up, w_down)
    return _encode((x_prompt, x_sample), params)
```

```python
import functools
import math

import jax
import jax.numpy as jnp
import numpy as np
from jax import lax
from jax.experimental import pallas as pl
from jax.experimental.pallas import tpu as pltpu

F32 = jnp.float32
BF16 = jnp.bfloat16
I32 = jnp.int32

D_MODEL = 1024
HEAD_DIM = 64
LANES = 128
MXU_COLS = 256
GRID_W = 64
ROT_DIM = HEAD_DIM // 4
ROPE_THETA = 500000.0
AXIAL_THETA = 10000.0
A_PATTERNS = ((128, 1), (512, 4), (2048, 16))
A_HEADS = 4
A_GROUPS = len(A_PATTERNS)
A_WIDTH = A_GROUPS * A_HEADS * HEAD_DIM
B_Q_HEADS = 12
B_KV_HEADS = 4
C_Q_HEADS = 16
C_KV_HEADS = 4
C_RADIUS = 128
N_EXPERTS = 16
EXPERT_FF = 1024
CAPACITY_FACTOR = 2
NEG_INF = -1e30
EPS = 1e-6

VMEM_LIMIT = 56 * 1024 * 1024

PROJ_TM = 512
DENSE_TQ = 256
BAND_TQ = 256
BAND_COLS = 1024
ACC_ROWS = LANES + 16
DENSE_TK = 512
MOE_TB = 256
MOE_W = 64
FFN_TM = 512


def _cparams(sem):
    return pltpu.CompilerParams(dimension_semantics=sem, vmem_limit_bytes=VMEM_LIMIT)


def _head_block_diag():
    idx = np.arange(MXU_COLS) // HEAD_DIM
    return jnp.asarray((idx[:, None] == idx[None, :]).astype(np.float32), dtype=BF16)


def _rope_tables(max_len):
    pos = jnp.arange(max_len, dtype=F32)
    j = np.arange(LANES) % HEAD_DIM

    def angles(p, dim, theta):
        exps = jnp.arange(0, dim, 2, dtype=F32) / dim
        inv = jnp.power(jnp.float32(theta), -exps)
        return p[:, None] * inv[None, :]

    half = ROT_DIM // 2
    ang = angles(pos, ROT_DIM, ROPE_THETA)
    cos, sin = jnp.cos(ang), jnp.sin(ang)
    fa = np.where(j < half, j, np.where(j < ROT_DIM, j - half, 0))
    cos_l, sin_l = cos[:, fa], sin[:, fa]
    lo = jnp.asarray(j < half)[None, :]
    hi = jnp.asarray((j >= half) & (j < ROT_DIM))[None, :]
    tab_a = jnp.stack([jnp.where(lo | hi, cos_l, 1.0),
                       jnp.where(lo, -sin_l, 0.0),
                       jnp.where(hi, sin_l, 0.0)])
    hb = HEAD_DIM // 2
    qb = hb // 2
    t = jnp.arange(max_len)
    ang_r = angles((t // GRID_W).astype(F32), hb, AXIAL_THETA)
    ang_c = angles((t % GRID_W).astype(F32), hb, AXIAL_THETA)
    fb = j % qb
    is_col = jnp.asarray(j >= hb)[None, :]
    ang_l = jnp.where(is_col, ang_c[:, fb], ang_r[:, fb])
    cos_b, sin_b = jnp.cos(ang_l), jnp.sin(ang_l)
    first = jnp.asarray((j % hb) < qb)[None, :]
    tab_b = jnp.stack([cos_b, jnp.where(first, -sin_b, 0.0), jnp.where(first, 0.0, sin_b)])
    return tab_a.astype(F32), tab_b.astype(F32)


def _pair_layout_perm(n_kv, group):
    cols = []
    for p in range(n_kv // 2):
        for g in range(group):
            for par in range(2):
                h = (2 * p + par) * group + g
                cols.extend(range(h * HEAD_DIM, (h + 1) * HEAD_DIM))
    return np.asarray(cols, dtype=np.int32)


def _proj_kernel(x_ref, g_ref, w_ref, s_ref, gain_ref, *rest, plan, n_tabs):
    tab_refs = rest[:n_tabs]
    out_refs = rest[n_tabs:]
    x = x_ref[...]
    ms = jnp.mean(x * x, axis=1, keepdims=True)
    xn = ((x * lax.rsqrt(ms + EPS)) * g_ref[...]).astype(BF16)
    def project(c):
        return jnp.dot(xn, w_ref[:, c * MXU_COLS:(c + 1) * MXU_COLS], preferred_element_type=F32)

    ahead = project(0)
    for c, (kind, out_i, out_col, tab_i, shift) in enumerate(plan):
        acc = ahead
        if c + 1 < len(plan):
            ahead = project(c + 1)
        o_ref = out_refs[out_i]
        if kind == "v":
            o_ref[...] = acc.T.astype(BF16)
            continue
        ss = jnp.dot((acc * acc).astype(BF16), s_ref[...], preferred_element_type=F32)
        y = (acc * lax.rsqrt(ss * (1.0 / HEAD_DIM) + EPS)) * gain_ref[:, c * MXU_COLS:(c + 1) * MXU_COLS]
        tab = tab_refs[tab_i]
        t0 = jnp.concatenate([tab[0], tab[0]], axis=1)
        t1 = jnp.concatenate([tab[1], tab[1]], axis=1)
        t2 = jnp.concatenate([tab[2], tab[2]], axis=1)
        y = y * t0 + pltpu.roll(y, MXU_COLS - shift, 1) * t1 + pltpu.roll(y, shift, 1) * t2
        o_ref[:, out_col:out_col + MXU_COLS] = y.astype(BF16)


def _projection(x, norm_g, w, gains, tabs, plan, out_widths, seq_len):
    t_tokens = x.shape[0]
    tm = PROJ_TM
    n_in = w.shape[1]
    blocks_per_seq = seq_len // tm
    in_specs = [
        pl.BlockSpec((tm, D_MODEL), lambda i: (i, 0)),
        pl.BlockSpec((1, D_MODEL), lambda i: (0, 0)),
        pl.BlockSpec((D_MODEL, n_in), lambda i: (0, 0)),
        pl.BlockSpec((MXU_COLS, MXU_COLS), lambda i: (0, 0)),
        pl.BlockSpec((1, n_in), lambda i: (0, 0)),
    ] + [pl.BlockSpec((3, tm, LANES), lambda i: (0, i % blocks_per_seq, 0)) for _ in tabs]
    out_specs, out_shape = [], []
    for wd in out_widths:
        if wd is None:
            out_specs.append(pl.BlockSpec((None, MXU_COLS, tm),
                                          lambda i: (i // blocks_per_seq, 0, i % blocks_per_seq)))
            out_shape.append(jax.ShapeDtypeStruct((t_tokens // seq_len, MXU_COLS, seq_len), BF16))
        else:
            out_specs.append(pl.BlockSpec((tm, wd), lambda i: (i, 0)))
            out_shape.append(jax.ShapeDtypeStruct((t_tokens, wd), BF16))
    return pl.pallas_call(
        functools.partial(_proj_kernel, plan=tuple(plan), n_tabs=len(tabs)),
        grid=(t_tokens // tm,),
        in_specs=in_specs, out_specs=out_specs, out_shape=out_shape,
        compiler_params=_cparams(("parallel",)),
        name="projection",
    )(x, norm_g, w, _head_block_diag(), gains, *tabs)


LOG2E = math.log2(math.e)
LN2 = math.log(2.0)
_NT = (((1,), (1,)), ((), ()))


def _attn_kernel(*refs, group, n_pairs, tq, tk, n_keys, band_r, has_sink, want_lse):
    refs = list(refs)
    sink_ref = refs.pop(0) if has_sink else None
    q_ref, k_ref, v_ref, o_ref = refs[:4]
    refs = refs[4:]
    lse_ref = refs.pop(0) if want_lse else None
    scratch = refs
    m_cols = group * tq
    n_chain = 2 * n_pairs
    i = pl.program_id(2)
    lane = lax.broadcasted_iota(I32, (1, LANES), 1)
    low = lane < HEAD_DIM
    half_mask = [jnp.where(low, 1.0, 0.0).astype(BF16), jnp.where(low, 0.0, 1.0).astype(BF16)]

    q_masked = []
    for p in range(n_pairs):
        q_p = jnp.concatenate(
            [q_ref[:, (p * group + g) * LANES:(p * group + g + 1) * LANES] for g in range(group)], axis=0)
        q_masked += [q_p * half_mask[par] for par in range(2)]

    sub = lax.broadcasted_iota(I32, (ACC_ROWS, m_cols), 0)
    if has_sink:
        acc0 = jnp.where(sub >= LANES, 1.0, 0.0).astype(F32)
        m0 = []
        for c in range(n_chain):
            m0.append(jnp.concatenate(
                [jnp.full((1, tq), sink_ref[c * group + g] * LOG2E, F32) for g in range(group)], axis=1))
    else:
        acc0 = jnp.zeros((ACC_ROWS, m_cols), F32)
        m0 = [jnp.full((1, m_cols), NEG_INF, F32) for _ in range(n_chain)]

    def values_t(p, k0, width):
        ones = jnp.ones((ACC_ROWS - LANES, width), BF16)
        return jnp.concatenate([v_ref[p * LANES:(p + 1) * LANES, pl.ds(k0, width)], ones], axis=0)

    def softmax_pv(s_t, v_t, m_prev, acc_prev):
        m_new = jnp.maximum(m_prev, jnp.max(s_t, axis=0, keepdims=True))
        alpha = jnp.exp2(m_prev - m_new)
        p_t = jnp.exp2(s_t - m_new).astype(BF16)
        return m_new, alpha * acc_prev + jnp.dot(v_t, p_t, preferred_element_type=F32)

    if band_r is None:
        qm_ref, acc_ref, s_even, s_odd = scratch
        for c in range(n_chain):
            qm_ref[c] = q_masked[c]
            acc_ref[c] = acc0

        def scores(kb, s_ref):
            k0 = pl.multiple_of(kb * tk, tk)
            for p in range(n_pairs):
                kt = k_ref[pl.ds(k0, tk), p * LANES:(p + 1) * LANES]
                for par in range(2):
                    c = 2 * p + par
                    s_ref[c] = lax.dot_general(kt, qm_ref[c], _NT, preferred_element_type=F32)

        def consume(kb, s_ref, m_prev):
            k0 = pl.multiple_of(kb * tk, tk)
            m_next = []
            for p in range(n_pairs):
                v_t = values_t(p, k0, tk)
                for par in range(2):
                    c = 2 * p + par
                    m_new, acc_ref[c] = softmax_pv(s_ref[c], v_t, m_prev[c], acc_ref[c])
                    m_next.append(m_new)
            return m_next

        n_blocks = n_keys // tk
        scores(0, s_even)

        def body(j, carry):
            m = list(carry)
            scores(2 * j + 1, s_odd)
            m = consume(2 * j, s_even, m)
            scores(2 * j + 2, s_even)
            m = consume(2 * j + 1, s_odd, m)
            return tuple(m)

        m_fin = list(lax.fori_loop(0, n_blocks // 2 - 1, body, tuple(m0)))
        scores(n_blocks - 1, s_odd)
        m_fin = consume(n_blocks - 2, s_even, m_fin)
        m_fin = consume(n_blocks - 1, s_odd, m_fin)
        acc_fin = [acc_ref[c] for c in range(n_chain)]
    else:
        halo = -(-band_r // LANES) * LANES
        width = min(tq + 2 * halo, n_keys)
        ws = pl.multiple_of(jnp.clip(i * tq - halo, 0, n_keys - width), LANES)
        kpos = ws + lax.broadcasted_iota(I32, (width, 1), 0)
        col = lax.broadcasted_iota(I32, (1, tq), 1)
        qpos = i * tq + jnp.concatenate([col] * group, axis=1)
        mask = jnp.abs(kpos - qpos) <= band_r
        s_all = []
        for p in range(n_pairs):
            kt = k_ref[pl.ds(ws, width), p * LANES:(p + 1) * LANES]
            s_all += [lax.dot_general(kt, q_masked[2 * p + par], _NT, preferred_element_type=F32)
                      for par in range(2)]
        m_fin, acc_fin = [], []
        for c in range(n_chain):
            m_new, acc_new = softmax_pv(jnp.where(mask, s_all[c], NEG_INF), values_t(c // 2, ws, width),
                                        m0[c], acc0)
            m_fin.append(m_new)
            acc_fin.append(acc_new)

    top = lax.broadcasted_iota(I32, (LANES, tq), 0) < HEAD_DIM
    for p in range(n_pairs):
        even, odd = acc_fin[2 * p], acc_fin[2 * p + 1]
        for g in range(group):
            cols = slice(g * tq, (g + 1) * tq)
            num = jnp.where(top, even[:LANES, cols], odd[:LANES, cols])
            den = jnp.where(top, even[LANES:LANES + 1, cols], odd[LANES:LANES + 1, cols])
            out_cols = slice((p * group + g) * LANES, (p * group + g + 1) * LANES)
            o_ref[:, out_cols] = (num / den).T.astype(o_ref.dtype)
            if want_lse:
                m_sel = jnp.where(top, m_fin[2 * p][:, cols], m_fin[2 * p + 1][:, cols])
                lse_ref[:, out_cols] = (m_sel * LN2 + jnp.log(den)).T


def _attention(q, k, v_t, *, batch, n_keys, residues, group, n_pairs, band_r, sink=None, want_lse=False):
    dense = band_r is None
    tq = DENSE_TQ if dense else min(BAND_TQ, BAND_COLS // group, n_keys)
    wq = n_pairs * group * LANES
    wk = n_pairs * LANES
    q3 = q.reshape(batch, n_keys, residues * wq)
    k3 = k.reshape(batch, n_keys, residues * wk)
    v4 = v_t.reshape(batch, wk, n_keys, residues).transpose(0, 3, 1, 2)

    in_specs = []
    args = []
    if sink is not None:
        in_specs.append(pl.BlockSpec(memory_space=pltpu.SMEM))
        args.append(sink)
    in_specs += [
        pl.BlockSpec((None, tq, wq), lambda b, r, i: (b, i, r)),
        pl.BlockSpec((None, n_keys, wk), lambda b, r, i: (b, 0, r)),
        pl.BlockSpec((None, None, wk, n_keys), lambda b, r, i: (b, r, 0, 0)),
    ]
    args += [q3, k3, v4]
    out_specs = [pl.BlockSpec((None, tq, wq), lambda b, r, i: (b, i, r))]
    out_shape = [jax.ShapeDtypeStruct((batch, n_keys, residues * wq), BF16)]
    if want_lse:
        out_specs.append(pl.BlockSpec((None, tq, wq), lambda b, r, i: (b, i, r)))
        out_shape.append(jax.ShapeDtypeStruct((batch, n_keys, residues * wq), F32))
    m_cols = group * tq
    scratch = []
    if dense:
        scratch = [pltpu.VMEM((2 * n_pairs, m_cols, LANES), BF16),
                   pltpu.VMEM((2 * n_pairs, ACC_ROWS, m_cols), F32)]
        scratch += [pltpu.VMEM((2 * n_pairs, DENSE_TK, m_cols), F32) for _ in range(2)]
    outs = pl.pallas_call(
        functools.partial(_attn_kernel, group=group, n_pairs=n_pairs, tq=tq, tk=DENSE_TK, n_keys=n_keys,
                          band_r=band_r, has_sink=sink is not None, want_lse=want_lse),
        grid=(batch, residues, n_keys // tq),
        in_specs=in_specs, out_specs=out_specs, out_shape=out_shape,
        scratch_shapes=scratch,
        compiler_params=_cparams(("parallel", "parallel", "arbitrary")),
        name="attention_dense" if dense else "attention_band",
    )(*args)
    return [o.reshape(batch * n_keys * residues, wq) for o in outs]


def _outproj_kernel(*refs, n_merge):
    refs = list(refs)
    if n_merge:
        o_parts = refs[:n_merge]
        lse_parts = refs[n_merge:2 * n_merge]
        refs = refs[2 * n_merge:]
    o_rest, x_ref, w_ref, g_ref, wrh_ref, wrl_ref, xo_ref, h_ref, aff_ref = refs
    tm = x_ref.shape[0]
    halves = [slice(0, tm // 2), slice(tm // 2, tm)]
    accs = []
    for rows in halves:
        acc = x_ref[rows, :]
        k0 = 0
        if n_merge:
            lses = [r[rows, :] for r in lse_parts]
            m = functools.reduce(jnp.maximum, lses)
            ws = [jnp.exp(l - m) for l in lses]
            num = sum(wgt * r[rows, :].astype(F32) for wgt, r in zip(ws, o_parts))
            oa = (num / sum(ws)).astype(BF16)
            k0 = oa.shape[1]
            acc = acc + jnp.dot(oa, w_ref[:k0, :], preferred_element_type=F32)
        accs.append(acc + jnp.dot(o_rest[rows, :], w_ref[k0:, :], preferred_element_type=F32))
    for rows, acc in zip(halves, accs):
        xo_ref[rows, :] = acc
        ms = jnp.mean(acc * acc, axis=1, keepdims=True)
        h = (acc * lax.rsqrt(ms + EPS)) * g_ref[...]
        h_hi = h.astype(BF16)
        h_ref[rows, :] = h_hi
        h_lo = (h - h_hi.astype(F32)).astype(BF16)
        logits = (lax.dot_general(wrh_ref[...], h_hi, _NT, preferred_element_type=F32)
                  + lax.dot_general(wrl_ref[...], h_hi, _NT, preferred_element_type=F32)
                  + lax.dot_general(wrh_ref[...], h_lo, _NT, preferred_element_type=F32))
        mx = jnp.max(logits, axis=0, keepdims=True)
        e = jnp.exp(logits - mx)
        aff_ref[:, rows] = e / jnp.sum(e, axis=0, keepdims=True)


def _out_projection(o_merge, lse_merge, o_rest, x, w_out, norm_g, wr_hi, wr_lo):
    t_tokens = x.shape[0]
    tm = PROJ_TM
    n_merge = len(o_merge)
    row = lambda i: (i, 0)
    fixed = lambda i: (0, 0)
    in_specs = ([pl.BlockSpec((tm, o.shape[1]), row) for o in o_merge]
                + [pl.BlockSpec((tm, l.shape[1]), row) for l in lse_merge]
                + [pl.BlockSpec((tm, o_rest.shape[1]), row),
                   pl.BlockSpec((tm, D_MODEL), row),
                   pl.BlockSpec(w_out.shape, fixed),
                   pl.BlockSpec((1, D_MODEL), fixed),
                   pl.BlockSpec(wr_hi.shape, fixed),
                   pl.BlockSpec(wr_lo.shape, fixed)])
    out_specs = [pl.BlockSpec((tm, D_MODEL), row),
                 pl.BlockSpec((tm, D_MODEL), row),
                 pl.BlockSpec((N_EXPERTS, tm), lambda i: (0, i))]
    out_shape = [jax.ShapeDtypeStruct((t_tokens, D_MODEL), F32),
                 jax.ShapeDtypeStruct((t_tokens, D_MODEL), BF16),
                 jax.ShapeDtypeStruct((N_EXPERTS, t_tokens), F32)]
    return pl.pallas_call(
        functools.partial(_outproj_kernel, n_merge=n_merge),
        grid=(t_tokens // tm,),
        in_specs=in_specs, out_specs=out_specs, out_shape=out_shape,
        compiler_params=_cparams(("parallel",)),
        name="out_projection",
    )(*o_merge, *lse_merge, o_rest, x, w_out, norm_g, wr_hi, wr_lo)


def _select_kernel(aff_ref, ind_ref, tri_ref, sel_ref, off_ref, *, cap):
    aff = aff_ref[...]
    n_tok = aff.shape[1]
    bits = pltpu.bitcast(aff, I32)

    def count(pred):
        return jnp.sum(jnp.where(pred, 1.0, 0.0), axis=1, keepdims=True)

    def value_step(j, thr):
        cand = thr | lax.shift_left(jnp.int32(1), 30 - j)
        return jnp.where(count(bits >= cand) >= cap, cand, thr)

    thr = lax.fori_loop(0, 31, value_step, jnp.zeros((N_EXPERTS, 1), I32))
    gt = bits > thr
    eq = bits == thr
    need = cap - count(gt)
    idx = lax.broadcasted_iota(I32, aff.shape, 1)
    idx_bits = int(math.log2(n_tok))

    def index_step(j, bound):
        cand = bound | lax.shift_left(jnp.int32(1), idx_bits - j)
        return jnp.where(count(eq & (idx < cand)) <= need, cand, bound)

    bound = lax.fori_loop(0, idx_bits + 1, index_step, jnp.zeros((N_EXPERTS, 1), I32))
    sel = jnp.where(gt | (eq & (idx < bound)), 1.0, 0.0).astype(BF16)
    sel_ref[...] = sel
    counts = jnp.dot(sel, ind_ref[...], preferred_element_type=F32)
    offs = jnp.dot(counts.astype(BF16), tri_ref[...], preferred_element_type=F32)
    off_ref[...] = offs.astype(I32)


def _strict_upper(n):
    return jnp.asarray(np.triu(np.ones((n, n), np.float32), k=1), dtype=BF16)


def _select(aff_t, cap):
    n_tok = aff_t.shape[1]
    ind = np.zeros((n_tok, LANES), np.float32)
    ind[np.arange(n_tok), np.arange(n_tok) // MOE_TB] = 1.0
    return pl.pallas_call(
        functools.partial(_select_kernel, cap=cap),
        out_shape=[jax.ShapeDtypeStruct((N_EXPERTS, n_tok), BF16),
                   jax.ShapeDtypeStruct((N_EXPERTS, LANES), I32)],
        compiler_params=pltpu.CompilerParams(vmem_limit_bytes=VMEM_LIMIT),
        name="select",
    )(aff_t, jnp.asarray(ind, dtype=BF16), _strict_upper(LANES))


SLOT_ALIGN = 16
ALIGN_BITS = 4
CHUNK_BITS = 6
NOT_SELECTED = -1e6


class _BlockLayout:
    def __init__(self, off_ref, b):
        self.start = [off_ref[e, b] for e in range(N_EXPERTS)]
        self.end = [off_ref[e, b + 1] for e in range(N_EXPERTS)]
        def floor_tile(v):
            return lax.shift_left(lax.shift_right_logical(v, ALIGN_BITS), ALIGN_BITS)

        self.base = [floor_tile(s) for s in self.start]
        self.shift = [s - a for s, a in zip(self.start, self.base)]
        span = [en - a for en, a in zip(self.end, self.base)]
        self.n_chunks = functools.reduce(
            jnp.maximum, [lax.shift_right_logical(sp, CHUNK_BITS) for sp in span]) + 1
        tail = [floor_tile(sp) for sp in span]
        self.tail_chunk = [lax.shift_right_logical(t, CHUNK_BITS) for t in tail]
        self.tail_row = [pl.multiple_of(t & (MOE_W - 1), SLOT_ALIGN) for t in tail]

    def window_row(self, e, chunk):
        return pl.multiple_of(self.base[e] + chunk * MOE_W, SLOT_ALIGN)


def _slot_onehots(pos, weight, chunk):
    n_tok = pos.shape[1]
    slot = (lax.broadcasted_iota(I32, (MOE_W, n_tok), 0) + chunk * MOE_W).astype(F32)
    parts = []
    for e in range(N_EXPERTS):
        hit = pos[e:e + 1, :] == slot
        val = 1.0 if weight is None else weight[e:e + 1, :]
        parts.append(jnp.where(hit, val, 0.0).astype(BF16))
    return jnp.concatenate(parts, axis=0)


def _window_positions(sel_ref, tri_ref, layout):
    sel = sel_ref[...]
    rank = jnp.dot(sel, tri_ref[...], preferred_element_type=F32)
    shift = jnp.concatenate([jnp.full((1, 1), s, I32) for s in layout.shift], axis=0).astype(F32)
    return jnp.where(sel > 0, rank + shift, NOT_SELECTED)


def _dispatch_kernel(off_ref, sel_ref, h_ref, tri_ref, xe_ref, stage_ref, extra_ref, tail_ref, zero_ref, sem,
                     sem_extra):
    b = pl.program_id(0)
    nb = pl.num_programs(0)
    slot = b % 2
    layout = _BlockLayout(off_ref, b)
    pos = _window_positions(sel_ref, tri_ref, layout)
    h = h_ref[...]

    def rows_for(chunk):
        return jnp.dot(_slot_onehots(pos, None, chunk), h, preferred_element_type=F32)

    def copy(src, e, row0, s):
        return pltpu.make_async_copy(src.at[e], xe_ref.at[e, pl.ds(row0, MOE_W)], s)

    @pl.when(b == 0)
    def _():
        tail_ref[...] = jnp.zeros(tail_ref.shape, BF16)
        zero_ref[...] = jnp.zeros(zero_ref.shape, BF16)
        pad = zero_ref.shape[0]
        fills = [pltpu.make_async_copy(zero_ref, xe_ref.at[e, pl.ds(xe_ref.shape[1] - pad, pad)], sem_extra.at[0])
                 for e in range(N_EXPERTS)]
        for f in fills:
            f.start()
        for f in fills:
            f.wait()

    rows0 = rows_for(0)

    for e in range(N_EXPERTS):
        r0 = e * MOE_W
        first = rows0[r0:r0 + SLOT_ALIGN] + tail_ref[e].astype(F32)
        stage_ref[slot, e, :SLOT_ALIGN] = first.astype(BF16)
        stage_ref[slot, e, SLOT_ALIGN:] = rows0[r0 + SLOT_ALIGN:r0 + MOE_W].astype(BF16)
    for e in range(N_EXPERTS):
        tile = stage_ref[slot, e, pl.ds(layout.tail_row[e], SLOT_ALIGN), :]
        tail_ref[e] = jnp.where(layout.tail_chunk[e] == 0, tile, tail_ref[e])

    @pl.when(b > 0)
    def _():
        for e in range(N_EXPERTS):
            copy(stage_ref.at[1 - slot], e, 0, sem.at[1 - slot]).wait()

    for e in range(N_EXPERTS):
        copy(stage_ref.at[slot], e, layout.window_row(e, 0), sem.at[slot]).start()

    def overflow(chunk, carry):
        more = rows_for(chunk).astype(BF16)
        for e in range(N_EXPERTS):
            extra_ref[e] = more[e * MOE_W:(e + 1) * MOE_W]
        for e in range(N_EXPERTS):
            copy(extra_ref, e, layout.window_row(e, chunk), sem_extra.at[0]).start()
        for e in range(N_EXPERTS):
            @pl.when(layout.tail_chunk[e] == chunk)
            def _():
                tail_ref[e] = extra_ref[e, pl.ds(layout.tail_row[e], SLOT_ALIGN), :]
        for e in range(N_EXPERTS):
            copy(extra_ref, e, 0, sem_extra.at[0]).wait()
        return carry

    lax.fori_loop(1, layout.n_chunks, overflow, 0)

    @pl.when(b == nb - 1)
    def _():
        for e in range(N_EXPERTS):
            copy(stage_ref.at[slot], e, 0, sem.at[slot]).wait()


def _slot_rows(cap):
    pad = ((SLOT_ALIGN + MOE_TB) // MOE_W + 1) * MOE_W
    return cap + -(-pad // FFN_TM) * FFN_TM


def _dispatch(offs, sel, h, cap):
    n_tok = h.shape[0]
    nb = n_tok // MOE_TB
    rows = _slot_rows(cap)
    grid_spec = pltpu.PrefetchScalarGridSpec(
        num_scalar_prefetch=1,
        grid=(nb,),
        in_specs=[pl.BlockSpec((N_EXPERTS, MOE_TB), lambda b, off: (0, b)),
                  pl.BlockSpec((MOE_TB, D_MODEL), lambda b, off: (b, 0)),
                  pl.BlockSpec((MOE_TB, MOE_TB), lambda b, off: (0, 0))],
        out_specs=pl.BlockSpec(memory_space=pl.ANY),
        scratch_shapes=[pltpu.VMEM((2, N_EXPERTS, MOE_W, D_MODEL), BF16),
                        pltpu.VMEM((N_EXPERTS, MOE_W, D_MODEL), BF16),
                        pltpu.VMEM((N_EXPERTS, SLOT_ALIGN, D_MODEL), BF16),
                        pltpu.VMEM((rows - cap, D_MODEL), BF16),
                        pltpu.SemaphoreType.DMA((2,)),
                        pltpu.SemaphoreType.DMA((1,))],
    )
    return pl.pallas_call(
        _dispatch_kernel,
        grid_spec=grid_spec,
        out_shape=jax.ShapeDtypeStruct((N_EXPERTS, rows, D_MODEL), BF16),
        compiler_params=_cparams(("arbitrary",)),
        name="dispatch",
    )(offs, sel, h, _strict_upper(MOE_TB))


class _FfnSchedule:
    def __init__(self, caps, rows):
        self.n_real = [cap // FFN_TM for cap in caps]
        self.n_zero = [r // FFN_TM - n for r, n in zip(rows, self.n_real)]
        self.zero_at, step = [], 0
        for n in self.n_zero:
            self.zero_at.append(step)
            step += n
        self.first_real = step
        self.real_at = []
        for n in self.n_real:
            self.real_at.append(step)
            step += n
        self.n_steps = step

    def in_tile(self, g, j):
        return jnp.clip(j - self.real_at[g], 0, self.n_real[g] - 1)

    def out_tile(self, g, j):
        pad = self.n_real[g] + jnp.clip(j - self.zero_at[g], 0, self.n_zero[g] - 1)
        return jnp.where(j < self.real_at[g], pad, self.in_tile(g, j))


def _ffn_kernel(*refs, sched):
    n_groups = len(sched.n_real)
    xe_refs = refs[:n_groups]
    wg_ref, wu_ref, wd_ref = refs[n_groups:n_groups + 3]
    ye_refs = refs[n_groups + 3:2 * n_groups + 3]
    wg_bf, wu_bf, wd_bf = refs[2 * n_groups + 3:]
    j = pl.program_id(1)

    @pl.when(j == 0)
    def _():
        wg_bf[...] = wg_ref[...].astype(BF16)
        wu_bf[...] = wu_ref[...].astype(BF16)
        wd_bf[...] = wd_ref[...].astype(BF16)

    for g in range(n_groups):
        @pl.when((j >= sched.zero_at[g]) & (j < sched.zero_at[g] + sched.n_zero[g]))
        def _():
            ye_refs[g][...] = jnp.zeros(ye_refs[g].shape, BF16)

        @pl.when((j >= sched.real_at[g]) & (j < sched.real_at[g] + sched.n_real[g]))
        def _():
            x = xe_refs[g][...]
            gate = jnp.dot(x, wg_bf[...], preferred_element_type=F32)
            up = jnp.dot(x, wu_bf[...], preferred_element_type=F32)
            hid = (gate * jax.nn.sigmoid(gate) * up).astype(BF16)
            ye_refs[g][...] = jnp.dot(hid, wd_bf[...], preferred_element_type=F32).astype(BF16)


def _expert_ffn(xes, w_gate, w_up, w_down, layer, caps):
    sched = _FfnSchedule(caps, [xe.shape[1] for xe in xes])

    def weight_index(e, j):
        return (layer, jnp.minimum(e + (j >= sched.first_real).astype(I32), N_EXPERTS - 1), 0, 0)

    wspec = lambda shape: pl.BlockSpec((None, None) + shape, weight_index)

    def tile_spec(g, tile_fn):
        return pl.BlockSpec((None, FFN_TM, D_MODEL), lambda e, j: (e, tile_fn(g, j), 0))

    n_groups = len(xes)
    return pl.pallas_call(
        functools.partial(_ffn_kernel, sched=sched),
        grid=(N_EXPERTS, sched.n_steps),
        in_specs=([tile_spec(g, sched.in_tile) for g in range(n_groups)]
                  + [wspec((D_MODEL, EXPERT_FF)), wspec((D_MODEL, EXPERT_FF)), wspec((EXPERT_FF, D_MODEL))]),
        out_specs=[tile_spec(g, sched.out_tile) for g in range(n_groups)],
        out_shape=[jax.ShapeDtypeStruct(xe.shape, BF16) for xe in xes],
        scratch_shapes=[pltpu.VMEM((D_MODEL, EXPERT_FF), BF16), pltpu.VMEM((D_MODEL, EXPERT_FF), BF16),
                        pltpu.VMEM((EXPERT_FF, D_MODEL), BF16)],
        compiler_params=_cparams(("arbitrary", "arbitrary")),
        name="expert_ffn",
    )(*xes, w_gate, w_up, w_down)


def _combine_kernel(off_ref, sel_ref, aff_ref, tri_ref, x_ref, ye_ref, out_ref, win_ref, extra_ref, sem, sem_extra,
                    *, rows):
    b = pl.program_id(0)
    nb = pl.num_programs(0)
    slot = b % 2

    def copy(dst, e, row0, s):
        return pltpu.make_async_copy(ye_ref.at[e, pl.ds(row0, MOE_W)], dst.at[e], s)

    def fetch(blk, dst_slot):
        ahead = _BlockLayout(off_ref, blk)
        for e in range(N_EXPERTS):
            copy(win_ref.at[dst_slot], e, ahead.window_row(e, 0), sem.at[dst_slot]).start()

    @pl.when(b == 0)
    def _():
        fetch(0, 0)

    @pl.when(b + 1 < nb)
    def _():
        fetch(b + 1, 1 - slot)

    layout = _BlockLayout(off_ref, b)
    pos = _window_positions(sel_ref, tri_ref, layout)
    aff = aff_ref[...]
    tn = (((0,), (0,)), ((), ()))

    def contribution(chunk, window):
        gates = _slot_onehots(pos, aff, chunk)
        vals = window.reshape(N_EXPERTS * MOE_W, window.shape[-1])
        return lax.dot_general(gates, vals, tn, preferred_element_type=F32)

    for e in range(N_EXPERTS):
        copy(win_ref.at[slot], e, 0, sem.at[slot]).wait()
    out_ref[...] = x_ref[...] + contribution(0, win_ref[slot])

    def overflow(chunk, carry):
        for e in range(N_EXPERTS):
            row0 = pl.multiple_of(jnp.minimum(layout.window_row(e, chunk), rows - MOE_W), SLOT_ALIGN)
            copy(extra_ref, e, row0, sem_extra.at[0]).start()
        for e in range(N_EXPERTS):
            copy(extra_ref, e, 0, sem_extra.at[0]).wait()
        out_ref[...] += contribution(chunk, extra_ref[...])
        return carry

    lax.fori_loop(1, layout.n_chunks, overflow, 0)


def _combine(offs, sel, aff_t, x, ye):
    n_tok = x.shape[0]
    nb = n_tok // MOE_TB
    rows = ye.shape[1]
    grid_spec = pltpu.PrefetchScalarGridSpec(
        num_scalar_prefetch=1,
        grid=(nb,),
        in_specs=[pl.BlockSpec((N_EXPERTS, MOE_TB), lambda b, off: (0, b)),
                  pl.BlockSpec((N_EXPERTS, MOE_TB), lambda b, off: (0, b)),
                  pl.BlockSpec((MOE_TB, MOE_TB), lambda b, off: (0, 0)),
                  pl.BlockSpec((MOE_TB, D_MODEL), lambda b, off: (b, 0)),
                  pl.BlockSpec(memory_space=pl.ANY)],
        out_specs=pl.BlockSpec((MOE_TB, D_MODEL), lambda b, off: (b, 0)),
        scratch_shapes=[pltpu.VMEM((2, N_EXPERTS, MOE_W, D_MODEL), BF16),
                        pltpu.VMEM((N_EXPERTS, MOE_W, D_MODEL), BF16),
                        pltpu.SemaphoreType.DMA((2,)),
                        pltpu.SemaphoreType.DMA((1,))],
    )
    return pl.pallas_call(
        functools.partial(_combine_kernel, rows=rows),
        grid_spec=grid_spec,
        out_shape=jax.ShapeDtypeStruct((n_tok, D_MODEL), F32),
        compiler_params=_cparams(("arbitrary",)),
        name="combine",
    )(offs, sel, aff_t, _strict_upper(MOE_TB), x, ye)


def _moe(routed, w_gate, w_up, w_down, layer):
    plans = []
    for x, h, aff_t in routed:
        cap = CAPACITY_FACTOR * x.shape[0] // N_EXPERTS
        sel, offs = _select(aff_t, cap)
        plans.append((cap, sel, offs, _dispatch(offs, sel, h, cap)))
    yes = _expert_ffn([p[3] for p in plans], w_gate, w_up, w_down, layer, [p[0] for p in plans])
    return [_combine(offs, sel, aff_t, x, ye)
            for (x, _, aff_t), (_, sel, offs, _), ye in zip(routed, plans, yes)]


def _row(v):
    return v.reshape(1, -1).astype(F32)


def _router_split(w_router):
    wt = w_router.T.astype(F32)
    hi = wt.astype(BF16)
    lo = (wt - hi.astype(F32)).astype(BF16)
    return hi, lo


def _prep_ab(norm_g, w_in, qn_a, kn_a, qn_b, kn_b, w_out, norm_ffn, w_router):
    scale = HEAD_DIM ** -0.5 * LOG2E
    perm_b = _pair_layout_perm(B_KV_HEADS, B_Q_HEADS // B_KV_HEADS)
    a3 = 3 * A_WIDTH
    bq = B_Q_HEADS * HEAD_DIM
    w_cols = np.concatenate([np.arange(a3), a3 + perm_b, np.arange(a3 + bq, w_in.shape[1])])
    gains = jnp.concatenate([
        jnp.repeat(qn_a, A_HEADS, axis=0).reshape(-1) * scale,
        jnp.repeat(kn_a, A_HEADS, axis=0).reshape(-1),
        jnp.ones((A_WIDTH,), F32),
        jnp.tile(qn_b, B_Q_HEADS) * scale,
        jnp.tile(kn_b, B_KV_HEADS),
        jnp.ones((B_KV_HEADS * HEAD_DIM,), F32)])
    a_out = A_HEADS * HEAD_DIM
    w_o = w_out[np.concatenate([np.arange(a_out), a_out + perm_b])].astype(BF16)
    return (_row(norm_g), w_in[:, w_cols].astype(BF16), _row(gains), w_o, _row(norm_ffn)) + _router_split(w_router)


def _layer_ab(x, batch, seq, tabs, prep):
    norm_g, w, gains, w_o, norm_ffn, wr_hi, wr_lo = prep
    tab_a, tab_b = tabs
    bq = B_Q_HEADS * HEAD_DIM
    n_a = A_GROUPS
    assert A_HEADS * HEAD_DIM == MXU_COLS
    plan = ([("qk", g, 0, 0, ROT_DIM // 2) for g in range(n_a)]
            + [("qk", n_a + g, 0, 0, ROT_DIM // 2) for g in range(n_a)]
            + [("v", 2 * n_a + g, 0, 0, 0) for g in range(n_a)]
            + [("qk", 3 * n_a, c * MXU_COLS, 1, HEAD_DIM // 4) for c in range(bq // MXU_COLS)]
            + [("qk", 3 * n_a + 1, 0, 1, HEAD_DIM // 4), ("v", 3 * n_a + 2, 0, 0, 0)])
    assert B_KV_HEADS * HEAD_DIM == MXU_COLS
    outs = _projection(
        x, norm_g, w, gains, (tab_a, tab_b), plan,
        (MXU_COLS,) * (2 * n_a) + (None,) * n_a + (bq, MXU_COLS, None), seq)
    qa, ka, va = outs[:n_a], outs[n_a:2 * n_a], outs[2 * n_a:3 * n_a]
    qb, kb, vb = outs[3 * n_a:]
    o_parts, lse_parts = [], []
    for gi, (window, dil) in enumerate(A_PATTERNS):
        o, lse = _attention(qa[gi], ka[gi], va[gi], batch=batch, n_keys=seq // dil, residues=dil, group=1,
                            n_pairs=A_HEADS // 2, band_r=window // (2 * dil), want_lse=True)
        o_parts.append(o)
        lse_parts.append(lse)
    (ob,) = _attention(qb, kb, vb, batch=batch, n_keys=seq, residues=1,
                       group=B_Q_HEADS // B_KV_HEADS, n_pairs=B_KV_HEADS // 2, band_r=None)
    return _out_projection(o_parts, lse_parts, ob, x, w_o, norm_ffn, wr_hi, wr_lo)


def _prep_c(norm_g, w_in, qn, kn, sink, w_out, norm_ffn, w_router):
    scale = HEAD_DIM ** -0.5 * LOG2E
    perm = _pair_layout_perm(C_KV_HEADS, C_Q_HEADS // C_KV_HEADS)
    cq = C_Q_HEADS * HEAD_DIM
    w_cols = np.concatenate([perm, np.arange(cq, w_in.shape[1])])
    gains = jnp.concatenate([jnp.tile(qn, C_Q_HEADS) * scale, jnp.tile(kn, C_KV_HEADS),
                             jnp.ones((C_KV_HEADS * HEAD_DIM,), F32)])
    return (_row(norm_g), w_in[:, w_cols].astype(BF16), _row(gains), sink.astype(F32),
            w_out[perm].astype(BF16), _row(norm_ffn)) + _router_split(w_router)


def _layer_c(x, batch, seq, tabs, prep):
    norm_g, w, gains, sink, w_o, norm_ffn, wr_hi, wr_lo = prep
    tab_a, _ = tabs
    cq = C_Q_HEADS * HEAD_DIM
    plan = ([("qk", 0, c * MXU_COLS, 0, ROT_DIM // 2) for c in range(cq // MXU_COLS)]
            + [("qk", 1, 0, 0, ROT_DIM // 2), ("v", 2, 0, 0, 0)])
    assert C_KV_HEADS * HEAD_DIM == MXU_COLS
    q, k, v = _projection(x, norm_g, w, gains, (tab_a,), plan, (cq, MXU_COLS, None), seq)
    (o,) = _attention(q, k, v, batch=batch, n_keys=seq, residues=1,
                      group=C_Q_HEADS // C_KV_HEADS, n_pairs=C_KV_HEADS // 2, band_r=C_RADIUS, sink=sink)
    return _out_projection([], [], o, x, w_o, norm_ffn, wr_hi, wr_lo)


def _encode(xs, params):
    (norm_mix, norm_ffn, w_in_ab, qn_a, kn_a, qn_b, kn_b, w_out_ab, w_in_c, qn_c, kn_c, sink_c, w_out_c,
     w_router, w_gate, w_up, w_down) = params
    shapes = [x.shape[:2] for x in xs]
    tabs = [_rope_tables(seq) for _, seq in shapes]
    xts = [x.reshape(batch * seq, D_MODEL) for x, (batch, seq) in zip(xs, shapes)]
    for layer in range(norm_mix.shape[0]):
        j = layer // 2
        if layer % 2 == 0:
            prep = _prep_ab(norm_mix[layer], w_in_ab[j], qn_a[j], kn_a[j], qn_b[j], kn_b[j], w_out_ab[j],
                            norm_ffn[layer], w_router[layer])
            mixer = _layer_ab
        else:
            prep = _prep_c(norm_mix[layer], w_in_c[j], qn_c[j], kn_c[j], sink_c[j], w_out_c[j],
                           norm_ffn[layer], w_router[layer])
            mixer = _layer_c
        routed = [mixer(xt, batch, seq, tab, prep) for xt, (batch, seq), tab in zip(xts, shapes, tabs)]
        xts = _moe(routed, w_gate, w_up, w_down, layer)
    return tuple(xt.reshape(batch, seq, D_MODEL) for xt, (batch, seq) in zip(xts, shapes))


def kernel(x_prompt, x_sample, norm_mix, norm_ffn, w_in_ab, qn_a, kn_a, qn_b, kn_b, w_out_ab, w_in_c, qn_c,
           kn_c, sink_c, w_out_c, w_router, w_gate, w_up, w_down):
    params = (norm_mix, norm_ffn, w_in_ab, qn_a, kn_a, qn_b, kn_b, w_out_ab, w_in_c, qn_c, kn_c, sink_c,
              w_out_c, w_router, w_gate, w_up, w_down)
    return _encode((x_prompt, x_sample), params)
```

```python
import functools
import math

import jax
import jax.numpy as jnp
import numpy as np
from jax import lax
from jax.experimental import pallas as pl
from jax.experimental.pallas import tpu as pltpu

F32 = jnp.float32
BF16 = jnp.bfloat16
I32 = jnp.int32

D_MODEL = 1024
HEAD_DIM = 64
LANES = 128
MXU_COLS = 256
GRID_W = 64
ROT_DIM = HEAD_DIM // 4
ROPE_THETA = 500000.0
AXIAL_THETA = 10000.0
A_PATTERNS = ((128, 1), (512, 4), (2048, 16))
A_HEADS = 4
A_GROUPS = len(A_PATTERNS)
A_WIDTH = A_GROUPS * A_HEADS * HEAD_DIM
B_Q_HEADS = 12
B_KV_HEADS = 4
C_Q_HEADS = 16
C_KV_HEADS = 4
C_RADIUS = 128
N_EXPERTS = 16
EXPERT_FF = 1024
CAPACITY_FACTOR = 2
NEG_INF = -1e30
EPS = 1e-6

VMEM_LIMIT = 56 * 1024 * 1024

PROJ_TM = 512
DENSE_TQ = 256
BAND_TQ = 256
BAND_COLS = 1024
ACC_ROWS = LANES + 16
DENSE_TK = 512
MOE_TB = 256
MOE_W = 64
FFN_TM = 512


def _cparams(sem):
    return pltpu.CompilerParams(dimension_semantics=sem, vmem_limit_bytes=VMEM_LIMIT)


def _head_block_diag():
    idx = np.arange(MXU_COLS) // HEAD_DIM
    return jnp.asarray((idx[:, None] == idx[None, :]).astype(np.float32), dtype=BF16)


def _rope_tables(max_len):
    pos = jnp.arange(max_len, dtype=F32)
    j = np.arange(LANES) % HEAD_DIM

    def angles(p, dim, theta):
        exps = jnp.arange(0, dim, 2, dtype=F32) / dim
        inv = jnp.power(jnp.float32(theta), -exps)
        return p[:, None] * inv[None, :]

    half = ROT_DIM // 2
    ang = angles(pos, ROT_DIM, ROPE_THETA)
    cos, sin = jnp.cos(ang), jnp.sin(ang)
    fa = np.where(j < half, j, np.where(j < ROT_DIM, j - half, 0))
    cos_l, sin_l = cos[:, fa], sin[:, fa]
    lo = jnp.asarray(j < half)[None, :]
    hi = jnp.asarray((j >= half) & (j < ROT_DIM))[None, :]
    tab_a = jnp.stack([jnp.where(lo | hi, cos_l, 1.0),
                       jnp.where(lo, -sin_l, 0.0),
                       jnp.where(hi, sin_l, 0.0)])
    hb = HEAD_DIM // 2
    qb = hb // 2
    t = jnp.arange(max_len)
    ang_r = angles((t // GRID_W).astype(F32), hb, AXIAL_THETA)
    ang_c = angles((t % GRID_W).astype(F32), hb, AXIAL_THETA)
    fb = j % qb
    is_col = jnp.asarray(j >= hb)[None, :]
    ang_l = jnp.where(is_col, ang_c[:, fb], ang_r[:, fb])
    cos_b, sin_b = jnp.cos(ang_l), jnp.sin(ang_l)
    first = jnp.asarray((j % hb) < qb)[None, :]
    tab_b = jnp.stack([cos_b, jnp.where(first, -sin_b, 0.0), jnp.where(first, 0.0, sin_b)])
    return tab_a.astype(F32), tab_b.astype(F32)


def _pair_layout_perm(n_kv, group):
    cols = []
    for p in range(n_kv // 2):
        for g in range(group):
            for par in range(2):
                h = (2 * p + par) * group + g
                cols.extend(range(h * HEAD_DIM, (h + 1) * HEAD_DIM))
    return np.asarray(cols, dtype=np.int32)


def _proj_kernel(x_ref, g_ref, w_ref, s_ref, gain_ref, *rest, plan, n_tabs):
    tab_refs = rest[:n_tabs]
    out_refs = rest[n_tabs:]
    x = x_ref[...]
    ms = jnp.mean(x * x, axis=1, keepdims=True)
    xn = ((x * lax.rsqrt(ms + EPS)) * g_ref[...]).astype(BF16)
    def project(c):
        return jnp.dot(xn, w_ref[:, c * MXU_COLS:(c + 1) * MXU_COLS], preferred_element_type=F32)

    ahead = project(0)
    for c, (kind, out_i, out_col, tab_i, shift) in enumerate(plan):
        acc = ahead
        if c + 1 < len(plan):
            ahead = project(c + 1)
        o_ref = out_refs[out_i]
        if kind == "v":
            o_ref[...] = acc.T.astype(BF16)
            continue
        ss = jnp.dot((acc * acc).astype(BF16), s_ref[...], preferred_element_type=F32)
        y = (acc * lax.rsqrt(ss * (1.0 / HEAD_DIM) + EPS)) * gain_ref[:, c * MXU_COLS:(c + 1) * MXU_COLS]
        tab = tab_refs[tab_i]
        t0 = jnp.concatenate([tab[0], tab[0]], axis=1)
        t1 = jnp.concatenate([tab[1], tab[1]], axis=1)
        t2 = jnp.concatenate([tab[2], tab[2]], axis=1)
        y = y * t0 + pltpu.roll(y, MXU_COLS - shift, 1) * t1 + pltpu.roll(y, shift, 1) * t2
        o_ref[:, out_col:out_col + MXU_COLS] = y.astype(BF16)


def _projection(x, norm_g, w, gains, tabs, plan, out_widths, seq_len):
    t_tokens = x.shape[0]
    tm = PROJ_TM
    n_in = w.shape[1]
    blocks_per_seq = seq_len // tm
    in_specs = [
        pl.BlockSpec((tm, D_MODEL), lambda i: (i, 0)),
        pl.BlockSpec((1, D_MODEL), lambda i: (0, 0)),
        pl.BlockSpec((D_MODEL, n_in), lambda i: (0, 0)),
        pl.BlockSpec((MXU_COLS, MXU_COLS), lambda i: (0, 0)),
        pl.BlockSpec((1, n_in), lambda i: (0, 0)),
    ] + [pl.BlockSpec((3, tm, LANES), lambda i: (0, i % blocks_per_seq, 0)) for _ in tabs]
    out_specs, out_shape = [], []
    for wd in out_widths:
        if wd is None:
            out_specs.append(pl.BlockSpec((None, MXU_COLS, tm),
                                          lambda i: (i // blocks_per_seq, 0, i % blocks_per_seq)))
            out_shape.append(jax.ShapeDtypeStruct((t_tokens // seq_len, MXU_COLS, seq_len), BF16))
        else:
            out_specs.append(pl.BlockSpec((tm, wd), lambda i: (i, 0)))
            out_shape.append(jax.ShapeDtypeStruct((t_tokens, wd), BF16))
    return pl.pallas_call(
        functools.partial(_proj_kernel, plan=tuple(plan), n_tabs=len(tabs)),
        grid=(t_tokens // tm,),
        in_specs=in_specs, out_specs=out_specs, out_shape=out_shape,
        compiler_params=_cparams(("parallel",)),
        name="projection",
    )(x, norm_g, w, _head_block_diag(), gains, *tabs)


LOG2E = math.log2(math.e)
LN2 = math.log(2.0)
_NT = (((1,), (1,)), ((), ()))


def _attn_kernel(*refs, group, n_pairs, tq, tk, n_keys, band_r, has_sink, want_lse):
    refs = list(refs)
    sink_ref = refs.pop(0) if has_sink else None
    q_ref, k_ref, v_ref, o_ref = refs[:4]
    refs = refs[4:]
    lse_ref = refs.pop(0) if want_lse else None
    scratch = refs
    m_cols = group * tq
    n_chain = 2 * n_pairs
    i = pl.program_id(2)
    lane = lax.broadcasted_iota(I32, (1, LANES), 1)
    low = lane < HEAD_DIM
    half_mask = [jnp.where(low, 1.0, 0.0).astype(BF16), jnp.where(low, 0.0, 1.0).astype(BF16)]

    q_masked = []
    for p in range(n_pairs):
        q_p = jnp.concatenate(
            [q_ref[:, (p * group + g) * LANES:(p * group + g + 1) * LANES] for g in range(group)], axis=0)
        q_masked += [q_p * half_mask[par] for par in range(2)]

    sub = lax.broadcasted_iota(I32, (ACC_ROWS, m_cols), 0)
    if has_sink:
        acc0 = jnp.where(sub >= LANES, 1.0, 0.0).astype(F32)
        m0 = []
        for c in range(n_chain):
            m0.append(jnp.concatenate(
                [jnp.full((1, tq), sink_ref[c * group + g] * LOG2E, F32) for g in range(group)], axis=1))
    else:
        acc0 = jnp.zeros((ACC_ROWS, m_cols), F32)
        m0 = [jnp.full((1, m_cols), NEG_INF, F32) for _ in range(n_chain)]

    def values_t(p, k0, width):
        ones = jnp.ones((ACC_ROWS - LANES, width), BF16)
        return jnp.concatenate([v_ref[p * LANES:(p + 1) * LANES, pl.ds(k0, width)], ones], axis=0)

    def softmax_pv(s_t, v_t, m_prev, acc_prev):
        m_new = jnp.maximum(m_prev, jnp.max(s_t, axis=0, keepdims=True))
        alpha = jnp.exp2(m_prev - m_new)
        p_t = jnp.exp2(s_t - m_new).astype(BF16)
        return m_new, alpha * acc_prev + jnp.dot(v_t, p_t, preferred_element_type=F32)

    if band_r is None:
        qm_ref, acc_ref, s_even, s_odd = scratch
        for c in range(n_chain):
            qm_ref[c] = q_masked[c]
            acc_ref[c] = acc0

        def step(kb_next, s_next, kb, s_ref, m_prev):
            k0 = pl.multiple_of(kb * tk, tk)
            m_next = []
            for p in range(n_pairs):
                v_t = values_t(p, k0, tk)
                if kb_next is not None:
                    kt = k_ref[pl.ds(pl.multiple_of(kb_next * tk, tk), tk), p * LANES:(p + 1) * LANES]
                for par in range(2):
                    c = 2 * p + par
                    if kb_next is not None:
                        s_next[c] = lax.dot_general(kt, qm_ref[c], _NT, preferred_element_type=F32)
                    m_new, acc_ref[c] = softmax_pv(s_ref[c], v_t, m_prev[c], acc_ref[c])
                    m_next.append(m_new)
            return m_next

        n_blocks = n_keys // tk
        for p in range(n_pairs):
            kt = k_ref[0:tk, p * LANES:(p + 1) * LANES]
            for par in range(2):
                s_even[2 * p + par] = lax.dot_general(kt, qm_ref[2 * p + par], _NT, preferred_element_type=F32)

        def body(j, carry):
            m = step(2 * j + 1, s_odd, 2 * j, s_even, list(carry))
            return tuple(step(2 * j + 2, s_even, 2 * j + 1, s_odd, m))

        m_fin = list(lax.fori_loop(0, n_blocks // 2 - 1, body, tuple(m0)))
        m_fin = step(n_blocks - 1, s_odd, n_blocks - 2, s_even, m_fin)
        m_fin = step(None, None, n_blocks - 1, s_odd, m_fin)
        acc_fin = [acc_ref[c] for c in range(n_chain)]
    else:
        halo = -(-band_r // LANES) * LANES
        width = min(tq + 2 * halo, n_keys)
        ws = pl.multiple_of(jnp.clip(i * tq - halo, 0, n_keys - width), LANES)
        kpos = ws + lax.broadcasted_iota(I32, (width, 1), 0)
        col = lax.broadcasted_iota(I32, (1, tq), 1)
        qpos = i * tq + jnp.concatenate([col] * group, axis=1)
        mask = jnp.abs(kpos - qpos) <= band_r
        s_all = []
        for p in range(n_pairs):
            kt = k_ref[pl.ds(ws, width), p * LANES:(p + 1) * LANES]
            s_all += [lax.dot_general(kt, q_masked[2 * p + par], _NT, preferred_element_type=F32)
                      for par in range(2)]
        m_fin, acc_fin = [], []
        for c in range(n_chain):
            m_new, acc_new = softmax_pv(jnp.where(mask, s_all[c], NEG_INF), values_t(c // 2, ws, width),
                                        m0[c], acc0)
            m_fin.append(m_new)
            acc_fin.append(acc_new)

    top = lax.broadcasted_iota(I32, (LANES, tq), 0) < HEAD_DIM
    for p in range(n_pairs):
        even, odd = acc_fin[2 * p], acc_fin[2 * p + 1]
        for g in range(group):
            cols = slice(g * tq, (g + 1) * tq)
            num = jnp.where(top, even[:LANES, cols], odd[:LANES, cols])
            den = jnp.where(top, even[LANES:LANES + 1, cols], odd[LANES:LANES + 1, cols])
            out_cols = slice((p * group + g) * LANES, (p * group + g + 1) * LANES)
            o_ref[:, out_cols] = (num / den).T.astype(o_ref.dtype)
            if want_lse:
                m_sel = jnp.where(top, m_fin[2 * p][:, cols], m_fin[2 * p + 1][:, cols])
                lse_ref[:, out_cols] = (m_sel * LN2 + jnp.log(den)).T


def _attention(q, k, v_t, *, batch, n_keys, residues, group, n_pairs, band_r, sink=None, want_lse=False):
    dense = band_r is None
    tq = DENSE_TQ if dense else min(BAND_TQ, BAND_COLS // group, n_keys)
    wq = n_pairs * group * LANES
    wk = n_pairs * LANES
    q3 = q.reshape(batch, n_keys, residues * wq)
    k3 = k.reshape(batch, n_keys, residues * wk)
    v4 = v_t.reshape(batch, wk, n_keys, residues).transpose(0, 3, 1, 2)

    in_specs = []
    args = []
    if sink is not None:
        in_specs.append(pl.BlockSpec(memory_space=pltpu.SMEM))
        args.append(sink)
    in_specs += [
        pl.BlockSpec((None, tq, wq), lambda b, r, i: (b, i, r)),
        pl.BlockSpec((None, n_keys, wk), lambda b, r, i: (b, 0, r)),
        pl.BlockSpec((None, None, wk, n_keys), lambda b, r, i: (b, r, 0, 0)),
    ]
    args += [q3, k3, v4]
    out_specs = [pl.BlockSpec((None, tq, wq), lambda b, r, i: (b, i, r))]
    out_shape = [jax.ShapeDtypeStruct((batch, n_keys, residues * wq), BF16)]
    if want_lse:
        out_specs.append(pl.BlockSpec((None, tq, wq), lambda b, r, i: (b, i, r)))
        out_shape.append(jax.ShapeDtypeStruct((batch, n_keys, residues * wq), F32))
    m_cols = group * tq
    scratch = []
    if dense:
        scratch = [pltpu.VMEM((2 * n_pairs, m_cols, LANES), BF16),
                   pltpu.VMEM((2 * n_pairs, ACC_ROWS, m_cols), F32)]
        scratch += [pltpu.VMEM((2 * n_pairs, DENSE_TK, m_cols), F32) for _ in range(2)]
    outs = pl.pallas_call(
        functools.partial(_attn_kernel, group=group, n_pairs=n_pairs, tq=tq, tk=DENSE_TK, n_keys=n_keys,
                          band_r=band_r, has_sink=sink is not None, want_lse=want_lse),
        grid=(batch, residues, n_keys // tq),
        in_specs=in_specs, out_specs=out_specs, out_shape=out_shape,
        scratch_shapes=scratch,
        compiler_params=_cparams(("parallel", "parallel", "arbitrary")),
        name="attention_dense" if dense else "attention_band",
    )(*args)
    return [o.reshape(batch * n_keys * residues, wq) for o in outs]


def _outproj_kernel(*refs, n_merge):
    refs = list(refs)
    if n_merge:
        o_parts = refs[:n_merge]
        lse_parts = refs[n_merge:2 * n_merge]
        refs = refs[2 * n_merge:]
    o_rest, x_ref, w_ref, g_ref, wrh_ref, wrl_ref, xo_ref, h_ref, aff_ref = refs
    tm = x_ref.shape[0]
    halves = [slice(0, tm // 2), slice(tm // 2, tm)]
    accs = []
    for rows in halves:
        acc = x_ref[rows, :]
        k0 = 0
        if n_merge:
            lses = [r[rows, :] for r in lse_parts]
            m = functools.reduce(jnp.maximum, lses)
            ws = [jnp.exp(l - m) for l in lses]
            num = sum(wgt * r[rows, :].astype(F32) for wgt, r in zip(ws, o_parts))
            oa = (num / sum(ws)).astype(BF16)
            k0 = oa.shape[1]
            acc = acc + jnp.dot(oa, w_ref[:k0, :], preferred_element_type=F32)
        accs.append(acc + jnp.dot(o_rest[rows, :], w_ref[k0:, :], preferred_element_type=F32))
    for rows, acc in zip(halves, accs):
        xo_ref[rows, :] = acc
        ms = jnp.mean(acc * acc, axis=1, keepdims=True)
        h = (acc * lax.rsqrt(ms + EPS)) * g_ref[...]
        h_hi = h.astype(BF16)
        h_ref[rows, :] = h_hi
        h_lo = (h - h_hi.astype(F32)).astype(BF16)
        w_both = jnp.concatenate([wrh_ref[...], wrl_ref[...]], axis=0)
        both = lax.dot_general(w_both, h_hi, _NT, preferred_element_type=F32)
        logits = (both[:N_EXPERTS] + both[N_EXPERTS:]
                  + lax.dot_general(wrh_ref[...], h_lo, _NT, preferred_element_type=F32))
        mx = jnp.max(logits, axis=0, keepdims=True)
        e = jnp.exp(logits - mx)
        aff_ref[:, rows] = e / jnp.sum(e, axis=0, keepdims=True)


def _out_projection(o_merge, lse_merge, o_rest, x, w_out, norm_g, wr_hi, wr_lo):
    t_tokens = x.shape[0]
    tm = PROJ_TM
    n_merge = len(o_merge)
    row = lambda i: (i, 0)
    fixed = lambda i: (0, 0)
    in_specs = ([pl.BlockSpec((tm, o.shape[1]), row) for o in o_merge]
                + [pl.BlockSpec((tm, l.shape[1]), row) for l in lse_merge]
                + [pl.BlockSpec((tm, o_rest.shape[1]), row),
                   pl.BlockSpec((tm, D_MODEL), row),
                   pl.BlockSpec(w_out.shape, fixed),
                   pl.BlockSpec((1, D_MODEL), fixed),
                   pl.BlockSpec(wr_hi.shape, fixed),
                   pl.BlockSpec(wr_lo.shape, fixed)])
    out_specs = [pl.BlockSpec((tm, D_MODEL), row),
                 pl.BlockSpec((tm, D_MODEL), row),
                 pl.BlockSpec((N_EXPERTS, tm), lambda i: (0, i))]
    out_shape = [jax.ShapeDtypeStruct((t_tokens, D_MODEL), F32),
                 jax.ShapeDtypeStruct((t_tokens, D_MODEL), BF16),
                 jax.ShapeDtypeStruct((N_EXPERTS, t_tokens), F32)]
    return pl.pallas_call(
        functools.partial(_outproj_kernel, n_merge=n_merge),
        grid=(t_tokens // tm,),
        in_specs=in_specs, out_specs=out_specs, out_shape=out_shape,
        compiler_params=_cparams(("parallel",)),
        name="out_projection",
    )(*o_merge, *lse_merge, o_rest, x, w_out, norm_g, wr_hi, wr_lo)


def _select_kernel(aff_ref, ind_ref, tri_ref, sel_ref, off_ref, *, cap):
    aff = aff_ref[...]
    n_tok = aff.shape[1]
    bits = pltpu.bitcast(aff, I32)

    def count(pred):
        return jnp.sum(jnp.where(pred, 1.0, 0.0), axis=1, keepdims=True)

    def value_step(j, thr):
        cand = thr | lax.shift_left(jnp.int32(1), 30 - j)
        return jnp.where(count(bits >= cand) >= cap, cand, thr)

    thr = lax.fori_loop(0, 31, value_step, jnp.zeros((N_EXPERTS, 1), I32))
    gt = bits > thr
    eq = bits == thr
    need = cap - count(gt)
    idx = lax.broadcasted_iota(I32, aff.shape, 1)
    idx_bits = int(math.log2(n_tok))

    def index_step(j, bound):
        cand = bound | lax.shift_left(jnp.int32(1), idx_bits - j)
        return jnp.where(count(eq & (idx < cand)) <= need, cand, bound)

    bound = lax.fori_loop(0, idx_bits + 1, index_step, jnp.zeros((N_EXPERTS, 1), I32))
    sel = jnp.where(gt | (eq & (idx < bound)), 1.0, 0.0).astype(BF16)
    sel_ref[...] = sel
    counts = jnp.dot(sel, ind_ref[...], preferred_element_type=F32)
    offs = jnp.dot(counts.astype(BF16), tri_ref[...], preferred_element_type=F32)
    off_ref[...] = offs.astype(I32)


def _strict_upper(n):
    return jnp.asarray(np.triu(np.ones((n, n), np.float32), k=1), dtype=BF16)


def _select(aff_t, cap):
    n_tok = aff_t.shape[1]
    ind = np.zeros((n_tok, LANES), np.float32)
    ind[np.arange(n_tok), np.arange(n_tok) // MOE_TB] = 1.0
    return pl.pallas_call(
        functools.partial(_select_kernel, cap=cap),
        out_shape=[jax.ShapeDtypeStruct((N_EXPERTS, n_tok), BF16),
                   jax.ShapeDtypeStruct((N_EXPERTS, LANES), I32)],
        compiler_params=pltpu.CompilerParams(vmem_limit_bytes=VMEM_LIMIT),
        name="select",
    )(aff_t, jnp.asarray(ind, dtype=BF16), _strict_upper(LANES))


SLOT_ALIGN = 16
ALIGN_BITS = 4
CHUNK_BITS = 6
NOT_SELECTED = -1e6


class _BlockLayout:
    def __init__(self, off_ref, b):
        self.start = [off_ref[e, b] for e in range(N_EXPERTS)]
        self.end = [off_ref[e, b + 1] for e in range(N_EXPERTS)]
        def floor_tile(v):
            return lax.shift_left(lax.shift_right_logical(v, ALIGN_BITS), ALIGN_BITS)

        self.base = [floor_tile(s) for s in self.start]
        self.shift = [s - a for s, a in zip(self.start, self.base)]
        span = [en - a for en, a in zip(self.end, self.base)]
        self.n_chunks = functools.reduce(
            jnp.maximum, [lax.shift_right_logical(sp, CHUNK_BITS) for sp in span]) + 1
        tail = [floor_tile(sp) for sp in span]
        self.tail_chunk = [lax.shift_right_logical(t, CHUNK_BITS) for t in tail]
        self.tail_row = [pl.multiple_of(t & (MOE_W - 1), SLOT_ALIGN) for t in tail]

    def window_row(self, e, chunk):
        return pl.multiple_of(self.base[e] + chunk * MOE_W, SLOT_ALIGN)


def _slot_onehots(pos, weight, chunk):
    n_tok = pos.shape[1]
    slot = (lax.broadcasted_iota(I32, (MOE_W, n_tok), 0) + chunk * MOE_W).astype(F32)
    parts = []
    for e in range(N_EXPERTS):
        hit = pos[e:e + 1, :] == slot
        val = 1.0 if weight is None else weight[e:e + 1, :]
        parts.append(jnp.where(hit, val, 0.0).astype(BF16))
    return jnp.concatenate(parts, axis=0)


def _window_positions(sel_ref, tri_ref, layout):
    sel = sel_ref[...]
    rank = jnp.dot(sel, tri_ref[...], preferred_element_type=F32)
    shift = jnp.concatenate([jnp.full((1, 1), s, I32) for s in layout.shift], axis=0).astype(F32)
    return jnp.where(sel > 0, rank + shift, NOT_SELECTED)


def _dispatch_kernel(off_ref, sel_ref, h_ref, tri_ref, xe_ref, stage_ref, extra_ref, tail_ref, zero_ref, sem,
                     sem_extra):
    b = pl.program_id(0)
    nb = pl.num_programs(0)
    slot = b % 2
    layout = _BlockLayout(off_ref, b)
    pos = _window_positions(sel_ref, tri_ref, layout)
    h = h_ref[...]

    def rows_for(chunk):
        return jnp.dot(_slot_onehots(pos, None, chunk), h, preferred_element_type=F32)

    def copy(src, e, row0, s):
        return pltpu.make_async_copy(src.at[e], xe_ref.at[e, pl.ds(row0, MOE_W)], s)

    @pl.when(b == 0)
    def _():
        tail_ref[...] = jnp.zeros(tail_ref.shape, BF16)
        zero_ref[...] = jnp.zeros(zero_ref.shape, BF16)
        pad = zero_ref.shape[0]
        fills = [pltpu.make_async_copy(zero_ref, xe_ref.at[e, pl.ds(xe_ref.shape[1] - pad, pad)], sem_extra.at[0])
                 for e in range(N_EXPERTS)]
        for f in fills:
            f.start()
        for f in fills:
            f.wait()

    rows0 = rows_for(0)

    for e in range(N_EXPERTS):
        r0 = e * MOE_W
        first = rows0[r0:r0 + SLOT_ALIGN] + tail_ref[e].astype(F32)
        stage_ref[slot, e, :SLOT_ALIGN] = first.astype(BF16)
        stage_ref[slot, e, SLOT_ALIGN:] = rows0[r0 + SLOT_ALIGN:r0 + MOE_W].astype(BF16)
    for e in range(N_EXPERTS):
        tile = stage_ref[slot, e, pl.ds(layout.tail_row[e], SLOT_ALIGN), :]
        tail_ref[e] = jnp.where(layout.tail_chunk[e] == 0, tile, tail_ref[e])

    @pl.when(b > 0)
    def _():
        for e in range(N_EXPERTS):
            copy(stage_ref.at[1 - slot], e, 0, sem.at[1 - slot]).wait()

    for e in range(N_EXPERTS):
        copy(stage_ref.at[slot], e, layout.window_row(e, 0), sem.at[slot]).start()

    def overflow(chunk, carry):
        more = rows_for(chunk).astype(BF16)
        for e in range(N_EXPERTS):
            extra_ref[e] = more[e * MOE_W:(e + 1) * MOE_W]
        for e in range(N_EXPERTS):
            copy(extra_ref, e, layout.window_row(e, chunk), sem_extra.at[0]).start()
        for e in range(N_EXPERTS):
            @pl.when(layout.tail_chunk[e] == chunk)
            def _():
                tail_ref[e] = extra_ref[e, pl.ds(layout.tail_row[e], SLOT_ALIGN), :]
        for e in range(N_EXPERTS):
            copy(extra_ref, e, 0, sem_extra.at[0]).wait()
        return carry

    lax.fori_loop(1, layout.n_chunks, overflow, 0)

    @pl.when(b == nb - 1)
    def _():
        for e in range(N_EXPERTS):
            copy(stage_ref.at[slot], e, 0, sem.at[slot]).wait()


def _slot_rows(cap):
    pad = ((SLOT_ALIGN + MOE_TB) // MOE_W + 1) * MOE_W
    return cap + -(-pad // FFN_TM) * FFN_TM


def _dispatch(offs, sel, h, cap):
    n_tok = h.shape[0]
    nb = n_tok // MOE_TB
    rows = _slot_rows(cap)
    grid_spec = pltpu.PrefetchScalarGridSpec(
        num_scalar_prefetch=1,
        grid=(nb,),
        in_specs=[pl.BlockSpec((N_EXPERTS, MOE_TB), lambda b, off: (0, b)),
                  pl.BlockSpec((MOE_TB, D_MODEL), lambda b, off: (b, 0)),
                  pl.BlockSpec((MOE_TB, MOE_TB), lambda b, off: (0, 0))],
        out_specs=pl.BlockSpec(memory_space=pl.ANY),
        scratch_shapes=[pltpu.VMEM((2, N_EXPERTS, MOE_W, D_MODEL), BF16),
                        pltpu.VMEM((N_EXPERTS, MOE_W, D_MODEL), BF16),
                        pltpu.VMEM((N_EXPERTS, SLOT_ALIGN, D_MODEL), BF16),
                        pltpu.VMEM((rows - cap, D_MODEL), BF16),
                        pltpu.SemaphoreType.DMA((2,)),
                        pltpu.SemaphoreType.DMA((1,))],
    )
    return pl.pallas_call(
        _dispatch_kernel,
        grid_spec=grid_spec,
        out_shape=jax.ShapeDtypeStruct((N_EXPERTS, rows, D_MODEL), BF16),
        compiler_params=_cparams(("arbitrary",)),
        name="dispatch",
    )(offs, sel, h, _strict_upper(MOE_TB))


class _FfnSchedule:
    def __init__(self, caps, rows):
        self.n_real = [cap // FFN_TM for cap in caps]
        self.n_zero = [r // FFN_TM - n for r, n in zip(rows, self.n_real)]
        self.zero_at, step = [], 0
        for n in self.n_zero:
            self.zero_at.append(step)
            step += n
        self.first_real = step
        self.real_at = []
        for n in self.n_real:
            self.real_at.append(step)
            step += n
        self.n_steps = step

    def in_tile(self, g, j):
        return jnp.clip(j - self.real_at[g], 0, self.n_real[g] - 1)

    def out_tile(self, g, j):
        pad = self.n_real[g] + jnp.clip(j - self.zero_at[g], 0, self.n_zero[g] - 1)
        return jnp.where(j < self.real_at[g], pad, self.in_tile(g, j))


def _ffn_kernel(*refs, sched):
    n_groups = len(sched.n_real)
    xe_refs = refs[:n_groups]
    wg_ref, wu_ref, wd_ref = refs[n_groups:n_groups + 3]
    ye_refs = refs[n_groups + 3:2 * n_groups + 3]
    wg_bf, wu_bf, wd_bf = refs[2 * n_groups + 3:]
    j = pl.program_id(1)

    @pl.when(j == 0)
    def _():
        wg_bf[...] = wg_ref[...].astype(BF16)
        wu_bf[...] = wu_ref[...].astype(BF16)
        wd_bf[...] = wd_ref[...].astype(BF16)

    for g in range(n_groups):
        @pl.when((j >= sched.zero_at[g]) & (j < sched.zero_at[g] + sched.n_zero[g]))
        def _():
            ye_refs[g][...] = jnp.zeros(ye_refs[g].shape, BF16)

        @pl.when((j >= sched.real_at[g]) & (j < sched.real_at[g] + sched.n_real[g]))
        def _():
            x = xe_refs[g][...]
            gate = jnp.dot(x, wg_bf[...], preferred_element_type=F32)
            up = jnp.dot(x, wu_bf[...], preferred_element_type=F32)
            hid = (gate * jax.nn.sigmoid(gate) * up).astype(BF16)
            ye_refs[g][...] = jnp.dot(hid, wd_bf[...], preferred_element_type=F32).astype(BF16)


def _expert_ffn(xes, w_gate, w_up, w_down, layer, caps):
    sched = _FfnSchedule(caps, [xe.shape[1] for xe in xes])

    def weight_index(e, j):
        return (layer, jnp.minimum(e + (j >= sched.first_real).astype(I32), N_EXPERTS - 1), 0, 0)

    wspec = lambda shape: pl.BlockSpec((None, None) + shape, weight_index)

    def tile_spec(g, tile_fn):
        return pl.BlockSpec((None, FFN_TM, D_MODEL), lambda e, j: (e, tile_fn(g, j), 0))

    n_groups = len(xes)
    return pl.pallas_call(
        functools.partial(_ffn_kernel, sched=sched),
        grid=(N_EXPERTS, sched.n_steps),
        in_specs=([tile_spec(g, sched.in_tile) for g in range(n_groups)]
                  + [wspec((D_MODEL, EXPERT_FF)), wspec((D_MODEL, EXPERT_FF)), wspec((EXPERT_FF, D_MODEL))]),
        out_specs=[tile_spec(g, sched.out_tile) for g in range(n_groups)],
        out_shape=[jax.ShapeDtypeStruct(xe.shape, BF16) for xe in xes],
        scratch_shapes=[pltpu.VMEM((D_MODEL, EXPERT_FF), BF16), pltpu.VMEM((D_MODEL, EXPERT_FF), BF16),
                        pltpu.VMEM((EXPERT_FF, D_MODEL), BF16)],
        compiler_params=_cparams(("arbitrary", "arbitrary")),
        name="expert_ffn",
    )(*xes, w_gate, w_up, w_down)


def _combine_kernel(off_ref, sel_ref, aff_ref, tri_ref, x_ref, ye_ref, out_ref, win_ref, extra_ref, sem, sem_extra,
                    *, rows):
    b = pl.program_id(0)
    nb = pl.num_programs(0)
    slot = b % 2

    def copy(dst, e, row0, s):
        return pltpu.make_async_copy(ye_ref.at[e, pl.ds(row0, MOE_W)], dst.at[e], s)

    def fetch(blk, dst_slot):
        ahead = _BlockLayout(off_ref, blk)
        for e in range(N_EXPERTS):
            copy(win_ref.at[dst_slot], e, ahead.window_row(e, 0), sem.at[dst_slot]).start()

    @pl.when(b == 0)
    def _():
        fetch(0, 0)

    @pl.when(b + 1 < nb)
    def _():
        fetch(b + 1, 1 - slot)

    layout = _BlockLayout(off_ref, b)
    pos = _window_positions(sel_ref, tri_ref, layout)
    aff = aff_ref[...]
    tn = (((0,), (0,)), ((), ()))

    def contribution(chunk, window):
        gates = _slot_onehots(pos, aff, chunk)
        vals = window.reshape(N_EXPERTS * MOE_W, window.shape[-1])
        return lax.dot_general(gates, vals, tn, preferred_element_type=F32)

    for e in range(N_EXPERTS):
        copy(win_ref.at[slot], e, 0, sem.at[slot]).wait()
    out_ref[...] = x_ref[...] + contribution(0, win_ref[slot])

    def overflow(chunk, carry):
        for e in range(N_EXPERTS):
            row0 = pl.multiple_of(jnp.minimum(layout.window_row(e, chunk), rows - MOE_W), SLOT_ALIGN)
            copy(extra_ref, e, row0, sem_extra.at[0]).start()
        for e in range(N_EXPERTS):
            copy(extra_ref, e, 0, sem_extra.at[0]).wait()
        out_ref[...] += contribution(chunk, extra_ref[...])
        return carry

    lax.fori_loop(1, layout.n_chunks, overflow, 0)


def _combine(offs, sel, aff_t, x, ye):
    n_tok = x.shape[0]
    nb = n_tok // MOE_TB
    rows = ye.shape[1]
    grid_spec = pltpu.PrefetchScalarGridSpec(
        num_scalar_prefetch=1,
        grid=(nb,),
        in_specs=[pl.BlockSpec((N_EXPERTS, MOE_TB), lambda b, off: (0, b)),
                  pl.BlockSpec((N_EXPERTS, MOE_TB), lambda b, off: (0, b)),
                  pl.BlockSpec((MOE_TB, MOE_TB), lambda b, off: (0, 0)),
                  pl.BlockSpec((MOE_TB, D_MODEL), lambda b, off: (b, 0)),
                  pl.BlockSpec(memory_space=pl.ANY)],
        out_specs=pl.BlockSpec((MOE_TB, D_MODEL), lambda b, off: (b, 0)),
        scratch_shapes=[pltpu.VMEM((2, N_EXPERTS, MOE_W, D_MODEL), BF16),
                        pltpu.VMEM((N_EXPERTS, MOE_W, D_MODEL), BF16),
                        pltpu.SemaphoreType.DMA((2,)),
                        pltpu.SemaphoreType.DMA((1,))],
    )
    return pl.pallas_call(
        functools.partial(_combine_kernel, rows=rows),
        grid_spec=grid_spec,
        out_shape=jax.ShapeDtypeStruct((n_tok, D_MODEL), F32),
        compiler_params=_cparams(("arbitrary",)),
        name="combine",
    )(offs, sel, aff_t, _strict_upper(MOE_TB), x, ye)


def _moe(routed, w_gate, w_up, w_down, layer):
    plans = []
    for x, h, aff_t in routed:
        cap = CAPACITY_FACTOR * x.shape[0] // N_EXPERTS
        sel, offs = _select(aff_t, cap)
        plans.append((cap, sel, offs, _dispatch(offs, sel, h, cap)))
    yes = _expert_ffn([p[3] for p in plans], w_gate, w_up, w_down, layer, [p[0] for p in plans])
    return [_combine(offs, sel, aff_t, x, ye)
            for (x, _, aff_t), (_, sel, offs, _), ye in zip(routed, plans, yes)]


def _row(v):
    return v.reshape(1, -1).astype(F32)


def _router_split(w_router):
    wt = w_router.T.astype(F32)
    hi = wt.astype(BF16)
    lo = (wt - hi.astype(F32)).astype(BF16)
    return hi, lo


def _prep_ab(norm_g, w_in, qn_a, kn_a, qn_b, kn_b, w_out, norm_ffn, w_router):
    scale = HEAD_DIM ** -0.5 * LOG2E
    perm_b = _pair_layout_perm(B_KV_HEADS, B_Q_HEADS // B_KV_HEADS)
    a3 = 3 * A_WIDTH
    bq = B_Q_HEADS * HEAD_DIM
    w_cols = np.concatenate([np.arange(a3), a3 + perm_b, np.arange(a3 + bq, w_in.shape[1])])
    gains = jnp.concatenate([
        jnp.repeat(qn_a, A_HEADS, axis=0).reshape(-1) * scale,
        jnp.repeat(kn_a, A_HEADS, axis=0).reshape(-1),
        jnp.ones((A_WIDTH,), F32),
        jnp.tile(qn_b, B_Q_HEADS) * scale,
        jnp.tile(kn_b, B_KV_HEADS),
        jnp.ones((B_KV_HEADS * HEAD_DIM,), F32)])
    a_out = A_HEADS * HEAD_DIM
    w_o = w_out[np.concatenate([np.arange(a_out), a_out + perm_b])].astype(BF16)
    return (_row(norm_g), w_in[:, w_cols].astype(BF16), _row(gains), w_o, _row(norm_ffn)) + _router_split(w_router)


def _layer_ab(x, batch, seq, tabs, prep):
    norm_g, w, gains, w_o, norm_ffn, wr_hi, wr_lo = prep
    tab_a, tab_b = tabs
    bq = B_Q_HEADS * HEAD_DIM
    n_a = A_GROUPS
    assert A_HEADS * HEAD_DIM == MXU_COLS
    plan = ([("qk", g, 0, 0, ROT_DIM // 2) for g in range(n_a)]
            + [("qk", n_a + g, 0, 0, ROT_DIM // 2) for g in range(n_a)]
            + [("v", 2 * n_a + g, 0, 0, 0) for g in range(n_a)]
            + [("qk", 3 * n_a, c * MXU_COLS, 1, HEAD_DIM // 4) for c in range(bq // MXU_COLS)]
            + [("qk", 3 * n_a + 1, 0, 1, HEAD_DIM // 4), ("v", 3 * n_a + 2, 0, 0, 0)])
    assert B_KV_HEADS * HEAD_DIM == MXU_COLS
    outs = _projection(
        x, norm_g, w, gains, (tab_a, tab_b), plan,
        (MXU_COLS,) * (2 * n_a) + (None,) * n_a + (bq, MXU_COLS, None), seq)
    qa, ka, va = outs[:n_a], outs[n_a:2 * n_a], outs[2 * n_a:3 * n_a]
    qb, kb, vb = outs[3 * n_a:]
    o_parts, lse_parts = [], []
    for gi, (window, dil) in enumerate(A_PATTERNS):
        o, lse = _attention(qa[gi], ka[gi], va[gi], batch=batch, n_keys=seq // dil, residues=dil, group=1,
                            n_pairs=A_HEADS // 2, band_r=window // (2 * dil), want_lse=True)
        o_parts.append(o)
        lse_parts.append(lse)
    (ob,) = _attention(qb, kb, vb, batch=batch, n_keys=seq, residues=1,
                       group=B_Q_HEADS // B_KV_HEADS, n_pairs=B_KV_HEADS // 2, band_r=None)
    return _out_projection(o_parts, lse_parts, ob, x, w_o, norm_ffn, wr_hi, wr_lo)


def _prep_c(norm_g, w_in, qn, kn, sink, w_out, norm_ffn, w_router):
    scale = HEAD_DIM ** -0.5 * LOG2E
    perm = _pair_layout_perm(C_KV_HEADS, C_Q_HEADS // C_KV_HEADS)
    cq = C_Q_HEADS * HEAD_DIM
    w_cols = np.concatenate([perm, np.arange(cq, w_in.shape[1])])
    gains = jnp.concatenate([jnp.tile(qn, C_Q_HEADS) * scale, jnp.tile(kn, C_KV_HEADS),
                             jnp.ones((C_KV_HEADS * HEAD_DIM,), F32)])
    return (_row(norm_g), w_in[:, w_cols].astype(BF16), _row(gains), sink.astype(F32),
            w_out[perm].astype(BF16), _row(norm_ffn)) + _router_split(w_router)


def _layer_c(x, batch, seq, tabs, prep):
    norm_g, w, gains, sink, w_o, norm_ffn, wr_hi, wr_lo = prep
    tab_a, _ = tabs
    cq = C_Q_HEADS * HEAD_DIM
    plan = ([("qk", 0, c * MXU_COLS, 0, ROT_DIM // 2) for c in range(cq // MXU_COLS)]
            + [("qk", 1, 0, 0, ROT_DIM // 2), ("v", 2, 0, 0, 0)])
    assert C_KV_HEADS * HEAD_DIM == MXU_COLS
    q, k, v = _projection(x, norm_g, w, gains, (tab_a,), plan, (cq, MXU_COLS, None), seq)
    (o,) = _attention(q, k, v, batch=batch, n_keys=seq, residues=1,
                      group=C_Q_HEADS // C_KV_HEADS, n_pairs=C_KV_HEADS // 2, band_r=C_RADIUS, sink=sink)
    return _out_projection([], [], o, x, w_o, norm_ffn, wr_hi, wr_lo)


def _encode(xs, params):
    (norm_mix, norm_ffn, w_in_ab, qn_a, kn_a, qn_b, kn_b, w_out_ab, w_in_c, qn_c, kn_c, sink_c, w_out_c,
     w_router, w_gate, w_up, w_down) = params
    shapes = [x.shape[:2] for x in xs]
    tabs = [_rope_tables(seq) for _, seq in shapes]
    xts = [x.reshape(batch * seq, D_MODEL) for x, (batch, seq) in zip(xs, shapes)]
    for layer in range(norm_mix.shape[0]):
        j = layer // 2
        if layer % 2 == 0:
            prep = _prep_ab(norm_mix[layer], w_in_ab[j], qn_a[j], kn_a[j], qn_b[j], kn_b[j], w_out_ab[j],
                            norm_ffn[layer], w_router[layer])
            mixer = _layer_ab
        else:
            prep = _prep_c(norm_mix[layer], w_in_c[j], qn_c[j], kn_c[j], sink_c[j], w_out_c[j],
                           norm_ffn[layer], w_router[layer])
            mixer = _layer_c
        routed = [mixer(xt, batch, seq, tab, prep) for xt, (batch, seq), tab in zip(xts, shapes, tabs)]
        xts = _moe(routed, w_gate, w_up, w_down, layer)
    return tuple(xt.reshape(batch, seq, D_MODEL) for xt, (batch, seq) in zip(xts, shapes))


def kernel(x_prompt, x_sample, norm_mix, norm_ffn, w_in_ab, qn_a, kn_a, qn_b, kn_b, w_out_ab, w_in_c, qn_c,
           kn_c, sink_c, w_out_c, w_router, w_gate, w_up, w_down):
    params = (norm_mix, norm_ffn, w_in_ab, qn_a, kn_a, qn_b, kn_b, w_out_ab, w_in_c, qn_c, kn_c, sink_c,
              w_out_c, w_router, w_gate, w_up, w_down)
    return _encode((x_prompt, x_sample), params)
```

```python
import functools
import math

import jax
import jax.numpy as jnp
import numpy as np
from jax import lax
from jax.experimental import pallas as pl
from jax.experimental.pallas import tpu as pltpu

F32 = jnp.float32
BF16 = jnp.bfloat16
I32 = jnp.int32

D_MODEL = 1024
HEAD_DIM = 64
LANES = 128
MXU_COLS = 256
GRID_W = 64
ROT_DIM = HEAD_DIM // 4
ROPE_THETA = 500000.0
AXIAL_THETA = 10000.0
A_PATTERNS = ((128, 1), (512, 4), (2048, 16))
A_HEADS = 4
A_GROUPS = len(A_PATTERNS)
A_WIDTH = A_GROUPS * A_HEADS * HEAD_DIM
B_Q_HEADS = 12
B_KV_HEADS = 4
C_Q_HEADS = 16
C_KV_HEADS = 4
C_RADIUS = 128
N_EXPERTS = 16
EXPERT_FF = 1024
CAPACITY_FACTOR = 2
NEG_INF = -1e30
EPS = 1e-6

VMEM_LIMIT = 56 * 1024 * 1024

PROJ_TM = 512
DENSE_TQ = 256
BAND_TQ = 256
BAND_COLS = 1024
ACC_ROWS = LANES + 16
LSE_ROWS = 8
DENSE_TK = 512
MOE_TB = 256
MOE_W = 64
FFN_TM = 512


def _cparams(sem):
    return pltpu.CompilerParams(dimension_semantics=sem, vmem_limit_bytes=VMEM_LIMIT)


def _head_block_diag():
    idx = np.arange(MXU_COLS) // HEAD_DIM
    return jnp.asarray((idx[:, None] == idx[None, :]).astype(np.float32), dtype=BF16)


def _rope_tables(max_len):
    pos = jnp.arange(max_len, dtype=F32)
    j = np.arange(LANES) % HEAD_DIM

    def angles(p, dim, theta):
        exps = jnp.arange(0, dim, 2, dtype=F32) / dim
        inv = jnp.power(jnp.float32(theta), -exps)
        return p[:, None] * inv[None, :]

    half = ROT_DIM // 2
    ang = angles(pos, ROT_DIM, ROPE_THETA)
    cos, sin = jnp.cos(ang), jnp.sin(ang)
    fa = np.where(j < half, j, np.where(j < ROT_DIM, j - half, 0))
    cos_l, sin_l = cos[:, fa], sin[:, fa]
    lo = jnp.asarray(j < half)[None, :]
    hi = jnp.asarray((j >= half) & (j < ROT_DIM))[None, :]
    tab_a = jnp.stack([jnp.where(lo | hi, cos_l, 1.0),
                       jnp.where(lo, -sin_l, 0.0),
                       jnp.where(hi, sin_l, 0.0)])
    hb = HEAD_DIM // 2
    qb = hb // 2
    t = jnp.arange(max_len)
    ang_r = angles((t // GRID_W).astype(F32), hb, AXIAL_THETA)
    ang_c = angles((t % GRID_W).astype(F32), hb, AXIAL_THETA)
    fb = j % qb
    is_col = jnp.asarray(j >= hb)[None, :]
    ang_l = jnp.where(is_col, ang_c[:, fb], ang_r[:, fb])
    cos_b, sin_b = jnp.cos(ang_l), jnp.sin(ang_l)
    first = jnp.asarray((j % hb) < qb)[None, :]
    tab_b = jnp.stack([cos_b, jnp.where(first, -sin_b, 0.0), jnp.where(first, 0.0, sin_b)])
    return tab_a.astype(F32), tab_b.astype(F32)


def _pair_layout_perm(n_kv, group):
    cols = []
    for p in range(n_kv // 2):
        for g in range(group):
            for par in range(2):
                h = (2 * p + par) * group + g
                cols.extend(range(h * HEAD_DIM, (h + 1) * HEAD_DIM))
    return np.asarray(cols, dtype=np.int32)


def _proj_kernel(x_ref, g_ref, w_ref, s_ref, gain_ref, *rest, plan, n_tabs):
    tab_refs = rest[:n_tabs]
    out_refs = rest[n_tabs:]
    x = x_ref[...]
    ms = jnp.mean(x * x, axis=1, keepdims=True)
    xn = ((x * lax.rsqrt(ms + EPS)) * g_ref[...]).astype(BF16)
    def project(c):
        return jnp.dot(xn, w_ref[:, c * MXU_COLS:(c + 1) * MXU_COLS], preferred_element_type=F32)

    ahead = project(0)
    for c, (kind, out_i, out_col, tab_i, shift) in enumerate(plan):
        acc = ahead
        if c + 1 < len(plan):
            ahead = project(c + 1)
        o_ref = out_refs[out_i]
        if kind == "v":
            o_ref[...] = acc.T.astype(BF16)
            continue
        ss = jnp.dot((acc * acc).astype(BF16), s_ref[...], preferred_element_type=F32)
        y = (acc * lax.rsqrt(ss * (1.0 / HEAD_DIM) + EPS)) * gain_ref[:, c * MXU_COLS:(c + 1) * MXU_COLS]
        tab = tab_refs[tab_i]
        t0 = jnp.concatenate([tab[0], tab[0]], axis=1)
        t1 = jnp.concatenate([tab[1], tab[1]], axis=1)
        t2 = jnp.concatenate([tab[2], tab[2]], axis=1)
        y = y * t0 + pltpu.roll(y, MXU_COLS - shift, 1) * t1 + pltpu.roll(y, shift, 1) * t2
        o_ref[:, out_col:out_col + MXU_COLS] = y.astype(BF16)


def _projection(x, norm_g, w, gains, tabs, plan, out_widths, seq_len):
    t_tokens = x.shape[0]
    tm = PROJ_TM
    n_in = w.shape[1]
    blocks_per_seq = seq_len // tm
    in_specs = [
        pl.BlockSpec((tm, D_MODEL), lambda i: (i, 0)),
        pl.BlockSpec((1, D_MODEL), lambda i: (0, 0)),
        pl.BlockSpec((D_MODEL, n_in), lambda i: (0, 0)),
        pl.BlockSpec((MXU_COLS, MXU_COLS), lambda i: (0, 0)),
        pl.BlockSpec((1, n_in), lambda i: (0, 0)),
    ] + [pl.BlockSpec((3, tm, LANES), lambda i: (0, i % blocks_per_seq, 0)) for _ in tabs]
    out_specs, out_shape = [], []
    for wd in out_widths:
        if wd is None:
            out_specs.append(pl.BlockSpec((None, MXU_COLS, tm),
                                          lambda i: (i // blocks_per_seq, 0, i % blocks_per_seq)))
            out_shape.append(jax.ShapeDtypeStruct((t_tokens // seq_len, MXU_COLS, seq_len), BF16))
        else:
            out_specs.append(pl.BlockSpec((tm, wd), lambda i: (i, 0)))
            out_shape.append(jax.ShapeDtypeStruct((t_tokens, wd), BF16))
    return pl.pallas_call(
        functools.partial(_proj_kernel, plan=tuple(plan), n_tabs=len(tabs)),
        grid=(t_tokens // tm,),
        in_specs=in_specs, out_specs=out_specs, out_shape=out_shape,
        compiler_params=_cparams(("parallel",)),
        name="projection",
    )(x, norm_g, w, _head_block_diag(), gains, *tabs)


LOG2E = math.log2(math.e)
LN2 = math.log(2.0)
_NT = (((1,), (1,)), ((), ()))


def _attn_kernel(*refs, group, n_pairs, tq, tk, n_keys, band_r, has_sink, want_lse):
    refs = list(refs)
    sink_ref = refs.pop(0) if has_sink else None
    q_ref, k_ref, v_ref, o_ref = refs[:4]
    refs = refs[4:]
    lse_ref = refs.pop(0) if want_lse else None
    scratch = refs
    m_cols = group * tq
    n_chain = 2 * n_pairs
    i = pl.program_id(2)
    lane = lax.broadcasted_iota(I32, (1, LANES), 1)
    low = lane < HEAD_DIM
    half_mask = [jnp.where(low, 1.0, 0.0).astype(BF16), jnp.where(low, 0.0, 1.0).astype(BF16)]

    q_masked = []
    for p in range(n_pairs):
        q_p = jnp.concatenate(
            [q_ref[:, (p * group + g) * LANES:(p * group + g + 1) * LANES] for g in range(group)], axis=0)
        q_masked += [q_p * half_mask[par] for par in range(2)]

    sub = lax.broadcasted_iota(I32, (ACC_ROWS, m_cols), 0)
    if has_sink:
        acc0 = jnp.where(sub >= LANES, 1.0, 0.0).astype(F32)
        m0 = []
        for c in range(n_chain):
            m0.append(jnp.concatenate(
                [jnp.full((1, tq), sink_ref[c * group + g] * LOG2E, F32) for g in range(group)], axis=1))
    else:
        acc0 = jnp.zeros((ACC_ROWS, m_cols), F32)
        m0 = [jnp.full((1, m_cols), NEG_INF, F32) for _ in range(n_chain)]

    def values_t(p, k0, width):
        ones = jnp.ones((ACC_ROWS - LANES, width), BF16)
        return jnp.concatenate([v_ref[p * LANES:(p + 1) * LANES, pl.ds(k0, width)], ones], axis=0)

    def softmax_pv(s_t, v_t, m_prev, acc_prev):
        m_new = jnp.maximum(m_prev, jnp.max(s_t, axis=0, keepdims=True))
        alpha = jnp.exp2(m_prev - m_new)
        p_t = jnp.exp2(s_t - m_new).astype(BF16)
        return m_new, alpha * acc_prev + jnp.dot(v_t, p_t, preferred_element_type=F32)

    if band_r is None:
        qm_ref, acc_ref, s_even, s_odd = scratch
        for c in range(n_chain):
            qm_ref[c] = q_masked[c]
            acc_ref[c] = acc0

        def step(kb_next, s_next, kb, s_ref, m_prev):
            k0 = pl.multiple_of(kb * tk, tk)
            m_next = []
            for p in range(n_pairs):
                v_t = values_t(p, k0, tk)
                if kb_next is not None:
                    kt = k_ref[pl.ds(pl.multiple_of(kb_next * tk, tk), tk), p * LANES:(p + 1) * LANES]
                for par in range(2):
                    c = 2 * p + par
                    if kb_next is not None:
                        s_next[c] = lax.dot_general(kt, qm_ref[c], _NT, preferred_element_type=F32)
                    m_new, acc_ref[c] = softmax_pv(s_ref[c], v_t, m_prev[c], acc_ref[c])
                    m_next.append(m_new)
            return m_next

        n_blocks = n_keys // tk
        for p in range(n_pairs):
            kt = k_ref[0:tk, p * LANES:(p + 1) * LANES]
            for par in range(2):
                s_even[2 * p + par] = lax.dot_general(kt, qm_ref[2 * p + par], _NT, preferred_element_type=F32)

        def body(j, carry):
            m = step(2 * j + 1, s_odd, 2 * j, s_even, list(carry))
            return tuple(step(2 * j + 2, s_even, 2 * j + 1, s_odd, m))

        m_fin = list(lax.fori_loop(0, n_blocks // 2 - 1, body, tuple(m0)))
        m_fin = step(n_blocks - 1, s_odd, n_blocks - 2, s_even, m_fin)
        m_fin = step(None, None, n_blocks - 1, s_odd, m_fin)
        acc_fin = [acc_ref[c] for c in range(n_chain)]
    else:
        halo = -(-band_r // LANES) * LANES
        width = min(tq + 2 * halo, n_keys)
        ws = pl.multiple_of(jnp.clip(i * tq - halo, 0, n_keys - width), LANES)
        kpos = ws + lax.broadcasted_iota(I32, (width, 1), 0)
        col = lax.broadcasted_iota(I32, (1, tq), 1)
        qpos = i * tq + jnp.concatenate([col] * group, axis=1)
        mask = jnp.abs(kpos - qpos) <= band_r
        s_all = []
        for p in range(n_pairs):
            kt = k_ref[pl.ds(ws, width), p * LANES:(p + 1) * LANES]
            s_all += [lax.dot_general(kt, q_masked[2 * p + par], _NT, preferred_element_type=F32)
                      for par in range(2)]
        m_fin, acc_fin = [], []
        for c in range(n_chain):
            m_new, acc_new = softmax_pv(jnp.where(mask, s_all[c], NEG_INF), values_t(c // 2, ws, width),
                                        m0[c], acc0)
            m_fin.append(m_new)
            acc_fin.append(acc_new)

    top = lax.broadcasted_iota(I32, (LANES, tq), 0) < HEAD_DIM
    for p in range(n_pairs):
        even, odd = acc_fin[2 * p], acc_fin[2 * p + 1]
        for g in range(group):
            cols = slice(g * tq, (g + 1) * tq)
            num = jnp.where(top, even[:LANES, cols], odd[:LANES, cols])
            den = jnp.where(top, even[LANES:LANES + 1, cols], odd[LANES:LANES + 1, cols])
            out_cols = slice((p * group + g) * LANES, (p * group + g + 1) * LANES)
            o_ref[:, out_cols] = (num / den).T.astype(o_ref.dtype)
    if want_lse:
        assert group == 1 and n_chain <= LSE_ROWS
        rows = [m_fin[c] * LN2 + jnp.log(acc_fin[c][LANES:LANES + 1, :]) for c in range(n_chain)]
        lse_ref[...] = jnp.concatenate(rows + [jnp.zeros((LSE_ROWS - n_chain, tq), F32)], axis=0)


def _attention(q, k, v_t, *, batch, n_keys, residues, group, n_pairs, band_r, sink=None, want_lse=False):
    dense = band_r is None
    tq = DENSE_TQ if dense else min(BAND_TQ, BAND_COLS // group, n_keys)
    wq = n_pairs * group * LANES
    wk = n_pairs * LANES
    q3 = q.reshape(batch, n_keys, residues * wq)
    k3 = k.reshape(batch, n_keys, residues * wk)
    v4 = v_t.reshape(batch, wk, n_keys, residues).transpose(0, 3, 1, 2)

    in_specs = []
    args = []
    if sink is not None:
        in_specs.append(pl.BlockSpec(memory_space=pltpu.SMEM))
        args.append(sink)
    in_specs += [
        pl.BlockSpec((None, tq, wq), lambda b, r, i: (b, i, r)),
        pl.BlockSpec((None, n_keys, wk), lambda b, r, i: (b, 0, r)),
        pl.BlockSpec((None, None, wk, n_keys), lambda b, r, i: (b, r, 0, 0)),
    ]
    args += [q3, k3, v4]
    out_specs = [pl.BlockSpec((None, tq, wq), lambda b, r, i: (b, i, r))]
    out_shape = [jax.ShapeDtypeStruct((batch, n_keys, residues * wq), BF16)]
    if want_lse:
        out_specs.append(pl.BlockSpec((None, None, LSE_ROWS, tq), lambda b, r, i: (b, r, 0, i)))
        out_shape.append(jax.ShapeDtypeStruct((batch, residues, LSE_ROWS, n_keys), F32))
    m_cols = group * tq
    scratch = []
    if dense:
        scratch = [pltpu.VMEM((2 * n_pairs, m_cols, LANES), BF16),
                   pltpu.VMEM((2 * n_pairs, ACC_ROWS, m_cols), F32)]
        scratch += [pltpu.VMEM((2 * n_pairs, DENSE_TK, m_cols), F32) for _ in range(2)]
    outs = pl.pallas_call(
        functools.partial(_attn_kernel, group=group, n_pairs=n_pairs, tq=tq, tk=DENSE_TK, n_keys=n_keys,
                          band_r=band_r, has_sink=sink is not None, want_lse=want_lse),
        grid=(batch, residues, n_keys // tq),
        in_specs=in_specs, out_specs=out_specs, out_shape=out_shape,
        scratch_shapes=scratch,
        compiler_params=_cparams(("parallel", "parallel", "arbitrary")),
        name="attention_dense" if dense else "attention_band",
    )(*args)
    result = [outs[0].reshape(batch * n_keys * residues, wq)]
    if want_lse:
        result.append(outs[1].transpose(2, 0, 3, 1).reshape(LSE_ROWS, batch * n_keys * residues))
    return result


def _outproj_kernel(*refs, n_merge):
    refs = list(refs)
    if n_merge:
        o_parts = refs[:n_merge]
        lse_parts = refs[n_merge:2 * n_merge]
        expand_ref = refs[2 * n_merge]
        refs = refs[2 * n_merge + 1:]
    o_rest, x_ref, w_ref, g_ref, wrh_ref, wrl_ref, xo_ref, h_ref, aff_ref = refs
    tm = x_ref.shape[0]
    halves = [slice(0, tm // 2), slice(tm // 2, tm)]
    accs = []
    for rows in halves:
        acc = x_ref[rows, :]
        k0 = 0
        if n_merge:
            lses = [r[:, rows] for r in lse_parts]
            m = functools.reduce(jnp.maximum, lses)
            ws = [jnp.exp(l - m) for l in lses]
            inv = 1.0 / sum(ws)
            num = 0.0
            for wgt, r in zip(ws, o_parts):
                wn = wgt * inv
                hi = wn.astype(BF16)
                split = jnp.concatenate([hi, (wn - hi.astype(F32)).astype(BF16)], axis=0)
                full = lax.dot_general(split, expand_ref[...], (((0,), (0,)), ((), ())),
                                       preferred_element_type=F32)
                num = num + full * r[rows, :].astype(F32)
            oa = num.astype(BF16)
            k0 = oa.shape[1]
            acc = acc + jnp.dot(oa, w_ref[:k0, :], preferred_element_type=F32)
        accs.append(acc + jnp.dot(o_rest[rows, :], w_ref[k0:, :], preferred_element_type=F32))
    for rows, acc in zip(halves, accs):
        xo_ref[rows, :] = acc
        ms = jnp.mean(acc * acc, axis=1, keepdims=True)
        h = (acc * lax.rsqrt(ms + EPS)) * g_ref[...]
        h_hi = h.astype(BF16)
        h_ref[rows, :] = h_hi
        h_lo = (h - h_hi.astype(F32)).astype(BF16)
        w_both = jnp.concatenate([wrh_ref[...], wrl_ref[...]], axis=0)
        both = lax.dot_general(w_both, h_hi, _NT, preferred_element_type=F32)
        logits = (both[:N_EXPERTS] + both[N_EXPERTS:]
                  + lax.dot_general(wrh_ref[...], h_lo, _NT, preferred_element_type=F32))
        mx = jnp.max(logits, axis=0, keepdims=True)
        e = jnp.exp(logits - mx)
        aff_ref[:, rows] = e / jnp.sum(e, axis=0, keepdims=True)


def _out_projection(o_merge, lse_merge, o_rest, x, w_out, norm_g, wr_hi, wr_lo):
    t_tokens = x.shape[0]
    tm = PROJ_TM
    n_merge = len(o_merge)
    row = lambda i: (i, 0)
    fixed = lambda i: (0, 0)
    merge_args = list(o_merge) + list(lse_merge)
    in_specs = ([pl.BlockSpec((tm, o.shape[1]), row) for o in o_merge]
                + [pl.BlockSpec((LSE_ROWS, tm), lambda i: (0, i)) for _ in lse_merge])
    if n_merge:
        head = np.arange(o_merge[0].shape[1]) // HEAD_DIM
        expand = (np.arange(2 * LSE_ROWS)[:, None] % LSE_ROWS == head[None, :]).astype(np.float32)
        merge_args.append(jnp.asarray(expand, dtype=BF16))
        in_specs.append(pl.BlockSpec(expand.shape, fixed))
    in_specs = (in_specs
                + [pl.BlockSpec((tm, o_rest.shape[1]), row),
                   pl.BlockSpec((tm, D_MODEL), row),
                   pl.BlockSpec(w_out.shape, fixed),
                   pl.BlockSpec((1, D_MODEL), fixed),
                   pl.BlockSpec(wr_hi.shape, fixed),
                   pl.BlockSpec(wr_lo.shape, fixed)])
    out_specs = [pl.BlockSpec((tm, D_MODEL), row),
                 pl.BlockSpec((tm, D_MODEL), row),
                 pl.BlockSpec((N_EXPERTS, tm), lambda i: (0, i))]
    out_shape = [jax.ShapeDtypeStruct((t_tokens, D_MODEL), F32),
                 jax.ShapeDtypeStruct((t_tokens, D_MODEL), BF16),
                 jax.ShapeDtypeStruct((N_EXPERTS, t_tokens), F32)]
    return pl.pallas_call(
        functools.partial(_outproj_kernel, n_merge=n_merge),
        grid=(t_tokens // tm,),
        in_specs=in_specs, out_specs=out_specs, out_shape=out_shape,
        compiler_params=_cparams(("parallel",)),
        name="out_projection",
    )(*merge_args, o_rest, x, w_out, norm_g, wr_hi, wr_lo)


def _select_kernel(aff_ref, ind_ref, tri_ref, sel_ref, off_ref, *, cap):
    aff = aff_ref[...]
    n_tok = aff.shape[1]
    bits = pltpu.bitcast(aff, I32)

    def count(pred):
        return jnp.sum(jnp.where(pred, 1.0, 0.0), axis=1, keepdims=True)

    def value_step(j, thr):
        cand = thr | lax.shift_left(jnp.int32(1), 30 - j)
        return jnp.where(count(bits >= cand) >= cap, cand, thr)

    thr = lax.fori_loop(0, 31, value_step, jnp.zeros((N_EXPERTS, 1), I32))
    gt = bits > thr
    eq = bits == thr
    need = cap - count(gt)
    idx = lax.broadcasted_iota(I32, aff.shape, 1)
    idx_bits = int(math.log2(n_tok))

    def index_step(j, bound):
        cand = bound | lax.shift_left(jnp.int32(1), idx_bits - j)
        return jnp.where(count(eq & (idx < cand)) <= need, cand, bound)

    bound = lax.fori_loop(0, idx_bits + 1, index_step, jnp.zeros((N_EXPERTS, 1), I32))
    sel = jnp.where(gt | (eq & (idx < bound)), 1.0, 0.0).astype(BF16)
    sel_ref[...] = sel
    counts = jnp.dot(sel, ind_ref[...], preferred_element_type=F32)
    offs = jnp.dot(counts.astype(BF16), tri_ref[...], preferred_element_type=F32)
    off_ref[...] = offs.astype(I32)


def _strict_upper(n):
    return jnp.asarray(np.triu(np.ones((n, n), np.float32), k=1), dtype=BF16)


def _select(aff_t, cap):
    n_tok = aff_t.shape[1]
    ind = np.zeros((n_tok, LANES), np.float32)
    ind[np.arange(n_tok), np.arange(n_tok) // MOE_TB] = 1.0
    return pl.pallas_call(
        functools.partial(_select_kernel, cap=cap),
        out_shape=[jax.ShapeDtypeStruct((N_EXPERTS, n_tok), BF16),
                   jax.ShapeDtypeStruct((N_EXPERTS, LANES), I32)],
        compiler_params=pltpu.CompilerParams(vmem_limit_bytes=VMEM_LIMIT),
        name="select",
    )(aff_t, jnp.asarray(ind, dtype=BF16), _strict_upper(LANES))


SLOT_ALIGN = 16
ALIGN_BITS = 4
CHUNK_BITS = 6
NOT_SELECTED = -1e6


class _BlockLayout:
    def __init__(self, off_ref, b):
        self.start = [off_ref[e, b] for e in range(N_EXPERTS)]
        self.end = [off_ref[e, b + 1] for e in range(N_EXPERTS)]
        def floor_tile(v):
            return lax.shift_left(lax.shift_right_logical(v, ALIGN_BITS), ALIGN_BITS)

        self.base = [floor_tile(s) for s in self.start]
        self.shift = [s - a for s, a in zip(self.start, self.base)]
        span = [en - a for en, a in zip(self.end, self.base)]
        self.n_chunks = functools.reduce(
            jnp.maximum, [lax.shift_right_logical(sp, CHUNK_BITS) for sp in span]) + 1
        tail = [floor_tile(sp) for sp in span]
        self.tail_chunk = [lax.shift_right_logical(t, CHUNK_BITS) for t in tail]
        self.tail_row = [pl.multiple_of(t & (MOE_W - 1), SLOT_ALIGN) for t in tail]

    def window_row(self, e, chunk):
        return pl.multiple_of(self.base[e] + chunk * MOE_W, SLOT_ALIGN)


def _slot_onehots(pos, weight, chunk):
    n_tok = pos.shape[1]
    slot = (lax.broadcasted_iota(I32, (MOE_W, n_tok), 0) + chunk * MOE_W).astype(F32)
    parts = []
    for e in range(N_EXPERTS):
        hit = pos[e:e + 1, :] == slot
        val = 1.0 if weight is None else weight[e:e + 1, :]
        parts.append(jnp.where(hit, val, 0.0).astype(BF16))
    return jnp.concatenate(parts, axis=0)


def _window_positions(sel_ref, tri_ref, layout):
    sel = sel_ref[...]
    rank = jnp.dot(sel, tri_ref[...], preferred_element_type=F32)
    shift = jnp.concatenate([jnp.full((1, 1), s, I32) for s in layout.shift], axis=0).astype(F32)
    return jnp.where(sel > 0, rank + shift, NOT_SELECTED)


def _dispatch_kernel(off_ref, sel_ref, h_ref, tri_ref, xe_ref, stage_ref, extra_ref, tail_ref, zero_ref, sem,
                     sem_extra):
    b = pl.program_id(0)
    nb = pl.num_programs(0)
    slot = b % 2
    layout = _BlockLayout(off_ref, b)
    pos = _window_positions(sel_ref, tri_ref, layout)
    h = h_ref[...]

    def rows_for(chunk):
        return jnp.dot(_slot_onehots(pos, None, chunk), h, preferred_element_type=F32)

    def copy(src, e, row0, s):
        return pltpu.make_async_copy(src.at[e], xe_ref.at[e, pl.ds(row0, MOE_W)], s)

    @pl.when(b == 0)
    def _():
        tail_ref[...] = jnp.zeros(tail_ref.shape, BF16)
        zero_ref[...] = jnp.zeros(zero_ref.shape, BF16)
        pad = zero_ref.shape[0]
        fills = [pltpu.make_async_copy(zero_ref, xe_ref.at[e, pl.ds(xe_ref.shape[1] - pad, pad)], sem_extra.at[0])
                 for e in range(N_EXPERTS)]
        for f in fills:
            f.start()
        for f in fills:
            f.wait()

    rows0 = rows_for(0)

    for e in range(N_EXPERTS):
        r0 = e * MOE_W
        first = rows0[r0:r0 + SLOT_ALIGN] + tail_ref[e].astype(F32)
        stage_ref[slot, e, :SLOT_ALIGN] = first.astype(BF16)
        stage_ref[slot, e, SLOT_ALIGN:] = rows0[r0 + SLOT_ALIGN:r0 + MOE_W].astype(BF16)
    for e in range(N_EXPERTS):
        tile = stage_ref[slot, e, pl.ds(layout.tail_row[e], SLOT_ALIGN), :]
        tail_ref[e] = jnp.where(layout.tail_chunk[e] == 0, tile, tail_ref[e])

    @pl.when(b > 0)
    def _():
        for e in range(N_EXPERTS):
            copy(stage_ref.at[1 - slot], e, 0, sem.at[1 - slot]).wait()

    for e in range(N_EXPERTS):
        copy(stage_ref.at[slot], e, layout.window_row(e, 0), sem.at[slot]).start()

    def overflow(chunk, carry):
        more = rows_for(chunk).astype(BF16)
        for e in range(N_EXPERTS):
            extra_ref[e] = more[e * MOE_W:(e + 1) * MOE_W]
        for e in range(N_EXPERTS):
            copy(extra_ref, e, layout.window_row(e, chunk), sem_extra.at[0]).start()
        for e in range(N_EXPERTS):
            @pl.when(layout.tail_chunk[e] == chunk)
            def _():
                tail_ref[e] = extra_ref[e, pl.ds(layout.tail_row[e], SLOT_ALIGN), :]
        for e in range(N_EXPERTS):
            copy(extra_ref, e, 0, sem_extra.at[0]).wait()
        return carry

    lax.fori_loop(1, layout.n_chunks, overflow, 0)

    @pl.when(b == nb - 1)
    def _():
        for e in range(N_EXPERTS):
            copy(stage_ref.at[slot], e, 0, sem.at[slot]).wait()


def _slot_rows(cap):
    pad = ((SLOT_ALIGN + MOE_TB) // MOE_W + 1) * MOE_W
    return cap + -(-pad // FFN_TM) * FFN_TM


def _dispatch(offs, sel, h, cap):
    n_tok = h.shape[0]
    nb = n_tok // MOE_TB
    rows = _slot_rows(cap)
    grid_spec = pltpu.PrefetchScalarGridSpec(
        num_scalar_prefetch=1,
        grid=(nb,),
        in_specs=[pl.BlockSpec((N_EXPERTS, MOE_TB), lambda b, off: (0, b)),
                  pl.BlockSpec((MOE_TB, D_MODEL), lambda b, off: (b, 0)),
                  pl.BlockSpec((MOE_TB, MOE_TB), lambda b, off: (0, 0))],
        out_specs=pl.BlockSpec(memory_space=pl.ANY),
        scratch_shapes=[pltpu.VMEM((2, N_EXPERTS, MOE_W, D_MODEL), BF16),
                        pltpu.VMEM((N_EXPERTS, MOE_W, D_MODEL), BF16),
                        pltpu.VMEM((N_EXPERTS, SLOT_ALIGN, D_MODEL), BF16),
                        pltpu.VMEM((rows - cap, D_MODEL), BF16),
                        pltpu.SemaphoreType.DMA((2,)),
                        pltpu.SemaphoreType.DMA((1,))],
    )
    return pl.pallas_call(
        _dispatch_kernel,
        grid_spec=grid_spec,
        out_shape=jax.ShapeDtypeStruct((N_EXPERTS, rows, D_MODEL), BF16),
        compiler_params=_cparams(("arbitrary",)),
        name="dispatch",
    )(offs, sel, h, _strict_upper(MOE_TB))


class _FfnSchedule:
    def __init__(self, caps, rows):
        self.n_real = [cap // FFN_TM for cap in caps]
        self.n_zero = [r // FFN_TM - n for r, n in zip(rows, self.n_real)]
        self.zero_at, step = [], 0
        for n in self.n_zero:
            self.zero_at.append(step)
            step += n
        self.first_real = step
        self.real_at = []
        for n in self.n_real:
            self.real_at.append(step)
            step += n
        self.n_steps = step

    def in_tile(self, g, j):
        return jnp.clip(j - self.real_at[g], 0, self.n_real[g] - 1)

    def out_tile(self, g, j):
        pad = self.n_real[g] + jnp.clip(j - self.zero_at[g], 0, self.n_zero[g] - 1)
        return jnp.where(j < self.real_at[g], pad, self.in_tile(g, j))


def _ffn_kernel(*refs, sched):
    n_groups = len(sched.n_real)
    xe_refs = refs[:n_groups]
    wg_ref, wu_ref, wd_ref = refs[n_groups:n_groups + 3]
    ye_refs = refs[n_groups + 3:2 * n_groups + 3]
    wg_bf, wu_bf, wd_bf = refs[2 * n_groups + 3:]
    j = pl.program_id(1)

    @pl.when(j == 0)
    def _():
        wg_bf[...] = wg_ref[...].astype(BF16)
        wu_bf[...] = wu_ref[...].astype(BF16)
        wd_bf[...] = wd_ref[...].astype(BF16)

    for g in range(n_groups):
        @pl.when((j >= sched.zero_at[g]) & (j < sched.zero_at[g] + sched.n_zero[g]))
        def _():
            ye_refs[g][...] = jnp.zeros(ye_refs[g].shape, BF16)

        @pl.when((j >= sched.real_at[g]) & (j < sched.real_at[g] + sched.n_real[g]))
        def _():
            x = xe_refs[g][...]
            gate = jnp.dot(x, wg_bf[...], preferred_element_type=F32)
            up = jnp.dot(x, wu_bf[...], preferred_element_type=F32)
            hid = (gate * jax.nn.sigmoid(gate) * up).astype(BF16)
            ye_refs[g][...] = jnp.dot(hid, wd_bf[...], preferred_element_type=F32).astype(BF16)


def _expert_ffn(xes, w_gate, w_up, w_down, layer, caps):
    sched = _FfnSchedule(caps, [xe.shape[1] for xe in xes])

    def weight_index(e, j):
        return (layer, jnp.minimum(e + (j >= sched.first_real).astype(I32), N_EXPERTS - 1), 0, 0)

    wspec = lambda shape: pl.BlockSpec((None, None) + shape, weight_index)

    def tile_spec(g, tile_fn):
        return pl.BlockSpec((None, FFN_TM, D_MODEL), lambda e, j: (e, tile_fn(g, j), 0))

    n_groups = len(xes)
    return pl.pallas_call(
        functools.partial(_ffn_kernel, sched=sched),
        grid=(N_EXPERTS, sched.n_steps),
        in_specs=([tile_spec(g, sched.in_tile) for g in range(n_groups)]
                  + [wspec((D_MODEL, EXPERT_FF)), wspec((D_MODEL, EXPERT_FF)), wspec((EXPERT_FF, D_MODEL))]),
        out_specs=[tile_spec(g, sched.out_tile) for g in range(n_groups)],
        out_shape=[jax.ShapeDtypeStruct(xe.shape, BF16) for xe in xes],
        scratch_shapes=[pltpu.VMEM((D_MODEL, EXPERT_FF), BF16), pltpu.VMEM((D_MODEL, EXPERT_FF), BF16),
                        pltpu.VMEM((EXPERT_FF, D_MODEL), BF16)],
        compiler_params=_cparams(("arbitrary", "arbitrary")),
        name="expert_ffn",
    )(*xes, w_gate, w_up, w_down)


def _combine_kernel(off_ref, sel_ref, aff_ref, tri_ref, x_ref, ye_ref, out_ref, win_ref, extra_ref, sem, sem_extra,
                    *, rows):
    b = pl.program_id(0)
    nb = pl.num_programs(0)
    slot = b % 2

    def copy(dst, e, row0, s):
        return pltpu.make_async_copy(ye_ref.at[e, pl.ds(row0, MOE_W)], dst.at[e], s)

    def fetch(blk, dst_slot):
        ahead = _BlockLayout(off_ref, blk)
        for e in range(N_EXPERTS):
            copy(win_ref.at[dst_slot], e, ahead.window_row(e, 0), sem.at[dst_slot]).start()

    @pl.when(b == 0)
    def _():
        fetch(0, 0)

    @pl.when(b + 1 < nb)
    def _():
        fetch(b + 1, 1 - slot)

    layout = _BlockLayout(off_ref, b)
    pos = _window_positions(sel_ref, tri_ref, layout)
    aff = aff_ref[...]
    tn = (((0,), (0,)), ((), ()))

    def contribution(chunk, window):
        gates = _slot_onehots(pos, aff, chunk)
        vals = window.reshape(N_EXPERTS * MOE_W, window.shape[-1])
        return lax.dot_general(gates, vals, tn, preferred_element_type=F32)

    for e in range(N_EXPERTS):
        copy(win_ref.at[slot], e, 0, sem.at[slot]).wait()
    out_ref[...] = x_ref[...] + contribution(0, win_ref[slot])

    def overflow(chunk, carry):
        for e in range(N_EXPERTS):
            row0 = pl.multiple_of(jnp.minimum(layout.window_row(e, chunk), rows - MOE_W), SLOT_ALIGN)
            copy(extra_ref, e, row0, sem_extra.at[0]).start()
        for e in range(N_EXPERTS):
            copy(extra_ref, e, 0, sem_extra.at[0]).wait()
        out_ref[...] += contribution(chunk, extra_ref[...])
        return carry

    lax.fori_loop(1, layout.n_chunks, overflow, 0)


def _combine(offs, sel, aff_t, x, ye):
    n_tok = x.shape[0]
    nb = n_tok // MOE_TB
    rows = ye.shape[1]
    grid_spec = pltpu.PrefetchScalarGridSpec(
        num_scalar_prefetch=1,
        grid=(nb,),
        in_specs=[pl.BlockSpec((N_EXPERTS, MOE_TB), lambda b, off: (0, b)),
                  pl.BlockSpec((N_EXPERTS, MOE_TB), lambda b, off: (0, b)),
                  pl.BlockSpec((MOE_TB, MOE_TB), lambda b, off: (0, 0)),
                  pl.BlockSpec((MOE_TB, D_MODEL), lambda b, off: (b, 0)),
                  pl.BlockSpec(memory_space=pl.ANY)],
        out_specs=pl.BlockSpec((MOE_TB, D_MODEL), lambda b, off: (b, 0)),
        scratch_shapes=[pltpu.VMEM((2, N_EXPERTS, MOE_W, D_MODEL), BF16),
                        pltpu.VMEM((N_EXPERTS, MOE_W, D_MODEL), BF16),
                        pltpu.SemaphoreType.DMA((2,)),
                        pltpu.SemaphoreType.DMA((1,))],
    )
    return pl.pallas_call(
        functools.partial(_combine_kernel, rows=rows),
        grid_spec=grid_spec,
        out_shape=jax.ShapeDtypeStruct((n_tok, D_MODEL), F32),
        compiler_params=_cparams(("arbitrary",)),
        name="combine",
    )(offs, sel, aff_t, _strict_upper(MOE_TB), x, ye)


def _moe(routed, w_gate, w_up, w_down, layer):
    plans = []
    for x, h, aff_t in routed:
        cap = CAPACITY_FACTOR * x.shape[0] // N_EXPERTS
        sel, offs = _select(aff_t, cap)
        plans.append((cap, sel, offs, _dispatch(offs, sel, h, cap)))
    yes = _expert_ffn([p[3] for p in plans], w_gate, w_up, w_down, layer, [p[0] for p in plans])
    return [_combine(offs, sel, aff_t, x, ye)
            for (x, _, aff_t), (_, sel, offs, _), ye in zip(routed, plans, yes)]


def _row(v):
    return v.reshape(1, -1).astype(F32)


def _router_split(w_router):
    wt = w_router.T.astype(F32)
    hi = wt.astype(BF16)
    lo = (wt - hi.astype(F32)).astype(BF16)
    return hi, lo


def _prep_ab(norm_g, w_in, qn_a, kn_a, qn_b, kn_b, w_out, norm_ffn, w_router):
    scale = HEAD_DIM ** -0.5 * LOG2E
    perm_b = _pair_layout_perm(B_KV_HEADS, B_Q_HEADS // B_KV_HEADS)
    a3 = 3 * A_WIDTH
    bq = B_Q_HEADS * HEAD_DIM
    w_cols = np.concatenate([np.arange(a3), a3 + perm_b, np.arange(a3 + bq, w_in.shape[1])])
    gains = jnp.concatenate([
        jnp.repeat(qn_a, A_HEADS, axis=0).reshape(-1) * scale,
        jnp.repeat(kn_a, A_HEADS, axis=0).reshape(-1),
        jnp.ones((A_WIDTH,), F32),
        jnp.tile(qn_b, B_Q_HEADS) * scale,
        jnp.tile(kn_b, B_KV_HEADS),
        jnp.ones((B_KV_HEADS * HEAD_DIM,), F32)])
    a_out = A_HEADS * HEAD_DIM
    w_o = w_out[np.concatenate([np.arange(a_out), a_out + perm_b])].astype(BF16)
    return (_row(norm_g), w_in[:, w_cols].astype(BF16), _row(gains), w_o, _row(norm_ffn)) + _router_split(w_router)


def _layer_ab(x, batch, seq, tabs, prep):
    norm_g, w, gains, w_o, norm_ffn, wr_hi, wr_lo = prep
    tab_a, tab_b = tabs
    bq = B_Q_HEADS * HEAD_DIM
    n_a = A_GROUPS
    assert A_HEADS * HEAD_DIM == MXU_COLS
    plan = ([("qk", g, 0, 0, ROT_DIM // 2) for g in range(n_a)]
            + [("qk", n_a + g, 0, 0, ROT_DIM // 2) for g in range(n_a)]
            + [("v", 2 * n_a + g, 0, 0, 0) for g in range(n_a)]
            + [("qk", 3 * n_a, c * MXU_COLS, 1, HEAD_DIM // 4) for c in range(bq // MXU_COLS)]
            + [("qk", 3 * n_a + 1, 0, 1, HEAD_DIM // 4), ("v", 3 * n_a + 2, 0, 0, 0)])
    assert B_KV_HEADS * HEAD_DIM == MXU_COLS
    outs = _projection(
        x, norm_g, w, gains, (tab_a, tab_b), plan,
        (MXU_COLS,) * (2 * n_a) + (None,) * n_a + (bq, MXU_COLS, None), seq)
    qa, ka, va = outs[:n_a], outs[n_a:2 * n_a], outs[2 * n_a:3 * n_a]
    qb, kb, vb = outs[3 * n_a:]
    o_parts, lse_parts = [], []
    for gi, (window, dil) in enumerate(A_PATTERNS):
        o, lse = _attention(qa[gi], ka[gi], va[gi], batch=batch, n_keys=seq // dil, residues=dil, group=1,
                            n_pairs=A_HEADS // 2, band_r=window // (2 * dil), want_lse=True)
        o_parts.append(o)
        lse_parts.append(lse)
    (ob,) = _attention(qb, kb, vb, batch=batch, n_keys=seq, residues=1,
                       group=B_Q_HEADS // B_KV_HEADS, n_pairs=B_KV_HEADS // 2, band_r=None)
    return _out_projection(o_parts, lse_parts, ob, x, w_o, norm_ffn, wr_hi, wr_lo)


def _prep_c(norm_g, w_in, qn, kn, sink, w_out, norm_ffn, w_router):
    scale = HEAD_DIM ** -0.5 * LOG2E
    perm = _pair_layout_perm(C_KV_HEADS, C_Q_HEADS // C_KV_HEADS)
    cq = C_Q_HEADS * HEAD_DIM
    w_cols = np.concatenate([perm, np.arange(cq, w_in.shape[1])])
    gains = jnp.concatenate([jnp.tile(qn, C_Q_HEADS) * scale, jnp.tile(kn, C_KV_HEADS),
                             jnp.ones((C_KV_HEADS * HEAD_DIM,), F32)])
    return (_row(norm_g), w_in[:, w_cols].astype(BF16), _row(gains), sink.astype(F32),
            w_out[perm].astype(BF16), _row(norm_ffn)) + _router_split(w_router)


def _layer_c(x, batch, seq, tabs, prep):
    norm_g, w, gains, sink, w_o, norm_ffn, wr_hi, wr_lo = prep
    tab_a, _ = tabs
    cq = C_Q_HEADS * HEAD_DIM
    plan = ([("qk", 0, c * MXU_COLS, 0, ROT_DIM // 2) for c in range(cq // MXU_COLS)]
            + [("qk", 1, 0, 0, ROT_DIM // 2), ("v", 2, 0, 0, 0)])
    assert C_KV_HEADS * HEAD_DIM == MXU_COLS
    q, k, v = _projection(x, norm_g, w, gains, (tab_a,), plan, (cq, MXU_COLS, None), seq)
    (o,) = _attention(q, k, v, batch=batch, n_keys=seq, residues=1,
                      group=C_Q_HEADS // C_KV_HEADS, n_pairs=C_KV_HEADS // 2, band_r=C_RADIUS, sink=sink)
    return _out_projection([], [], o, x, w_o, norm_ffn, wr_hi, wr_lo)


def _encode(xs, params):
    (norm_mix, norm_ffn, w_in_ab, qn_a, kn_a, qn_b, kn_b, w_out_ab, w_in_c, qn_c, kn_c, sink_c, w_out_c,
     w_router, w_gate, w_up, w_down) = params
    shapes = [x.shape[:2] for x in xs]
    tabs = [_rope_tables(seq) for _, seq in shapes]
    xts = [x.reshape(batch * seq, D_MODEL) for x, (batch, seq) in zip(xs, shapes)]
    for layer in range(norm_mix.shape[0]):
        j = layer // 2
        if layer % 2 == 0:
            prep = _prep_ab(norm_mix[layer], w_in_ab[j], qn_a[j], kn_a[j], qn_b[j], kn_b[j], w_out_ab[j],
                            norm_ffn[layer], w_router[layer])
            mixer = _layer_ab
        else:
            prep = _prep_c(norm_mix[layer], w_in_c[j], qn_c[j], kn_c[j], sink_c[j], w_out_c[j],
                           norm_ffn[layer], w_router[layer])
            mixer = _layer_c
        routed = [mixer(xt, batch, seq, tab, prep) for xt, (batch, seq), tab in zip(xts, shapes, tabs)]
        xts = _moe(routed, w_gate, w_up, w_down, layer)
    return tuple(xt.reshape(batch, seq, D_MODEL) for xt, (batch, seq) in zip(xts, shapes))


def kernel(x_prompt, x_sample, norm_mix, norm_ffn, w_in_ab, qn_a, kn_a, qn_b, kn_b, w_out_ab, w_in_c, qn_c,
           kn_c, sink_c, w_out_c, w_router, w_gate, w_up, w_down):
    params = (norm_mix, norm_ffn, w_in_ab, qn_a, kn_a, qn_b, kn_b, w_out_ab, w_in_c, qn_c, kn_c, sink_c,
              w_out_c, w_router, w_gate, w_up, w_down)
    return _encode((x_prompt, x_sample), params)
```

```python
import functools
import math

import jax
import jax.numpy as jnp
import numpy as np
from jax import lax
from jax.experimental import pallas as pl
from jax.experimental.pallas import tpu as pltpu

F32 = jnp.float32
BF16 = jnp.bfloat16
I32 = jnp.int32

D_MODEL = 1024
HEAD_DIM = 64
LANES = 128
MXU_COLS = 256
GRID_W = 64
ROT_DIM = HEAD_DIM // 4
ROPE_THETA = 500000.0
AXIAL_THETA = 10000.0
A_PATTERNS = ((128, 1), (512, 4), (2048, 16))
A_HEADS = 4
A_GROUPS = len(A_PATTERNS)
A_WIDTH = A_GROUPS * A_HEADS * HEAD_DIM
B_Q_HEADS = 12
B_KV_HEADS = 4
C_Q_HEADS = 16
C_KV_HEADS = 4
C_RADIUS = 128
N_EXPERTS = 16
EXPERT_FF = 1024
CAPACITY_FACTOR = 2
NEG_INF = -1e30
EPS = 1e-6

VMEM_LIMIT = 56 * 1024 * 1024

PROJ_TM = 512
DENSE_TQ = 256
BAND_TQ = 256
BAND_COLS = 1024
BAND_SUBTILES = 2
ACC_ROWS = LANES + 16
LSE_ROWS = 8
DENSE_TK = 512
MOE_TB = 256
MOE_W = 64
FFN_TM = 512


def _cparams(sem):
    return pltpu.CompilerParams(dimension_semantics=sem, vmem_limit_bytes=VMEM_LIMIT)


def _head_block_diag():
    idx = np.arange(MXU_COLS) // HEAD_DIM
    return jnp.asarray((idx[:, None] == idx[None, :]).astype(np.float32), dtype=BF16)


def _rope_tables(max_len):
    pos = jnp.arange(max_len, dtype=F32)
    j = np.arange(LANES) % HEAD_DIM

    def angles(p, dim, theta):
        exps = jnp.arange(0, dim, 2, dtype=F32) / dim
        inv = jnp.power(jnp.float32(theta), -exps)
        return p[:, None] * inv[None, :]

    half = ROT_DIM // 2
    ang = angles(pos, ROT_DIM, ROPE_THETA)
    cos, sin = jnp.cos(ang), jnp.sin(ang)
    fa = np.where(j < half, j, np.where(j < ROT_DIM, j - half, 0))
    cos_l, sin_l = cos[:, fa], sin[:, fa]
    lo = jnp.asarray(j < half)[None, :]
    hi = jnp.asarray((j >= half) & (j < ROT_DIM))[None, :]
    tab_a = jnp.stack([jnp.where(lo | hi, cos_l, 1.0),
                       jnp.where(lo, -sin_l, 0.0),
                       jnp.where(hi, sin_l, 0.0)])
    hb = HEAD_DIM // 2
    qb = hb // 2
    t = jnp.arange(max_len)
    ang_r = angles((t // GRID_W).astype(F32), hb, AXIAL_THETA)
    ang_c = angles((t % GRID_W).astype(F32), hb, AXIAL_THETA)
    fb = j % qb
    is_col = jnp.asarray(j >= hb)[None, :]
    ang_l = jnp.where(is_col, ang_c[:, fb], ang_r[:, fb])
    cos_b, sin_b = jnp.cos(ang_l), jnp.sin(ang_l)
    first = jnp.asarray((j % hb) < qb)[None, :]
    tab_b = jnp.stack([cos_b, jnp.where(first, -sin_b, 0.0), jnp.where(first, 0.0, sin_b)])
    return tab_a.astype(F32), tab_b.astype(F32)


def _pair_layout_perm(n_kv, group):
    cols = []
    for p in range(n_kv // 2):
        for g in range(group):
            for par in range(2):
                h = (2 * p + par) * group + g
                cols.extend(range(h * HEAD_DIM, (h + 1) * HEAD_DIM))
    return np.asarray(cols, dtype=np.int32)


def _proj_kernel(x_ref, g_ref, w_ref, s_ref, gain_ref, *rest, plan, n_tabs):
    tab_refs = rest[:n_tabs]
    out_refs = rest[n_tabs:]
    x = x_ref[...]
    ms = jnp.mean(x * x, axis=1, keepdims=True)
    xn = ((x * lax.rsqrt(ms + EPS)) * g_ref[...]).astype(BF16)
    def project(c):
        return jnp.dot(xn, w_ref[:, c * MXU_COLS:(c + 1) * MXU_COLS], preferred_element_type=F32)

    ahead = project(0)
    for c, (kind, out_i, out_col, tab_i, shift) in enumerate(plan):
        acc = ahead
        if c + 1 < len(plan):
            ahead = project(c + 1)
        o_ref = out_refs[out_i]
        if kind == "v":
            o_ref[...] = acc.T.astype(BF16)
            continue
        ss = jnp.dot((acc * acc).astype(BF16), s_ref[...], preferred_element_type=F32)
        y = (acc * lax.rsqrt(ss * (1.0 / HEAD_DIM) + EPS)) * gain_ref[:, c * MXU_COLS:(c + 1) * MXU_COLS]
        tab = tab_refs[tab_i]
        t0 = jnp.concatenate([tab[0], tab[0]], axis=1)
        t1 = jnp.concatenate([tab[1], tab[1]], axis=1)
        t2 = jnp.concatenate([tab[2], tab[2]], axis=1)
        y = y * t0 + pltpu.roll(y, MXU_COLS - shift, 1) * t1 + pltpu.roll(y, shift, 1) * t2
        o_ref[:, out_col:out_col + MXU_COLS] = y.astype(BF16)


def _projection(x, norm_g, w, gains, tabs, plan, out_widths, seq_len):
    t_tokens = x.shape[0]
    tm = PROJ_TM
    n_in = w.shape[1]
    blocks_per_seq = seq_len // tm
    in_specs = [
        pl.BlockSpec((tm, D_MODEL), lambda i: (i, 0)),
        pl.BlockSpec((1, D_MODEL), lambda i: (0, 0)),
        pl.BlockSpec((D_MODEL, n_in), lambda i: (0, 0)),
        pl.BlockSpec((MXU_COLS, MXU_COLS), lambda i: (0, 0)),
        pl.BlockSpec((1, n_in), lambda i: (0, 0)),
    ] + [pl.BlockSpec((3, tm, LANES), lambda i: (0, i % blocks_per_seq, 0)) for _ in tabs]
    out_specs, out_shape = [], []
    for wd in out_widths:
        if wd is None:
            out_specs.append(pl.BlockSpec((None, MXU_COLS, tm),
                                          lambda i: (i // blocks_per_seq, 0, i % blocks_per_seq)))
            out_shape.append(jax.ShapeDtypeStruct((t_tokens // seq_len, MXU_COLS, seq_len), BF16))
        else:
            out_specs.append(pl.BlockSpec((tm, wd), lambda i: (i, 0)))
            out_shape.append(jax.ShapeDtypeStruct((t_tokens, wd), BF16))
    return pl.pallas_call(
        functools.partial(_proj_kernel, plan=tuple(plan), n_tabs=len(tabs)),
        grid=(t_tokens // tm,),
        in_specs=in_specs, out_specs=out_specs, out_shape=out_shape,
        compiler_params=_cparams(("parallel",)),
        name="projection",
    )(x, norm_g, w, _head_block_diag(), gains, *tabs)


LOG2E = math.log2(math.e)
LN2 = math.log(2.0)
_NT = (((1,), (1,)), ((), ()))


def _attn_kernel(*refs, group, n_pairs, tq, n_sub, tk, n_keys, band_r, has_sink, want_lse):
    refs = list(refs)
    sink_ref = refs.pop(0) if has_sink else None
    q_ref, k_ref, v_ref, o_ref = refs[:4]
    refs = refs[4:]
    lse_ref = refs.pop(0) if want_lse else None
    scratch = refs
    n_chain = 2 * n_pairs
    lane = lax.broadcasted_iota(I32, (1, LANES), 1)
    low = lane < HEAD_DIM
    half_mask = [jnp.where(low, 1.0, 0.0).astype(BF16), jnp.where(low, 0.0, 1.0).astype(BF16)]
    sub_q = tq // n_sub
    for u in range(n_sub):
        _attn_tile(sink_ref, q_ref, k_ref, v_ref, o_ref, lse_ref, scratch, half_mask,
                   slice(u * sub_q, (u + 1) * sub_q), pl.program_id(2) * n_sub + u,
                   group=group, n_pairs=n_pairs, tq=sub_q, tk=tk, n_keys=n_keys, band_r=band_r)


def _attn_tile(sink_ref, q_ref, k_ref, v_ref, o_ref, lse_ref, scratch, half_mask, q_rows, i, *,
               group, n_pairs, tq, tk, n_keys, band_r):
    has_sink = sink_ref is not None
    want_lse = lse_ref is not None
    n_chain = 2 * n_pairs
    m_cols = group * tq
    q_masked = []
    for p in range(n_pairs):
        q_p = jnp.concatenate(
            [q_ref[q_rows, (p * group + g) * LANES:(p * group + g + 1) * LANES] for g in range(group)], axis=0)
        q_masked += [q_p * half_mask[par] for par in range(2)]

    sub = lax.broadcasted_iota(I32, (ACC_ROWS, m_cols), 0)
    if has_sink:
        acc0 = jnp.where(sub >= LANES, 1.0, 0.0).astype(F32)
        m0 = []
        for c in range(n_chain):
            m0.append(jnp.concatenate(
                [jnp.full((1, tq), sink_ref[c * group + g] * LOG2E, F32) for g in range(group)], axis=1))
    else:
        acc0 = jnp.zeros((ACC_ROWS, m_cols), F32)
        m0 = [jnp.full((1, m_cols), NEG_INF, F32) for _ in range(n_chain)]

    def values_t(p, k0, width):
        ones = jnp.ones((ACC_ROWS - LANES, width), BF16)
        return jnp.concatenate([v_ref[p * LANES:(p + 1) * LANES, pl.ds(k0, width)], ones], axis=0)

    def softmax_pv(s_t, v_t, m_prev, acc_prev):
        m_new = jnp.maximum(m_prev, jnp.max(s_t, axis=0, keepdims=True))
        alpha = jnp.exp2(m_prev - m_new)
        p_t = jnp.exp2(s_t - m_new).astype(BF16)
        return m_new, alpha * acc_prev + jnp.dot(v_t, p_t, preferred_element_type=F32)

    if band_r is None:
        qm_ref, acc_ref, s_even, s_odd = scratch
        for c in range(n_chain):
            qm_ref[c] = q_masked[c]
            acc_ref[c] = acc0

        def step(kb_next, s_next, kb, s_ref, m_prev):
            k0 = pl.multiple_of(kb * tk, tk)
            m_next = []
            for p in range(n_pairs):
                v_t = values_t(p, k0, tk)
                if kb_next is not None:
                    kt = k_ref[pl.ds(pl.multiple_of(kb_next * tk, tk), tk), p * LANES:(p + 1) * LANES]
                for par in range(2):
                    c = 2 * p + par
                    if kb_next is not None:
                        s_next[c] = lax.dot_general(kt, qm_ref[c], _NT, preferred_element_type=F32)
                    m_new, acc_ref[c] = softmax_pv(s_ref[c], v_t, m_prev[c], acc_ref[c])
                    m_next.append(m_new)
            return m_next

        n_blocks = n_keys // tk
        for p in range(n_pairs):
            kt = k_ref[0:tk, p * LANES:(p + 1) * LANES]
            for par in range(2):
                s_even[2 * p + par] = lax.dot_general(kt, qm_ref[2 * p + par], _NT, preferred_element_type=F32)

        def body(j, carry):
            m = step(2 * j + 1, s_odd, 2 * j, s_even, list(carry))
            return tuple(step(2 * j + 2, s_even, 2 * j + 1, s_odd, m))

        m_fin = list(lax.fori_loop(0, n_blocks // 2 - 1, body, tuple(m0)))
        m_fin = step(n_blocks - 1, s_odd, n_blocks - 2, s_even, m_fin)
        m_fin = step(None, None, n_blocks - 1, s_odd, m_fin)
        acc_fin = [acc_ref[c] for c in range(n_chain)]
    else:
        halo = -(-band_r // LANES) * LANES
        width = min(tq + 2 * halo, n_keys)
        ws = pl.multiple_of(jnp.clip(i * tq - halo, 0, n_keys - width), LANES)
        kpos = ws + lax.broadcasted_iota(I32, (width, 1), 0)
        col = lax.broadcasted_iota(I32, (1, tq), 1)
        qpos = i * tq + jnp.concatenate([col] * group, axis=1)
        mask = jnp.abs(kpos - qpos) <= band_r
        s_all = []
        for p in range(n_pairs):
            kt = k_ref[pl.ds(ws, width), p * LANES:(p + 1) * LANES]
            s_all += [lax.dot_general(kt, q_masked[2 * p + par], _NT, preferred_element_type=F32)
                      for par in range(2)]
        m_fin, acc_fin = [], []
        for c in range(n_chain):
            m_new, acc_new = softmax_pv(jnp.where(mask, s_all[c], NEG_INF), values_t(c // 2, ws, width),
                                        m0[c], acc0)
            m_fin.append(m_new)
            acc_fin.append(acc_new)

    top = lax.broadcasted_iota(I32, (LANES, tq), 0) < HEAD_DIM
    for p in range(n_pairs):
        even, odd = acc_fin[2 * p], acc_fin[2 * p + 1]
        for g in range(group):
            cols = slice(g * tq, (g + 1) * tq)
            num = jnp.where(top, even[:LANES, cols], odd[:LANES, cols])
            den = jnp.where(top, even[LANES:LANES + 1, cols], odd[LANES:LANES + 1, cols])
            out_cols = slice((p * group + g) * LANES, (p * group + g + 1) * LANES)
            o_ref[q_rows, out_cols] = (num / den).T.astype(o_ref.dtype)
    if want_lse:
        assert group == 1 and n_chain <= LSE_ROWS
        rows = [m_fin[c] * LN2 + jnp.log(acc_fin[c][LANES:LANES + 1, :]) for c in range(n_chain)]
        lse_ref[:, q_rows] = jnp.concatenate(rows + [jnp.zeros((LSE_ROWS - n_chain, tq), F32)], axis=0)


def _attention(q, k, v_t, *, batch, n_keys, residues, group, n_pairs, band_r, sink=None, want_lse=False):
    dense = band_r is None
    sub_q = DENSE_TQ if dense else min(BAND_TQ, BAND_COLS // group, n_keys)
    n_sub = 1 if dense or n_keys < 2 * sub_q else BAND_SUBTILES
    tq = sub_q * n_sub
    wq = n_pairs * group * LANES
    wk = n_pairs * LANES
    q3 = q.reshape(batch, n_keys, residues * wq)
    k3 = k.reshape(batch, n_keys, residues * wk)
    v4 = v_t.reshape(batch, wk, n_keys, residues).transpose(0, 3, 1, 2)

    in_specs = []
    args = []
    if sink is not None:
        in_specs.append(pl.BlockSpec(memory_space=pltpu.SMEM))
        args.append(sink)
    in_specs += [
        pl.BlockSpec((None, tq, wq), lambda b, r, i: (b, i, r)),
        pl.BlockSpec((None, n_keys, wk), lambda b, r, i: (b, 0, r)),
        pl.BlockSpec((None, None, wk, n_keys), lambda b, r, i: (b, r, 0, 0)),
    ]
    args += [q3, k3, v4]
    out_specs = [pl.BlockSpec((None, tq, wq), lambda b, r, i: (b, i, r))]
    out_shape = [jax.ShapeDtypeStruct((batch, n_keys, residues * wq), BF16)]
    if want_lse:
        out_specs.append(pl.BlockSpec((None, None, LSE_ROWS, tq), lambda b, r, i: (b, r, 0, i)))
        out_shape.append(jax.ShapeDtypeStruct((batch, residues, LSE_ROWS, n_keys), F32))
    m_cols = group * tq
    scratch = []
    if dense:
        scratch = [pltpu.VMEM((2 * n_pairs, m_cols, LANES), BF16),
                   pltpu.VMEM((2 * n_pairs, ACC_ROWS, m_cols), F32)]
        scratch += [pltpu.VMEM((2 * n_pairs, DENSE_TK, m_cols), F32) for _ in range(2)]
    outs = pl.pallas_call(
        functools.partial(_attn_kernel, group=group, n_pairs=n_pairs, tq=tq, n_sub=n_sub, tk=DENSE_TK,
                          n_keys=n_keys, band_r=band_r, has_sink=sink is not None, want_lse=want_lse),
        grid=(batch, residues, n_keys // tq),
        in_specs=in_specs, out_specs=out_specs, out_shape=out_shape,
        scratch_shapes=scratch,
        compiler_params=_cparams(("parallel", "parallel", "arbitrary")),
        name="attention_dense" if dense else "attention_band",
    )(*args)
    result = [outs[0].reshape(batch * n_keys * residues, wq)]
    if want_lse:
        result.append(outs[1].transpose(2, 0, 3, 1).reshape(LSE_ROWS, batch * n_keys * residues))
    return result


def _outproj_kernel(*refs, n_merge):
    refs = list(refs)
    if n_merge:
        o_parts = refs[:n_merge]
        lse_parts = refs[n_merge:2 * n_merge]
        expand_ref = refs[2 * n_merge]
        refs = refs[2 * n_merge + 1:]
    o_rest, x_ref, w_ref, g_ref, wrh_ref, wrl_ref, xo_ref, h_ref, aff_ref = refs
    tm = x_ref.shape[0]
    halves = [slice(0, tm // 2), slice(tm // 2, tm)]
    accs = []
    for rows in halves:
        acc = x_ref[rows, :]
        k0 = 0
        if n_merge:
            lses = [r[:, rows] for r in lse_parts]
            m = functools.reduce(jnp.maximum, lses)
            ws = [jnp.exp(l - m) for l in lses]
            inv = 1.0 / sum(ws)
            num = 0.0
            for wgt, r in zip(ws, o_parts):
                wn = wgt * inv
                hi = wn.astype(BF16)
                split = jnp.concatenate([hi, (wn - hi.astype(F32)).astype(BF16)], axis=0)
                full = lax.dot_general(split, expand_ref[...], (((0,), (0,)), ((), ())),
                                       preferred_element_type=F32)
                num = num + full * r[rows, :].astype(F32)
            oa = num.astype(BF16)
            k0 = oa.shape[1]
            acc = acc + jnp.dot(oa, w_ref[:k0, :], preferred_element_type=F32)
        accs.append(acc + jnp.dot(o_rest[rows, :], w_ref[k0:, :], preferred_element_type=F32))
    for rows, acc in zip(halves, accs):
        xo_ref[rows, :] = acc
        ms = jnp.mean(acc * acc, axis=1, keepdims=True)
        h = (acc * lax.rsqrt(ms + EPS)) * g_ref[...]
        h_hi = h.astype(BF16)
        h_ref[rows, :] = h_hi
        h_lo = (h - h_hi.astype(F32)).astype(BF16)
        w_both = jnp.concatenate([wrh_ref[...], wrl_ref[...]], axis=0)
        both = lax.dot_general(w_both, h_hi, _NT, preferred_element_type=F32)
        logits = (both[:N_EXPERTS] + both[N_EXPERTS:]
                  + lax.dot_general(wrh_ref[...], h_lo, _NT, preferred_element_type=F32))
        mx = jnp.max(logits, axis=0, keepdims=True)
        e = jnp.exp(logits - mx)
        aff_ref[:, rows] = e / jnp.sum(e, axis=0, keepdims=True)


def _out_projection(o_merge, lse_merge, o_rest, x, w_out, norm_g, wr_hi, wr_lo):
    t_tokens = x.shape[0]
    tm = PROJ_TM
    n_merge = len(o_merge)
    row = lambda i: (i, 0)
    fixed = lambda i: (0, 0)
    merge_args = list(o_merge) + list(lse_merge)
    in_specs = ([pl.BlockSpec((tm, o.shape[1]), row) for o in o_merge]
                + [pl.BlockSpec((LSE_ROWS, tm), lambda i: (0, i)) for _ in lse_merge])
    if n_merge:
        head = np.arange(o_merge[0].shape[1]) // HEAD_DIM
        expand = (np.arange(2 * LSE_ROWS)[:, None] % LSE_ROWS == head[None, :]).astype(np.float32)
        merge_args.append(jnp.asarray(expand, dtype=BF16))
        in_specs.append(pl.BlockSpec(expand.shape, fixed))
    in_specs = (in_specs
                + [pl.BlockSpec((tm, o_rest.shape[1]), row),
                   pl.BlockSpec((tm, D_MODEL), row),
                   pl.BlockSpec(w_out.shape, fixed),
                   pl.BlockSpec((1, D_MODEL), fixed),
                   pl.BlockSpec(wr_hi.shape, fixed),
                   pl.BlockSpec(wr_lo.shape, fixed)])
    out_specs = [pl.BlockSpec((tm, D_MODEL), row),
                 pl.BlockSpec((tm, D_MODEL), row),
                 pl.BlockSpec((N_EXPERTS, tm), lambda i: (0, i))]
    out_shape = [jax.ShapeDtypeStruct((t_tokens, D_MODEL), F32),
                 jax.ShapeDtypeStruct((t_tokens, D_MODEL), BF16),
                 jax.ShapeDtypeStruct((N_EXPERTS, t_tokens), F32)]
    return pl.pallas_call(
        functools.partial(_outproj_kernel, n_merge=n_merge),
        grid=(t_tokens // tm,),
        in_specs=in_specs, out_specs=out_specs, out_shape=out_shape,
        compiler_params=_cparams(("parallel",)),
        name="out_projection",
    )(*merge_args, o_rest, x, w_out, norm_g, wr_hi, wr_lo)


def _select_kernel(aff_ref, ind_ref, tri_ref, sel_ref, off_ref, *, cap):
    aff = aff_ref[...]
    n_tok = aff.shape[1]
    bits = pltpu.bitcast(aff, I32)

    def count(pred):
        return jnp.sum(jnp.where(pred, 1.0, 0.0), axis=1, keepdims=True)

    def value_step(j, thr):
        cand = thr | lax.shift_left(jnp.int32(1), 30 - j)
        return jnp.where(count(bits >= cand) >= cap, cand, thr)

    thr = lax.fori_loop(0, 31, value_step, jnp.zeros((N_EXPERTS, 1), I32))
    gt = bits > thr
    eq = bits == thr
    need = cap - count(gt)
    idx = lax.broadcasted_iota(I32, aff.shape, 1)
    idx_bits = int(math.log2(n_tok))

    def index_step(j, bound):
        cand = bound | lax.shift_left(jnp.int32(1), idx_bits - j)
        return jnp.where(count(eq & (idx < cand)) <= need, cand, bound)

    bound = lax.fori_loop(0, idx_bits + 1, index_step, jnp.zeros((N_EXPERTS, 1), I32))
    sel = jnp.where(gt | (eq & (idx < bound)), 1.0, 0.0).astype(BF16)
    sel_ref[...] = sel
    counts = jnp.dot(sel, ind_ref[...], preferred_element_type=F32)
    offs = jnp.dot(counts.astype(BF16), tri_ref[...], preferred_element_type=F32)
    off_ref[...] = offs.astype(I32)


def _strict_upper(n):
    return jnp.asarray(np.triu(np.ones((n, n), np.float32), k=1), dtype=BF16)


def _select(aff_t, cap):
    n_tok = aff_t.shape[1]
    ind = np.zeros((n_tok, LANES), np.float32)
    ind[np.arange(n_tok), np.arange(n_tok) // MOE_TB] = 1.0
    return pl.pallas_call(
        functools.partial(_select_kernel, cap=cap),
        out_shape=[jax.ShapeDtypeStruct((N_EXPERTS, n_tok), BF16),
                   jax.ShapeDtypeStruct((N_EXPERTS, LANES), I32)],
        compiler_params=pltpu.CompilerParams(vmem_limit_bytes=VMEM_LIMIT),
        name="select",
    )(aff_t, jnp.asarray(ind, dtype=BF16), _strict_upper(LANES))


SLOT_ALIGN = 16
ALIGN_BITS = 4
CHUNK_BITS = 6
NOT_SELECTED = -1e6


class _BlockLayout:
    def __init__(self, off_ref, b):
        self.start = [off_ref[e, b] for e in range(N_EXPERTS)]
        self.end = [off_ref[e, b + 1] for e in range(N_EXPERTS)]
        def floor_tile(v):
            return lax.shift_left(lax.shift_right_logical(v, ALIGN_BITS), ALIGN_BITS)

        self.base = [floor_tile(s) for s in self.start]
        self.shift = [s - a for s, a in zip(self.start, self.base)]
        span = [en - a for en, a in zip(self.end, self.base)]
        self.n_chunks = functools.reduce(
            jnp.maximum, [lax.shift_right_logical(sp, CHUNK_BITS) for sp in span]) + 1
        tail = [floor_tile(sp) for sp in span]
        self.tail_chunk = [lax.shift_right_logical(t, CHUNK_BITS) for t in tail]
        self.tail_row = [pl.multiple_of(t & (MOE_W - 1), SLOT_ALIGN) for t in tail]

    def window_row(self, e, chunk):
        return pl.multiple_of(self.base[e] + chunk * MOE_W, SLOT_ALIGN)


def _slot_onehots(pos, weight, chunk):
    n_tok = pos.shape[1]
    slot = (lax.broadcasted_iota(I32, (MOE_W, n_tok), 0) + chunk * MOE_W).astype(F32)
    parts = []
    for e in range(N_EXPERTS):
        hit = pos[e:e + 1, :] == slot
        val = 1.0 if weight is None else weight[e:e + 1, :]
        parts.append(jnp.where(hit, val, 0.0).astype(BF16))
    return jnp.concatenate(parts, axis=0)


def _window_positions(sel_ref, tri_ref, layout):
    sel = sel_ref[...]
    rank = jnp.dot(sel, tri_ref[...], preferred_element_type=F32)
    shift = jnp.concatenate([jnp.full((1, 1), s, I32) for s in layout.shift], axis=0).astype(F32)
    return jnp.where(sel > 0, rank + shift, NOT_SELECTED)


def _dispatch_kernel(off_ref, sel_ref, h_ref, tri_ref, xe_ref, stage_ref, extra_ref, tail_ref, zero_ref, sem,
                     sem_extra):
    b = pl.program_id(0)
    nb = pl.num_programs(0)
    slot = b % 2
    layout = _BlockLayout(off_ref, b)
    pos = _window_positions(sel_ref, tri_ref, layout)
    h = h_ref[...]

    def rows_for(chunk):
        return jnp.dot(_slot_onehots(pos, None, chunk), h, preferred_element_type=F32)

    def copy(src, e, row0, s):
        return pltpu.make_async_copy(src.at[e], xe_ref.at[e, pl.ds(row0, MOE_W)], s)

    @pl.when(b == 0)
    def _():
        tail_ref[...] = jnp.zeros(tail_ref.shape, BF16)
        zero_ref[...] = jnp.zeros(zero_ref.shape, BF16)
        pad = zero_ref.shape[0]
        fills = [pltpu.make_async_copy(zero_ref, xe_ref.at[e, pl.ds(xe_ref.shape[1] - pad, pad)], sem_extra.at[0])
                 for e in range(N_EXPERTS)]
        for f in fills:
            f.start()
        for f in fills:
            f.wait()

    rows0 = rows_for(0)

    for e in range(N_EXPERTS):
        r0 = e * MOE_W
        first = rows0[r0:r0 + SLOT_ALIGN] + tail_ref[e].astype(F32)
        stage_ref[slot, e, :SLOT_ALIGN] = first.astype(BF16)
        stage_ref[slot, e, SLOT_ALIGN:] = rows0[r0 + SLOT_ALIGN:r0 + MOE_W].astype(BF16)
    for e in range(N_EXPERTS):
        tile = stage_ref[slot, e, pl.ds(layout.tail_row[e], SLOT_ALIGN), :]
        tail_ref[e] = jnp.where(layout.tail_chunk[e] == 0, tile, tail_ref[e])

    @pl.when(b > 0)
    def _():
        for e in range(N_EXPERTS):
            copy(stage_ref.at[1 - slot], e, 0, sem.at[1 - slot]).wait()

    for e in range(N_EXPERTS):
        copy(stage_ref.at[slot], e, layout.window_row(e, 0), sem.at[slot]).start()

    def overflow(chunk, carry):
        more = rows_for(chunk).astype(BF16)
        for e in range(N_EXPERTS):
            extra_ref[e] = more[e * MOE_W:(e + 1) * MOE_W]
        for e in range(N_EXPERTS):
            copy(extra_ref, e, layout.window_row(e, chunk), sem_extra.at[0]).start()
        for e in range(N_EXPERTS):
            @pl.when(layout.tail_chunk[e] == chunk)
            def _():
                tail_ref[e] = extra_ref[e, pl.ds(layout.tail_row[e], SLOT_ALIGN), :]
        for e in range(N_EXPERTS):
            copy(extra_ref, e, 0, sem_extra.at[0]).wait()
        return carry

    lax.fori_loop(1, layout.n_chunks, overflow, 0)

    @pl.when(b == nb - 1)
    def _():
        for e in range(N_EXPERTS):
            copy(stage_ref.at[slot], e, 0, sem.at[slot]).wait()


def _slot_rows(cap):
    pad = ((SLOT_ALIGN + MOE_TB) // MOE_W + 1) * MOE_W
    return cap + -(-pad // FFN_TM) * FFN_TM


def _dispatch(offs, sel, h, cap):
    n_tok = h.shape[0]
    nb = n_tok // MOE_TB
    rows = _slot_rows(cap)
    grid_spec = pltpu.PrefetchScalarGridSpec(
        num_scalar_prefetch=1,
        grid=(nb,),
        in_specs=[pl.BlockSpec((N_EXPERTS, MOE_TB), lambda b, off: (0, b)),
                  pl.BlockSpec((MOE_TB, D_MODEL), lambda b, off: (b, 0)),
                  pl.BlockSpec((MOE_TB, MOE_TB), lambda b, off: (0, 0))],
        out_specs=pl.BlockSpec(memory_space=pl.ANY),
        scratch_shapes=[pltpu.VMEM((2, N_EXPERTS, MOE_W, D_MODEL), BF16),
                        pltpu.VMEM((N_EXPERTS, MOE_W, D_MODEL), BF16),
                        pltpu.VMEM((N_EXPERTS, SLOT_ALIGN, D_MODEL), BF16),
                        pltpu.VMEM((rows - cap, D_MODEL), BF16),
                        pltpu.SemaphoreType.DMA((2,)),
                        pltpu.SemaphoreType.DMA((1,))],
    )
    return pl.pallas_call(
        _dispatch_kernel,
        grid_spec=grid_spec,
        out_shape=jax.ShapeDtypeStruct((N_EXPERTS, rows, D_MODEL), BF16),
        compiler_params=_cparams(("arbitrary",)),
        name="dispatch",
    )(offs, sel, h, _strict_upper(MOE_TB))


def _ffn_kernel(*refs, first_step, n_tiles):
    n_groups = len(n_tiles)
    xe_refs = refs[:n_groups]
    wg_ref, wu_ref, wd_ref = refs[n_groups:n_groups + 3]
    ye_refs = refs[n_groups + 3:2 * n_groups + 3]
    wg_bf, wu_bf, wd_bf = refs[2 * n_groups + 3:]
    j = pl.program_id(1)

    @pl.when(j == 0)
    def _():
        wg_bf[...] = wg_ref[...].astype(BF16)
        wu_bf[...] = wu_ref[...].astype(BF16)
        wd_bf[...] = wd_ref[...].astype(BF16)

    for g in range(n_groups):
        @pl.when((j >= first_step[g]) & (j < first_step[g] + n_tiles[g]))
        def _():
            x = xe_refs[g][...]
            gate = jnp.dot(x, wg_bf[...], preferred_element_type=F32)
            up = jnp.dot(x, wu_bf[...], preferred_element_type=F32)
            hid = (gate * jax.nn.sigmoid(gate) * up).astype(BF16)
            ye_refs[g][...] = jnp.dot(hid, wd_bf[...], preferred_element_type=F32).astype(BF16)


def _expert_ffn(xes, w_gate, w_up, w_down, layer, caps):
    n_groups = len(xes)
    n_tiles = [cap // FFN_TM for cap in caps]
    first_step = [sum(n_tiles[:g]) for g in range(n_groups)]

    def weight_index(e, j):
        return (layer, jnp.minimum(e + (j >= 1).astype(I32), N_EXPERTS - 1), 0, 0)

    wspec = lambda shape: pl.BlockSpec((None, None) + shape, weight_index)

    def tile_spec(g):
        return pl.BlockSpec((None, FFN_TM, D_MODEL),
                            lambda e, j: (e, jnp.clip(j - first_step[g], 0, n_tiles[g] - 1), 0))

    return pl.pallas_call(
        functools.partial(_ffn_kernel, first_step=tuple(first_step), n_tiles=tuple(n_tiles)),
        grid=(N_EXPERTS, sum(n_tiles)),
        in_specs=([tile_spec(g) for g in range(n_groups)]
                  + [wspec((D_MODEL, EXPERT_FF)), wspec((D_MODEL, EXPERT_FF)), wspec((EXPERT_FF, D_MODEL))]),
        out_specs=[tile_spec(g) for g in range(n_groups)],
        out_shape=[jax.ShapeDtypeStruct(xe.shape, BF16) for xe in xes],
        input_output_aliases={g: g for g in range(n_groups)},
        scratch_shapes=[pltpu.VMEM((D_MODEL, EXPERT_FF), BF16), pltpu.VMEM((D_MODEL, EXPERT_FF), BF16),
                        pltpu.VMEM((EXPERT_FF, D_MODEL), BF16)],
        compiler_params=_cparams(("arbitrary", "arbitrary")),
        name="expert_ffn",
    )(*xes, w_gate, w_up, w_down)


def _combine_kernel(off_ref, sel_ref, aff_ref, tri_ref, x_ref, ye_ref, out_ref, win_ref, extra_ref, sem, sem_extra,
                    *, rows):
    b = pl.program_id(0)
    nb = pl.num_programs(0)
    slot = b % 2

    def copy(dst, e, row0, s):
        return pltpu.make_async_copy(ye_ref.at[e, pl.ds(row0, MOE_W)], dst.at[e], s)

    def fetch(blk, dst_slot):
        ahead = _BlockLayout(off_ref, blk)
        for e in range(N_EXPERTS):
            copy(win_ref.at[dst_slot], e, ahead.window_row(e, 0), sem.at[dst_slot]).start()

    @pl.when(b == 0)
    def _():
        fetch(0, 0)

    @pl.when(b + 1 < nb)
    def _():
        fetch(b + 1, 1 - slot)

    layout = _BlockLayout(off_ref, b)
    pos = _window_positions(sel_ref, tri_ref, layout)
    aff = aff_ref[...]
    tn = (((0,), (0,)), ((), ()))

    def contribution(chunk, window):
        gates = _slot_onehots(pos, aff, chunk)
        vals = window.reshape(N_EXPERTS * MOE_W, window.shape[-1])
        return lax.dot_general(gates, vals, tn, preferred_element_type=F32)

    for e in range(N_EXPERTS):
        copy(win_ref.at[slot], e, 0, sem.at[slot]).wait()
    out_ref[...] = x_ref[...] + contribution(0, win_ref[slot])

    def overflow(chunk, carry):
        for e in range(N_EXPERTS):
            row0 = pl.multiple_of(jnp.minimum(layout.window_row(e, chunk), rows - MOE_W), SLOT_ALIGN)
            copy(extra_ref, e, row0, sem_extra.at[0]).start()
        for e in range(N_EXPERTS):
            copy(extra_ref, e, 0, sem_extra.at[0]).wait()
        out_ref[...] += contribution(chunk, extra_ref[...])
        return carry

    lax.fori_loop(1, layout.n_chunks, overflow, 0)


def _combine(offs, sel, aff_t, x, ye):
    n_tok = x.shape[0]
    nb = n_tok // MOE_TB
    rows = ye.shape[1]
    grid_spec = pltpu.PrefetchScalarGridSpec(
        num_scalar_prefetch=1,
        grid=(nb,),
        in_specs=[pl.BlockSpec((N_EXPERTS, MOE_TB), lambda b, off: (0, b)),
                  pl.BlockSpec((N_EXPERTS, MOE_TB), lambda b, off: (0, b)),
                  pl.BlockSpec((MOE_TB, MOE_TB), lambda b, off: (0, 0)),
                  pl.BlockSpec((MOE_TB, D_MODEL), lambda b, off: (b, 0)),
                  pl.BlockSpec(memory_space=pl.ANY)],
        out_specs=pl.BlockSpec((MOE_TB, D_MODEL), lambda b, off: (b, 0)),
        scratch_shapes=[pltpu.VMEM((2, N_EXPERTS, MOE_W, D_MODEL), BF16),
                        pltpu.VMEM((N_EXPERTS, MOE_W, D_MODEL), BF16),
                        pltpu.SemaphoreType.DMA((2,)),
                        pltpu.SemaphoreType.DMA((1,))],
    )
    return pl.pallas_call(
        functools.partial(_combine_kernel, rows=rows),
        grid_spec=grid_spec,
        out_shape=jax.ShapeDtypeStruct((n_tok, D_MODEL), F32),
        compiler_params=_cparams(("arbitrary",)),
        name="combine",
    )(offs, sel, aff_t, _strict_upper(MOE_TB), x, ye)


def _moe(routed, w_gate, w_up, w_down, layer):
    plans = []
    for x, h, aff_t in routed:
        cap = CAPACITY_FACTOR * x.shape[0] // N_EXPERTS
        sel, offs = _select(aff_t, cap)
        plans.append((cap, sel, offs, _dispatch(offs, sel, h, cap)))
    yes = _expert_ffn([p[3] for p in plans], w_gate, w_up, w_down, layer, [p[0] for p in plans])
    return [_combine(offs, sel, aff_t, x, ye)
            for (x, _, aff_t), (_, sel, offs, _), ye in zip(routed, plans, yes)]


def _row(v):
    return v.reshape(1, -1).astype(F32)


def _router_split(w_router):
    wt = w_router.T.astype(F32)
    hi = wt.astype(BF16)
    lo = (wt - hi.astype(F32)).astype(BF16)
    return hi, lo


def _prep_ab(norm_g, w_in, qn_a, kn_a, qn_b, kn_b, w_out, norm_ffn, w_router):
    scale = HEAD_DIM ** -0.5 * LOG2E
    perm_b = _pair_layout_perm(B_KV_HEADS, B_Q_HEADS // B_KV_HEADS)
    a3 = 3 * A_WIDTH
    bq = B_Q_HEADS * HEAD_DIM
    w_cols = np.concatenate([np.arange(a3), a3 + perm_b, np.arange(a3 + bq, w_in.shape[1])])
    gains = jnp.concatenate([
        jnp.repeat(qn_a, A_HEADS, axis=0).reshape(-1) * scale,
        jnp.repeat(kn_a, A_HEADS, axis=0).reshape(-1),
        jnp.ones((A_WIDTH,), F32),
        jnp.tile(qn_b, B_Q_HEADS) * scale,
        jnp.tile(kn_b, B_KV_HEADS),
        jnp.ones((B_KV_HEADS * HEAD_DIM,), F32)])
    a_out = A_HEADS * HEAD_DIM
    w_o = w_out[np.concatenate([np.arange(a_out), a_out + perm_b])].astype(BF16)
    return (_row(norm_g), w_in[:, w_cols].astype(BF16), _row(gains), w_o, _row(norm_ffn)) + _router_split(w_router)


def _layer_ab(x, batch, seq, tabs, prep):
    norm_g, w, gains, w_o, norm_ffn, wr_hi, wr_lo = prep
    tab_a, tab_b = tabs
    bq = B_Q_HEADS * HEAD_DIM
    n_a = A_GROUPS
    assert A_HEADS * HEAD_DIM == MXU_COLS
    plan = ([("qk", g, 0, 0, ROT_DIM // 2) for g in range(n_a)]
            + [("qk", n_a + g, 0, 0, ROT_DIM // 2) for g in range(n_a)]
            + [("v", 2 * n_a + g, 0, 0, 0) for g in range(n_a)]
            + [("qk", 3 * n_a, c * MXU_COLS, 1, HEAD_DIM // 4) for c in range(bq // MXU_COLS)]
            + [("qk", 3 * n_a + 1, 0, 1, HEAD_DIM // 4), ("v", 3 * n_a + 2, 0, 0, 0)])
    assert B_KV_HEADS * HEAD_DIM == MXU_COLS
    outs = _projection(
        x, norm_g, w, gains, (tab_a, tab_b), plan,
        (MXU_COLS,) * (2 * n_a) + (None,) * n_a + (bq, MXU_COLS, None), seq)
    qa, ka, va = outs[:n_a], outs[n_a:2 * n_a], outs[2 * n_a:3 * n_a]
    qb, kb, vb = outs[3 * n_a:]
    o_parts, lse_parts = [], []
    for gi, (window, dil) in enumerate(A_PATTERNS):
        o, lse = _attention(qa[gi], ka[gi], va[gi], batch=batch, n_keys=seq // dil, residues=dil, group=1,
                            n_pairs=A_HEADS // 2, band_r=window // (2 * dil), want_lse=True)
        o_parts.append(o)
        lse_parts.append(lse)
    (ob,) = _attention(qb, kb, vb, batch=batch, n_keys=seq, residues=1,
                       group=B_Q_HEADS // B_KV_HEADS, n_pairs=B_KV_HEADS // 2, band_r=None)
    return _out_projection(o_parts, lse_parts, ob, x, w_o, norm_ffn, wr_hi, wr_lo)


def _prep_c(norm_g, w_in, qn, kn, sink, w_out, norm_ffn, w_router):
    scale = HEAD_DIM ** -0.5 * LOG2E
    perm = _pair_layout_perm(C_KV_HEADS, C_Q_HEADS // C_KV_HEADS)
    cq = C_Q_HEADS * HEAD_DIM
    w_cols = np.concatenate([perm, np.arange(cq, w_in.shape[1])])
    gains = jnp.concatenate([jnp.tile(qn, C_Q_HEADS) * scale, jnp.tile(kn, C_KV_HEADS),
                             jnp.ones((C_KV_HEADS * HEAD_DIM,), F32)])
    return (_row(norm_g), w_in[:, w_cols].astype(BF16), _row(gains), sink.astype(F32),
            w_out[perm].astype(BF16), _row(norm_ffn)) + _router_split(w_router)


def _layer_c(x, batch, seq, tabs, prep):
    norm_g, w, gains, sink, w_o, norm_ffn, wr_hi, wr_lo = prep
    tab_a, _ = tabs
    cq = C_Q_HEADS * HEAD_DIM
    plan = ([("qk", 0, c * MXU_COLS, 0, ROT_DIM // 2) for c in range(cq // MXU_COLS)]
            + [("qk", 1, 0, 0, ROT_DIM // 2), ("v", 2, 0, 0, 0)])
    assert C_KV_HEADS * HEAD_DIM == MXU_COLS
    q, k, v = _projection(x, norm_g, w, gains, (tab_a,), plan, (cq, MXU_COLS, None), seq)
    (o,) = _attention(q, k, v, batch=batch, n_keys=seq, residues=1,
                      group=C_Q_HEADS // C_KV_HEADS, n_pairs=C_KV_HEADS // 2, band_r=C_RADIUS, sink=sink)
    return _out_projection([], [], o, x, w_o, norm_ffn, wr_hi, wr_lo)


def _encode(xs, params):
    (norm_mix, norm_ffn, w_in_ab, qn_a, kn_a, qn_b, kn_b, w_out_ab, w_in_c, qn_c, kn_c, sink_c, w_out_c,
     w_router, w_gate, w_up, w_down) = params
    shapes = [x.shape[:2] for x in xs]
    tabs = [_rope_tables(seq) for _, seq in shapes]
    xts = [x.reshape(batch * seq, D_MODEL) for x, (batch, seq) in zip(xs, shapes)]
    for layer in range(norm_mix.shape[0]):
        j = layer // 2
        if layer % 2 == 0:
            prep = _prep_ab(norm_mix[layer], w_in_ab[j], qn_a[j], kn_a[j], qn_b[j], kn_b[j], w_out_ab[j],
                            norm_ffn[layer], w_router[layer])
            mixer = _layer_ab
        else:
            prep = _prep_c(norm_mix[layer], w_in_c[j], qn_c[j], kn_c[j], sink_c[j], w_out_c[j],
                           norm_ffn[layer], w_router[layer])
            mixer = _layer_c
        routed = [mixer(xt, batch, seq, tab, prep) for xt, (batch, seq), tab in zip(xts, shapes, tabs)]
        xts = _moe(routed, w_gate, w_up, w_down, layer)
    return tuple(xt.reshape(batch, seq, D_MODEL) for xt, (batch, seq) in zip(xts, shapes))


def kernel(x_prompt, x_sample, norm_mix, norm_ffn, w_in_ab, qn_a, kn_a, qn_b, kn_b, w_out_ab, w_in_c, qn_c,
           kn_c, sink_c, w_out_c, w_router, w_gate, w_up, w_down):
    params = (norm_mix, norm_ffn, w_in_ab, qn_a, kn_a, qn_b, kn_b, w_out_ab, w_in_c, qn_c, kn_c, sink_c,
              w_out_c, w_router, w_gate, w_up, w_down)
    return _encode((x_prompt, x_sample), params)
```

```python
import functools
import math

import jax
import jax.numpy as jnp
import numpy as np
from jax import lax
from jax.experimental import pallas as pl
from jax.experimental.pallas import tpu as pltpu

F32 = jnp.float32
BF16 = jnp.bfloat16
I32 = jnp.int32

D_MODEL = 1024
HEAD_DIM = 64
LANES = 128
MXU_COLS = 256
GRID_W = 64
ROT_DIM = HEAD_DIM // 4
ROPE_THETA = 500000.0
AXIAL_THETA = 10000.0
A_PATTERNS = ((128, 1), (512, 4), (2048, 16))
A_HEADS = 4
A_GROUPS = len(A_PATTERNS)
A_WIDTH = A_GROUPS * A_HEADS * HEAD_DIM
B_Q_HEADS = 12
B_KV_HEADS = 4
C_Q_HEADS = 16
C_KV_HEADS = 4
C_RADIUS = 128
N_EXPERTS = 16
EXPERT_FF = 1024
CAPACITY_FACTOR = 2
NEG_INF = -1e30
EPS = 1e-6

VMEM_LIMIT = 56 * 1024 * 1024

PROJ_TM = 512
DENSE_TQ = 256
BAND_TQ = 256
BAND_COLS = 1024
BAND_SUBTILES = 2
ACC_ROWS = LANES + 16
LSE_ROWS = 8
DENSE_TK = 512
MOE_TB = 256
MOE_W = 64
FFN_TM = 512


def _cparams(sem):
    return pltpu.CompilerParams(dimension_semantics=sem, vmem_limit_bytes=VMEM_LIMIT)


def _head_block_diag():
    idx = np.arange(MXU_COLS) // HEAD_DIM
    return jnp.asarray((idx[:, None] == idx[None, :]).astype(np.float32), dtype=BF16)


def _rope_tables(max_len):
    pos = jnp.arange(max_len, dtype=F32)
    j = np.arange(LANES) % HEAD_DIM

    def angles(p, dim, theta):
        exps = jnp.arange(0, dim, 2, dtype=F32) / dim
        inv = jnp.power(jnp.float32(theta), -exps)
        return p[:, None] * inv[None, :]

    half = ROT_DIM // 2
    ang = angles(pos, ROT_DIM, ROPE_THETA)
    cos, sin = jnp.cos(ang), jnp.sin(ang)
    fa = np.where(j < half, j, np.where(j < ROT_DIM, j - half, 0))
    cos_l, sin_l = cos[:, fa], sin[:, fa]
    lo = jnp.asarray(j < half)[None, :]
    hi = jnp.asarray((j >= half) & (j < ROT_DIM))[None, :]
    tab_a = jnp.stack([jnp.where(lo | hi, cos_l, 1.0),
                       jnp.where(lo, -sin_l, 0.0),
                       jnp.where(hi, sin_l, 0.0)])
    hb = HEAD_DIM // 2
    qb = hb // 2
    t = jnp.arange(max_len)
    ang_r = angles((t // GRID_W).astype(F32), hb, AXIAL_THETA)
    ang_c = angles((t % GRID_W).astype(F32), hb, AXIAL_THETA)
    fb = j % qb
    is_col = jnp.asarray(j >= hb)[None, :]
    ang_l = jnp.where(is_col, ang_c[:, fb], ang_r[:, fb])
    cos_b, sin_b = jnp.cos(ang_l), jnp.sin(ang_l)
    first = jnp.asarray((j % hb) < qb)[None, :]
    tab_b = jnp.stack([cos_b, jnp.where(first, -sin_b, 0.0), jnp.where(first, 0.0, sin_b)])
    return tab_a.astype(F32), tab_b.astype(F32)


def _pair_layout_perm(n_kv, group):
    cols = []
    for p in range(n_kv // 2):
        for g in range(group):
            for par in range(2):
                h = (2 * p + par) * group + g
                cols.extend(range(h * HEAD_DIM, (h + 1) * HEAD_DIM))
    return np.asarray(cols, dtype=np.int32)


def _proj_kernel(x_ref, g_ref, w_ref, s_ref, gain_ref, *rest, plan, n_tabs):
    tab_refs = rest[:n_tabs]
    out_refs = rest[n_tabs:-1]
    rows_ref = rest[-1]
    x = x_ref[...]
    tm = x.shape[0]
    ms = jnp.mean(x * x, axis=1, keepdims=True)
    xn = ((x * lax.rsqrt(ms + EPS)) * g_ref[...]).astype(BF16)

    def project(c):
        return jnp.dot(xn, w_ref[:, c * MXU_COLS:(c + 1) * MXU_COLS], preferred_element_type=F32)

    ahead = project(0)
    for c, (kind, out_i, out_col, tab_i, shift, dil) in enumerate(plan):
        acc = ahead
        if c + 1 < len(plan):
            ahead = project(c + 1)
        o_ref = out_refs[out_i]
        if kind == "v":
            o_ref[...] = acc.T.astype(BF16)
            continue
        ss = jnp.dot((acc * acc).astype(BF16), s_ref[...], preferred_element_type=F32)
        y = (acc * lax.rsqrt(ss * (1.0 / HEAD_DIM) + EPS)) * gain_ref[:, c * MXU_COLS:(c + 1) * MXU_COLS]
        tab = tab_refs[tab_i]
        t0 = jnp.concatenate([tab[0], tab[0]], axis=1)
        t1 = jnp.concatenate([tab[1], tab[1]], axis=1)
        t2 = jnp.concatenate([tab[2], tab[2]], axis=1)
        y = y * t0 + pltpu.roll(y, MXU_COLS - shift, 1) * t1 + pltpu.roll(y, shift, 1) * t2
        if dil == 1:
            o_ref[:, out_col:out_col + MXU_COLS] = y.astype(BF16)
        else:
            for half in range(MXU_COLS // LANES):
                rows_ref[half] = y[:, half * LANES:(half + 1) * LANES]
            for r in range(dil):
                for half in range(MXU_COLS // LANES):
                    piece = rows_ref[half, pl.ds(r, tm // dil, stride=dil), :]
                    col = r * MXU_COLS + half * LANES
                    o_ref[:, col:col + LANES] = piece.astype(BF16)


def _projection(x, norm_g, w, gains, tabs, plan, out_widths, seq_len):
    t_tokens = x.shape[0]
    tm = PROJ_TM
    n_in = w.shape[1]
    blocks_per_seq = seq_len // tm
    in_specs = [
        pl.BlockSpec((tm, D_MODEL), lambda i: (i, 0)),
        pl.BlockSpec((1, D_MODEL), lambda i: (0, 0)),
        pl.BlockSpec((D_MODEL, n_in), lambda i: (0, 0)),
        pl.BlockSpec((MXU_COLS, MXU_COLS), lambda i: (0, 0)),
        pl.BlockSpec((1, n_in), lambda i: (0, 0)),
    ] + [pl.BlockSpec((3, tm, LANES), lambda i: (0, i % blocks_per_seq, 0)) for _ in tabs]
    out_specs, out_shape = [], []
    for wd in out_widths:
        if wd is None:
            out_specs.append(pl.BlockSpec((None, MXU_COLS, tm),
                                          lambda i: (i // blocks_per_seq, 0, i % blocks_per_seq)))
            out_shape.append(jax.ShapeDtypeStruct((t_tokens // seq_len, MXU_COLS, seq_len), BF16))
        elif isinstance(wd, tuple):
            dil = wd[1]
            out_specs.append(pl.BlockSpec((tm // dil, dil * MXU_COLS), lambda i: (i, 0)))
            out_shape.append(jax.ShapeDtypeStruct((t_tokens // dil, dil * MXU_COLS), BF16))
        else:
            out_specs.append(pl.BlockSpec((tm, wd), lambda i: (i, 0)))
            out_shape.append(jax.ShapeDtypeStruct((t_tokens, wd), BF16))
    return pl.pallas_call(
        functools.partial(_proj_kernel, plan=tuple(plan), n_tabs=len(tabs)),
        grid=(t_tokens // tm,),
        in_specs=in_specs, out_specs=out_specs, out_shape=out_shape,
        scratch_shapes=[pltpu.VMEM((MXU_COLS // LANES, tm, LANES), F32)],
        compiler_params=_cparams(("parallel",)),
        name="projection",
    )(x, norm_g, w, _head_block_diag(), gains, *tabs)


LOG2E = math.log2(math.e)
LN2 = math.log(2.0)
_NT = (((1,), (1,)), ((), ()))


def _attn_kernel(*refs, group, n_pairs, tq, n_sub, tk, n_keys, band_r, has_sink, want_lse):
    refs = list(refs)
    sink_ref = refs.pop(0) if has_sink else None
    q_ref, k_ref, v_ref, o_ref = refs[:4]
    refs = refs[4:]
    lse_ref = refs.pop(0) if want_lse else None
    scratch = refs
    n_chain = 2 * n_pairs
    lane = lax.broadcasted_iota(I32, (1, LANES), 1)
    low = lane < HEAD_DIM
    half_mask = [jnp.where(low, 1.0, 0.0).astype(BF16), jnp.where(low, 0.0, 1.0).astype(BF16)]
    sub_q = tq // n_sub
    for u in range(n_sub):
        _attn_tile(sink_ref, q_ref, k_ref, v_ref, o_ref, lse_ref, scratch, half_mask,
                   slice(u * sub_q, (u + 1) * sub_q), pl.program_id(2) * n_sub + u,
                   group=group, n_pairs=n_pairs, tq=sub_q, tk=tk, n_keys=n_keys, band_r=band_r)


def _attn_tile(sink_ref, q_ref, k_ref, v_ref, o_ref, lse_ref, scratch, half_mask, q_rows, i, *,
               group, n_pairs, tq, tk, n_keys, band_r):
    has_sink = sink_ref is not None
    want_lse = lse_ref is not None
    n_chain = 2 * n_pairs
    m_cols = group * tq
    q_masked = []
    for p in range(n_pairs):
        q_p = jnp.concatenate(
            [q_ref[q_rows, (p * group + g) * LANES:(p * group + g + 1) * LANES] for g in range(group)], axis=0)
        q_masked += [q_p * half_mask[par] for par in range(2)]

    sub = lax.broadcasted_iota(I32, (ACC_ROWS, m_cols), 0)
    if has_sink:
        acc0 = jnp.where(sub >= LANES, 1.0, 0.0).astype(F32)
        m0 = []
        for c in range(n_chain):
            m0.append(jnp.concatenate(
                [jnp.full((1, tq), sink_ref[c * group + g] * LOG2E, F32) for g in range(group)], axis=1))
    else:
        acc0 = jnp.zeros((ACC_ROWS, m_cols), F32)
        m0 = [jnp.full((1, m_cols), NEG_INF, F32) for _ in range(n_chain)]

    def values_t(p, k0, width):
        ones = jnp.ones((ACC_ROWS - LANES, width), BF16)
        return jnp.concatenate([v_ref[p * LANES:(p + 1) * LANES, pl.ds(k0, width)], ones], axis=0)

    def softmax_pv(s_t, v_t, m_prev, acc_prev):
        m_new = jnp.maximum(m_prev, jnp.max(s_t, axis=0, keepdims=True))
        alpha = jnp.exp2(m_prev - m_new)
        p_t = jnp.exp2(s_t - m_new).astype(BF16)
        return m_new, alpha * acc_prev + jnp.dot(v_t, p_t, preferred_element_type=F32)

    if band_r is None:
        qm_ref, acc_ref, s_even, s_odd = scratch
        for c in range(n_chain):
            qm_ref[c] = q_masked[c]
            acc_ref[c] = acc0

        def step(kb_next, s_next, kb, s_ref, m_prev):
            k0 = pl.multiple_of(kb * tk, tk)
            m_next = []
            for p in range(n_pairs):
                v_t = values_t(p, k0, tk)
                if kb_next is not None:
                    kt = k_ref[pl.ds(pl.multiple_of(kb_next * tk, tk), tk), p * LANES:(p + 1) * LANES]
                for par in range(2):
                    c = 2 * p + par
                    if kb_next is not None:
                        s_next[c] = lax.dot_general(kt, qm_ref[c], _NT, preferred_element_type=F32)
                    m_new, acc_ref[c] = softmax_pv(s_ref[c], v_t, m_prev[c], acc_ref[c])
                    m_next.append(m_new)
            return m_next

        n_blocks = n_keys // tk
        for p in range(n_pairs):
            kt = k_ref[0:tk, p * LANES:(p + 1) * LANES]
            for par in range(2):
                s_even[2 * p + par] = lax.dot_general(kt, qm_ref[2 * p + par], _NT, preferred_element_type=F32)

        def body(j, carry):
            m = step(2 * j + 1, s_odd, 2 * j, s_even, list(carry))
            return tuple(step(2 * j + 2, s_even, 2 * j + 1, s_odd, m))

        m_fin = list(lax.fori_loop(0, n_blocks // 2 - 1, body, tuple(m0)))
        m_fin = step(n_blocks - 1, s_odd, n_blocks - 2, s_even, m_fin)
        m_fin = step(None, None, n_blocks - 1, s_odd, m_fin)
        acc_fin = [acc_ref[c] for c in range(n_chain)]
    else:
        halo = -(-band_r // LANES) * LANES
        width = min(tq + 2 * halo, n_keys)
        ws = pl.multiple_of(jnp.clip(i * tq - halo, 0, n_keys - width), LANES)
        kpos = ws + lax.broadcasted_iota(I32, (width, 1), 0)
        col = lax.broadcasted_iota(I32, (1, tq), 1)
        qpos = i * tq + jnp.concatenate([col] * group, axis=1)
        mask = jnp.abs(kpos - qpos) <= band_r
        s_all = []
        for p in range(n_pairs):
            kt = k_ref[pl.ds(ws, width), p * LANES:(p + 1) * LANES]
            s_all += [lax.dot_general(kt, q_masked[2 * p + par], _NT, preferred_element_type=F32)
                      for par in range(2)]
        m_fin, acc_fin = [], []
        for c in range(n_chain):
            m_new, acc_new = softmax_pv(jnp.where(mask, s_all[c], NEG_INF), values_t(c // 2, ws, width),
                                        m0[c], acc0)
            m_fin.append(m_new)
            acc_fin.append(acc_new)

    top = lax.broadcasted_iota(I32, (LANES, tq), 0) < HEAD_DIM
    for p in range(n_pairs):
        even, odd = acc_fin[2 * p], acc_fin[2 * p + 1]
        for g in range(group):
            cols = slice(g * tq, (g + 1) * tq)
            num = jnp.where(top, even[:LANES, cols], odd[:LANES, cols])
            den = jnp.where(top, even[LANES:LANES + 1, cols], odd[LANES:LANES + 1, cols])
            out_cols = slice((p * group + g) * LANES, (p * group + g + 1) * LANES)
            o_ref[q_rows, out_cols] = (num / den).T.astype(o_ref.dtype)
    if want_lse:
        assert group == 1 and n_chain <= LSE_ROWS
        rows = [m_fin[c] * LN2 + jnp.log(acc_fin[c][LANES:LANES + 1, :]) for c in range(n_chain)]
        lse_ref[:, q_rows] = jnp.concatenate(rows + [jnp.zeros((LSE_ROWS - n_chain, tq), F32)], axis=0)


def _attention(q, k, v_t, *, batch, n_keys, residues, group, n_pairs, band_r, sink=None, want_lse=False):
    dense = band_r is None
    sub_q = DENSE_TQ if dense else min(BAND_TQ, BAND_COLS // group, n_keys)
    n_sub = 1 if dense or n_keys < 2 * sub_q else BAND_SUBTILES
    tq = sub_q * n_sub
    wq = n_pairs * group * LANES
    wk = n_pairs * LANES
    q3 = q.reshape(batch, n_keys, residues * wq)
    k3 = k.reshape(batch, n_keys, residues * wk)
    v4 = v_t.reshape(batch, wk, n_keys, residues).transpose(0, 3, 1, 2)

    in_specs = []
    args = []
    if sink is not None:
        in_specs.append(pl.BlockSpec(memory_space=pltpu.SMEM))
        args.append(sink)
    in_specs += [
        pl.BlockSpec((None, tq, wq), lambda b, r, i: (b, i, r)),
        pl.BlockSpec((None, n_keys, wk), lambda b, r, i: (b, 0, r)),
        pl.BlockSpec((None, None, wk, n_keys), lambda b, r, i: (b, r, 0, 0)),
    ]
    args += [q3, k3, v4]
    out_specs = [pl.BlockSpec((None, tq, wq), lambda b, r, i: (b, i, r))]
    out_shape = [jax.ShapeDtypeStruct((batch, n_keys, residues * wq), BF16)]
    if want_lse:
        out_specs.append(pl.BlockSpec((None, None, LSE_ROWS, tq), lambda b, r, i: (b, r, 0, i)))
        out_shape.append(jax.ShapeDtypeStruct((batch, residues, LSE_ROWS, n_keys), F32))
    m_cols = group * tq
    scratch = []
    if dense:
        scratch = [pltpu.VMEM((2 * n_pairs, m_cols, LANES), BF16),
                   pltpu.VMEM((2 * n_pairs, ACC_ROWS, m_cols), F32)]
        scratch += [pltpu.VMEM((2 * n_pairs, DENSE_TK, m_cols), F32) for _ in range(2)]
    outs = pl.pallas_call(
        functools.partial(_attn_kernel, group=group, n_pairs=n_pairs, tq=tq, n_sub=n_sub, tk=DENSE_TK,
                          n_keys=n_keys, band_r=band_r, has_sink=sink is not None, want_lse=want_lse),
        grid=(batch, residues, n_keys // tq),
        in_specs=in_specs, out_specs=out_specs, out_shape=out_shape,
        scratch_shapes=scratch,
        compiler_params=_cparams(("parallel", "parallel", "arbitrary")),
        name="attention_dense" if dense else "attention_band",
    )(*args)
    result = [outs[0].reshape(batch * n_keys, residues * wq)]
    if want_lse:
        result.append(outs[1].transpose(2, 0, 3, 1).reshape(LSE_ROWS, batch * n_keys * residues))
    return result


def _outproj_kernel(*refs, merge_dils):
    refs = list(refs)
    n_merge = len(merge_dils)
    n_scratch = sum(d > 1 for d in merge_dils)
    if n_merge:
        o_parts = refs[:n_merge]
        lse_parts = refs[n_merge:2 * n_merge]
        expand_ref = refs[2 * n_merge]
        scratch = refs[len(refs) - n_scratch:]
        refs = refs[2 * n_merge + 1:len(refs) - n_scratch]
    o_rest, x_ref, w_ref, g_ref, wrh_ref, wrl_ref, xo_ref, h_ref, aff_ref = refs
    tm = x_ref.shape[0]
    if n_merge:
        ordered = []
        free_scratch = list(scratch)
        for o_ref, dil in zip(o_parts, merge_dils):
            if dil > 1:
                rows_ref = free_scratch.pop(0)
                wd = o_ref.shape[1] // dil
                for r in range(dil):
                    for half in range(wd // LANES):
                        col = r * wd + half * LANES
                        rows_ref[half, pl.ds(r, tm // dil, stride=dil), :] = o_ref[:, col:col + LANES].astype(F32)
                ordered.append(lambda rows, ref=rows_ref: jnp.concatenate(
                    [ref[half, rows, :] for half in range(ref.shape[0])], axis=1))
            else:
                ordered.append(lambda rows, ref=o_ref: ref[rows, :].astype(F32))
        o_parts = ordered
    halves = [slice(0, tm // 2), slice(tm // 2, tm)]
    accs = []
    for rows in halves:
        acc = x_ref[rows, :]
        k0 = 0
        if n_merge:
            lses = [r[:, rows] for r in lse_parts]
            m = functools.reduce(jnp.maximum, lses)
            ws = [jnp.exp(l - m) for l in lses]
            inv = 1.0 / sum(ws)
            num = 0.0
            for wgt, r in zip(ws, o_parts):
                wn = wgt * inv
                hi = wn.astype(BF16)
                split = jnp.concatenate([hi, (wn - hi.astype(F32)).astype(BF16)], axis=0)
                full = lax.dot_general(split, expand_ref[...], (((0,), (0,)), ((), ())),
                                       preferred_element_type=F32)
                num = num + full * r(rows)
            oa = num.astype(BF16)
            k0 = oa.shape[1]
            acc = acc + jnp.dot(oa, w_ref[:k0, :], preferred_element_type=F32)
        accs.append(acc + jnp.dot(o_rest[rows, :], w_ref[k0:, :], preferred_element_type=F32))
    for rows, acc in zip(halves, accs):
        xo_ref[rows, :] = acc
        ms = jnp.mean(acc * acc, axis=1, keepdims=True)
        h = (acc * lax.rsqrt(ms + EPS)) * g_ref[...]
        h_hi = h.astype(BF16)
        h_ref[rows, :] = h_hi
        h_lo = (h - h_hi.astype(F32)).astype(BF16)
        w_both = jnp.concatenate([wrh_ref[...], wrl_ref[...]], axis=0)
        both = lax.dot_general(w_both, h_hi, _NT, preferred_element_type=F32)
        logits = (both[:N_EXPERTS] + both[N_EXPERTS:]
                  + lax.dot_general(wrh_ref[...], h_lo, _NT, preferred_element_type=F32))
        mx = jnp.max(logits, axis=0, keepdims=True)
        e = jnp.exp(logits - mx)
        aff_ref[:, rows] = e / jnp.sum(e, axis=0, keepdims=True)


def _out_projection(o_merge, lse_merge, o_rest, x, w_out, norm_g, wr_hi, wr_lo):
    t_tokens = x.shape[0]
    tm = PROJ_TM
    n_merge = len(o_merge)
    row = lambda i: (i, 0)
    fixed = lambda i: (0, 0)
    merge_args = list(o_merge) + list(lse_merge)
    merge_dils = tuple(t_tokens // o.shape[0] for o in o_merge)
    in_specs = ([pl.BlockSpec((tm // dil, o.shape[1]), row) for o, dil in zip(o_merge, merge_dils)]
                + [pl.BlockSpec((LSE_ROWS, tm), lambda i: (0, i)) for _ in lse_merge])
    if n_merge:
        merged_width = o_merge[0].shape[1] // merge_dils[0]
        head = np.arange(merged_width) // HEAD_DIM
        expand = (np.arange(2 * LSE_ROWS)[:, None] % LSE_ROWS == head[None, :]).astype(np.float32)
        merge_args.append(jnp.asarray(expand, dtype=BF16))
        in_specs.append(pl.BlockSpec(expand.shape, fixed))
    in_specs = (in_specs
                + [pl.BlockSpec((tm, o_rest.shape[1]), row),
                   pl.BlockSpec((tm, D_MODEL), row),
                   pl.BlockSpec(w_out.shape, fixed),
                   pl.BlockSpec((1, D_MODEL), fixed),
                   pl.BlockSpec(wr_hi.shape, fixed),
                   pl.BlockSpec(wr_lo.shape, fixed)])
    out_specs = [pl.BlockSpec((tm, D_MODEL), row),
                 pl.BlockSpec((tm, D_MODEL), row),
                 pl.BlockSpec((N_EXPERTS, tm), lambda i: (0, i))]
    out_shape = [jax.ShapeDtypeStruct((t_tokens, D_MODEL), F32),
                 jax.ShapeDtypeStruct((t_tokens, D_MODEL), BF16),
                 jax.ShapeDtypeStruct((N_EXPERTS, t_tokens), F32)]
    return pl.pallas_call(
        functools.partial(_outproj_kernel, merge_dils=merge_dils),
        grid=(t_tokens // tm,),
        in_specs=in_specs, out_specs=out_specs, out_shape=out_shape,
        scratch_shapes=[pltpu.VMEM((merged_width // LANES, tm, LANES), F32) for dil in merge_dils if dil > 1],
        compiler_params=_cparams(("parallel",)),
        name="out_projection",
    )(*merge_args, o_rest, x, w_out, norm_g, wr_hi, wr_lo)


def _select_kernel(aff_ref, ind_ref, tri_ref, sel_ref, off_ref, *, cap):
    aff = aff_ref[...]
    n_tok = aff.shape[1]
    bits = pltpu.bitcast(aff, I32)

    def count(pred):
        return jnp.sum(jnp.where(pred, 1.0, 0.0), axis=1, keepdims=True)

    def value_step(j, thr):
        cand = thr | lax.shift_left(jnp.int32(1), 30 - j)
        return jnp.where(count(bits >= cand) >= cap, cand, thr)

    thr = lax.fori_loop(0, 31, value_step, jnp.zeros((N_EXPERTS, 1), I32))
    gt = bits > thr
    eq = bits == thr
    need = cap - count(gt)
    idx = lax.broadcasted_iota(I32, aff.shape, 1)
    idx_bits = int(math.log2(n_tok))

    def index_step(j, bound):
        cand = bound | lax.shift_left(jnp.int32(1), idx_bits - j)
        return jnp.where(count(eq & (idx < cand)) <= need, cand, bound)

    bound = lax.fori_loop(0, idx_bits + 1, index_step, jnp.zeros((N_EXPERTS, 1), I32))
    sel = jnp.where(gt | (eq & (idx < bound)), 1.0, 0.0).astype(BF16)
    sel_ref[...] = sel
    counts = jnp.dot(sel, ind_ref[...], preferred_element_type=F32)
    offs = jnp.dot(counts.astype(BF16), tri_ref[...], preferred_element_type=F32)
    off_ref[...] = offs.astype(I32)


def _strict_upper(n):
    return jnp.asarray(np.triu(np.ones((n, n), np.float32), k=1), dtype=BF16)


def _select(aff_t, cap):
    n_tok = aff_t.shape[1]
    ind = np.zeros((n_tok, LANES), np.float32)
    ind[np.arange(n_tok), np.arange(n_tok) // MOE_TB] = 1.0
    return pl.pallas_call(
        functools.partial(_select_kernel, cap=cap),
        out_shape=[jax.ShapeDtypeStruct((N_EXPERTS, n_tok), BF16),
                   jax.ShapeDtypeStruct((N_EXPERTS, LANES), I32)],
        compiler_params=pltpu.CompilerParams(vmem_limit_bytes=VMEM_LIMIT),
        name="select",
    )(aff_t, jnp.asarray(ind, dtype=BF16), _strict_upper(LANES))


SLOT_ALIGN = 16
ALIGN_BITS = 4
CHUNK_BITS = 6
NOT_SELECTED = -1e6


class _BlockLayout:
    def __init__(self, off_ref, b):
        self.start = [off_ref[e, b] for e in range(N_EXPERTS)]
        self.end = [off_ref[e, b + 1] for e in range(N_EXPERTS)]
        def floor_tile(v):
            return lax.shift_left(lax.shift_right_logical(v, ALIGN_BITS), ALIGN_BITS)

        self.base = [floor_tile(s) for s in self.start]
        self.shift = [s - a for s, a in zip(self.start, self.base)]
        span = [en - a for en, a in zip(self.end, self.base)]
        self.n_chunks = functools.reduce(
            jnp.maximum, [lax.shift_right_logical(sp, CHUNK_BITS) for sp in span]) + 1
        tail = [floor_tile(sp) for sp in span]
        self.tail_chunk = [lax.shift_right_logical(t, CHUNK_BITS) for t in tail]
        self.tail_row = [pl.multiple_of(t & (MOE_W - 1), SLOT_ALIGN) for t in tail]

    def window_row(self, e, chunk):
        return pl.multiple_of(self.base[e] + chunk * MOE_W, SLOT_ALIGN)


def _slot_onehots(pos, weight, chunk):
    n_tok = pos.shape[1]
    slot = (lax.broadcasted_iota(I32, (MOE_W, n_tok), 0) + chunk * MOE_W).astype(F32)
    parts = []
    for e in range(N_EXPERTS):
        hit = pos[e:e + 1, :] == slot
        val = 1.0 if weight is None else weight[e:e + 1, :]
        parts.append(jnp.where(hit, val, 0.0).astype(BF16))
    return jnp.concatenate(parts, axis=0)


def _window_positions(sel_ref, tri_ref, layout):
    sel = sel_ref[...]
    rank = jnp.dot(sel, tri_ref[...], preferred_element_type=F32)
    shift = jnp.concatenate([jnp.full((1, 1), s, I32) for s in layout.shift], axis=0).astype(F32)
    return jnp.where(sel > 0, rank + shift, NOT_SELECTED)


def _dispatch_kernel(off_ref, sel_ref, h_ref, tri_ref, xe_ref, stage_ref, extra_ref, tail_ref, zero_ref, sem,
                     sem_extra):
    b = pl.program_id(0)
    nb = pl.num_programs(0)
    slot = b % 2
    layout = _BlockLayout(off_ref, b)
    pos = _window_positions(sel_ref, tri_ref, layout)
    h = h_ref[...]

    def rows_for(chunk):
        return jnp.dot(_slot_onehots(pos, None, chunk), h, preferred_element_type=F32)

    def copy(src, e, row0, s):
        return pltpu.make_async_copy(src.at[e], xe_ref.at[e, pl.ds(row0, MOE_W)], s)

    @pl.when(b == 0)
    def _():
        tail_ref[...] = jnp.zeros(tail_ref.shape, BF16)
        zero_ref[...] = jnp.zeros(zero_ref.shape, BF16)
        pad = zero_ref.shape[0]
        fills = [pltpu.make_async_copy(zero_ref, xe_ref.at[e, pl.ds(xe_ref.shape[1] - pad, pad)], sem_extra.at[0])
                 for e in range(N_EXPERTS)]
        for f in fills:
            f.start()
        for f in fills:
            f.wait()

    rows0 = rows_for(0)

    for e in range(N_EXPERTS):
        r0 = e * MOE_W
        first = rows0[r0:r0 + SLOT_ALIGN] + tail_ref[e].astype(F32)
        stage_ref[slot, e, :SLOT_ALIGN] = first.astype(BF16)
        stage_ref[slot, e, SLOT_ALIGN:] = rows0[r0 + SLOT_ALIGN:r0 + MOE_W].astype(BF16)
    for e in range(N_EXPERTS):
        tile = stage_ref[slot, e, pl.ds(layout.tail_row[e], SLOT_ALIGN), :]
        tail_ref[e] = jnp.where(layout.tail_chunk[e] == 0, tile, tail_ref[e])

    @pl.when(b > 0)
    def _():
        for e in range(N_EXPERTS):
            copy(stage_ref.at[1 - slot], e, 0, sem.at[1 - slot]).wait()

    for e in range(N_EXPERTS):
        copy(stage_ref.at[slot], e, layout.window_row(e, 0), sem.at[slot]).start()

    def overflow(chunk, carry):
        more = rows_for(chunk).astype(BF16)
        for e in range(N_EXPERTS):
            extra_ref[e] = more[e * MOE_W:(e + 1) * MOE_W]
        for e in range(N_EXPERTS):
            copy(extra_ref, e, layout.window_row(e, chunk), sem_extra.at[0]).start()
        for e in range(N_EXPERTS):
            @pl.when(layout.tail_chunk[e] == chunk)
            def _():
                tail_ref[e] = extra_ref[e, pl.ds(layout.tail_row[e], SLOT_ALIGN), :]
        for e in range(N_EXPERTS):
            copy(extra_ref, e, 0, sem_extra.at[0]).wait()
        return carry

    lax.fori_loop(1, layout.n_chunks, overflow, 0)

    @pl.when(b == nb - 1)
    def _():
        for e in range(N_EXPERTS):
            copy(stage_ref.at[slot], e, 0, sem.at[slot]).wait()


def _slot_rows(cap):
    pad = ((SLOT_ALIGN + MOE_TB) // MOE_W + 1) * MOE_W
    return cap + -(-pad // FFN_TM) * FFN_TM


def _dispatch(offs, sel, h, cap):
    n_tok = h.shape[0]
    nb = n_tok // MOE_TB
    rows = _slot_rows(cap)
    grid_spec = pltpu.PrefetchScalarGridSpec(
        num_scalar_prefetch=1,
        grid=(nb,),
        in_specs=[pl.BlockSpec((N_EXPERTS, MOE_TB), lambda b, off: (0, b)),
                  pl.BlockSpec((MOE_TB, D_MODEL), lambda b, off: (b, 0)),
                  pl.BlockSpec((MOE_TB, MOE_TB), lambda b, off: (0, 0))],
        out_specs=pl.BlockSpec(memory_space=pl.ANY),
        scratch_shapes=[pltpu.VMEM((2, N_EXPERTS, MOE_W, D_MODEL), BF16),
                        pltpu.VMEM((N_EXPERTS, MOE_W, D_MODEL), BF16),
                        pltpu.VMEM((N_EXPERTS, SLOT_ALIGN, D_MODEL), BF16),
                        pltpu.VMEM((rows - cap, D_MODEL), BF16),
                        pltpu.SemaphoreType.DMA((2,)),
                        pltpu.SemaphoreType.DMA((1,))],
    )
    return pl.pallas_call(
        _dispatch_kernel,
        grid_spec=grid_spec,
        out_shape=jax.ShapeDtypeStruct((N_EXPERTS, rows, D_MODEL), BF16),
        compiler_params=_cparams(("arbitrary",)),
        name="dispatch",
    )(offs, sel, h, _strict_upper(MOE_TB))


def _ffn_kernel(*refs, first_step, n_tiles):
    n_groups = len(n_tiles)
    xe_refs = refs[:n_groups]
    wg_ref, wu_ref, wd_ref = refs[n_groups:n_groups + 3]
    ye_refs = refs[n_groups + 3:2 * n_groups + 3]
    wg_bf, wu_bf, wd_bf = refs[2 * n_groups + 3:]
    j = pl.program_id(1)

    @pl.when(j == 0)
    def _():
        wg_bf[...] = wg_ref[...].astype(BF16)
        wu_bf[...] = wu_ref[...].astype(BF16)
        wd_bf[...] = wd_ref[...].astype(BF16)

    for g in range(n_groups):
        @pl.when((j >= first_step[g]) & (j < first_step[g] + n_tiles[g]))
        def _():
            x = xe_refs[g][...]
            gate = jnp.dot(x, wg_bf[...], preferred_element_type=F32)
            up = jnp.dot(x, wu_bf[...], preferred_element_type=F32)
            hid = (gate * jax.nn.sigmoid(gate) * up).astype(BF16)
            ye_refs[g][...] = jnp.dot(hid, wd_bf[...], preferred_element_type=F32).astype(BF16)


def _expert_ffn(xes, w_gate, w_up, w_down, layer, caps):
    n_groups = len(xes)
    n_tiles = [cap // FFN_TM for cap in caps]
    first_step = [sum(n_tiles[:g]) for g in range(n_groups)]

    def weight_index(e, j):
        return (layer, jnp.minimum(e + (j >= 1).astype(I32), N_EXPERTS - 1), 0, 0)

    wspec = lambda shape: pl.BlockSpec((None, None) + shape, weight_index)

    def tile_spec(g):
        return pl.BlockSpec((None, FFN_TM, D_MODEL),
                            lambda e, j: (e, jnp.clip(j - first_step[g], 0, n_tiles[g] - 1), 0))

    return pl.pallas_call(
        functools.partial(_ffn_kernel, first_step=tuple(first_step), n_tiles=tuple(n_tiles)),
        grid=(N_EXPERTS, sum(n_tiles)),
        in_specs=([tile_spec(g) for g in range(n_groups)]
                  + [wspec((D_MODEL, EXPERT_FF)), wspec((D_MODEL, EXPERT_FF)), wspec((EXPERT_FF, D_MODEL))]),
        out_specs=[tile_spec(g) for g in range(n_groups)],
        out_shape=[jax.ShapeDtypeStruct(xe.shape, BF16) for xe in xes],
        input_output_aliases={g: g for g in range(n_groups)},
        scratch_shapes=[pltpu.VMEM((D_MODEL, EXPERT_FF), BF16), pltpu.VMEM((D_MODEL, EXPERT_FF), BF16),
                        pltpu.VMEM((EXPERT_FF, D_MODEL), BF16)],
        compiler_params=_cparams(("arbitrary", "arbitrary")),
        name="expert_ffn",
    )(*xes, w_gate, w_up, w_down)


def _combine_kernel(off_ref, sel_ref, aff_ref, tri_ref, x_ref, ye_ref, out_ref, win_ref, extra_ref, sem, sem_extra,
                    *, rows):
    b = pl.program_id(0)
    nb = pl.num_programs(0)
    slot = b % 2

    def copy(dst, e, row0, s):
        return pltpu.make_async_copy(ye_ref.at[e, pl.ds(row0, MOE_W)], dst.at[e], s)

    def fetch(blk, dst_slot):
        ahead = _BlockLayout(off_ref, blk)
        for e in range(N_EXPERTS):
            copy(win_ref.at[dst_slot], e, ahead.window_row(e, 0), sem.at[dst_slot]).start()

    @pl.when(b == 0)
    def _():
        fetch(0, 0)

    @pl.when(b + 1 < nb)
    def _():
        fetch(b + 1, 1 - slot)

    layout = _BlockLayout(off_ref, b)
    pos = _window_positions(sel_ref, tri_ref, layout)
    aff = aff_ref[...]
    tn = (((0,), (0,)), ((), ()))

    def contribution(chunk, window):
        gates = _slot_onehots(pos, aff, chunk)
        vals = window.reshape(N_EXPERTS * MOE_W, window.shape[-1])
        return lax.dot_general(gates, vals, tn, preferred_element_type=F32)

    for e in range(N_EXPERTS):
        copy(win_ref.at[slot], e, 0, sem.at[slot]).wait()
    out_ref[...] = x_ref[...] + contribution(0, win_ref[slot])

    def overflow(chunk, carry):
        for e in range(N_EXPERTS):
            row0 = pl.multiple_of(jnp.minimum(layout.window_row(e, chunk), rows - MOE_W), SLOT_ALIGN)
            copy(extra_ref, e, row0, sem_extra.at[0]).start()
        for e in range(N_EXPERTS):
            copy(extra_ref, e, 0, sem_extra.at[0]).wait()
        out_ref[...] += contribution(chunk, extra_ref[...])
        return carry

    lax.fori_loop(1, layout.n_chunks, overflow, 0)


def _combine(offs, sel, aff_t, x, ye):
    n_tok = x.shape[0]
    nb = n_tok // MOE_TB
    rows = ye.shape[1]
    grid_spec = pltpu.PrefetchScalarGridSpec(
        num_scalar_prefetch=1,
        grid=(nb,),
        in_specs=[pl.BlockSpec((N_EXPERTS, MOE_TB), lambda b, off: (0, b)),
                  pl.BlockSpec((N_EXPERTS, MOE_TB), lambda b, off: (0, b)),
                  pl.BlockSpec((MOE_TB, MOE_TB), lambda b, off: (0, 0)),
                  pl.BlockSpec((MOE_TB, D_MODEL), lambda b, off: (b, 0)),
                  pl.BlockSpec(memory_space=pl.ANY)],
        out_specs=pl.BlockSpec((MOE_TB, D_MODEL), lambda b, off: (b, 0)),
        scratch_shapes=[pltpu.VMEM((2, N_EXPERTS, MOE_W, D_MODEL), BF16),
                        pltpu.VMEM((N_EXPERTS, MOE_W, D_MODEL), BF16),
                        pltpu.SemaphoreType.DMA((2,)),
                        pltpu.SemaphoreType.DMA((1,))],
    )
    return pl.pallas_call(
        functools.partial(_combine_kernel, rows=rows),
        grid_spec=grid_spec,
        out_shape=jax.ShapeDtypeStruct((n_tok, D_MODEL), F32),
        compiler_params=_cparams(("arbitrary",)),
        name="combine",
    )(offs, sel, aff_t, _strict_upper(MOE_TB), x, ye)


def _moe(routed, w_gate, w_up, w_down, layer):
    plans = []
    for x, h, aff_t in routed:
        cap = CAPACITY_FACTOR * x.shape[0] // N_EXPERTS
        sel, offs = _select(aff_t, cap)
        plans.append((cap, sel, offs, _dispatch(offs, sel, h, cap)))
    yes = _expert_ffn([p[3] for p in plans], w_gate, w_up, w_down, layer, [p[0] for p in plans])
    return [_combine(offs, sel, aff_t, x, ye)
            for (x, _, aff_t), (_, sel, offs, _), ye in zip(routed, plans, yes)]


def _row(v):
    return v.reshape(1, -1).astype(F32)


def _router_split(w_router):
    wt = w_router.T.astype(F32)
    hi = wt.astype(BF16)
    lo = (wt - hi.astype(F32)).astype(BF16)
    return hi, lo


def _prep_ab(norm_g, w_in, qn_a, kn_a, qn_b, kn_b, w_out, norm_ffn, w_router):
    scale = HEAD_DIM ** -0.5 * LOG2E
    perm_b = _pair_layout_perm(B_KV_HEADS, B_Q_HEADS // B_KV_HEADS)
    a3 = 3 * A_WIDTH
    bq = B_Q_HEADS * HEAD_DIM
    w_cols = np.concatenate([np.arange(a3), a3 + perm_b, np.arange(a3 + bq, w_in.shape[1])])
    gains = jnp.concatenate([
        jnp.repeat(qn_a, A_HEADS, axis=0).reshape(-1) * scale,
        jnp.repeat(kn_a, A_HEADS, axis=0).reshape(-1),
        jnp.ones((A_WIDTH,), F32),
        jnp.tile(qn_b, B_Q_HEADS) * scale,
        jnp.tile(kn_b, B_KV_HEADS),
        jnp.ones((B_KV_HEADS * HEAD_DIM,), F32)])
    a_out = A_HEADS * HEAD_DIM
    w_o = w_out[np.concatenate([np.arange(a_out), a_out + perm_b])].astype(BF16)
    return (_row(norm_g), w_in[:, w_cols].astype(BF16), _row(gains), w_o, _row(norm_ffn)) + _router_split(w_router)


def _layer_ab(x, batch, seq, tabs, prep):
    norm_g, w, gains, w_o, norm_ffn, wr_hi, wr_lo = prep
    tab_a, tab_b = tabs
    bq = B_Q_HEADS * HEAD_DIM
    n_a = A_GROUPS
    assert A_HEADS * HEAD_DIM == MXU_COLS
    dils = [dil for _, dil in A_PATTERNS]
    plan = ([("qk", g, 0, 0, ROT_DIM // 2, dils[g]) for g in range(n_a)]
            + [("qk", n_a + g, 0, 0, ROT_DIM // 2, dils[g]) for g in range(n_a)]
            + [("v", 2 * n_a + g, 0, 0, 0, 1) for g in range(n_a)]
            + [("qk", 3 * n_a, c * MXU_COLS, 1, HEAD_DIM // 4, 1) for c in range(bq // MXU_COLS)]
            + [("qk", 3 * n_a + 1, 0, 1, HEAD_DIM // 4, 1), ("v", 3 * n_a + 2, 0, 0, 0, 1)])
    assert B_KV_HEADS * HEAD_DIM == MXU_COLS
    a_widths = tuple(MXU_COLS if dil == 1 else ("dilated", dil) for dil in dils)
    outs = _projection(
        x, norm_g, w, gains, (tab_a, tab_b), plan,
        a_widths * 2 + (None,) * n_a + (bq, MXU_COLS, None), seq)
    qa, ka, va = outs[:n_a], outs[n_a:2 * n_a], outs[2 * n_a:3 * n_a]
    qb, kb, vb = outs[3 * n_a:]
    o_parts, lse_parts = [], []
    for gi, (window, dil) in enumerate(A_PATTERNS):
        o, lse = _attention(qa[gi], ka[gi], va[gi], batch=batch, n_keys=seq // dil, residues=dil, group=1,
                            n_pairs=A_HEADS // 2, band_r=window // (2 * dil), want_lse=True)
        o_parts.append(o)
        lse_parts.append(lse)
    (ob,) = _attention(qb, kb, vb, batch=batch, n_keys=seq, residues=1,
                       group=B_Q_HEADS // B_KV_HEADS, n_pairs=B_KV_HEADS // 2, band_r=None)
    return _out_projection(o_parts, lse_parts, ob, x, w_o, norm_ffn, wr_hi, wr_lo)


def _prep_c(norm_g, w_in, qn, kn, sink, w_out, norm_ffn, w_router):
    scale = HEAD_DIM ** -0.5 * LOG2E
    perm = _pair_layout_perm(C_KV_HEADS, C_Q_HEADS // C_KV_HEADS)
    cq = C_Q_HEADS * HEAD_DIM
    w_cols = np.concatenate([perm, np.arange(cq, w_in.shape[1])])
    gains = jnp.concatenate([jnp.tile(qn, C_Q_HEADS) * scale, jnp.tile(kn, C_KV_HEADS),
                             jnp.ones((C_KV_HEADS * HEAD_DIM,), F32)])
    return (_row(norm_g), w_in[:, w_cols].astype(BF16), _row(gains), sink.astype(F32),
            w_out[perm].astype(BF16), _row(norm_ffn)) + _router_split(w_router)


def _layer_c(x, batch, seq, tabs, prep):
    norm_g, w, gains, sink, w_o, norm_ffn, wr_hi, wr_lo = prep
    tab_a, _ = tabs
    cq = C_Q_HEADS * HEAD_DIM
    plan = ([("qk", 0, c * MXU_COLS, 0, ROT_DIM // 2, 1) for c in range(cq // MXU_COLS)]
            + [("qk", 1, 0, 0, ROT_DIM // 2, 1), ("v", 2, 0, 0, 0, 1)])
    assert C_KV_HEADS * HEAD_DIM == MXU_COLS
    q, k, v = _projection(x, norm_g, w, gains, (tab_a,), plan, (cq, MXU_COLS, None), seq)
    (o,) = _attention(q, k, v, batch=batch, n_keys=seq, residues=1,
                      group=C_Q_HEADS // C_KV_HEADS, n_pairs=C_KV_HEADS // 2, band_r=C_RADIUS, sink=sink)
    return _out_projection([], [], o, x, w_o, norm_ffn, wr_hi, wr_lo)


def _encode(xs, params):
    (norm_mix, norm_ffn, w_in_ab, qn_a, kn_a, qn_b, kn_b, w_out_ab, w_in_c, qn_c, kn_c, sink_c, w_out_c,
     w_router, w_gate, w_up, w_down) = params
    shapes = [x.shape[:2] for x in xs]
    tabs = [_rope_tables(seq) for _, seq in shapes]
    xts = [x.reshape(batch * seq, D_MODEL) for x, (batch, seq) in zip(xs, shapes)]
    for layer in range(norm_mix.shape[0]):
        j = layer // 2
        if layer % 2 == 0:
            prep = _prep_ab(norm_mix[layer], w_in_ab[j], qn_a[j], kn_a[j], qn_b[j], kn_b[j], w_out_ab[j],
                            norm_ffn[layer], w_router[layer])
            mixer = _layer_ab
        else:
            prep = _prep_c(norm_mix[layer], w_in_c[j], qn_c[j], kn_c[j], sink_c[j], w_out_c[j],
                           norm_ffn[layer], w_router[layer])
            mixer = _layer_c
        routed = [mixer(xt, batch, seq, tab, prep) for xt, (batch, seq), tab in zip(xts, shapes, tabs)]
        xts = _moe(routed, w_gate, w_up, w_down, layer)
    return tuple(xt.reshape(batch, seq, D_MODEL) for xt, (batch, seq) in zip(xts, shapes))


def kernel(x_prompt, x_sample, norm_mix, norm_ffn, w_in_ab, qn_a, kn_a, qn_b, kn_b, w_out_ab, w_in_c, qn_c,
           kn_c, sink_c, w_out_c, w_router, w_gate, w_up, w_down):
    params = (norm_mix, norm_ffn, w_in_ab, qn_a, kn_a, qn_b, kn_b, w_out_ab, w_in_c, qn_c, kn_c, sink_c,
              w_out_c, w_router, w_gate, w_up, w_down)
    return _encode((x_prompt, x_sample), params)
```

```python
import functools
import math

import jax
import jax.numpy as jnp
import numpy as np
from jax import lax
from jax.experimental import pallas as pl
from jax.experimental.pallas import tpu as pltpu

F32 = jnp.float32
BF16 = jnp.bfloat16
I32 = jnp.int32

D_MODEL = 1024
HEAD_DIM = 64
LANES = 128
MXU_COLS = 256
GRID_W = 64
ROT_DIM = HEAD_DIM // 4
ROPE_THETA = 500000.0
AXIAL_THETA = 10000.0
A_PATTERNS = ((128, 1), (512, 4), (2048, 16))
A_HEADS = 4
A_GROUPS = len(A_PATTERNS)
A_WIDTH = A_GROUPS * A_HEADS * HEAD_DIM
B_Q_HEADS = 12
B_KV_HEADS = 4
C_Q_HEADS = 16
C_KV_HEADS = 4
C_RADIUS = 128
N_EXPERTS = 16
EXPERT_FF = 1024
CAPACITY_FACTOR = 2
NEG_INF = -1e30
EPS = 1e-6

VMEM_LIMIT = 56 * 1024 * 1024

PROJ_TM = 512
OUT_TM = 1024
DENSE_TQ = 256
BAND_TQ = 256
BAND_COLS = 1024
BAND_SUBTILES = 4
ACC_ROWS = LANES + 16
LSE_ROWS = 8
DENSE_TK = 512
MOE_TB = 256
MOE_W = 64
FFN_TM = 512


def _cparams(sem):
    return pltpu.CompilerParams(dimension_semantics=sem, vmem_limit_bytes=VMEM_LIMIT)


def _head_block_diag():
    idx = np.arange(MXU_COLS) // HEAD_DIM
    return jnp.asarray((idx[:, None] == idx[None, :]).astype(np.float32), dtype=BF16)


def _rope_tables(max_len):
    pos = jnp.arange(max_len, dtype=F32)
    j = np.arange(LANES) % HEAD_DIM

    def angles(p, dim, theta):
        exps = jnp.arange(0, dim, 2, dtype=F32) / dim
        inv = jnp.power(jnp.float32(theta), -exps)
        return p[:, None] * inv[None, :]

    half = ROT_DIM // 2
    ang = angles(pos, ROT_DIM, ROPE_THETA)
    cos, sin = jnp.cos(ang), jnp.sin(ang)
    fa = np.where(j < half, j, np.where(j < ROT_DIM, j - half, 0))
    cos_l, sin_l = cos[:, fa], sin[:, fa]
    lo = jnp.asarray(j < half)[None, :]
    hi = jnp.asarray((j >= half) & (j < ROT_DIM))[None, :]
    tab_a = jnp.stack([jnp.where(lo | hi, cos_l, 1.0),
                       jnp.where(lo, -sin_l, 0.0),
                       jnp.where(hi, sin_l, 0.0)])
    hb = HEAD_DIM // 2
    qb = hb // 2
    t = jnp.arange(max_len)
    ang_r = angles((t // GRID_W).astype(F32), hb, AXIAL_THETA)
    ang_c = angles((t % GRID_W).astype(F32), hb, AXIAL_THETA)
    fb = j % qb
    is_col = jnp.asarray(j >= hb)[None, :]
    ang_l = jnp.where(is_col, ang_c[:, fb], ang_r[:, fb])
    cos_b, sin_b = jnp.cos(ang_l), jnp.sin(ang_l)
    first = jnp.asarray((j % hb) < qb)[None, :]
    tab_b = jnp.stack([cos_b, jnp.where(first, -sin_b, 0.0), jnp.where(first, 0.0, sin_b)])
    return tab_a.astype(F32), tab_b.astype(F32)


def _pair_layout_perm(n_kv, group):
    cols = []
    for p in range(n_kv // 2):
        for g in range(group):
            for par in range(2):
                h = (2 * p + par) * group + g
                cols.extend(range(h * HEAD_DIM, (h + 1) * HEAD_DIM))
    return np.asarray(cols, dtype=np.int32)


def _proj_kernel(x_ref, g_ref, w_ref, s_ref, gain_ref, *rest, plan, n_tabs):
    tab_refs = rest[:n_tabs]
    out_refs = rest[n_tabs:-1]
    rows_ref = rest[-1]
    x = x_ref[...]
    tm = x.shape[0]
    ms = jnp.mean(x * x, axis=1, keepdims=True)
    xn = ((x * lax.rsqrt(ms + EPS)) * g_ref[...]).astype(BF16)

    def project(c):
        return jnp.dot(xn, w_ref[:, c * MXU_COLS:(c + 1) * MXU_COLS], preferred_element_type=F32)

    ahead = project(0)
    for c, (kind, out_i, out_col, tab_i, shift, dil) in enumerate(plan):
        acc = ahead
        if c + 1 < len(plan):
            ahead = project(c + 1)
        o_ref = out_refs[out_i]
        if kind == "v":
            o_ref[...] = acc.T.astype(BF16)
            continue
        ss = jnp.dot((acc * acc).astype(BF16), s_ref[...], preferred_element_type=F32)
        y = (acc * lax.rsqrt(ss * (1.0 / HEAD_DIM) + EPS)) * gain_ref[:, c * MXU_COLS:(c + 1) * MXU_COLS]
        tab = tab_refs[tab_i]
        t0 = jnp.concatenate([tab[0], tab[0]], axis=1)
        t1 = jnp.concatenate([tab[1], tab[1]], axis=1)
        t2 = jnp.concatenate([tab[2], tab[2]], axis=1)
        y = y * t0 + pltpu.roll(y, MXU_COLS - shift, 1) * t1 + pltpu.roll(y, shift, 1) * t2
        if dil == 1:
            o_ref[:, out_col:out_col + MXU_COLS] = y.astype(BF16)
        else:
            for half in range(MXU_COLS // LANES):
                rows_ref[half] = y[:, half * LANES:(half + 1) * LANES]
            for r in range(dil):
                for half in range(MXU_COLS // LANES):
                    piece = rows_ref[half, pl.ds(r, tm // dil, stride=dil), :]
                    col = r * MXU_COLS + half * LANES
                    o_ref[:, col:col + LANES] = piece.astype(BF16)


def _projection(x, norm_g, w, gains, tabs, plan, out_widths, seq_len):
    t_tokens = x.shape[0]
    tm = PROJ_TM
    n_in = w.shape[1]
    blocks_per_seq = seq_len // tm
    in_specs = [
        pl.BlockSpec((tm, D_MODEL), lambda i: (i, 0)),
        pl.BlockSpec((1, D_MODEL), lambda i: (0, 0)),
        pl.BlockSpec((D_MODEL, n_in), lambda i: (0, 0)),
        pl.BlockSpec((MXU_COLS, MXU_COLS), lambda i: (0, 0)),
        pl.BlockSpec((1, n_in), lambda i: (0, 0)),
    ] + [pl.BlockSpec((3, tm, LANES), lambda i: (0, i % blocks_per_seq, 0)) for _ in tabs]
    out_specs, out_shape = [], []
    for wd in out_widths:
        if wd is None:
            out_specs.append(pl.BlockSpec((None, MXU_COLS, tm),
                                          lambda i: (i // blocks_per_seq, 0, i % blocks_per_seq)))
            out_shape.append(jax.ShapeDtypeStruct((t_tokens // seq_len, MXU_COLS, seq_len), BF16))
        elif isinstance(wd, tuple):
            dil = wd[1]
            out_specs.append(pl.BlockSpec((tm // dil, dil * MXU_COLS), lambda i: (i, 0)))
            out_shape.append(jax.ShapeDtypeStruct((t_tokens // dil, dil * MXU_COLS), BF16))
        else:
            out_specs.append(pl.BlockSpec((tm, wd), lambda i: (i, 0)))
            out_shape.append(jax.ShapeDtypeStruct((t_tokens, wd), BF16))
    return pl.pallas_call(
        functools.partial(_proj_kernel, plan=tuple(plan), n_tabs=len(tabs)),
        grid=(t_tokens // tm,),
        in_specs=in_specs, out_specs=out_specs, out_shape=out_shape,
        scratch_shapes=[pltpu.VMEM((MXU_COLS // LANES, tm, LANES), F32)],
        compiler_params=_cparams(("parallel",)),
        name="projection",
    )(x, norm_g, w, _head_block_diag(), gains, *tabs)


LOG2E = math.log2(math.e)
LN2 = math.log(2.0)
_NT = (((1,), (1,)), ((), ()))


def _attn_kernel(*refs, group, n_pairs, tq, n_sub, tk, n_keys, band_r, has_sink, want_lse):
    refs = list(refs)
    sink_ref = refs.pop(0) if has_sink else None
    q_ref, k_ref, v_ref, o_ref = refs[:4]
    refs = refs[4:]
    lse_ref = refs.pop(0) if want_lse else None
    scratch = refs
    n_chain = 2 * n_pairs
    lane = lax.broadcasted_iota(I32, (1, LANES), 1)
    low = lane < HEAD_DIM
    half_mask = [jnp.where(low, 1.0, 0.0).astype(BF16), jnp.where(low, 0.0, 1.0).astype(BF16)]
    sub_q = tq // n_sub
    for u in range(n_sub):
        _attn_tile(sink_ref, q_ref, k_ref, v_ref, o_ref, lse_ref, scratch, half_mask,
                   slice(u * sub_q, (u + 1) * sub_q), pl.program_id(2) * n_sub + u,
                   group=group, n_pairs=n_pairs, tq=sub_q, tk=tk, n_keys=n_keys, band_r=band_r)


def _attn_tile(sink_ref, q_ref, k_ref, v_ref, o_ref, lse_ref, scratch, half_mask, q_rows, i, *,
               group, n_pairs, tq, tk, n_keys, band_r):
    has_sink = sink_ref is not None
    want_lse = lse_ref is not None
    n_chain = 2 * n_pairs
    m_cols = group * tq
    q_masked = []
    for p in range(n_pairs):
        q_p = jnp.concatenate(
            [q_ref[q_rows, (p * group + g) * LANES:(p * group + g + 1) * LANES] for g in range(group)], axis=0)
        q_masked += [q_p * half_mask[par] for par in range(2)]

    sub = lax.broadcasted_iota(I32, (ACC_ROWS, m_cols), 0)
    if has_sink:
        acc0 = jnp.where(sub >= LANES, 1.0, 0.0).astype(F32)
        m0 = []
        for c in range(n_chain):
            m0.append(jnp.concatenate(
                [jnp.full((1, tq), sink_ref[c * group + g] * LOG2E, F32) for g in range(group)], axis=1))
    else:
        acc0 = jnp.zeros((ACC_ROWS, m_cols), F32)
        m0 = [jnp.full((1, m_cols), NEG_INF, F32) for _ in range(n_chain)]

    def values_t(p, k0, width):
        ones = jnp.ones((ACC_ROWS - LANES, width), BF16)
        return jnp.concatenate([v_ref[p * LANES:(p + 1) * LANES, pl.ds(k0, width)], ones], axis=0)

    def softmax_pv(s_t, v_t, m_prev, acc_prev):
        m_new = jnp.maximum(m_prev, jnp.max(s_t, axis=0, keepdims=True))
        alpha = jnp.exp2(m_prev - m_new)
        p_t = jnp.exp2(s_t - m_new).astype(BF16)
        return m_new, alpha * acc_prev + jnp.dot(v_t, p_t, preferred_element_type=F32)

    if band_r is None:
        qm_ref, acc_ref, s_even, s_odd = scratch
        for c in range(n_chain):
            qm_ref[c] = q_masked[c]
            acc_ref[c] = acc0

        def step(kb_next, s_next, kb, s_ref, m_prev):
            k0 = pl.multiple_of(kb * tk, tk)
            m_next = []
            for p in range(n_pairs):
                v_t = values_t(p, k0, tk)
                if kb_next is not None:
                    kt = k_ref[pl.ds(pl.multiple_of(kb_next * tk, tk), tk), p * LANES:(p + 1) * LANES]
                for par in range(2):
                    c = 2 * p + par
                    if kb_next is not None:
                        s_next[c] = lax.dot_general(kt, qm_ref[c], _NT, preferred_element_type=F32)
                    m_new, acc_ref[c] = softmax_pv(s_ref[c], v_t, m_prev[c], acc_ref[c])
                    m_next.append(m_new)
            return m_next

        n_blocks = n_keys // tk
        for p in range(n_pairs):
            kt = k_ref[0:tk, p * LANES:(p + 1) * LANES]
            for par in range(2):
                s_even[2 * p + par] = lax.dot_general(kt, qm_ref[2 * p + par], _NT, preferred_element_type=F32)

        def body(j, carry):
            m = step(2 * j + 1, s_odd, 2 * j, s_even, list(carry))
            return tuple(step(2 * j + 2, s_even, 2 * j + 1, s_odd, m))

        m_fin = list(lax.fori_loop(0, n_blocks // 2 - 1, body, tuple(m0)))
        m_fin = step(n_blocks - 1, s_odd, n_blocks - 2, s_even, m_fin)
        m_fin = step(None, None, n_blocks - 1, s_odd, m_fin)
        acc_fin = [acc_ref[c] for c in range(n_chain)]
    else:
        halo = -(-band_r // LANES) * LANES
        width = min(tq + 2 * halo, n_keys)
        ws = pl.multiple_of(jnp.clip(i * tq - halo, 0, n_keys - width), LANES)
        kpos = ws + lax.broadcasted_iota(I32, (width, 1), 0)
        col = lax.broadcasted_iota(I32, (1, tq), 1)
        qpos = i * tq + jnp.concatenate([col] * group, axis=1)
        mask = jnp.abs(kpos - qpos) <= band_r
        s_all = []
        for p in range(n_pairs):
            kt = k_ref[pl.ds(ws, width), p * LANES:(p + 1) * LANES]
            s_all += [lax.dot_general(kt, q_masked[2 * p + par], _NT, preferred_element_type=F32)
                      for par in range(2)]
        m_fin, acc_fin = [], []
        for c in range(n_chain):
            m_new, acc_new = softmax_pv(jnp.where(mask, s_all[c], NEG_INF), values_t(c // 2, ws, width),
                                        m0[c], acc0)
            m_fin.append(m_new)
            acc_fin.append(acc_new)

    top = lax.broadcasted_iota(I32, (LANES, tq), 0) < HEAD_DIM
    for p in range(n_pairs):
        even, odd = acc_fin[2 * p], acc_fin[2 * p + 1]
        for g in range(group):
            cols = slice(g * tq, (g + 1) * tq)
            num = jnp.where(top, even[:LANES, cols], odd[:LANES, cols])
            den = jnp.where(top, even[LANES:LANES + 1, cols], odd[LANES:LANES + 1, cols])
            out_cols = slice((p * group + g) * LANES, (p * group + g + 1) * LANES)
            o_ref[q_rows, out_cols] = (num / den).T.astype(o_ref.dtype)
    if want_lse:
        assert group == 1 and n_chain <= LSE_ROWS
        rows = [m_fin[c] * LN2 + jnp.log(acc_fin[c][LANES:LANES + 1, :]) for c in range(n_chain)]
        lse_ref[:, q_rows] = jnp.concatenate(rows + [jnp.zeros((LSE_ROWS - n_chain, tq), F32)], axis=0)


def _attention(q, k, v_t, *, batch, n_keys, residues, group, n_pairs, band_r, sink=None, want_lse=False):
    dense = band_r is None
    sub_q = DENSE_TQ if dense else min(BAND_TQ, BAND_COLS // group, n_keys)
    n_sub = 1 if dense else max(1, min(BAND_SUBTILES, n_keys // sub_q))
    tq = sub_q * n_sub
    wq = n_pairs * group * LANES
    wk = n_pairs * LANES
    q3 = q.reshape(batch, n_keys, residues * wq)
    k3 = k.reshape(batch, n_keys, residues * wk)
    v4 = v_t.reshape(batch, wk, n_keys, residues).transpose(0, 3, 1, 2)

    in_specs = []
    args = []
    if sink is not None:
        in_specs.append(pl.BlockSpec(memory_space=pltpu.SMEM))
        args.append(sink)
    in_specs += [
        pl.BlockSpec((None, tq, wq), lambda b, r, i: (b, i, r)),
        pl.BlockSpec((None, n_keys, wk), lambda b, r, i: (b, 0, r)),
        pl.BlockSpec((None, None, wk, n_keys), lambda b, r, i: (b, r, 0, 0)),
    ]
    args += [q3, k3, v4]
    out_specs = [pl.BlockSpec((None, tq, wq), lambda b, r, i: (b, i, r))]
    out_shape = [jax.ShapeDtypeStruct((batch, n_keys, residues * wq), BF16)]
    if want_lse:
        out_specs.append(pl.BlockSpec((None, None, LSE_ROWS, tq), lambda b, r, i: (b, r, 0, i)))
        out_shape.append(jax.ShapeDtypeStruct((batch, residues, LSE_ROWS, n_keys), F32))
    m_cols = group * tq
    scratch = []
    if dense:
        scratch = [pltpu.VMEM((2 * n_pairs, m_cols, LANES), BF16),
                   pltpu.VMEM((2 * n_pairs, ACC_ROWS, m_cols), F32)]
        scratch += [pltpu.VMEM((2 * n_pairs, DENSE_TK, m_cols), F32) for _ in range(2)]
    outs = pl.pallas_call(
        functools.partial(_attn_kernel, group=group, n_pairs=n_pairs, tq=tq, n_sub=n_sub, tk=DENSE_TK,
                          n_keys=n_keys, band_r=band_r, has_sink=sink is not None, want_lse=want_lse),
        grid=(batch, residues, n_keys // tq),
        in_specs=in_specs, out_specs=out_specs, out_shape=out_shape,
        scratch_shapes=scratch,
        compiler_params=_cparams(("parallel", "parallel", "arbitrary")),
        name="attention_dense" if dense else "attention_band",
    )(*args)
    result = [outs[0].reshape(batch * n_keys, residues * wq)]
    if want_lse:
        result.append(outs[1].transpose(2, 0, 3, 1).reshape(LSE_ROWS, batch * n_keys * residues))
    return result


def _outproj_kernel(*refs, merge_dils):
    refs = list(refs)
    n_merge = len(merge_dils)
    n_scratch = sum(d > 1 for d in merge_dils)
    if n_merge:
        o_parts = refs[:n_merge]
        lse_parts = refs[n_merge:2 * n_merge]
        expand_ref = refs[2 * n_merge]
        scratch = refs[len(refs) - n_scratch:]
        refs = refs[2 * n_merge + 1:len(refs) - n_scratch]
    o_rest, x_ref, w_ref, g_ref, wrh_ref, wrl_ref, xo_ref, h_ref, aff_ref = refs
    tm = x_ref.shape[0]
    if n_merge:
        ordered = []
        free_scratch = list(scratch)
        for o_ref, dil in zip(o_parts, merge_dils):
            if dil > 1:
                rows_ref = free_scratch.pop(0)
                wd = o_ref.shape[1] // dil
                for r in range(dil):
                    for half in range(wd // LANES):
                        col = r * wd + half * LANES
                        rows_ref[half, pl.ds(r, tm // dil, stride=dil), :] = o_ref[:, col:col + LANES].astype(F32)
                ordered.append(lambda rows, ref=rows_ref: jnp.concatenate(
                    [ref[half, rows, :] for half in range(ref.shape[0])], axis=1))
            else:
                ordered.append(lambda rows, ref=o_ref: ref[rows, :].astype(F32))
        o_parts = ordered
    halves = [slice(0, tm // 2), slice(tm // 2, tm)]
    accs = []
    for rows in halves:
        acc = x_ref[rows, :]
        k0 = 0
        if n_merge:
            lses = [r[:, rows] for r in lse_parts]
            m = functools.reduce(jnp.maximum, lses)
            ws = [jnp.exp(l - m) for l in lses]
            inv = 1.0 / sum(ws)
            num = 0.0
            for wgt, r in zip(ws, o_parts):
                wn = wgt * inv
                hi = wn.astype(BF16)
                split = jnp.concatenate([hi, (wn - hi.astype(F32)).astype(BF16)], axis=0)
                full = lax.dot_general(split, expand_ref[...], (((0,), (0,)), ((), ())),
                                       preferred_element_type=F32)
                num = num + full * r(rows)
            oa = num.astype(BF16)
            k0 = oa.shape[1]
            acc = acc + jnp.dot(oa, w_ref[:k0, :], preferred_element_type=F32)
        accs.append(acc + jnp.dot(o_rest[rows, :], w_ref[k0:, :], preferred_element_type=F32))
    for rows, acc in zip(halves, accs):
        xo_ref[rows, :] = acc
        ms = jnp.mean(acc * acc, axis=1, keepdims=True)
        h = (acc * lax.rsqrt(ms + EPS)) * g_ref[...]
        h_hi = h.astype(BF16)
        h_ref[rows, :] = h_hi
        h_lo = (h - h_hi.astype(F32)).astype(BF16)
        w_both = jnp.concatenate([wrh_ref[...], wrl_ref[...]], axis=0)
        both = lax.dot_general(w_both, h_hi, _NT, preferred_element_type=F32)
        logits = (both[:N_EXPERTS] + both[N_EXPERTS:]
                  + lax.dot_general(wrh_ref[...], h_lo, _NT, preferred_element_type=F32))
        mx = jnp.max(logits, axis=0, keepdims=True)
        e = jnp.exp(logits - mx)
        aff_ref[:, rows] = e / jnp.sum(e, axis=0, keepdims=True)


def _out_projection(o_merge, lse_merge, o_rest, x, w_out, norm_g, wr_hi, wr_lo):
    t_tokens = x.shape[0]
    tm = OUT_TM
    n_merge = len(o_merge)
    row = lambda i: (i, 0)
    fixed = lambda i: (0, 0)
    merge_args = list(o_merge) + list(lse_merge)
    merge_dils = tuple(t_tokens // o.shape[0] for o in o_merge)
    in_specs = ([pl.BlockSpec((tm // dil, o.shape[1]), row) for o, dil in zip(o_merge, merge_dils)]
                + [pl.BlockSpec((LSE_ROWS, tm), lambda i: (0, i)) for _ in lse_merge])
    if n_merge:
        merged_width = o_merge[0].shape[1] // merge_dils[0]
        head = np.arange(merged_width) // HEAD_DIM
        expand = (np.arange(2 * LSE_ROWS)[:, None] % LSE_ROWS == head[None, :]).astype(np.float32)
        merge_args.append(jnp.asarray(expand, dtype=BF16))
        in_specs.append(pl.BlockSpec(expand.shape, fixed))
    in_specs = (in_specs
                + [pl.BlockSpec((tm, o_rest.shape[1]), row),
                   pl.BlockSpec((tm, D_MODEL), row),
                   pl.BlockSpec(w_out.shape, fixed),
                   pl.BlockSpec((1, D_MODEL), fixed),
                   pl.BlockSpec(wr_hi.shape, fixed),
                   pl.BlockSpec(wr_lo.shape, fixed)])
    out_specs = [pl.BlockSpec((tm, D_MODEL), row),
                 pl.BlockSpec((tm, D_MODEL), row),
                 pl.BlockSpec((N_EXPERTS, tm), lambda i: (0, i))]
    out_shape = [jax.ShapeDtypeStruct((t_tokens, D_MODEL), F32),
                 jax.ShapeDtypeStruct((t_tokens, D_MODEL), BF16),
                 jax.ShapeDtypeStruct((N_EXPERTS, t_tokens), F32)]
    return pl.pallas_call(
        functools.partial(_outproj_kernel, merge_dils=merge_dils),
        grid=(t_tokens // tm,),
        in_specs=in_specs, out_specs=out_specs, out_shape=out_shape,
        scratch_shapes=[pltpu.VMEM((merged_width // LANES, tm, LANES), F32) for dil in merge_dils if dil > 1],
        compiler_params=_cparams(("parallel",)),
        name="out_projection",
    )(*merge_args, o_rest, x, w_out, norm_g, wr_hi, wr_lo)


def _select_kernel(aff_ref, ind_ref, tri_ref, sel_ref, off_ref, *, cap):
    aff = aff_ref[...]
    n_tok = aff.shape[1]
    bits = pltpu.bitcast(aff, I32)

    def count(pred):
        return jnp.sum(jnp.where(pred, 1.0, 0.0), axis=1, keepdims=True)

    def value_step(j, thr):
        cand = thr | lax.shift_left(jnp.int32(1), 30 - j)
        return jnp.where(count(bits >= cand) >= cap, cand, thr)

    thr = lax.fori_loop(0, 31, value_step, jnp.zeros((N_EXPERTS, 1), I32))
    gt = bits > thr
    eq = bits == thr
    need = cap - count(gt)
    idx = lax.broadcasted_iota(I32, aff.shape, 1)
    idx_bits = int(math.log2(n_tok))

    def index_step(j, bound):
        cand = bound | lax.shift_left(jnp.int32(1), idx_bits - j)
        return jnp.where(count(eq & (idx < cand)) <= need, cand, bound)

    bound = lax.fori_loop(0, idx_bits + 1, index_step, jnp.zeros((N_EXPERTS, 1), I32))
    sel = jnp.where(gt | (eq & (idx < bound)), 1.0, 0.0).astype(BF16)
    sel_ref[...] = sel
    counts = jnp.dot(sel, ind_ref[...], preferred_element_type=F32)
    offs = jnp.dot(counts.astype(BF16), tri_ref[...], preferred_element_type=F32)
    off_ref[...] = offs.astype(I32)


def _strict_upper(n):
    return jnp.asarray(np.triu(np.ones((n, n), np.float32), k=1), dtype=BF16)


def _select(aff_t, cap):
    n_tok = aff_t.shape[1]
    ind = np.zeros((n_tok, LANES), np.float32)
    ind[np.arange(n_tok), np.arange(n_tok) // MOE_TB] = 1.0
    return pl.pallas_call(
        functools.partial(_select_kernel, cap=cap),
        out_shape=[jax.ShapeDtypeStruct((N_EXPERTS, n_tok), BF16),
                   jax.ShapeDtypeStruct((N_EXPERTS, LANES), I32)],
        compiler_params=pltpu.CompilerParams(vmem_limit_bytes=VMEM_LIMIT),
        name="select",
    )(aff_t, jnp.asarray(ind, dtype=BF16), _strict_upper(LANES))


SLOT_ALIGN = 16
ALIGN_BITS = 4
CHUNK_BITS = 6
NOT_SELECTED = -1e6


class _BlockLayout:
    def __init__(self, off_ref, b):
        self.start = [off_ref[e, b] for e in range(N_EXPERTS)]
        self.end = [off_ref[e, b + 1] for e in range(N_EXPERTS)]
        def floor_tile(v):
            return lax.shift_left(lax.shift_right_logical(v, ALIGN_BITS), ALIGN_BITS)

        self.base = [floor_tile(s) for s in self.start]
        self.shift = [s - a for s, a in zip(self.start, self.base)]
        span = [en - a for en, a in zip(self.end, self.base)]
        self.n_chunks = functools.reduce(
            jnp.maximum, [lax.shift_right_logical(sp, CHUNK_BITS) for sp in span]) + 1
        tail = [floor_tile(sp) for sp in span]
        self.tail_chunk = [lax.shift_right_logical(t, CHUNK_BITS) for t in tail]
        self.tail_row = [pl.multiple_of(t & (MOE_W - 1), SLOT_ALIGN) for t in tail]

    def window_row(self, e, chunk):
        return pl.multiple_of(self.base[e] + chunk * MOE_W, SLOT_ALIGN)


def _slot_onehots(pos, weight, chunk):
    n_tok = pos.shape[1]
    slot = (lax.broadcasted_iota(I32, (MOE_W, n_tok), 0) + chunk * MOE_W).astype(F32)
    parts = []
    for e in range(N_EXPERTS):
        hit = pos[e:e + 1, :] == slot
        val = 1.0 if weight is None else weight[e:e + 1, :]
        parts.append(jnp.where(hit, val, 0.0).astype(BF16))
    return jnp.concatenate(parts, axis=0)


def _window_positions(sel_ref, tri_ref, layout):
    sel = sel_ref[...]
    rank = jnp.dot(sel, tri_ref[...], preferred_element_type=F32)
    shift = jnp.concatenate([jnp.full((1, 1), s, I32) for s in layout.shift], axis=0).astype(F32)
    return jnp.where(sel > 0, rank + shift, NOT_SELECTED)


def _dispatch_kernel(off_ref, sel_ref, h_ref, tri_ref, xe_ref, stage_ref, extra_ref, tail_ref, zero_ref, sem,
                     sem_extra):
    b = pl.program_id(0)
    nb = pl.num_programs(0)
    slot = b % 2
    layout = _BlockLayout(off_ref, b)
    pos = _window_positions(sel_ref, tri_ref, layout)
    h = h_ref[...]

    def rows_for(chunk):
        return jnp.dot(_slot_onehots(pos, None, chunk), h, preferred_element_type=F32)

    def copy(src, e, row0, s):
        return pltpu.make_async_copy(src.at[e], xe_ref.at[e, pl.ds(row0, MOE_W)], s)

    @pl.when(b == 0)
    def _():
        tail_ref[...] = jnp.zeros(tail_ref.shape, BF16)
        zero_ref[...] = jnp.zeros(zero_ref.shape, BF16)
        pad = zero_ref.shape[0]
        fills = [pltpu.make_async_copy(zero_ref, xe_ref.at[e, pl.ds(xe_ref.shape[1] - pad, pad)], sem_extra.at[0])
                 for e in range(N_EXPERTS)]
        for f in fills:
            f.start()
        for f in fills:
            f.wait()

    rows0 = rows_for(0)

    for e in range(N_EXPERTS):
        r0 = e * MOE_W
        first = rows0[r0:r0 + SLOT_ALIGN] + tail_ref[e].astype(F32)
        stage_ref[slot, e, :SLOT_ALIGN] = first.astype(BF16)
        stage_ref[slot, e, SLOT_ALIGN:] = rows0[r0 + SLOT_ALIGN:r0 + MOE_W].astype(BF16)
    for e in range(N_EXPERTS):
        tile = stage_ref[slot, e, pl.ds(layout.tail_row[e], SLOT_ALIGN), :]
        tail_ref[e] = jnp.where(layout.tail_chunk[e] == 0, tile, tail_ref[e])

    @pl.when(b > 0)
    def _():
        for e in range(N_EXPERTS):
            copy(stage_ref.at[1 - slot], e, 0, sem.at[1 - slot]).wait()

    for e in range(N_EXPERTS):
        copy(stage_ref.at[slot], e, layout.window_row(e, 0), sem.at[slot]).start()

    def overflow(chunk, carry):
        more = rows_for(chunk).astype(BF16)
        for e in range(N_EXPERTS):
            extra_ref[e] = more[e * MOE_W:(e + 1) * MOE_W]
        for e in range(N_EXPERTS):
            copy(extra_ref, e, layout.window_row(e, chunk), sem_extra.at[0]).start()
        for e in range(N_EXPERTS):
            @pl.when(layout.tail_chunk[e] == chunk)
            def _():
                tail_ref[e] = extra_ref[e, pl.ds(layout.tail_row[e], SLOT_ALIGN), :]
        for e in range(N_EXPERTS):
            copy(extra_ref, e, 0, sem_extra.at[0]).wait()
        return carry

    lax.fori_loop(1, layout.n_chunks, overflow, 0)

    @pl.when(b == nb - 1)
    def _():
        for e in range(N_EXPERTS):
            copy(stage_ref.at[slot], e, 0, sem.at[slot]).wait()


def _slot_rows(cap):
    pad = ((SLOT_ALIGN + MOE_TB) // MOE_W + 1) * MOE_W
    return cap + -(-pad // FFN_TM) * FFN_TM


def _dispatch(offs, sel, h, cap):
    n_tok = h.shape[0]
    nb = n_tok // MOE_TB
    rows = _slot_rows(cap)
    grid_spec = pltpu.PrefetchScalarGridSpec(
        num_scalar_prefetch=1,
        grid=(nb,),
        in_specs=[pl.BlockSpec((N_EXPERTS, MOE_TB), lambda b, off: (0, b)),
                  pl.BlockSpec((MOE_TB, D_MODEL), lambda b, off: (b, 0)),
                  pl.BlockSpec((MOE_TB, MOE_TB), lambda b, off: (0, 0))],
        out_specs=pl.BlockSpec(memory_space=pl.ANY),
        scratch_shapes=[pltpu.VMEM((2, N_EXPERTS, MOE_W, D_MODEL), BF16),
                        pltpu.VMEM((N_EXPERTS, MOE_W, D_MODEL), BF16),
                        pltpu.VMEM((N_EXPERTS, SLOT_ALIGN, D_MODEL), BF16),
                        pltpu.VMEM((rows - cap, D_MODEL), BF16),
                        pltpu.SemaphoreType.DMA((2,)),
                        pltpu.SemaphoreType.DMA((1,))],
    )
    return pl.pallas_call(
        _dispatch_kernel,
        grid_spec=grid_spec,
        out_shape=jax.ShapeDtypeStruct((N_EXPERTS, rows, D_MODEL), BF16),
        compiler_params=_cparams(("arbitrary",)),
        name="dispatch",
    )(offs, sel, h, _strict_upper(MOE_TB))


def _ffn_kernel(*refs, first_step, n_tiles):
    n_groups = len(n_tiles)
    xe_refs = refs[:n_groups]
    wg_ref, wu_ref, wd_ref = refs[n_groups:n_groups + 3]
    ye_refs = refs[n_groups + 3:2 * n_groups + 3]
    wg_bf, wu_bf, wd_bf = refs[2 * n_groups + 3:]
    j = pl.program_id(1)

    @pl.when(j == 0)
    def _():
        wg_bf[...] = wg_ref[...].astype(BF16)
        wu_bf[...] = wu_ref[...].astype(BF16)
        wd_bf[...] = wd_ref[...].astype(BF16)

    for g in range(n_groups):
        @pl.when((j >= first_step[g]) & (j < first_step[g] + n_tiles[g]))
        def _():
            x = xe_refs[g][...]
            gate = jnp.dot(x, wg_bf[...], preferred_element_type=F32)
            up = jnp.dot(x, wu_bf[...], preferred_element_type=F32)
            hid = (gate * jax.nn.sigmoid(gate) * up).astype(BF16)
            ye_refs[g][...] = jnp.dot(hid, wd_bf[...], preferred_element_type=F32).astype(BF16)


def _expert_ffn(xes, w_gate, w_up, w_down, layer, caps):
    n_groups = len(xes)
    n_tiles = [cap // FFN_TM for cap in caps]
    first_step = [sum(n_tiles[:g]) for g in range(n_groups)]

    def weight_index(e, j):
        return (layer, jnp.minimum(e + (j >= 1).astype(I32), N_EXPERTS - 1), 0, 0)

    wspec = lambda shape: pl.BlockSpec((None, None) + shape, weight_index)

    def tile_spec(g):
        return pl.BlockSpec((None, FFN_TM, D_MODEL),
                            lambda e, j: (e, jnp.clip(j - first_step[g], 0, n_tiles[g] - 1), 0))

    return pl.pallas_call(
        functools.partial(_ffn_kernel, first_step=tuple(first_step), n_tiles=tuple(n_tiles)),
        grid=(N_EXPERTS, sum(n_tiles)),
        in_specs=([tile_spec(g) for g in range(n_groups)]
                  + [wspec((D_MODEL, EXPERT_FF)), wspec((D_MODEL, EXPERT_FF)), wspec((EXPERT_FF, D_MODEL))]),
        out_specs=[tile_spec(g) for g in range(n_groups)],
        out_shape=[jax.ShapeDtypeStruct(xe.shape, BF16) for xe in xes],
        input_output_aliases={g: g for g in range(n_groups)},
        scratch_shapes=[pltpu.VMEM((D_MODEL, EXPERT_FF), BF16), pltpu.VMEM((D_MODEL, EXPERT_FF), BF16),
                        pltpu.VMEM((EXPERT_FF, D_MODEL), BF16)],
        compiler_params=_cparams(("arbitrary", "arbitrary")),
        name="expert_ffn",
    )(*xes, w_gate, w_up, w_down)


def _combine_kernel(off_ref, sel_ref, aff_ref, tri_ref, x_ref, ye_ref, out_ref, win_ref, extra_ref, sem, sem_extra,
                    *, rows):
    b = pl.program_id(0)
    nb = pl.num_programs(0)
    slot = b % 2

    def copy(dst, e, row0, s):
        return pltpu.make_async_copy(ye_ref.at[e, pl.ds(row0, MOE_W)], dst.at[e], s)

    def fetch(blk, dst_slot):
        ahead = _BlockLayout(off_ref, blk)
        for e in range(N_EXPERTS):
            copy(win_ref.at[dst_slot], e, ahead.window_row(e, 0), sem.at[dst_slot]).start()

    @pl.when(b == 0)
    def _():
        fetch(0, 0)

    @pl.when(b + 1 < nb)
    def _():
        fetch(b + 1, 1 - slot)

    layout = _BlockLayout(off_ref, b)
    pos = _window_positions(sel_ref, tri_ref, layout)
    aff = aff_ref[...]
    tn = (((0,), (0,)), ((), ()))

    def contribution(chunk, window):
        gates = _slot_onehots(pos, aff, chunk)
        vals = window.reshape(N_EXPERTS * MOE_W, window.shape[-1])
        return lax.dot_general(gates, vals, tn, preferred_element_type=F32)

    for e in range(N_EXPERTS):
        copy(win_ref.at[slot], e, 0, sem.at[slot]).wait()
    out_ref[...] = x_ref[...] + contribution(0, win_ref[slot])

    def overflow(chunk, carry):
        for e in range(N_EXPERTS):
            row0 = pl.multiple_of(jnp.minimum(layout.window_row(e, chunk), rows - MOE_W), SLOT_ALIGN)
            copy(extra_ref, e, row0, sem_extra.at[0]).start()
        for e in range(N_EXPERTS):
            copy(extra_ref, e, 0, sem_extra.at[0]).wait()
        out_ref[...] += contribution(chunk, extra_ref[...])
        return carry

    lax.fori_loop(1, layout.n_chunks, overflow, 0)


def _combine(offs, sel, aff_t, x, ye):
    n_tok = x.shape[0]
    nb = n_tok // MOE_TB
    rows = ye.shape[1]
    grid_spec = pltpu.PrefetchScalarGridSpec(
        num_scalar_prefetch=1,
        grid=(nb,),
        in_specs=[pl.BlockSpec((N_EXPERTS, MOE_TB), lambda b, off: (0, b)),
                  pl.BlockSpec((N_EXPERTS, MOE_TB), lambda b, off: (0, b)),
                  pl.BlockSpec((MOE_TB, MOE_TB), lambda b, off: (0, 0)),
                  pl.BlockSpec((MOE_TB, D_MODEL), lambda b, off: (b, 0)),
                  pl.BlockSpec(memory_space=pl.ANY)],
        out_specs=pl.BlockSpec((MOE_TB, D_MODEL), lambda b, off: (b, 0)),
        scratch_shapes=[pltpu.VMEM((2, N_EXPERTS, MOE_W, D_MODEL), BF16),
                        pltpu.VMEM((N_EXPERTS, MOE_W, D_MODEL), BF16),
                        pltpu.SemaphoreType.DMA((2,)),
                        pltpu.SemaphoreType.DMA((1,))],
    )
    return pl.pallas_call(
        functools.partial(_combine_kernel, rows=rows),
        grid_spec=grid_spec,
        out_shape=jax.ShapeDtypeStruct((n_tok, D_MODEL), F32),
        compiler_params=_cparams(("arbitrary",)),
        name="combine",
    )(offs, sel, aff_t, _strict_upper(MOE_TB), x, ye)


def _moe(routed, w_gate, w_up, w_down, layer):
    plans = []
    for x, h, aff_t in routed:
        cap = CAPACITY_FACTOR * x.shape[0] // N_EXPERTS
        sel, offs = _select(aff_t, cap)
        plans.append((cap, sel, offs, _dispatch(offs, sel, h, cap)))
    yes = _expert_ffn([p[3] for p in plans], w_gate, w_up, w_down, layer, [p[0] for p in plans])
    return [_combine(offs, sel, aff_t, x, ye)
            for (x, _, aff_t), (_, sel, offs, _), ye in zip(routed, plans, yes)]


def _row(v):
    return v.reshape(1, -1).astype(F32)


def _router_split(w_router):
    wt = w_router.T.astype(F32)
    hi = wt.astype(BF16)
    lo = (wt - hi.astype(F32)).astype(BF16)
    return hi, lo


def _prep_ab(norm_g, w_in, qn_a, kn_a, qn_b, kn_b, w_out, norm_ffn, w_router):
    scale = HEAD_DIM ** -0.5 * LOG2E
    perm_b = _pair_layout_perm(B_KV_HEADS, B_Q_HEADS // B_KV_HEADS)
    a3 = 3 * A_WIDTH
    bq = B_Q_HEADS * HEAD_DIM
    w_cols = np.concatenate([np.arange(a3), a3 + perm_b, np.arange(a3 + bq, w_in.shape[1])])
    gains = jnp.concatenate([
        jnp.repeat(qn_a, A_HEADS, axis=0).reshape(-1) * scale,
        jnp.repeat(kn_a, A_HEADS, axis=0).reshape(-1),
        jnp.ones((A_WIDTH,), F32),
        jnp.tile(qn_b, B_Q_HEADS) * scale,
        jnp.tile(kn_b, B_KV_HEADS),
        jnp.ones((B_KV_HEADS * HEAD_DIM,), F32)])
    a_out = A_HEADS * HEAD_DIM
    w_o = w_out[np.concatenate([np.arange(a_out), a_out + perm_b])].astype(BF16)
    return (_row(norm_g), w_in[:, w_cols].astype(BF16), _row(gains), w_o, _row(norm_ffn)) + _router_split(w_router)


def _layer_ab(x, batch, seq, tabs, prep):
    norm_g, w, gains, w_o, norm_ffn, wr_hi, wr_lo = prep
    tab_a, tab_b = tabs
    bq = B_Q_HEADS * HEAD_DIM
    n_a = A_GROUPS
    assert A_HEADS * HEAD_DIM == MXU_COLS
    dils = [dil for _, dil in A_PATTERNS]
    plan = ([("qk", g, 0, 0, ROT_DIM // 2, dils[g]) for g in range(n_a)]
            + [("qk", n_a + g, 0, 0, ROT_DIM // 2, dils[g]) for g in range(n_a)]
            + [("v", 2 * n_a + g, 0, 0, 0, 1) for g in range(n_a)]
            + [("qk", 3 * n_a, c * MXU_COLS, 1, HEAD_DIM // 4, 1) for c in range(bq // MXU_COLS)]
            + [("qk", 3 * n_a + 1, 0, 1, HEAD_DIM // 4, 1), ("v", 3 * n_a + 2, 0, 0, 0, 1)])
    assert B_KV_HEADS * HEAD_DIM == MXU_COLS
    a_widths = tuple(MXU_COLS if dil == 1 else ("dilated", dil) for dil in dils)
    outs = _projection(
        x, norm_g, w, gains, (tab_a, tab_b), plan,
        a_widths * 2 + (None,) * n_a + (bq, MXU_COLS, None), seq)
    qa, ka, va = outs[:n_a], outs[n_a:2 * n_a], outs[2 * n_a:3 * n_a]
    qb, kb, vb = outs[3 * n_a:]
    o_parts, lse_parts = [], []
    for gi, (window, dil) in enumerate(A_PATTERNS):
        o, lse = _attention(qa[gi], ka[gi], va[gi], batch=batch, n_keys=seq // dil, residues=dil, group=1,
                            n_pairs=A_HEADS // 2, band_r=window // (2 * dil), want_lse=True)
        o_parts.append(o)
        lse_parts.append(lse)
    (ob,) = _attention(qb, kb, vb, batch=batch, n_keys=seq, residues=1,
                       group=B_Q_HEADS // B_KV_HEADS, n_pairs=B_KV_HEADS // 2, band_r=None)
    return _out_projection(o_parts, lse_parts, ob, x, w_o, norm_ffn, wr_hi, wr_lo)


def _prep_c(norm_g, w_in, qn, kn, sink, w_out, norm_ffn, w_router):
    scale = HEAD_DIM ** -0.5 * LOG2E
    perm = _pair_layout_perm(C_KV_HEADS, C_Q_HEADS // C_KV_HEADS)
    cq = C_Q_HEADS * HEAD_DIM
    w_cols = np.concatenate([perm, np.arange(cq, w_in.shape[1])])
    gains = jnp.concatenate([jnp.tile(qn, C_Q_HEADS) * scale, jnp.tile(kn, C_KV_HEADS),
                             jnp.ones((C_KV_HEADS * HEAD_DIM,), F32)])
    return (_row(norm_g), w_in[:, w_cols].astype(BF16), _row(gains), sink.astype(F32),
            w_out[perm].astype(BF16), _row(norm_ffn)) + _router_split(w_router)


def _layer_c(x, batch, seq, tabs, prep):
    norm_g, w, gains, sink, w_o, norm_ffn, wr_hi, wr_lo = prep
    tab_a, _ = tabs
    cq = C_Q_HEADS * HEAD_DIM
    plan = ([("qk", 0, c * MXU_COLS, 0, ROT_DIM // 2, 1) for c in range(cq // MXU_COLS)]
            + [("qk", 1, 0, 0, ROT_DIM // 2, 1), ("v", 2, 0, 0, 0, 1)])
    assert C_KV_HEADS * HEAD_DIM == MXU_COLS
    q, k, v = _projection(x, norm_g, w, gains, (tab_a,), plan, (cq, MXU_COLS, None), seq)
    (o,) = _attention(q, k, v, batch=batch, n_keys=seq, residues=1,
                      group=C_Q_HEADS // C_KV_HEADS, n_pairs=C_KV_HEADS // 2, band_r=C_RADIUS, sink=sink)
    return _out_projection([], [], o, x, w_o, norm_ffn, wr_hi, wr_lo)


def _encode(xs, params):
    (norm_mix, norm_ffn, w_in_ab, qn_a, kn_a, qn_b, kn_b, w_out_ab, w_in_c, qn_c, kn_c, sink_c, w_out_c,
     w_router, w_gate, w_up, w_down) = params
    shapes = [x.shape[:2] for x in xs]
    tabs = [_rope_tables(seq) for _, seq in shapes]
    xts = [x.reshape(batch * seq, D_MODEL) for x, (batch, seq) in zip(xs, shapes)]
    for layer in range(norm_mix.shape[0]):
        j = layer // 2
        if layer % 2 == 0:
            prep = _prep_ab(norm_mix[layer], w_in_ab[j], qn_a[j], kn_a[j], qn_b[j], kn_b[j], w_out_ab[j],
                            norm_ffn[layer], w_router[layer])
            mixer = _layer_ab
        else:
            prep = _prep_c(norm_mix[layer], w_in_c[j], qn_c[j], kn_c[j], sink_c[j], w_out_c[j],
                           norm_ffn[layer], w_router[layer])
            mixer = _layer_c
        routed = [mixer(xt, batch, seq, tab, prep) for xt, (batch, seq), tab in zip(xts, shapes, tabs)]
        xts = _moe(routed, w_gate, w_up, w_down, layer)
    return tuple(xt.reshape(batch, seq, D_MODEL) for xt, (batch, seq) in zip(xts, shapes))


def kernel(x_prompt, x_sample, norm_mix, norm_ffn, w_in_ab, qn_a, kn_a, qn_b, kn_b, w_out_ab, w_in_c, qn_c,
           kn_c, sink_c, w_out_c, w_router, w_gate, w_up, w_down):
    params = (norm_mix, norm_ffn, w_in_ab, qn_a, kn_a, qn_b, kn_b, w_out_ab, w_in_c, qn_c, kn_c, sink_c,
              w_out_c, w_router, w_gate, w_up, w_down)
    return _encode((x_prompt, x_sample), params)
```

```python
import functools
import math

import jax
import jax.numpy as jnp
import numpy as np
from jax import lax
from jax.experimental import pallas as pl
from jax.experimental.pallas import tpu as pltpu

F32 = jnp.float32
BF16 = jnp.bfloat16
I32 = jnp.int32

D_MODEL = 1024
HEAD_DIM = 64
LANES = 128
MXU_COLS = 256
GRID_W = 64
ROT_DIM = HEAD_DIM // 4
ROPE_THETA = 500000.0
AXIAL_THETA = 10000.0
A_PATTERNS = ((128, 1), (512, 4), (2048, 16))
A_HEADS = 4
A_GROUPS = len(A_PATTERNS)
A_WIDTH = A_GROUPS * A_HEADS * HEAD_DIM
B_Q_HEADS = 12
B_KV_HEADS = 4
C_Q_HEADS = 16
C_KV_HEADS = 4
C_RADIUS = 128
N_EXPERTS = 16
EXPERT_FF = 1024
CAPACITY_FACTOR = 2
NEG_INF = -1e30
EPS = 1e-6

VMEM_LIMIT = 56 * 1024 * 1024

PROJ_TM = 512
OUT_TM = 1024
DENSE_TQ = 256
BAND_TQ = 256
BAND_COLS = 1024
BAND_SUBTILES = 4
ACC_ROWS = LANES + 16
LSE_ROWS = 8
DENSE_TK = 512
MOE_TB = 256
MOE_W = 64
FFN_TM = 512


def _cparams(sem):
    return pltpu.CompilerParams(dimension_semantics=sem, vmem_limit_bytes=VMEM_LIMIT)


def _head_block_diag():
    idx = np.arange(MXU_COLS) // HEAD_DIM
    return jnp.asarray((idx[:, None] == idx[None, :]).astype(np.float32), dtype=BF16)


def _rope_tables(max_len):
    pos = jnp.arange(max_len, dtype=F32)
    j = np.arange(LANES) % HEAD_DIM

    def angles(p, dim, theta):
        exps = jnp.arange(0, dim, 2, dtype=F32) / dim
        inv = jnp.power(jnp.float32(theta), -exps)
        return p[:, None] * inv[None, :]

    half = ROT_DIM // 2
    ang = angles(pos, ROT_DIM, ROPE_THETA)
    cos, sin = jnp.cos(ang), jnp.sin(ang)
    fa = np.where(j < half, j, np.where(j < ROT_DIM, j - half, 0))
    cos_l, sin_l = cos[:, fa], sin[:, fa]
    lo = jnp.asarray(j < half)[None, :]
    hi = jnp.asarray((j >= half) & (j < ROT_DIM))[None, :]
    tab_a = jnp.stack([jnp.where(lo | hi, cos_l, 1.0),
                       jnp.where(lo, -sin_l, 0.0),
                       jnp.where(hi, sin_l, 0.0)])
    hb = HEAD_DIM // 2
    qb = hb // 2
    t = jnp.arange(max_len)
    ang_r = angles((t // GRID_W).astype(F32), hb, AXIAL_THETA)
    ang_c = angles((t % GRID_W).astype(F32), hb, AXIAL_THETA)
    fb = j % qb
    is_col = jnp.asarray(j >= hb)[None, :]
    ang_l = jnp.where(is_col, ang_c[:, fb], ang_r[:, fb])
    cos_b, sin_b = jnp.cos(ang_l), jnp.sin(ang_l)
    first = jnp.asarray((j % hb) < qb)[None, :]
    tab_b = jnp.stack([cos_b, jnp.where(first, -sin_b, 0.0), jnp.where(first, 0.0, sin_b)])
    return tab_a.astype(F32), tab_b.astype(F32)


def _pair_layout_perm(n_kv, group):
    cols = []
    for p in range(n_kv // 2):
        for g in range(group):
            for par in range(2):
                h = (2 * p + par) * group + g
                cols.extend(range(h * HEAD_DIM, (h + 1) * HEAD_DIM))
    return np.asarray(cols, dtype=np.int32)


def _proj_kernel(x_ref, g_ref, w_ref, s_ref, gain_ref, *rest, plan, n_tabs):
    tab_refs = rest[:n_tabs]
    out_refs = rest[n_tabs:-1]
    rows_ref = rest[-1]
    x = x_ref[...]
    tm = x.shape[0]
    ms = jnp.mean(x * x, axis=1, keepdims=True)
    xn = ((x * lax.rsqrt(ms + EPS)) * g_ref[...]).astype(BF16)

    def project(c):
        return jnp.dot(xn, w_ref[:, c * MXU_COLS:(c + 1) * MXU_COLS], preferred_element_type=F32)

    ahead = project(0)
    for c, (kind, out_i, out_col, tab_i, shift, dil) in enumerate(plan):
        acc = ahead
        if c + 1 < len(plan):
            ahead = project(c + 1)
        o_ref = out_refs[out_i]
        if kind == "v":
            o_ref[...] = acc.T.astype(BF16)
            continue
        ss = jnp.dot((acc * acc).astype(BF16), s_ref[...], preferred_element_type=F32)
        y = (acc * lax.rsqrt(ss * (1.0 / HEAD_DIM) + EPS)) * gain_ref[:, c * MXU_COLS:(c + 1) * MXU_COLS]
        tab = tab_refs[tab_i]
        t0 = jnp.concatenate([tab[0], tab[0]], axis=1)
        t1 = jnp.concatenate([tab[1], tab[1]], axis=1)
        t2 = jnp.concatenate([tab[2], tab[2]], axis=1)
        y = y * t0 + pltpu.roll(y, MXU_COLS - shift, 1) * t1 + pltpu.roll(y, shift, 1) * t2
        if dil == 1:
            o_ref[:, out_col:out_col + MXU_COLS] = y.astype(BF16)
        else:
            for half in range(MXU_COLS // LANES):
                rows_ref[half] = y[:, half * LANES:(half + 1) * LANES]
            for r in range(dil):
                for half in range(MXU_COLS // LANES):
                    piece = rows_ref[half, pl.ds(r, tm // dil, stride=dil), :]
                    col = r * MXU_COLS + half * LANES
                    o_ref[:, col:col + LANES] = piece.astype(BF16)


def _projection(x, norm_g, w, gains, tabs, plan, out_widths, seq_len):
    t_tokens = x.shape[0]
    tm = PROJ_TM
    n_in = w.shape[1]
    blocks_per_seq = seq_len // tm
    in_specs = [
        pl.BlockSpec((tm, D_MODEL), lambda i: (i, 0)),
        pl.BlockSpec((1, D_MODEL), lambda i: (0, 0)),
        pl.BlockSpec((D_MODEL, n_in), lambda i: (0, 0)),
        pl.BlockSpec((MXU_COLS, MXU_COLS), lambda i: (0, 0)),
        pl.BlockSpec((1, n_in), lambda i: (0, 0)),
    ] + [pl.BlockSpec((3, tm, LANES), lambda i: (0, i % blocks_per_seq, 0)) for _ in tabs]
    out_specs, out_shape = [], []
    for wd in out_widths:
        if wd is None:
            out_specs.append(pl.BlockSpec((None, MXU_COLS, tm),
                                          lambda i: (i // blocks_per_seq, 0, i % blocks_per_seq)))
            out_shape.append(jax.ShapeDtypeStruct((t_tokens // seq_len, MXU_COLS, seq_len), BF16))
        elif isinstance(wd, tuple):
            dil = wd[1]
            out_specs.append(pl.BlockSpec((tm // dil, dil * MXU_COLS), lambda i: (i, 0)))
            out_shape.append(jax.ShapeDtypeStruct((t_tokens // dil, dil * MXU_COLS), BF16))
        else:
            out_specs.append(pl.BlockSpec((tm, wd), lambda i: (i, 0)))
            out_shape.append(jax.ShapeDtypeStruct((t_tokens, wd), BF16))
    return pl.pallas_call(
        functools.partial(_proj_kernel, plan=tuple(plan), n_tabs=len(tabs)),
        grid=(t_tokens // tm,),
        in_specs=in_specs, out_specs=out_specs, out_shape=out_shape,
        scratch_shapes=[pltpu.VMEM((MXU_COLS // LANES, tm, LANES), F32)],
        compiler_params=_cparams(("parallel",)),
        name="projection",
    )(x, norm_g, w, _head_block_diag(), gains, *tabs)


LOG2E = math.log2(math.e)
LN2 = math.log(2.0)
_NT = (((1,), (1,)), ((), ()))


def _attn_kernel(*refs, group, n_pairs, tq, n_sub, tk, n_keys, band_r, has_sink, want_lse):
    refs = list(refs)
    sink_ref = refs.pop(0) if has_sink else None
    q_ref, k_ref, v_ref, o_ref = refs[:4]
    refs = refs[4:]
    lse_ref = refs.pop(0) if want_lse else None
    scratch = refs
    n_chain = 2 * n_pairs
    lane = lax.broadcasted_iota(I32, (1, LANES), 1)
    low = lane < HEAD_DIM
    half_mask = [jnp.where(low, 1.0, 0.0).astype(BF16), jnp.where(low, 0.0, 1.0).astype(BF16)]
    sub_q = tq // n_sub
    for u in range(n_sub):
        _attn_tile(sink_ref, q_ref, k_ref, v_ref, o_ref, lse_ref, scratch, half_mask,
                   slice(u * sub_q, (u + 1) * sub_q), pl.program_id(2) * n_sub + u,
                   group=group, n_pairs=n_pairs, tq=sub_q, tk=tk, n_keys=n_keys, band_r=band_r)


def _attn_tile(sink_ref, q_ref, k_ref, v_ref, o_ref, lse_ref, scratch, half_mask, q_rows, i, *,
               group, n_pairs, tq, tk, n_keys, band_r):
    has_sink = sink_ref is not None
    want_lse = lse_ref is not None
    n_chain = 2 * n_pairs
    m_cols = group * tq
    q_masked = []
    for p in range(n_pairs):
        q_p = jnp.concatenate(
            [q_ref[q_rows, (p * group + g) * LANES:(p * group + g + 1) * LANES] for g in range(group)], axis=0)
        q_masked += [q_p * half_mask[par] for par in range(2)]

    sub = lax.broadcasted_iota(I32, (ACC_ROWS, m_cols), 0)
    if has_sink:
        acc0 = jnp.where(sub >= LANES, 1.0, 0.0).astype(F32)
        m0 = []
        for c in range(n_chain):
            m0.append(jnp.concatenate(
                [jnp.full((1, tq), sink_ref[c * group + g] * LOG2E, F32) for g in range(group)], axis=1))
    else:
        acc0 = jnp.zeros((ACC_ROWS, m_cols), F32)
        m0 = [jnp.full((1, m_cols), NEG_INF, F32) for _ in range(n_chain)]

    def values_t(p, k0, width):
        ones = jnp.ones((ACC_ROWS - LANES, width), BF16)
        return jnp.concatenate([v_ref[p * LANES:(p + 1) * LANES, pl.ds(k0, width)], ones], axis=0)

    def softmax_pv(s_t, v_t, m_prev, acc_prev, s_max=None):
        if s_max is None:
            s_max = jnp.max(s_t, axis=0, keepdims=True)
        m_new = jnp.maximum(m_prev, s_max)
        alpha = jnp.exp2(m_prev - m_new)
        p_t = jnp.exp2(s_t - m_new).astype(BF16)
        return m_new, alpha * acc_prev + jnp.dot(v_t, p_t, preferred_element_type=F32)

    if band_r is None:
        qm_ref, acc_ref, s_even, s_odd = scratch
        for c in range(n_chain):
            qm_ref[c] = q_masked[c]
            acc_ref[c] = acc0

        def scores(kt, c, s_ref):
            s_t = lax.dot_general(kt, qm_ref[c], _NT, preferred_element_type=F32)
            s_ref[c] = s_t
            return jnp.max(s_t, axis=0, keepdims=True)

        def step(kb_next, s_next, kb, s_ref, m_prev, s_max):
            k0 = pl.multiple_of(kb * tk, tk)
            m_next, max_next = [], []
            for p in range(n_pairs):
                v_t = values_t(p, k0, tk)
                if kb_next is not None:
                    kt = k_ref[pl.ds(pl.multiple_of(kb_next * tk, tk), tk), p * LANES:(p + 1) * LANES]
                for par in range(2):
                    c = 2 * p + par
                    if kb_next is not None:
                        max_next.append(scores(kt, c, s_next))
                    m_new, acc_ref[c] = softmax_pv(s_ref[c], v_t, m_prev[c], acc_ref[c], s_max[c])
                    m_next.append(m_new)
            return m_next, max_next

        n_blocks = n_keys // tk
        max0 = []
        for p in range(n_pairs):
            kt = k_ref[0:tk, p * LANES:(p + 1) * LANES]
            max0 += [scores(kt, 2 * p + par, s_even) for par in range(2)]

        def body(j, carry):
            m, s_max = list(carry[:n_chain]), list(carry[n_chain:])
            m, s_max = step(2 * j + 1, s_odd, 2 * j, s_even, m, s_max)
            m, s_max = step(2 * j + 2, s_even, 2 * j + 1, s_odd, m, s_max)
            return tuple(m + s_max)

        carry = lax.fori_loop(0, n_blocks // 2 - 1, body, tuple(m0 + max0))
        m_fin, s_max = step(n_blocks - 1, s_odd, n_blocks - 2, s_even, list(carry[:n_chain]), list(carry[n_chain:]))
        m_fin, _ = step(None, None, n_blocks - 1, s_odd, m_fin, s_max)
        acc_fin = [acc_ref[c] for c in range(n_chain)]
    else:
        halo = -(-band_r // LANES) * LANES
        width = min(tq + 2 * halo, n_keys)
        ws = pl.multiple_of(jnp.clip(i * tq - halo, 0, n_keys - width), LANES)
        kpos = ws + lax.broadcasted_iota(I32, (width, 1), 0)
        col = lax.broadcasted_iota(I32, (1, tq), 1)
        qpos = i * tq + jnp.concatenate([col] * group, axis=1)
        mask = jnp.abs(kpos - qpos) <= band_r
        s_all = []
        for p in range(n_pairs):
            kt = k_ref[pl.ds(ws, width), p * LANES:(p + 1) * LANES]
            s_all += [lax.dot_general(kt, q_masked[2 * p + par], _NT, preferred_element_type=F32)
                      for par in range(2)]
        m_fin, acc_fin = [], []
        for c in range(n_chain):
            m_new, acc_new = softmax_pv(jnp.where(mask, s_all[c], NEG_INF), values_t(c // 2, ws, width),
                                        m0[c], acc0)
            m_fin.append(m_new)
            acc_fin.append(acc_new)

    top = lax.broadcasted_iota(I32, (LANES, tq), 0) < HEAD_DIM
    for p in range(n_pairs):
        even, odd = acc_fin[2 * p], acc_fin[2 * p + 1]
        for g in range(group):
            cols = slice(g * tq, (g + 1) * tq)
            num = jnp.where(top, even[:LANES, cols], odd[:LANES, cols])
            den = jnp.where(top, even[LANES:LANES + 1, cols], odd[LANES:LANES + 1, cols])
            out_cols = slice((p * group + g) * LANES, (p * group + g + 1) * LANES)
            o_ref[q_rows, out_cols] = (num / den).T.astype(o_ref.dtype)
    if want_lse:
        assert group == 1 and n_chain <= LSE_ROWS
        rows = [m_fin[c] * LN2 + jnp.log(acc_fin[c][LANES:LANES + 1, :]) for c in range(n_chain)]
        lse_ref[:, q_rows] = jnp.concatenate(rows + [jnp.zeros((LSE_ROWS - n_chain, tq), F32)], axis=0)


def _attention(q, k, v_t, *, batch, n_keys, residues, group, n_pairs, band_r, sink=None, want_lse=False):
    dense = band_r is None
    sub_q = DENSE_TQ if dense else min(BAND_TQ, BAND_COLS // group, n_keys)
    n_sub = 1 if dense else max(1, min(BAND_SUBTILES, n_keys // sub_q))
    tq = sub_q * n_sub
    wq = n_pairs * group * LANES
    wk = n_pairs * LANES
    q3 = q.reshape(batch, n_keys, residues * wq)
    k3 = k.reshape(batch, n_keys, residues * wk)
    v4 = v_t.reshape(batch, wk, n_keys, residues).transpose(0, 3, 1, 2)

    in_specs = []
    args = []
    if sink is not None:
        in_specs.append(pl.BlockSpec(memory_space=pltpu.SMEM))
        args.append(sink)
    in_specs += [
        pl.BlockSpec((None, tq, wq), lambda b, r, i: (b, i, r)),
        pl.BlockSpec((None, n_keys, wk), lambda b, r, i: (b, 0, r)),
        pl.BlockSpec((None, None, wk, n_keys), lambda b, r, i: (b, r, 0, 0)),
    ]
    args += [q3, k3, v4]
    out_specs = [pl.BlockSpec((None, tq, wq), lambda b, r, i: (b, i, r))]
    out_shape = [jax.ShapeDtypeStruct((batch, n_keys, residues * wq), BF16)]
    if want_lse:
        out_specs.append(pl.BlockSpec((None, None, LSE_ROWS, tq), lambda b, r, i: (b, r, 0, i)))
        out_shape.append(jax.ShapeDtypeStruct((batch, residues, LSE_ROWS, n_keys), F32))
    m_cols = group * tq
    scratch = []
    if dense:
        scratch = [pltpu.VMEM((2 * n_pairs, m_cols, LANES), BF16),
                   pltpu.VMEM((2 * n_pairs, ACC_ROWS, m_cols), F32)]
        scratch += [pltpu.VMEM((2 * n_pairs, DENSE_TK, m_cols), F32) for _ in range(2)]
    outs = pl.pallas_call(
        functools.partial(_attn_kernel, group=group, n_pairs=n_pairs, tq=tq, n_sub=n_sub, tk=DENSE_TK,
                          n_keys=n_keys, band_r=band_r, has_sink=sink is not None, want_lse=want_lse),
        grid=(batch, residues, n_keys // tq),
        in_specs=in_specs, out_specs=out_specs, out_shape=out_shape,
        scratch_shapes=scratch,
        compiler_params=_cparams(("parallel", "parallel", "arbitrary")),
        name="attention_dense" if dense else "attention_band",
    )(*args)
    result = [outs[0].reshape(batch * n_keys, residues * wq)]
    if want_lse:
        result.append(outs[1].transpose(2, 0, 3, 1).reshape(LSE_ROWS, batch * n_keys * residues))
    return result


def _outproj_kernel(*refs, merge_dils):
    refs = list(refs)
    n_merge = len(merge_dils)
    n_scratch = sum(d > 1 for d in merge_dils)
    if n_merge:
        o_parts = refs[:n_merge]
        lse_parts = refs[n_merge:2 * n_merge]
        expand_ref = refs[2 * n_merge]
        scratch = refs[len(refs) - n_scratch:]
        refs = refs[2 * n_merge + 1:len(refs) - n_scratch]
    o_rest, x_ref, w_ref, g_ref, wrh_ref, wrl_ref, xo_ref, h_ref, aff_ref = refs
    tm = x_ref.shape[0]
    if n_merge:
        ordered = []
        free_scratch = list(scratch)
        for o_ref, dil in zip(o_parts, merge_dils):
            if dil > 1:
                rows_ref = free_scratch.pop(0)
                wd = o_ref.shape[1] // dil
                for r in range(dil):
                    for half in range(wd // LANES):
                        col = r * wd + half * LANES
                        rows_ref[half, pl.ds(r, tm // dil, stride=dil), :] = o_ref[:, col:col + LANES].astype(F32)
                ordered.append(lambda rows, ref=rows_ref: jnp.concatenate(
                    [ref[half, rows, :] for half in range(ref.shape[0])], axis=1))
            else:
                ordered.append(lambda rows, ref=o_ref: ref[rows, :].astype(F32))
        o_parts = ordered
    halves = [slice(0, tm // 2), slice(tm // 2, tm)]
    accs = []
    for rows in halves:
        acc = x_ref[rows, :]
        k0 = 0
        if n_merge:
            lses = [r[:, rows] for r in lse_parts]
            m = functools.reduce(jnp.maximum, lses)
            ws = [jnp.exp(l - m) for l in lses]
            inv = 1.0 / sum(ws)
            num = 0.0
            for wgt, r in zip(ws, o_parts):
                wn = wgt * inv
                hi = wn.astype(BF16)
                split = jnp.concatenate([hi, (wn - hi.astype(F32)).astype(BF16)], axis=0)
                full = lax.dot_general(split, expand_ref[...], (((0,), (0,)), ((), ())),
                                       preferred_element_type=F32)
                num = num + full * r(rows)
            oa = num.astype(BF16)
            k0 = oa.shape[1]
            acc = acc + jnp.dot(oa, w_ref[:k0, :], preferred_element_type=F32)
        accs.append(acc + jnp.dot(o_rest[rows, :], w_ref[k0:, :], preferred_element_type=F32))
    for rows, acc in zip(halves, accs):
        xo_ref[rows, :] = acc
        ms = jnp.mean(acc * acc, axis=1, keepdims=True)
        h = (acc * lax.rsqrt(ms + EPS)) * g_ref[...]
        h_hi = h.astype(BF16)
        h_ref[rows, :] = h_hi
        h_lo = (h - h_hi.astype(F32)).astype(BF16)
        w_both = jnp.concatenate([wrh_ref[...], wrl_ref[...]], axis=0)
        both = lax.dot_general(w_both, h_hi, _NT, preferred_element_type=F32)
        logits = (both[:N_EXPERTS] + both[N_EXPERTS:]
                  + lax.dot_general(wrh_ref[...], h_lo, _NT, preferred_element_type=F32))
        mx = jnp.max(logits, axis=0, keepdims=True)
        e = jnp.exp(logits - mx)
        aff_ref[:, rows] = e / jnp.sum(e, axis=0, keepdims=True)


def _out_projection(o_merge, lse_merge, o_rest, x, w_out, norm_g, wr_hi, wr_lo):
    t_tokens = x.shape[0]
    tm = OUT_TM
    n_merge = len(o_merge)
    row = lambda i: (i, 0)
    fixed = lambda i: (0, 0)
    merge_args = list(o_merge) + list(lse_merge)
    merge_dils = tuple(t_tokens // o.shape[0] for o in o_merge)
    in_specs = ([pl.BlockSpec((tm // dil, o.shape[1]), row) for o, dil in zip(o_merge, merge_dils)]
                + [pl.BlockSpec((LSE_ROWS, tm), lambda i: (0, i)) for _ in lse_merge])
    if n_merge:
        merged_width = o_merge[0].shape[1] // merge_dils[0]
        head = np.arange(merged_width) // HEAD_DIM
        expand = (np.arange(2 * LSE_ROWS)[:, None] % LSE_ROWS == head[None, :]).astype(np.float32)
        merge_args.append(jnp.asarray(expand, dtype=BF16))
        in_specs.append(pl.BlockSpec(expand.shape, fixed))
    in_specs = (in_specs
                + [pl.BlockSpec((tm, o_rest.shape[1]), row),
                   pl.BlockSpec((tm, D_MODEL), row),
                   pl.BlockSpec(w_out.shape, fixed),
                   pl.BlockSpec((1, D_MODEL), fixed),
                   pl.BlockSpec(wr_hi.shape, fixed),
                   pl.BlockSpec(wr_lo.shape, fixed)])
    out_specs = [pl.BlockSpec((tm, D_MODEL), row),
                 pl.BlockSpec((tm, D_MODEL), row),
                 pl.BlockSpec((N_EXPERTS, tm), lambda i: (0, i))]
    out_shape = [jax.ShapeDtypeStruct((t_tokens, D_MODEL), F32),
                 jax.ShapeDtypeStruct((t_tokens, D_MODEL), BF16),
                 jax.ShapeDtypeStruct((N_EXPERTS, t_tokens), F32)]
    return pl.pallas_call(
        functools.partial(_outproj_kernel, merge_dils=merge_dils),
        grid=(t_tokens // tm,),
        in_specs=in_specs, out_specs=out_specs, out_shape=out_shape,
        scratch_shapes=[pltpu.VMEM((merged_width // LANES, tm, LANES), F32) for dil in merge_dils if dil > 1],
        compiler_params=_cparams(("parallel",)),
        name="out_projection",
    )(*merge_args, o_rest, x, w_out, norm_g, wr_hi, wr_lo)


def _select_kernel(aff_ref, ind_ref, tri_ref, sel_ref, off_ref, *, cap):
    aff = aff_ref[...]
    n_tok = aff.shape[1]
    bits = pltpu.bitcast(aff, I32)

    def count(pred):
        return jnp.sum(jnp.where(pred, 1.0, 0.0), axis=1, keepdims=True)

    def value_step(j, thr):
        cand = thr | lax.shift_left(jnp.int32(1), 30 - j)
        return jnp.where(count(bits >= cand) >= cap, cand, thr)

    thr = lax.fori_loop(0, 31, value_step, jnp.zeros((N_EXPERTS, 1), I32))
    gt = bits > thr
    eq = bits == thr
    need = cap - count(gt)
    idx = lax.broadcasted_iota(I32, aff.shape, 1)
    idx_bits = int(math.log2(n_tok))

    def index_step(j, bound):
        cand = bound | lax.shift_left(jnp.int32(1), idx_bits - j)
        return jnp.where(count(eq & (idx < cand)) <= need, cand, bound)

    bound = lax.fori_loop(0, idx_bits + 1, index_step, jnp.zeros((N_EXPERTS, 1), I32))
    sel = jnp.where(gt | (eq & (idx < bound)), 1.0, 0.0).astype(BF16)
    sel_ref[...] = sel
    counts = jnp.dot(sel, ind_ref[...], preferred_element_type=F32)
    offs = jnp.dot(counts.astype(BF16), tri_ref[...], preferred_element_type=F32)
    off_ref[...] = offs.astype(I32)


def _strict_upper(n):
    return jnp.asarray(np.triu(np.ones((n, n), np.float32), k=1), dtype=BF16)


def _select(aff_t, cap):
    n_tok = aff_t.shape[1]
    ind = np.zeros((n_tok, LANES), np.float32)
    ind[np.arange(n_tok), np.arange(n_tok) // MOE_TB] = 1.0
    return pl.pallas_call(
        functools.partial(_select_kernel, cap=cap),
        out_shape=[jax.ShapeDtypeStruct((N_EXPERTS, n_tok), BF16),
                   jax.ShapeDtypeStruct((N_EXPERTS, LANES), I32)],
        compiler_params=pltpu.CompilerParams(vmem_limit_bytes=VMEM_LIMIT),
        name="select",
    )(aff_t, jnp.asarray(ind, dtype=BF16), _strict_upper(LANES))


SLOT_ALIGN = 16
ALIGN_BITS = 4
CHUNK_BITS = 6
NOT_SELECTED = -1e6


class _BlockLayout:
    def __init__(self, off_ref, b):
        self.start = [off_ref[e, b] for e in range(N_EXPERTS)]
        self.end = [off_ref[e, b + 1] for e in range(N_EXPERTS)]
        def floor_tile(v):
            return lax.shift_left(lax.shift_right_logical(v, ALIGN_BITS), ALIGN_BITS)

        self.base = [floor_tile(s) for s in self.start]
        self.shift = [s - a for s, a in zip(self.start, self.base)]
        span = [en - a for en, a in zip(self.end, self.base)]
        self.n_chunks = functools.reduce(
            jnp.maximum, [lax.shift_right_logical(sp, CHUNK_BITS) for sp in span]) + 1
        tail = [floor_tile(sp) for sp in span]
        self.tail_chunk = [lax.shift_right_logical(t, CHUNK_BITS) for t in tail]
        self.tail_row = [pl.multiple_of(t & (MOE_W - 1), SLOT_ALIGN) for t in tail]

    def window_row(self, e, chunk):
        return pl.multiple_of(self.base[e] + chunk * MOE_W, SLOT_ALIGN)


def _slot_onehots(pos, weight, chunk):
    n_tok = pos.shape[1]
    slot = (lax.broadcasted_iota(I32, (MOE_W, n_tok), 0) + chunk * MOE_W).astype(F32)
    parts = []
    for e in range(N_EXPERTS):
        hit = pos[e:e + 1, :] == slot
        val = 1.0 if weight is None else weight[e:e + 1, :]
        parts.append(jnp.where(hit, val, 0.0).astype(BF16))
    return jnp.concatenate(parts, axis=0)


def _window_positions(sel_ref, tri_ref, layout):
    sel = sel_ref[...]
    rank = jnp.dot(sel, tri_ref[...], preferred_element_type=F32)
    shift = jnp.concatenate([jnp.full((1, 1), s, I32) for s in layout.shift], axis=0).astype(F32)
    return jnp.where(sel > 0, rank + shift, NOT_SELECTED)


def _dispatch_kernel(off_ref, sel_ref, h_ref, tri_ref, xe_ref, stage_ref, extra_ref, tail_ref, zero_ref, sem,
                     sem_extra):
    b = pl.program_id(0)
    nb = pl.num_programs(0)
    slot = b % 2
    layout = _BlockLayout(off_ref, b)
    pos = _window_positions(sel_ref, tri_ref, layout)
    h = h_ref[...]

    def rows_for(chunk):
        return jnp.dot(_slot_onehots(pos, None, chunk), h, preferred_element_type=F32)

    def copy(src, e, row0, s):
        return pltpu.make_async_copy(src.at[e], xe_ref.at[e, pl.ds(row0, MOE_W)], s)

    @pl.when(b == 0)
    def _():
        tail_ref[...] = jnp.zeros(tail_ref.shape, BF16)
        zero_ref[...] = jnp.zeros(zero_ref.shape, BF16)
        pad = zero_ref.shape[0]
        fills = [pltpu.make_async_copy(zero_ref, xe_ref.at[e, pl.ds(xe_ref.shape[1] - pad, pad)], sem_extra.at[0])
                 for e in range(N_EXPERTS)]
        for f in fills:
            f.start()
        for f in fills:
            f.wait()

    rows0 = rows_for(0)

    for e in range(N_EXPERTS):
        r0 = e * MOE_W
        first = rows0[r0:r0 + SLOT_ALIGN] + tail_ref[e].astype(F32)
        stage_ref[slot, e, :SLOT_ALIGN] = first.astype(BF16)
        stage_ref[slot, e, SLOT_ALIGN:] = rows0[r0 + SLOT_ALIGN:r0 + MOE_W].astype(BF16)
    for e in range(N_EXPERTS):
        tile = stage_ref[slot, e, pl.ds(layout.tail_row[e], SLOT_ALIGN), :]
        tail_ref[e] = jnp.where(layout.tail_chunk[e] == 0, tile, tail_ref[e])

    @pl.when(b > 0)
    def _():
        for e in range(N_EXPERTS):
            copy(stage_ref.at[1 - slot], e, 0, sem.at[1 - slot]).wait()

    for e in range(N_EXPERTS):
        copy(stage_ref.at[slot], e, layout.window_row(e, 0), sem.at[slot]).start()

    def overflow(chunk, carry):
        more = rows_for(chunk).astype(BF16)
        for e in range(N_EXPERTS):
            extra_ref[e] = more[e * MOE_W:(e + 1) * MOE_W]
        for e in range(N_EXPERTS):
            copy(extra_ref, e, layout.window_row(e, chunk), sem_extra.at[0]).start()
        for e in range(N_EXPERTS):
            @pl.when(layout.tail_chunk[e] == chunk)
            def _():
                tail_ref[e] = extra_ref[e, pl.ds(layout.tail_row[e], SLOT_ALIGN), :]
        for e in range(N_EXPERTS):
            copy(extra_ref, e, 0, sem_extra.at[0]).wait()
        return carry

    lax.fori_loop(1, layout.n_chunks, overflow, 0)

    @pl.when(b == nb - 1)
    def _():
        for e in range(N_EXPERTS):
            copy(stage_ref.at[slot], e, 0, sem.at[slot]).wait()


def _slot_rows(cap):
    pad = ((SLOT_ALIGN + MOE_TB) // MOE_W + 1) * MOE_W
    return cap + -(-pad // FFN_TM) * FFN_TM


def _dispatch(offs, sel, h, cap):
    n_tok = h.shape[0]
    nb = n_tok // MOE_TB
    rows = _slot_rows(cap)
    grid_spec = pltpu.PrefetchScalarGridSpec(
        num_scalar_prefetch=1,
        grid=(nb,),
        in_specs=[pl.BlockSpec((N_EXPERTS, MOE_TB), lambda b, off: (0, b)),
                  pl.BlockSpec((MOE_TB, D_MODEL), lambda b, off: (b, 0)),
                  pl.BlockSpec((MOE_TB, MOE_TB), lambda b, off: (0, 0))],
        out_specs=pl.BlockSpec(memory_space=pl.ANY),
        scratch_shapes=[pltpu.VMEM((2, N_EXPERTS, MOE_W, D_MODEL), BF16),
                        pltpu.VMEM((N_EXPERTS, MOE_W, D_MODEL), BF16),
                        pltpu.VMEM((N_EXPERTS, SLOT_ALIGN, D_MODEL), BF16),
                        pltpu.VMEM((rows - cap, D_MODEL), BF16),
                        pltpu.SemaphoreType.DMA((2,)),
                        pltpu.SemaphoreType.DMA((1,))],
    )
    return pl.pallas_call(
        _dispatch_kernel,
        grid_spec=grid_spec,
        out_shape=jax.ShapeDtypeStruct((N_EXPERTS, rows, D_MODEL), BF16),
        compiler_params=_cparams(("arbitrary",)),
        name="dispatch",
    )(offs, sel, h, _strict_upper(MOE_TB))


def _ffn_kernel(*refs, first_step, n_tiles):
    n_groups = len(n_tiles)
    xe_refs = refs[:n_groups]
    wg_ref, wu_ref, wd_ref = refs[n_groups:n_groups + 3]
    ye_refs = refs[n_groups + 3:2 * n_groups + 3]
    wg_bf, wu_bf, wd_bf = refs[2 * n_groups + 3:]
    j = pl.program_id(1)

    @pl.when(j == 0)
    def _():
        wg_bf[...] = wg_ref[...].astype(BF16)
        wu_bf[...] = wu_ref[...].astype(BF16)
        wd_bf[...] = wd_ref[...].astype(BF16)

    for g in range(n_groups):
        @pl.when((j >= first_step[g]) & (j < first_step[g] + n_tiles[g]))
        def _():
            x = xe_refs[g][...]
            gate = jnp.dot(x, wg_bf[...], preferred_element_type=F32)
            up = jnp.dot(x, wu_bf[...], preferred_element_type=F32)
            hid = (gate * jax.nn.sigmoid(gate) * up).astype(BF16)
            ye_refs[g][...] = jnp.dot(hid, wd_bf[...], preferred_element_type=F32).astype(BF16)


def _expert_ffn(xes, w_gate, w_up, w_down, layer, caps):
    n_groups = len(xes)
    n_tiles = [cap // FFN_TM for cap in caps]
    first_step = [sum(n_tiles[:g]) for g in range(n_groups)]

    def weight_index(e, j):
        return (layer, jnp.minimum(e + (j >= 1).astype(I32), N_EXPERTS - 1), 0, 0)

    wspec = lambda shape: pl.BlockSpec((None, None) + shape, weight_index)

    def tile_spec(g):
        return pl.BlockSpec((None, FFN_TM, D_MODEL),
                            lambda e, j: (e, jnp.clip(j - first_step[g], 0, n_tiles[g] - 1), 0))

    return pl.pallas_call(
        functools.partial(_ffn_kernel, first_step=tuple(first_step), n_tiles=tuple(n_tiles)),
        grid=(N_EXPERTS, sum(n_tiles)),
        in_specs=([tile_spec(g) for g in range(n_groups)]
                  + [wspec((D_MODEL, EXPERT_FF)), wspec((D_MODEL, EXPERT_FF)), wspec((EXPERT_FF, D_MODEL))]),
        out_specs=[tile_spec(g) for g in range(n_groups)],
        out_shape=[jax.ShapeDtypeStruct(xe.shape, BF16) for xe in xes],
        input_output_aliases={g: g for g in range(n_groups)},
        scratch_shapes=[pltpu.VMEM((D_MODEL, EXPERT_FF), BF16), pltpu.VMEM((D_MODEL, EXPERT_FF), BF16),
                        pltpu.VMEM((EXPERT_FF, D_MODEL), BF16)],
        compiler_params=_cparams(("arbitrary", "arbitrary")),
        name="expert_ffn",
    )(*xes, w_gate, w_up, w_down)


def _combine_kernel(off_ref, sel_ref, aff_ref, tri_ref, x_ref, ye_ref, out_ref, win_ref, extra_ref, sem, sem_extra,
                    *, rows):
    b = pl.program_id(0)
    nb = pl.num_programs(0)
    slot = b % 2

    def copy(dst, e, row0, s):
        return pltpu.make_async_copy(ye_ref.at[e, pl.ds(row0, MOE_W)], dst.at[e], s)

    def fetch(blk, dst_slot):
        ahead = _BlockLayout(off_ref, blk)
        for e in range(N_EXPERTS):
            copy(win_ref.at[dst_slot], e, ahead.window_row(e, 0), sem.at[dst_slot]).start()

    @pl.when(b == 0)
    def _():
        fetch(0, 0)

    @pl.when(b + 1 < nb)
    def _():
        fetch(b + 1, 1 - slot)

    layout = _BlockLayout(off_ref, b)
    pos = _window_positions(sel_ref, tri_ref, layout)
    aff = aff_ref[...]
    tn = (((0,), (0,)), ((), ()))

    def contribution(chunk, window):
        gates = _slot_onehots(pos, aff, chunk)
        vals = window.reshape(N_EXPERTS * MOE_W, window.shape[-1])
        return lax.dot_general(gates, vals, tn, preferred_element_type=F32)

    for e in range(N_EXPERTS):
        copy(win_ref.at[slot], e, 0, sem.at[slot]).wait()
    out_ref[...] = x_ref[...] + contribution(0, win_ref[slot])

    def overflow(chunk, carry):
        for e in range(N_EXPERTS):
            row0 = pl.multiple_of(jnp.minimum(layout.window_row(e, chunk), rows - MOE_W), SLOT_ALIGN)
            copy(extra_ref, e, row0, sem_extra.at[0]).start()
        for e in range(N_EXPERTS):
            copy(extra_ref, e, 0, sem_extra.at[0]).wait()
        out_ref[...] += contribution(chunk, extra_ref[...])
        return carry

    lax.fori_loop(1, layout.n_chunks, overflow, 0)


def _combine(offs, sel, aff_t, x, ye):
    n_tok = x.shape[0]
    nb = n_tok // MOE_TB
    rows = ye.shape[1]
    grid_spec = pltpu.PrefetchScalarGridSpec(
        num_scalar_prefetch=1,
        grid=(nb,),
        in_specs=[pl.BlockSpec((N_EXPERTS, MOE_TB), lambda b, off: (0, b)),
                  pl.BlockSpec((N_EXPERTS, MOE_TB), lambda b, off: (0, b)),
                  pl.BlockSpec((MOE_TB, MOE_TB), lambda b, off: (0, 0)),
                  pl.BlockSpec((MOE_TB, D_MODEL), lambda b, off: (b, 0)),
                  pl.BlockSpec(memory_space=pl.ANY)],
        out_specs=pl.BlockSpec((MOE_TB, D_MODEL), lambda b, off: (b, 0)),
        scratch_shapes=[pltpu.VMEM((2, N_EXPERTS, MOE_W, D_MODEL), BF16),
                        pltpu.VMEM((N_EXPERTS, MOE_W, D_MODEL), BF16),
                        pltpu.SemaphoreType.DMA((2,)),
                        pltpu.SemaphoreType.DMA((1,))],
    )
    return pl.pallas_call(
        functools.partial(_combine_kernel, rows=rows),
        grid_spec=grid_spec,
        out_shape=jax.ShapeDtypeStruct((n_tok, D_MODEL), F32),
        compiler_params=_cparams(("arbitrary",)),
        name="combine",
    )(offs, sel, aff_t, _strict_upper(MOE_TB), x, ye)


def _moe(routed, w_gate, w_up, w_down, layer):
    plans = []
    for x, h, aff_t in routed:
        cap = CAPACITY_FACTOR * x.shape[0] // N_EXPERTS
        sel, offs = _select(aff_t, cap)
        plans.append((cap, sel, offs, _dispatch(offs, sel, h, cap)))
    yes = _expert_ffn([p[3] for p in plans], w_gate, w_up, w_down, layer, [p[0] for p in plans])
    return [_combine(offs, sel, aff_t, x, ye)
            for (x, _, aff_t), (_, sel, offs, _), ye in zip(routed, plans, yes)]


def _row(v):
    return v.reshape(1, -1).astype(F32)


def _router_split(w_router):
    wt = w_router.T.astype(F32)
    hi = wt.astype(BF16)
    lo = (wt - hi.astype(F32)).astype(BF16)
    return hi, lo


def _prep_ab(norm_g, w_in, qn_a, kn_a, qn_b, kn_b, w_out, norm_ffn, w_router):
    scale = HEAD_DIM ** -0.5 * LOG2E
    perm_b = _pair_layout_perm(B_KV_HEADS, B_Q_HEADS // B_KV_HEADS)
    a3 = 3 * A_WIDTH
    bq = B_Q_HEADS * HEAD_DIM
    w_cols = np.concatenate([np.arange(a3), a3 + perm_b, np.arange(a3 + bq, w_in.shape[1])])
    gains = jnp.concatenate([
        jnp.repeat(qn_a, A_HEADS, axis=0).reshape(-1) * scale,
        jnp.repeat(kn_a, A_HEADS, axis=0).reshape(-1),
        jnp.ones((A_WIDTH,), F32),
        jnp.tile(qn_b, B_Q_HEADS) * scale,
        jnp.tile(kn_b, B_KV_HEADS),
        jnp.ones((B_KV_HEADS * HEAD_DIM,), F32)])
    a_out = A_HEADS * HEAD_DIM
    w_o = w_out[np.concatenate([np.arange(a_out), a_out + perm_b])].astype(BF16)
    return (_row(norm_g), w_in[:, w_cols].astype(BF16), _row(gains), w_o, _row(norm_ffn)) + _router_split(w_router)


def _layer_ab(x, batch, seq, tabs, prep):
    norm_g, w, gains, w_o, norm_ffn, wr_hi, wr_lo = prep
    tab_a, tab_b = tabs
    bq = B_Q_HEADS * HEAD_DIM
    n_a = A_GROUPS
    assert A_HEADS * HEAD_DIM == MXU_COLS
    dils = [dil for _, dil in A_PATTERNS]
    plan = ([("qk", g, 0, 0, ROT_DIM // 2, dils[g]) for g in range(n_a)]
            + [("qk", n_a + g, 0, 0, ROT_DIM // 2, dils[g]) for g in range(n_a)]
            + [("v", 2 * n_a + g, 0, 0, 0, 1) for g in range(n_a)]
            + [("qk", 3 * n_a, c * MXU_COLS, 1, HEAD_DIM // 4, 1) for c in range(bq // MXU_COLS)]
            + [("qk", 3 * n_a + 1, 0, 1, HEAD_DIM // 4, 1), ("v", 3 * n_a + 2, 0, 0, 0, 1)])
    assert B_KV_HEADS * HEAD_DIM == MXU_COLS
    a_widths = tuple(MXU_COLS if dil == 1 else ("dilated", dil) for dil in dils)
    outs = _projection(
        x, norm_g, w, gains, (tab_a, tab_b), plan,
        a_widths * 2 + (None,) * n_a + (bq, MXU_COLS, None), seq)
    qa, ka, va = outs[:n_a], outs[n_a:2 * n_a], outs[2 * n_a:3 * n_a]
    qb, kb, vb = outs[3 * n_a:]
    o_parts, lse_parts = [], []
    for gi, (window, dil) in enumerate(A_PATTERNS):
        o, lse = _attention(qa[gi], ka[gi], va[gi], batch=batch, n_keys=seq // dil, residues=dil, group=1,
                            n_pairs=A_HEADS // 2, band_r=window // (2 * dil), want_lse=True)
        o_parts.append(o)
        lse_parts.append(lse)
    (ob,) = _attention(qb, kb, vb, batch=batch, n_keys=seq, residues=1,
                       group=B_Q_HEADS // B_KV_HEADS, n_pairs=B_KV_HEADS // 2, band_r=None)
    return _out_projection(o_parts, lse_parts, ob, x, w_o, norm_ffn, wr_hi, wr_lo)


def _prep_c(norm_g, w_in, qn, kn, sink, w_out, norm_ffn, w_router):
    scale = HEAD_DIM ** -0.5 * LOG2E
    perm = _pair_layout_perm(C_KV_HEADS, C_Q_HEADS // C_KV_HEADS)
    cq = C_Q_HEADS * HEAD_DIM
    w_cols = np.concatenate([perm, np.arange(cq, w_in.shape[1])])
    gains = jnp.concatenate([jnp.tile(qn, C_Q_HEADS) * scale, jnp.tile(kn, C_KV_HEADS),
                             jnp.ones((C_KV_HEADS * HEAD_DIM,), F32)])
    return (_row(norm_g), w_in[:, w_cols].astype(BF16), _row(gains), sink.astype(F32),
            w_out[perm].astype(BF16), _row(norm_ffn)) + _router_split(w_router)


def _layer_c(x, batch, seq, tabs, prep):
    norm_g, w, gains, sink, w_o, norm_ffn, wr_hi, wr_lo = prep
    tab_a, _ = tabs
    cq = C_Q_HEADS * HEAD_DIM
    plan = ([("qk", 0, c * MXU_COLS, 0, ROT_DIM // 2, 1) for c in range(cq // MXU_COLS)]
            + [("qk", 1, 0, 0, ROT_DIM // 2, 1), ("v", 2, 0, 0, 0, 1)])
    assert C_KV_HEADS * HEAD_DIM == MXU_COLS
    q, k, v = _projection(x, norm_g, w, gains, (tab_a,), plan, (cq, MXU_COLS, None), seq)
    (o,) = _attention(q, k, v, batch=batch, n_keys=seq, residues=1,
                      group=C_Q_HEADS // C_KV_HEADS, n_pairs=C_KV_HEADS // 2, band_r=C_RADIUS, sink=sink)
    return _out_projection([], [], o, x, w_o, norm_ffn, wr_hi, wr_lo)


def _encode(xs, params):
    (norm_mix, norm_ffn, w_in_ab, qn_a, kn_a, qn_b, kn_b, w_out_ab, w_in_c, qn_c, kn_c, sink_c, w_out_c,
     w_router, w_gate, w_up, w_down) = params
    shapes = [x.shape[:2] for x in xs]
    tabs = [_rope_tables(seq) for _, seq in shapes]
    xts = [x.reshape(batch * seq, D_MODEL) for x, (batch, seq) in zip(xs, shapes)]
    for layer in range(norm_mix.shape[0]):
        j = layer // 2
        if layer % 2 == 0:
            prep = _prep_ab(norm_mix[layer], w_in_ab[j], qn_a[j], kn_a[j], qn_b[j], kn_b[j], w_out_ab[j],
                            norm_ffn[layer], w_router[layer])
            mixer = _layer_ab
        else:
            prep = _prep_c(norm_mix[layer], w_in_c[j], qn_c[j], kn_c[j], sink_c[j], w_out_c[j],
                           norm_ffn[layer], w_router[layer])
            mixer = _layer_c
        routed = [mixer(xt, batch, seq, tab, prep) for xt, (batch, seq), tab in zip(xts, shapes, tabs)]
        xts = _moe(routed, w_gate, w_up, w_down, layer)
    return tuple(xt.reshape(batch, seq, D_MODEL) for xt, (batch, seq) in zip(xts, shapes))


def kernel(x_prompt, x_sample, norm_mix, norm_ffn, w_in_ab, qn_a, kn_a, qn_b, kn_b, w_out_ab, w_in_c, qn_c,
           kn_c, sink_c, w_out_c, w_router, w_gate, w_up, w_down):
    params = (norm_mix, norm_ffn, w_in_ab, qn_a, kn_a, qn_b, kn_b, w_out_ab, w_in_c, qn_c, kn_c, sink_c,
              w_out_c, w_router, w_gate, w_up, w_down)
    return _encode((x_prompt, x_sample), params)
```

```python
import functools
import math

import jax
import jax.numpy as jnp
import numpy as np
from jax import lax
from jax.experimental import pallas as pl
from jax.experimental.pallas import tpu as pltpu

F32 = jnp.float32
BF16 = jnp.bfloat16
I32 = jnp.int32

D_MODEL = 1024
HEAD_DIM = 64
LANES = 128
MXU_COLS = 256
GRID_W = 64
ROT_DIM = HEAD_DIM // 4
ROPE_THETA = 500000.0
AXIAL_THETA = 10000.0
A_PATTERNS = ((128, 1), (512, 4), (2048, 16))
A_HEADS = 4
A_GROUPS = len(A_PATTERNS)
A_WIDTH = A_GROUPS * A_HEADS * HEAD_DIM
B_Q_HEADS = 12
B_KV_HEADS = 4
C_Q_HEADS = 16
C_KV_HEADS = 4
C_RADIUS = 128
N_EXPERTS = 16
EXPERT_FF = 1024
CAPACITY_FACTOR = 2
NEG_INF = -1e30
EPS = 1e-6

VMEM_LIMIT = 56 * 1024 * 1024

PROJ_TM = 512
OUT_TM = 1024
DENSE_TQ = 256
BAND_TQ = 256
BAND_COLS = 1024
BAND_SUBTILES = 4
DENSE_SUBTILES = 2
ACC_ROWS = LANES + 16
LSE_ROWS = 8
DENSE_TK = 512
MOE_TB = 256
MOE_W = 64
FFN_TM = 512


def _cparams(sem):
    return pltpu.CompilerParams(dimension_semantics=sem, vmem_limit_bytes=VMEM_LIMIT)


def _head_block_diag():
    idx = np.arange(MXU_COLS) // HEAD_DIM
    return jnp.asarray((idx[:, None] == idx[None, :]).astype(np.float32), dtype=BF16)


def _rope_tables(max_len):
    pos = jnp.arange(max_len, dtype=F32)
    j = np.arange(LANES) % HEAD_DIM

    def angles(p, dim, theta):
        exps = jnp.arange(0, dim, 2, dtype=F32) / dim
        inv = jnp.power(jnp.float32(theta), -exps)
        return p[:, None] * inv[None, :]

    half = ROT_DIM // 2
    ang = angles(pos, ROT_DIM, ROPE_THETA)
    cos, sin = jnp.cos(ang), jnp.sin(ang)
    fa = np.where(j < half, j, np.where(j < ROT_DIM, j - half, 0))
    cos_l, sin_l = cos[:, fa], sin[:, fa]
    lo = jnp.asarray(j < half)[None, :]
    hi = jnp.asarray((j >= half) & (j < ROT_DIM))[None, :]
    tab_a = jnp.stack([jnp.where(lo | hi, cos_l, 1.0),
                       jnp.where(lo, -sin_l, 0.0),
                       jnp.where(hi, sin_l, 0.0)])
    hb = HEAD_DIM // 2
    qb = hb // 2
    t = jnp.arange(max_len)
    ang_r = angles((t // GRID_W).astype(F32), hb, AXIAL_THETA)
    ang_c = angles((t % GRID_W).astype(F32), hb, AXIAL_THETA)
    fb = j % qb
    is_col = jnp.asarray(j >= hb)[None, :]
    ang_l = jnp.where(is_col, ang_c[:, fb], ang_r[:, fb])
    cos_b, sin_b = jnp.cos(ang_l), jnp.sin(ang_l)
    first = jnp.asarray((j % hb) < qb)[None, :]
    tab_b = jnp.stack([cos_b, jnp.where(first, -sin_b, 0.0), jnp.where(first, 0.0, sin_b)])
    return tab_a.astype(F32), tab_b.astype(F32)


def _pair_layout_perm(n_kv, group):
    cols = []
    for p in range(n_kv // 2):
        for g in range(group):
            for par in range(2):
                h = (2 * p + par) * group + g
                cols.extend(range(h * HEAD_DIM, (h + 1) * HEAD_DIM))
    return np.asarray(cols, dtype=np.int32)


def _proj_kernel(x_ref, g_ref, w_ref, s_ref, gain_ref, *rest, plan, n_tabs):
    tab_refs = rest[:n_tabs]
    out_refs = rest[n_tabs:-1]
    rows_ref = rest[-1]
    x = x_ref[...]
    tm = x.shape[0]
    ms = jnp.mean(x * x, axis=1, keepdims=True)
    xn = ((x * lax.rsqrt(ms + EPS)) * g_ref[...]).astype(BF16)

    def project(c):
        return jnp.dot(xn, w_ref[:, c * MXU_COLS:(c + 1) * MXU_COLS], preferred_element_type=F32)

    ahead = project(0)
    for c, (kind, out_i, out_col, tab_i, shift, dil) in enumerate(plan):
        acc = ahead
        if c + 1 < len(plan):
            ahead = project(c + 1)
        o_ref = out_refs[out_i]
        if kind == "v":
            o_ref[...] = acc.T.astype(BF16)
            continue
        ss = jnp.dot((acc * acc).astype(BF16), s_ref[...], preferred_element_type=F32)
        y = (acc * lax.rsqrt(ss * (1.0 / HEAD_DIM) + EPS)) * gain_ref[:, c * MXU_COLS:(c + 1) * MXU_COLS]
        tab = tab_refs[tab_i]
        t0 = jnp.concatenate([tab[0], tab[0]], axis=1)
        t1 = jnp.concatenate([tab[1], tab[1]], axis=1)
        t2 = jnp.concatenate([tab[2], tab[2]], axis=1)
        y = y * t0 + pltpu.roll(y, MXU_COLS - shift, 1) * t1 + pltpu.roll(y, shift, 1) * t2
        if dil == 1:
            o_ref[:, out_col:out_col + MXU_COLS] = y.astype(BF16)
        else:
            for half in range(MXU_COLS // LANES):
                rows_ref[half] = y[:, half * LANES:(half + 1) * LANES]
            for r in range(dil):
                for half in range(MXU_COLS // LANES):
                    piece = rows_ref[half, pl.ds(r, tm // dil, stride=dil), :]
                    col = r * MXU_COLS + half * LANES
                    o_ref[:, col:col + LANES] = piece.astype(BF16)


def _projection(x, norm_g, w, gains, tabs, plan, out_widths, seq_len):
    t_tokens = x.shape[0]
    tm = PROJ_TM
    n_in = w.shape[1]
    blocks_per_seq = seq_len // tm
    in_specs = [
        pl.BlockSpec((tm, D_MODEL), lambda i: (i, 0)),
        pl.BlockSpec((1, D_MODEL), lambda i: (0, 0)),
        pl.BlockSpec((D_MODEL, n_in), lambda i: (0, 0)),
        pl.BlockSpec((MXU_COLS, MXU_COLS), lambda i: (0, 0)),
        pl.BlockSpec((1, n_in), lambda i: (0, 0)),
    ] + [pl.BlockSpec((3, tm, LANES), lambda i: (0, i % blocks_per_seq, 0)) for _ in tabs]
    out_specs, out_shape = [], []
    for wd in out_widths:
        if wd is None:
            out_specs.append(pl.BlockSpec((None, MXU_COLS, tm),
                                          lambda i: (i // blocks_per_seq, 0, i % blocks_per_seq)))
            out_shape.append(jax.ShapeDtypeStruct((t_tokens // seq_len, MXU_COLS, seq_len), BF16))
        elif isinstance(wd, tuple):
            dil = wd[1]
            out_specs.append(pl.BlockSpec((tm // dil, dil * MXU_COLS), lambda i: (i, 0)))
            out_shape.append(jax.ShapeDtypeStruct((t_tokens // dil, dil * MXU_COLS), BF16))
        else:
            out_specs.append(pl.BlockSpec((tm, wd), lambda i: (i, 0)))
            out_shape.append(jax.ShapeDtypeStruct((t_tokens, wd), BF16))
    return pl.pallas_call(
        functools.partial(_proj_kernel, plan=tuple(plan), n_tabs=len(tabs)),
        grid=(t_tokens // tm,),
        in_specs=in_specs, out_specs=out_specs, out_shape=out_shape,
        scratch_shapes=[pltpu.VMEM((MXU_COLS // LANES, tm, LANES), F32)],
        compiler_params=_cparams(("parallel",)),
        name="projection",
    )(x, norm_g, w, _head_block_diag(), gains, *tabs)


LOG2E = math.log2(math.e)
LN2 = math.log(2.0)
_NT = (((1,), (1,)), ((), ()))


def _attn_kernel(*refs, group, n_pairs, tq, n_sub, tk, n_keys, band_r, has_sink, want_lse):
    refs = list(refs)
    sink_ref = refs.pop(0) if has_sink else None
    q_ref, k_ref, v_ref, o_ref = refs[:4]
    refs = refs[4:]
    lse_ref = refs.pop(0) if want_lse else None
    scratch = refs
    n_chain = 2 * n_pairs
    lane = lax.broadcasted_iota(I32, (1, LANES), 1)
    low = lane < HEAD_DIM
    half_mask = [jnp.where(low, 1.0, 0.0).astype(BF16), jnp.where(low, 0.0, 1.0).astype(BF16)]
    sub_q = tq // n_sub
    tile_rows = [slice(u * sub_q, (u + 1) * sub_q) for u in range(n_sub)]
    common = dict(group=group, n_pairs=n_pairs, tq=sub_q, tk=tk, n_keys=n_keys, band_r=band_r)
    if band_r is None:
        _attn_tiles(sink_ref, q_ref, k_ref, v_ref, o_ref, lse_ref, scratch, half_mask, tile_rows, None, **common)
    else:
        for u in range(n_sub):
            _attn_tiles(sink_ref, q_ref, k_ref, v_ref, o_ref, lse_ref, scratch, half_mask, tile_rows[u:u + 1],
                        pl.program_id(2) * n_sub + u, **common)


def _attn_tiles(sink_ref, q_ref, k_ref, v_ref, o_ref, lse_ref, scratch, half_mask, tile_rows, i, *,
                group, n_pairs, tq, tk, n_keys, band_r):
    has_sink = sink_ref is not None
    want_lse = lse_ref is not None
    n_chain = 2 * n_pairs
    m_cols = group * tq

    def masked_queries(q_rows):
        q_masked = []
        for p in range(n_pairs):
            q_p = jnp.concatenate(
                [q_ref[q_rows, (p * group + g) * LANES:(p * group + g + 1) * LANES] for g in range(group)],
                axis=0)
            q_masked += [q_p * half_mask[par] for par in range(2)]
        return q_masked

    sub = lax.broadcasted_iota(I32, (ACC_ROWS, m_cols), 0)
    if has_sink:
        acc0 = jnp.where(sub >= LANES, 1.0, 0.0).astype(F32)
        m0 = []
        for c in range(n_chain):
            m0.append(jnp.concatenate(
                [jnp.full((1, tq), sink_ref[c * group + g] * LOG2E, F32) for g in range(group)], axis=1))
    else:
        acc0 = jnp.zeros((ACC_ROWS, m_cols), F32)
        m0 = [jnp.full((1, m_cols), NEG_INF, F32) for _ in range(n_chain)]

    def values_t(p, k0, width):
        ones = jnp.ones((ACC_ROWS - LANES, width), BF16)
        return jnp.concatenate([v_ref[p * LANES:(p + 1) * LANES, pl.ds(k0, width)], ones], axis=0)

    def softmax_pv(s_t, v_t, m_prev, acc_prev, s_max=None):
        if s_max is None:
            s_max = jnp.max(s_t, axis=0, keepdims=True)
        m_new = jnp.maximum(m_prev, s_max)
        alpha = jnp.exp2(m_prev - m_new)
        p_t = jnp.exp2(s_t - m_new).astype(BF16)
        return m_new, alpha * acc_prev + jnp.dot(v_t, p_t, preferred_element_type=F32)

    def write_tile(q_rows, acc_fin, m_fin):
        top = lax.broadcasted_iota(I32, (LANES, tq), 0) < HEAD_DIM
        for p in range(n_pairs):
            even, odd = acc_fin[2 * p], acc_fin[2 * p + 1]
            for g in range(group):
                cols = slice(g * tq, (g + 1) * tq)
                num = jnp.where(top, even[:LANES, cols], odd[:LANES, cols])
                den = jnp.where(top, even[LANES:LANES + 1, cols], odd[LANES:LANES + 1, cols])
                out_cols = slice((p * group + g) * LANES, (p * group + g + 1) * LANES)
                o_ref[q_rows, out_cols] = (num / den).T.astype(o_ref.dtype)
        if want_lse:
            assert group == 1 and n_chain <= LSE_ROWS
            rows = [m_fin[c] * LN2 + jnp.log(acc_fin[c][LANES:LANES + 1, :]) for c in range(n_chain)]
            lse_ref[:, q_rows] = jnp.concatenate(rows + [jnp.zeros((LSE_ROWS - n_chain, tq), F32)], axis=0)

    if band_r is None:
        qm_ref, acc_ref, s_even, s_odd = scratch
        n_blocks = n_keys // tk

        def start_tile(u):
            q_masked = masked_queries(tile_rows[u])
            for c in range(n_chain):
                qm_ref[c] = q_masked[c]
                acc_ref[u % 2, c] = acc0

        def scores(kt, c, s_ref):
            s_t = lax.dot_general(kt, qm_ref[c], _NT, preferred_element_type=F32)
            s_ref[c] = s_t
            return jnp.max(s_t, axis=0, keepdims=True)

        def step(kb_next, s_next, kb, s_ref, m_prev, s_max, acc):
            k0 = pl.multiple_of(kb * tk, tk)
            m_next, max_next = [], []
            for p in range(n_pairs):
                v_t = values_t(p, k0, tk)
                if kb_next is not None:
                    next0 = kb_next * tk if isinstance(kb_next, int) else pl.multiple_of(kb_next * tk, tk)
                    kt = k_ref[pl.ds(next0, tk), p * LANES:(p + 1) * LANES]
                for par in range(2):
                    c = 2 * p + par
                    if kb_next is not None:
                        max_next.append(scores(kt, c, s_next))
                    m_new, acc[c] = softmax_pv(s_ref[c], v_t, m_prev[c], acc[c], s_max[c])
                    m_next.append(m_new)
            return m_next, max_next

        start_tile(0)
        s_max = []
        for p in range(n_pairs):
            kt = k_ref[0:tk, p * LANES:(p + 1) * LANES]
            s_max += [scores(kt, 2 * p + par, s_even) for par in range(2)]
        for u, q_rows in enumerate(tile_rows):
            acc = acc_ref.at[u % 2]

            def body(j, carry, acc=acc):
                m, mx = list(carry[:n_chain]), list(carry[n_chain:])
                m, mx = step(2 * j + 1, s_odd, 2 * j, s_even, m, mx, acc)
                m, mx = step(2 * j + 2, s_even, 2 * j + 1, s_odd, m, mx, acc)
                return tuple(m + mx)

            carry = lax.fori_loop(0, n_blocks // 2 - 1, body, tuple(m0 + s_max))
            m_fin, s_max = step(n_blocks - 1, s_odd, n_blocks - 2, s_even,
                                list(carry[:n_chain]), list(carry[n_chain:]), acc)
            if u + 1 < len(tile_rows):
                start_tile(u + 1)
                m_fin, s_max = step(0, s_even, n_blocks - 1, s_odd, m_fin, s_max, acc)
            else:
                m_fin, _ = step(None, None, n_blocks - 1, s_odd, m_fin, s_max, acc)
            write_tile(q_rows, [acc[c] for c in range(n_chain)], m_fin)
    else:
        halo = -(-band_r // LANES) * LANES
        width = min(tq + 2 * halo, n_keys)
        ws = pl.multiple_of(jnp.clip(i * tq - halo, 0, n_keys - width), LANES)
        kpos = ws + lax.broadcasted_iota(I32, (width, 1), 0)
        col = lax.broadcasted_iota(I32, (1, tq), 1)
        qpos = i * tq + jnp.concatenate([col] * group, axis=1)
        mask = jnp.abs(kpos - qpos) <= band_r
        q_masked = masked_queries(tile_rows[0])
        s_all = []
        for p in range(n_pairs):
            kt = k_ref[pl.ds(ws, width), p * LANES:(p + 1) * LANES]
            s_all += [lax.dot_general(kt, q_masked[2 * p + par], _NT, preferred_element_type=F32)
                      for par in range(2)]
        m_fin, acc_fin = [], []
        for c in range(n_chain):
            m_new, acc_new = softmax_pv(jnp.where(mask, s_all[c], NEG_INF), values_t(c // 2, ws, width),
                                        m0[c], acc0)
            m_fin.append(m_new)
            acc_fin.append(acc_new)
        write_tile(tile_rows[0], acc_fin, m_fin)


def _attention(q, k, v_t, *, batch, n_keys, residues, group, n_pairs, band_r, sink=None, want_lse=False):
    dense = band_r is None
    sub_q = DENSE_TQ if dense else min(BAND_TQ, BAND_COLS // group, n_keys)
    n_sub = max(1, min(DENSE_SUBTILES if dense else BAND_SUBTILES, n_keys // sub_q))
    tq = sub_q * n_sub
    wq = n_pairs * group * LANES
    wk = n_pairs * LANES
    q3 = q.reshape(batch, n_keys, residues * wq)
    k3 = k.reshape(batch, n_keys, residues * wk)
    v4 = v_t.reshape(batch, wk, n_keys, residues).transpose(0, 3, 1, 2)

    in_specs = []
    args = []
    if sink is not None:
        in_specs.append(pl.BlockSpec(memory_space=pltpu.SMEM))
        args.append(sink)
    in_specs += [
        pl.BlockSpec((None, tq, wq), lambda b, r, i: (b, i, r)),
        pl.BlockSpec((None, n_keys, wk), lambda b, r, i: (b, 0, r)),
        pl.BlockSpec((None, None, wk, n_keys), lambda b, r, i: (b, r, 0, 0)),
    ]
    args += [q3, k3, v4]
    out_specs = [pl.BlockSpec((None, tq, wq), lambda b, r, i: (b, i, r))]
    out_shape = [jax.ShapeDtypeStruct((batch, n_keys, residues * wq), BF16)]
    if want_lse:
        out_specs.append(pl.BlockSpec((None, None, LSE_ROWS, tq), lambda b, r, i: (b, r, 0, i)))
        out_shape.append(jax.ShapeDtypeStruct((batch, residues, LSE_ROWS, n_keys), F32))
    m_cols = group * sub_q
    scratch = []
    if dense:
        scratch = [pltpu.VMEM((2 * n_pairs, m_cols, LANES), BF16),
                   pltpu.VMEM((2, 2 * n_pairs, ACC_ROWS, m_cols), F32)]
        scratch += [pltpu.VMEM((2 * n_pairs, DENSE_TK, m_cols), F32) for _ in range(2)]
    outs = pl.pallas_call(
        functools.partial(_attn_kernel, group=group, n_pairs=n_pairs, tq=tq, n_sub=n_sub, tk=DENSE_TK,
                          n_keys=n_keys, band_r=band_r, has_sink=sink is not None, want_lse=want_lse),
        grid=(batch, residues, n_keys // tq),
        in_specs=in_specs, out_specs=out_specs, out_shape=out_shape,
        scratch_shapes=scratch,
        compiler_params=_cparams(("parallel", "parallel", "arbitrary")),
        name="attention_dense" if dense else "attention_band",
    )(*args)
    result = [outs[0].reshape(batch * n_keys, residues * wq)]
    if want_lse:
        result.append(outs[1].transpose(2, 0, 3, 1).reshape(LSE_ROWS, batch * n_keys * residues))
    return result


def _outproj_kernel(*refs, merge_dils):
    refs = list(refs)
    n_merge = len(merge_dils)
    n_scratch = sum(d > 1 for d in merge_dils)
    if n_merge:
        o_parts = refs[:n_merge]
        lse_parts = refs[n_merge:2 * n_merge]
        expand_ref = refs[2 * n_merge]
        scratch = refs[len(refs) - n_scratch:]
        refs = refs[2 * n_merge + 1:len(refs) - n_scratch]
    o_rest, x_ref, w_ref, g_ref, wrh_ref, wrl_ref, xo_ref, h_ref, aff_ref = refs
    tm = x_ref.shape[0]
    if n_merge:
        ordered = []
        free_scratch = list(scratch)
        for o_ref, dil in zip(o_parts, merge_dils):
            if dil > 1:
                rows_ref = free_scratch.pop(0)
                wd = o_ref.shape[1] // dil
                for r in range(dil):
                    for half in range(wd // LANES):
                        col = r * wd + half * LANES
                        rows_ref[half, pl.ds(r, tm // dil, stride=dil), :] = o_ref[:, col:col + LANES].astype(F32)
                ordered.append(lambda rows, ref=rows_ref: jnp.concatenate(
                    [ref[half, rows, :] for half in range(ref.shape[0])], axis=1))
            else:
                ordered.append(lambda rows, ref=o_ref: ref[rows, :].astype(F32))
        o_parts = ordered
    halves = [slice(0, tm // 2), slice(tm // 2, tm)]
    accs = []
    for rows in halves:
        acc = x_ref[rows, :]
        k0 = 0
        if n_merge:
            lses = [r[:, rows] for r in lse_parts]
            m = functools.reduce(jnp.maximum, lses)
            ws = [jnp.exp(l - m) for l in lses]
            inv = 1.0 / sum(ws)
            num = 0.0
            for wgt, r in zip(ws, o_parts):
                wn = wgt * inv
                hi = wn.astype(BF16)
                split = jnp.concatenate([hi, (wn - hi.astype(F32)).astype(BF16)], axis=0)
                full = lax.dot_general(split, expand_ref[...], (((0,), (0,)), ((), ())),
                                       preferred_element_type=F32)
                num = num + full * r(rows)
            oa = num.astype(BF16)
            k0 = oa.shape[1]
            acc = acc + jnp.dot(oa, w_ref[:k0, :], preferred_element_type=F32)
        accs.append(acc + jnp.dot(o_rest[rows, :], w_ref[k0:, :], preferred_element_type=F32))
    for rows, acc in zip(halves, accs):
        xo_ref[rows, :] = acc
        ms = jnp.mean(acc * acc, axis=1, keepdims=True)
        h = (acc * lax.rsqrt(ms + EPS)) * g_ref[...]
        h_hi = h.astype(BF16)
        h_ref[rows, :] = h_hi
        h_lo = (h - h_hi.astype(F32)).astype(BF16)
        w_both = jnp.concatenate([wrh_ref[...], wrl_ref[...]], axis=0)
        both = lax.dot_general(w_both, h_hi, _NT, preferred_element_type=F32)
        logits = (both[:N_EXPERTS] + both[N_EXPERTS:]
                  + lax.dot_general(wrh_ref[...], h_lo, _NT, preferred_element_type=F32))
        mx = jnp.max(logits, axis=0, keepdims=True)
        e = jnp.exp(logits - mx)
        aff_ref[:, rows] = e / jnp.sum(e, axis=0, keepdims=True)


def _out_projection(o_merge, lse_merge, o_rest, x, w_out, norm_g, wr_hi, wr_lo):
    t_tokens = x.shape[0]
    tm = OUT_TM
    n_merge = len(o_merge)
    row = lambda i: (i, 0)
    fixed = lambda i: (0, 0)
    merge_args = list(o_merge) + list(lse_merge)
    merge_dils = tuple(t_tokens // o.shape[0] for o in o_merge)
    in_specs = ([pl.BlockSpec((tm // dil, o.shape[1]), row) for o, dil in zip(o_merge, merge_dils)]
                + [pl.BlockSpec((LSE_ROWS, tm), lambda i: (0, i)) for _ in lse_merge])
    if n_merge:
        merged_width = o_merge[0].shape[1] // merge_dils[0]
        head = np.arange(merged_width) // HEAD_DIM
        expand = (np.arange(2 * LSE_ROWS)[:, None] % LSE_ROWS == head[None, :]).astype(np.float32)
        merge_args.append(jnp.asarray(expand, dtype=BF16))
        in_specs.append(pl.BlockSpec(expand.shape, fixed))
    in_specs = (in_specs
                + [pl.BlockSpec((tm, o_rest.shape[1]), row),
                   pl.BlockSpec((tm, D_MODEL), row),
                   pl.BlockSpec(w_out.shape, fixed),
                   pl.BlockSpec((1, D_MODEL), fixed),
                   pl.BlockSpec(wr_hi.shape, fixed),
                   pl.BlockSpec(wr_lo.shape, fixed)])
    out_specs = [pl.BlockSpec((tm, D_MODEL), row),
                 pl.BlockSpec((tm, D_MODEL), row),
                 pl.BlockSpec((N_EXPERTS, tm), lambda i: (0, i))]
    out_shape = [jax.ShapeDtypeStruct((t_tokens, D_MODEL), F32),
                 jax.ShapeDtypeStruct((t_tokens, D_MODEL), BF16),
                 jax.ShapeDtypeStruct((N_EXPERTS, t_tokens), F32)]
    return pl.pallas_call(
        functools.partial(_outproj_kernel, merge_dils=merge_dils),
        grid=(t_tokens // tm,),
        in_specs=in_specs, out_specs=out_specs, out_shape=out_shape,
        scratch_shapes=[pltpu.VMEM((merged_width // LANES, tm, LANES), F32) for dil in merge_dils if dil > 1],
        compiler_params=_cparams(("parallel",)),
        name="out_projection",
    )(*merge_args, o_rest, x, w_out, norm_g, wr_hi, wr_lo)


def _select_kernel(aff_ref, ind_ref, tri_ref, sel_ref, off_ref, *, cap):
    aff = aff_ref[...]
    n_tok = aff.shape[1]
    bits = pltpu.bitcast(aff, I32)

    def count(pred):
        return jnp.sum(jnp.where(pred, 1.0, 0.0), axis=1, keepdims=True)

    def value_step(j, thr):
        cand = thr | lax.shift_left(jnp.int32(1), 30 - j)
        return jnp.where(count(bits >= cand) >= cap, cand, thr)

    thr = lax.fori_loop(0, 31, value_step, jnp.zeros((N_EXPERTS, 1), I32))
    gt = bits > thr
    eq = bits == thr
    need = cap - count(gt)
    idx = lax.broadcasted_iota(I32, aff.shape, 1)
    idx_bits = int(math.log2(n_tok))

    def index_step(j, bound):
        cand = bound | lax.shift_left(jnp.int32(1), idx_bits - j)
        return jnp.where(count(eq & (idx < cand)) <= need, cand, bound)

    bound = lax.fori_loop(0, idx_bits + 1, index_step, jnp.zeros((N_EXPERTS, 1), I32))
    sel = jnp.where(gt | (eq & (idx < bound)), 1.0, 0.0).astype(BF16)
    sel_ref[...] = sel
    counts = jnp.dot(sel, ind_ref[...], preferred_element_type=F32)
    offs = jnp.dot(counts.astype(BF16), tri_ref[...], preferred_element_type=F32)
    off_ref[...] = offs.astype(I32)


def _strict_upper(n):
    return jnp.asarray(np.triu(np.ones((n, n), np.float32), k=1), dtype=BF16)


def _select(aff_t, cap):
    n_tok = aff_t.shape[1]
    ind = np.zeros((n_tok, LANES), np.float32)
    ind[np.arange(n_tok), np.arange(n_tok) // MOE_TB] = 1.0
    return pl.pallas_call(
        functools.partial(_select_kernel, cap=cap),
        out_shape=[jax.ShapeDtypeStruct((N_EXPERTS, n_tok), BF16),
                   jax.ShapeDtypeStruct((N_EXPERTS, LANES), I32)],
        compiler_params=pltpu.CompilerParams(vmem_limit_bytes=VMEM_LIMIT),
        name="select",
    )(aff_t, jnp.asarray(ind, dtype=BF16), _strict_upper(LANES))


SLOT_ALIGN = 16
ALIGN_BITS = 4
CHUNK_BITS = 6
NOT_SELECTED = -1e6


class _BlockLayout:
    def __init__(self, off_ref, b):
        self.start = [off_ref[e, b] for e in range(N_EXPERTS)]
        self.end = [off_ref[e, b + 1] for e in range(N_EXPERTS)]
        def floor_tile(v):
            return lax.shift_left(lax.shift_right_logical(v, ALIGN_BITS), ALIGN_BITS)

        self.base = [floor_tile(s) for s in self.start]
        self.shift = [s - a for s, a in zip(self.start, self.base)]
        span = [en - a for en, a in zip(self.end, self.base)]
        self.n_chunks = functools.reduce(
            jnp.maximum, [lax.shift_right_logical(sp, CHUNK_BITS) for sp in span]) + 1
        tail = [floor_tile(sp) for sp in span]
        self.tail_chunk = [lax.shift_right_logical(t, CHUNK_BITS) for t in tail]
        self.tail_row = [pl.multiple_of(t & (MOE_W - 1), SLOT_ALIGN) for t in tail]

    def window_row(self, e, chunk):
        return pl.multiple_of(self.base[e] + chunk * MOE_W, SLOT_ALIGN)


def _slot_onehots(pos, weight, chunk):
    n_tok = pos.shape[1]
    slot = (lax.broadcasted_iota(I32, (MOE_W, n_tok), 0) + chunk * MOE_W).astype(F32)
    parts = []
    for e in range(N_EXPERTS):
        hit = pos[e:e + 1, :] == slot
        val = 1.0 if weight is None else weight[e:e + 1, :]
        parts.append(jnp.where(hit, val, 0.0).astype(BF16))
    return jnp.concatenate(parts, axis=0)


def _window_positions(sel_ref, tri_ref, layout):
    sel = sel_ref[...]
    rank = jnp.dot(sel, tri_ref[...], preferred_element_type=F32)
    shift = jnp.concatenate([jnp.full((1, 1), s, I32) for s in layout.shift], axis=0).astype(F32)
    return jnp.where(sel > 0, rank + shift, NOT_SELECTED)


def _dispatch_kernel(off_ref, sel_ref, h_ref, tri_ref, xe_ref, stage_ref, extra_ref, tail_ref, zero_ref, sem,
                     sem_extra):
    b = pl.program_id(0)
    nb = pl.num_programs(0)
    slot = b % 2
    layout = _BlockLayout(off_ref, b)
    pos = _window_positions(sel_ref, tri_ref, layout)
    h = h_ref[...]

    def rows_for(chunk):
        return jnp.dot(_slot_onehots(pos, None, chunk), h, preferred_element_type=F32)

    def copy(src, e, row0, s):
        return pltpu.make_async_copy(src.at[e], xe_ref.at[e, pl.ds(row0, MOE_W)], s)

    @pl.when(b == 0)
    def _():
        tail_ref[...] = jnp.zeros(tail_ref.shape, BF16)
        zero_ref[...] = jnp.zeros(zero_ref.shape, BF16)
        pad = zero_ref.shape[0]
        fills = [pltpu.make_async_copy(zero_ref, xe_ref.at[e, pl.ds(xe_ref.shape[1] - pad, pad)], sem_extra.at[0])
                 for e in range(N_EXPERTS)]
        for f in fills:
            f.start()
        for f in fills:
            f.wait()

    rows0 = rows_for(0)

    for e in range(N_EXPERTS):
        r0 = e * MOE_W
        first = rows0[r0:r0 + SLOT_ALIGN] + tail_ref[e].astype(F32)
        stage_ref[slot, e, :SLOT_ALIGN] = first.astype(BF16)
        stage_ref[slot, e, SLOT_ALIGN:] = rows0[r0 + SLOT_ALIGN:r0 + MOE_W].astype(BF16)
    for e in range(N_EXPERTS):
        tile = stage_ref[slot, e, pl.ds(layout.tail_row[e], SLOT_ALIGN), :]
        tail_ref[e] = jnp.where(layout.tail_chunk[e] == 0, tile, tail_ref[e])

    @pl.when(b > 0)
    def _():
        for e in range(N_EXPERTS):
            copy(stage_ref.at[1 - slot], e, 0, sem.at[1 - slot]).wait()

    for e in range(N_EXPERTS):
        copy(stage_ref.at[slot], e, layout.window_row(e, 0), sem.at[slot]).start()

    def overflow(chunk, carry):
        more = rows_for(chunk).astype(BF16)
        for e in range(N_EXPERTS):
            extra_ref[e] = more[e * MOE_W:(e + 1) * MOE_W]
        for e in range(N_EXPERTS):
            copy(extra_ref, e, layout.window_row(e, chunk), sem_extra.at[0]).start()
        for e in range(N_EXPERTS):
            @pl.when(layout.tail_chunk[e] == chunk)
            def _():
                tail_ref[e] = extra_ref[e, pl.ds(layout.tail_row[e], SLOT_ALIGN), :]
        for e in range(N_EXPERTS):
            copy(extra_ref, e, 0, sem_extra.at[0]).wait()
        return carry

    lax.fori_loop(1, layout.n_chunks, overflow, 0)

    @pl.when(b == nb - 1)
    def _():
        for e in range(N_EXPERTS):
            copy(stage_ref.at[slot], e, 0, sem.at[slot]).wait()


def _slot_rows(cap):
    pad = ((SLOT_ALIGN + MOE_TB) // MOE_W + 1) * MOE_W
    return cap + -(-pad // FFN_TM) * FFN_TM


def _dispatch(offs, sel, h, cap):
    n_tok = h.shape[0]
    nb = n_tok // MOE_TB
    rows = _slot_rows(cap)
    grid_spec = pltpu.PrefetchScalarGridSpec(
        num_scalar_prefetch=1,
        grid=(nb,),
        in_specs=[pl.BlockSpec((N_EXPERTS, MOE_TB), lambda b, off: (0, b)),
                  pl.BlockSpec((MOE_TB, D_MODEL), lambda b, off: (b, 0)),
                  pl.BlockSpec((MOE_TB, MOE_TB), lambda b, off: (0, 0))],
        out_specs=pl.BlockSpec(memory_space=pl.ANY),
        scratch_shapes=[pltpu.VMEM((2, N_EXPERTS, MOE_W, D_MODEL), BF16),
                        pltpu.VMEM((N_EXPERTS, MOE_W, D_MODEL), BF16),
                        pltpu.VMEM((N_EXPERTS, SLOT_ALIGN, D_MODEL), BF16),
                        pltpu.VMEM((rows - cap, D_MODEL), BF16),
                        pltpu.SemaphoreType.DMA((2,)),
                        pltpu.SemaphoreType.DMA((1,))],
    )
    return pl.pallas_call(
        _dispatch_kernel,
        grid_spec=grid_spec,
        out_shape=jax.ShapeDtypeStruct((N_EXPERTS, rows, D_MODEL), BF16),
        compiler_params=_cparams(("arbitrary",)),
        name="dispatch",
    )(offs, sel, h, _strict_upper(MOE_TB))


def _ffn_kernel(*refs, first_step, n_tiles):
    n_groups = len(n_tiles)
    xe_refs = refs[:n_groups]
    wg_ref, wu_ref, wd_ref = refs[n_groups:n_groups + 3]
    ye_refs = refs[n_groups + 3:2 * n_groups + 3]
    wg_bf, wu_bf, wd_bf = refs[2 * n_groups + 3:]
    j = pl.program_id(1)

    @pl.when(j == 0)
    def _():
        wg_bf[...] = wg_ref[...].astype(BF16)
        wu_bf[...] = wu_ref[...].astype(BF16)
        wd_bf[...] = wd_ref[...].astype(BF16)

    for g in range(n_groups):
        @pl.when((j >= first_step[g]) & (j < first_step[g] + n_tiles[g]))
        def _():
            x = xe_refs[g][...]
            gate = jnp.dot(x, wg_bf[...], preferred_element_type=F32)
            up = jnp.dot(x, wu_bf[...], preferred_element_type=F32)
            hid = (gate * jax.nn.sigmoid(gate) * up).astype(BF16)
            ye_refs[g][...] = jnp.dot(hid, wd_bf[...], preferred_element_type=F32).astype(BF16)


def _expert_ffn(xes, w_gate, w_up, w_down, layer, caps):
    n_groups = len(xes)
    n_tiles = [cap // FFN_TM for cap in caps]
    first_step = [sum(n_tiles[:g]) for g in range(n_groups)]

    def weight_index(e, j):
        return (layer, jnp.minimum(e + (j >= 1).astype(I32), N_EXPERTS - 1), 0, 0)

    wspec = lambda shape: pl.BlockSpec((None, None) + shape, weight_index)

    def tile_spec(g):
        return pl.BlockSpec((None, FFN_TM, D_MODEL),
                            lambda e, j: (e, jnp.clip(j - first_step[g], 0, n_tiles[g] - 1), 0))

    return pl.pallas_call(
        functools.partial(_ffn_kernel, first_step=tuple(first_step), n_tiles=tuple(n_tiles)),
        grid=(N_EXPERTS, sum(n_tiles)),
        in_specs=([tile_spec(g) for g in range(n_groups)]
                  + [wspec((D_MODEL, EXPERT_FF)), wspec((D_MODEL, EXPERT_FF)), wspec((EXPERT_FF, D_MODEL))]),
        out_specs=[tile_spec(g) for g in range(n_groups)],
        out_shape=[jax.ShapeDtypeStruct(xe.shape, BF16) for xe in xes],
        input_output_aliases={g: g for g in range(n_groups)},
        scratch_shapes=[pltpu.VMEM((D_MODEL, EXPERT_FF), BF16), pltpu.VMEM((D_MODEL, EXPERT_FF), BF16),
                        pltpu.VMEM((EXPERT_FF, D_MODEL), BF16)],
        compiler_params=_cparams(("arbitrary", "arbitrary")),
        name="expert_ffn",
    )(*xes, w_gate, w_up, w_down)


def _combine_kernel(off_ref, sel_ref, aff_ref, tri_ref, x_ref, ye_ref, out_ref, win_ref, extra_ref, sem, sem_extra,
                    *, rows):
    b = pl.program_id(0)
    nb = pl.num_programs(0)
    slot = b % 2

    def copy(dst, e, row0, s):
        return pltpu.make_async_copy(ye_ref.at[e, pl.ds(row0, MOE_W)], dst.at[e], s)

    def fetch(blk, dst_slot):
        ahead = _BlockLayout(off_ref, blk)
        for e in range(N_EXPERTS):
            copy(win_ref.at[dst_slot], e, ahead.window_row(e, 0), sem.at[dst_slot]).start()

    @pl.when(b == 0)
    def _():
        fetch(0, 0)

    @pl.when(b + 1 < nb)
    def _():
        fetch(b + 1, 1 - slot)

    layout = _BlockLayout(off_ref, b)
    pos = _window_positions(sel_ref, tri_ref, layout)
    aff = aff_ref[...]
    tn = (((0,), (0,)), ((), ()))

    def contribution(chunk, window):
        gates = _slot_onehots(pos, aff, chunk)
        vals = window.reshape(N_EXPERTS * MOE_W, window.shape[-1])
        return lax.dot_general(gates, vals, tn, preferred_element_type=F32)

    for e in range(N_EXPERTS):
        copy(win_ref.at[slot], e, 0, sem.at[slot]).wait()
    out_ref[...] = x_ref[...] + contribution(0, win_ref[slot])

    def overflow(chunk, carry):
        for e in range(N_EXPERTS):
            row0 = pl.multiple_of(jnp.minimum(layout.window_row(e, chunk), rows - MOE_W), SLOT_ALIGN)
            copy(extra_ref, e, row0, sem_extra.at[0]).start()
        for e in range(N_EXPERTS):
            copy(extra_ref, e, 0, sem_extra.at[0]).wait()
        out_ref[...] += contribution(chunk, extra_ref[...])
        return carry

    lax.fori_loop(1, layout.n_chunks, overflow, 0)


def _combine(offs, sel, aff_t, x, ye):
    n_tok = x.shape[0]
    nb = n_tok // MOE_TB
    rows = ye.shape[1]
    grid_spec = pltpu.PrefetchScalarGridSpec(
        num_scalar_prefetch=1,
        grid=(nb,),
        in_specs=[pl.BlockSpec((N_EXPERTS, MOE_TB), lambda b, off: (0, b)),
                  pl.BlockSpec((N_EXPERTS, MOE_TB), lambda b, off: (0, b)),
                  pl.BlockSpec((MOE_TB, MOE_TB), lambda b, off: (0, 0)),
                  pl.BlockSpec((MOE_TB, D_MODEL), lambda b, off: (b, 0)),
                  pl.BlockSpec(memory_space=pl.ANY)],
        out_specs=pl.BlockSpec((MOE_TB, D_MODEL), lambda b, off: (b, 0)),
        scratch_shapes=[pltpu.VMEM((2, N_EXPERTS, MOE_W, D_MODEL), BF16),
                        pltpu.VMEM((N_EXPERTS, MOE_W, D_MODEL), BF16),
                        pltpu.SemaphoreType.DMA((2,)),
                        pltpu.SemaphoreType.DMA((1,))],
    )
    return pl.pallas_call(
        functools.partial(_combine_kernel, rows=rows),
        grid_spec=grid_spec,
        out_shape=jax.ShapeDtypeStruct((n_tok, D_MODEL), F32),
        compiler_params=_cparams(("arbitrary",)),
        name="combine",
    )(offs, sel, aff_t, _strict_upper(MOE_TB), x, ye)


def _moe(routed, w_gate, w_up, w_down, layer):
    plans = []
    for x, h, aff_t in routed:
        cap = CAPACITY_FACTOR * x.shape[0] // N_EXPERTS
        sel, offs = _select(aff_t, cap)
        plans.append((cap, sel, offs, _dispatch(offs, sel, h, cap)))
    yes = _expert_ffn([p[3] for p in plans], w_gate, w_up, w_down, layer, [p[0] for p in plans])
    return [_combine(offs, sel, aff_t, x, ye)
            for (x, _, aff_t), (_, sel, offs, _), ye in zip(routed, plans, yes)]


def _row(v):
    return v.reshape(1, -1).astype(F32)


def _router_split(w_router):
    wt = w_router.T.astype(F32)
    hi = wt.astype(BF16)
    lo = (wt - hi.astype(F32)).astype(BF16)
    return hi, lo


def _prep_ab(norm_g, w_in, qn_a, kn_a, qn_b, kn_b, w_out, norm_ffn, w_router):
    scale = HEAD_DIM ** -0.5 * LOG2E
    perm_b = _pair_layout_perm(B_KV_HEADS, B_Q_HEADS // B_KV_HEADS)
    a3 = 3 * A_WIDTH
    bq = B_Q_HEADS * HEAD_DIM
    w_cols = np.concatenate([np.arange(a3), a3 + perm_b, np.arange(a3 + bq, w_in.shape[1])])
    gains = jnp.concatenate([
        jnp.repeat(qn_a, A_HEADS, axis=0).reshape(-1) * scale,
        jnp.repeat(kn_a, A_HEADS, axis=0).reshape(-1),
        jnp.ones((A_WIDTH,), F32),
        jnp.tile(qn_b, B_Q_HEADS) * scale,
        jnp.tile(kn_b, B_KV_HEADS),
        jnp.ones((B_KV_HEADS * HEAD_DIM,), F32)])
    a_out = A_HEADS * HEAD_DIM
    w_o = w_out[np.concatenate([np.arange(a_out), a_out + perm_b])].astype(BF16)
    return (_row(norm_g), w_in[:, w_cols].astype(BF16), _row(gains), w_o, _row(norm_ffn)) + _router_split(w_router)


def _layer_ab(x, batch, seq, tabs, prep):
    norm_g, w, gains, w_o, norm_ffn, wr_hi, wr_lo = prep
    tab_a, tab_b = tabs
    bq = B_Q_HEADS * HEAD_DIM
    n_a = A_GROUPS
    assert A_HEADS * HEAD_DIM == MXU_COLS
    dils = [dil for _, dil in A_PATTERNS]
    plan = ([("qk", g, 0, 0, ROT_DIM // 2, dils[g]) for g in range(n_a)]
            + [("qk", n_a + g, 0, 0, ROT_DIM // 2, dils[g]) for g in range(n_a)]
            + [("v", 2 * n_a + g, 0, 0, 0, 1) for g in range(n_a)]
            + [("qk", 3 * n_a, c * MXU_COLS, 1, HEAD_DIM // 4, 1) for c in range(bq // MXU_COLS)]
            + [("qk", 3 * n_a + 1, 0, 1, HEAD_DIM // 4, 1), ("v", 3 * n_a + 2, 0, 0, 0, 1)])
    assert B_KV_HEADS * HEAD_DIM == MXU_COLS
    a_widths = tuple(MXU_COLS if dil == 1 else ("dilated", dil) for dil in dils)
    outs = _projection(
        x, norm_g, w, gains, (tab_a, tab_b), plan,
        a_widths * 2 + (None,) * n_a + (bq, MXU_COLS, None), seq)
    qa, ka, va = outs[:n_a], outs[n_a:2 * n_a], outs[2 * n_a:3 * n_a]
    qb, kb, vb = outs[3 * n_a:]
    o_parts, lse_parts = [], []
    for gi, (window, dil) in enumerate(A_PATTERNS):
        o, lse = _attention(qa[gi], ka[gi], va[gi], batch=batch, n_keys=seq // dil, residues=dil, group=1,
                            n_pairs=A_HEADS // 2, band_r=window // (2 * dil), want_lse=True)
        o_parts.append(o)
        lse_parts.append(lse)
    (ob,) = _attention(qb, kb, vb, batch=batch, n_keys=seq, residues=1,
                       group=B_Q_HEADS // B_KV_HEADS, n_pairs=B_KV_HEADS // 2, band_r=None)
    return _out_projection(o_parts, lse_parts, ob, x, w_o, norm_ffn, wr_hi, wr_lo)


def _prep_c(norm_g, w_in, qn, kn, sink, w_out, norm_ffn, w_router):
    scale = HEAD_DIM ** -0.5 * LOG2E
    perm = _pair_layout_perm(C_KV_HEADS, C_Q_HEADS // C_KV_HEADS)
    cq = C_Q_HEADS * HEAD_DIM
    w_cols = np.concatenate([perm, np.arange(cq, w_in.shape[1])])
    gains = jnp.concatenate([jnp.tile(qn, C_Q_HEADS) * scale, jnp.tile(kn, C_KV_HEADS),
                             jnp.ones((C_KV_HEADS * HEAD_DIM,), F32)])
    return (_row(norm_g), w_in[:, w_cols].astype(BF16), _row(gains), sink.astype(F32),
            w_out[perm].astype(BF16), _row(norm_ffn)) + _router_split(w_router)


def _layer_c(x, batch, seq, tabs, prep):
    norm_g, w, gains, sink, w_o, norm_ffn, wr_hi, wr_lo = prep
    tab_a, _ = tabs
    cq = C_Q_HEADS * HEAD_DIM
    plan = ([("qk", 0, c * MXU_COLS, 0, ROT_DIM // 2, 1) for c in range(cq // MXU_COLS)]
            + [("qk", 1, 0, 0, ROT_DIM // 2, 1), ("v", 2, 0, 0, 0, 1)])
    assert C_KV_HEADS * HEAD_DIM == MXU_COLS
    q, k, v = _projection(x, norm_g, w, gains, (tab_a,), plan, (cq, MXU_COLS, None), seq)
    (o,) = _attention(q, k, v, batch=batch, n_keys=seq, residues=1,
                      group=C_Q_HEADS // C_KV_HEADS, n_pairs=C_KV_HEADS // 2, band_r=C_RADIUS, sink=sink)
    return _out_projection([], [], o, x, w_o, norm_ffn, wr_hi, wr_lo)


def _encode(xs, params):
    (norm_mix, norm_ffn, w_in_ab, qn_a, kn_a, qn_b, kn_b, w_out_ab, w_in_c, qn_c, kn_c, sink_c, w_out_c,
     w_router, w_gate, w_up, w_down) = params
    shapes = [x.shape[:2] for x in xs]
    tabs = [_rope_tables(seq) for _, seq in shapes]
    xts = [x.reshape(batch * seq, D_MODEL) for x, (batch, seq) in zip(xs, shapes)]
    for layer in range(norm_mix.shape[0]):
        j = layer // 2
        if layer % 2 == 0:
            prep = _prep_ab(norm_mix[layer], w_in_ab[j], qn_a[j], kn_a[j], qn_b[j], kn_b[j], w_out_ab[j],
                            norm_ffn[layer], w_router[layer])
            mixer = _layer_ab
        else:
            prep = _prep_c(norm_mix[layer], w_in_c[j], qn_c[j], kn_c[j], sink_c[j], w_out_c[j],
                           norm_ffn[layer], w_router[layer])
            mixer = _layer_c
        routed = [mixer(xt, batch, seq, tab, prep) for xt, (batch, seq), tab in zip(xts, shapes, tabs)]
        xts = _moe(routed, w_gate, w_up, w_down, layer)
    return tuple(xt.reshape(batch, seq, D_MODEL) for xt, (batch, seq) in zip(xts, shapes))


def kernel(x_prompt, x_sample, norm_mix, norm_ffn, w_in_ab, qn_a, kn_a, qn_b, kn_b, w_out_ab, w_in_c, qn_c,
           kn_c, sink_c, w_out_c, w_router, w_gate, w_up, w_down):
    params = (norm_mix, norm_ffn, w_in_ab, qn_a, kn_a, qn_b, kn_b, w_out_ab, w_in_c, qn_c, kn_c, sink_c,
              w_out_c, w_router, w_gate, w_up, w_down)
    return _encode((x_prompt, x_sample), params)
```

```python
import functools
import math

import jax
import jax.numpy as jnp
import numpy as np
from jax import lax
from jax.experimental import pallas as pl
from jax.experimental.pallas import tpu as pltpu

F32 = jnp.float32
BF16 = jnp.bfloat16
I32 = jnp.int32

D_MODEL = 1024
HEAD_DIM = 64
LANES = 128
MXU_COLS = 256
GRID_W = 64
ROT_DIM = HEAD_DIM // 4
ROPE_THETA = 500000.0
AXIAL_THETA = 10000.0
A_PATTERNS = ((128, 1), (512, 4), (2048, 16))
A_HEADS = 4
A_GROUPS = len(A_PATTERNS)
A_WIDTH = A_GROUPS * A_HEADS * HEAD_DIM
B_Q_HEADS = 12
B_KV_HEADS = 4
C_Q_HEADS = 16
C_KV_HEADS = 4
C_RADIUS = 128
N_EXPERTS = 16
EXPERT_FF = 1024
CAPACITY_FACTOR = 2
NEG_INF = -1e30
EPS = 1e-6

VMEM_LIMIT = 56 * 1024 * 1024

PROJ_TM = 512
OUT_TM = 1024
DENSE_TQ = 256
BAND_TQ = 256
BAND_COLS = 1024
BAND_SUBTILES = 4
DENSE_SUBTILES = 2
ACC_ROWS = LANES + 16
LSE_ROWS = 8
DENSE_TK = 512
MOE_TB = 256
MOE_W = 64
MOE_BLOCKS_PER_STEP = 2
FFN_TM = 512


def _cparams(sem):
    return pltpu.CompilerParams(dimension_semantics=sem, vmem_limit_bytes=VMEM_LIMIT)


def _head_block_diag():
    idx = np.arange(MXU_COLS) // HEAD_DIM
    return jnp.asarray((idx[:, None] == idx[None, :]).astype(np.float32), dtype=BF16)


def _rope_tables(max_len):
    pos = jnp.arange(max_len, dtype=F32)
    j = np.arange(LANES) % HEAD_DIM

    def angles(p, dim, theta):
        exps = jnp.arange(0, dim, 2, dtype=F32) / dim
        inv = jnp.power(jnp.float32(theta), -exps)
        return p[:, None] * inv[None, :]

    half = ROT_DIM // 2
    ang = angles(pos, ROT_DIM, ROPE_THETA)
    cos, sin = jnp.cos(ang), jnp.sin(ang)
    fa = np.where(j < half, j, np.where(j < ROT_DIM, j - half, 0))
    cos_l, sin_l = cos[:, fa], sin[:, fa]
    lo = jnp.asarray(j < half)[None, :]
    hi = jnp.asarray((j >= half) & (j < ROT_DIM))[None, :]
    tab_a = jnp.stack([jnp.where(lo | hi, cos_l, 1.0),
                       jnp.where(lo, -sin_l, 0.0),
                       jnp.where(hi, sin_l, 0.0)])
    hb = HEAD_DIM // 2
    qb = hb // 2
    t = jnp.arange(max_len)
    ang_r = angles((t // GRID_W).astype(F32), hb, AXIAL_THETA)
    ang_c = angles((t % GRID_W).astype(F32), hb, AXIAL_THETA)
    fb = j % qb
    is_col = jnp.asarray(j >= hb)[None, :]
    ang_l = jnp.where(is_col, ang_c[:, fb], ang_r[:, fb])
    cos_b, sin_b = jnp.cos(ang_l), jnp.sin(ang_l)
    first = jnp.asarray((j % hb) < qb)[None, :]
    tab_b = jnp.stack([cos_b, jnp.where(first, -sin_b, 0.0), jnp.where(first, 0.0, sin_b)])
    return tab_a.astype(F32), tab_b.astype(F32)


def _pair_layout_perm(n_kv, group):
    cols = []
    for p in range(n_kv // 2):
        for g in range(group):
            for par in range(2):
                h = (2 * p + par) * group + g
                cols.extend(range(h * HEAD_DIM, (h + 1) * HEAD_DIM))
    return np.asarray(cols, dtype=np.int32)


def _proj_kernel(x_ref, g_ref, w_ref, s_ref, gain_ref, *rest, plan, n_tabs):
    tab_refs = rest[:n_tabs]
    out_refs = rest[n_tabs:-1]
    rows_ref = rest[-1]
    x = x_ref[...]
    tm = x.shape[0]
    ms = jnp.mean(x * x, axis=1, keepdims=True)
    xn = ((x * lax.rsqrt(ms + EPS)) * g_ref[...]).astype(BF16)

    def project(c):
        return jnp.dot(xn, w_ref[:, c * MXU_COLS:(c + 1) * MXU_COLS], preferred_element_type=F32)

    ahead = project(0)
    for c, (kind, out_i, out_col, tab_i, shift, dil) in enumerate(plan):
        acc = ahead
        if c + 1 < len(plan):
            ahead = project(c + 1)
        o_ref = out_refs[out_i]
        if kind == "v":
            o_ref[...] = acc.T.astype(BF16)
            continue
        ss = jnp.dot((acc * acc).astype(BF16), s_ref[...], preferred_element_type=F32)
        y = (acc * lax.rsqrt(ss * (1.0 / HEAD_DIM) + EPS)) * gain_ref[:, c * MXU_COLS:(c + 1) * MXU_COLS]
        tab = tab_refs[tab_i]
        t0 = jnp.concatenate([tab[0], tab[0]], axis=1)
        t1 = jnp.concatenate([tab[1], tab[1]], axis=1)
        t2 = jnp.concatenate([tab[2], tab[2]], axis=1)
        y = y * t0 + pltpu.roll(y, MXU_COLS - shift, 1) * t1 + pltpu.roll(y, shift, 1) * t2
        if dil == 1:
            o_ref[:, out_col:out_col + MXU_COLS] = y.astype(BF16)
        else:
            for half in range(MXU_COLS // LANES):
                rows_ref[half] = y[:, half * LANES:(half + 1) * LANES]
            for r in range(dil):
                for half in range(MXU_COLS // LANES):
                    piece = rows_ref[half, pl.ds(r, tm // dil, stride=dil), :]
                    col = r * MXU_COLS + half * LANES
                    o_ref[:, col:col + LANES] = piece.astype(BF16)


def _projection(x, norm_g, w, gains, tabs, plan, out_widths, seq_len):
    t_tokens = x.shape[0]
    tm = PROJ_TM
    n_in = w.shape[1]
    blocks_per_seq = seq_len // tm
    in_specs = [
        pl.BlockSpec((tm, D_MODEL), lambda i: (i, 0)),
        pl.BlockSpec((1, D_MODEL), lambda i: (0, 0)),
        pl.BlockSpec((D_MODEL, n_in), lambda i: (0, 0)),
        pl.BlockSpec((MXU_COLS, MXU_COLS), lambda i: (0, 0)),
        pl.BlockSpec((1, n_in), lambda i: (0, 0)),
    ] + [pl.BlockSpec((3, tm, LANES), lambda i: (0, i % blocks_per_seq, 0)) for _ in tabs]
    out_specs, out_shape = [], []
    for wd in out_widths:
        if wd is None:
            out_specs.append(pl.BlockSpec((None, MXU_COLS, tm),
                                          lambda i: (i // blocks_per_seq, 0, i % blocks_per_seq)))
            out_shape.append(jax.ShapeDtypeStruct((t_tokens // seq_len, MXU_COLS, seq_len), BF16))
        elif isinstance(wd, tuple):
            dil = wd[1]
            out_specs.append(pl.BlockSpec((tm // dil, dil * MXU_COLS), lambda i: (i, 0)))
            out_shape.append(jax.ShapeDtypeStruct((t_tokens // dil, dil * MXU_COLS), BF16))
        else:
            out_specs.append(pl.BlockSpec((tm, wd), lambda i: (i, 0)))
            out_shape.append(jax.ShapeDtypeStruct((t_tokens, wd), BF16))
    return pl.pallas_call(
        functools.partial(_proj_kernel, plan=tuple(plan), n_tabs=len(tabs)),
        grid=(t_tokens // tm,),
        in_specs=in_specs, out_specs=out_specs, out_shape=out_shape,
        scratch_shapes=[pltpu.VMEM((MXU_COLS // LANES, tm, LANES), F32)],
        compiler_params=_cparams(("parallel",)),
        name="projection",
    )(x, norm_g, w, _head_block_diag(), gains, *tabs)


LOG2E = math.log2(math.e)
LN2 = math.log(2.0)
_NT = (((1,), (1,)), ((), ()))


def _attn_kernel(*refs, group, n_pairs, tq, n_sub, tk, n_keys, band_r, has_sink, want_lse):
    refs = list(refs)
    sink_ref = refs.pop(0) if has_sink else None
    q_ref, k_ref, v_ref, o_ref = refs[:4]
    refs = refs[4:]
    lse_ref = refs.pop(0) if want_lse else None
    scratch = refs
    n_chain = 2 * n_pairs
    lane = lax.broadcasted_iota(I32, (1, LANES), 1)
    low = lane < HEAD_DIM
    half_mask = [jnp.where(low, 1.0, 0.0).astype(BF16), jnp.where(low, 0.0, 1.0).astype(BF16)]
    sub_q = tq // n_sub
    tile_rows = [slice(u * sub_q, (u + 1) * sub_q) for u in range(n_sub)]
    common = dict(group=group, n_pairs=n_pairs, tq=sub_q, tk=tk, n_keys=n_keys, band_r=band_r)
    if band_r is None:
        _attn_tiles(sink_ref, q_ref, k_ref, v_ref, o_ref, lse_ref, scratch, half_mask, tile_rows, None, **common)
    else:
        for u in range(n_sub):
            _attn_tiles(sink_ref, q_ref, k_ref, v_ref, o_ref, lse_ref, scratch, half_mask, tile_rows[u:u + 1],
                        pl.program_id(2) * n_sub + u, **common)


def _attn_tiles(sink_ref, q_ref, k_ref, v_ref, o_ref, lse_ref, scratch, half_mask, tile_rows, i, *,
                group, n_pairs, tq, tk, n_keys, band_r):
    has_sink = sink_ref is not None
    want_lse = lse_ref is not None
    n_chain = 2 * n_pairs
    m_cols = group * tq

    def masked_queries(q_rows):
        q_masked = []
        for p in range(n_pairs):
            q_p = jnp.concatenate(
                [q_ref[q_rows, (p * group + g) * LANES:(p * group + g + 1) * LANES] for g in range(group)],
                axis=0)
            q_masked += [q_p * half_mask[par] for par in range(2)]
        return q_masked

    sub = lax.broadcasted_iota(I32, (ACC_ROWS, m_cols), 0)
    if has_sink:
        acc0 = jnp.where(sub >= LANES, 1.0, 0.0).astype(F32)
        m0 = []
        for c in range(n_chain):
            m0.append(jnp.concatenate(
                [jnp.full((1, tq), sink_ref[c * group + g] * LOG2E, F32) for g in range(group)], axis=1))
    else:
        acc0 = jnp.zeros((ACC_ROWS, m_cols), F32)
        m0 = [jnp.full((1, m_cols), NEG_INF, F32) for _ in range(n_chain)]

    def values_t(p, k0, width):
        ones = jnp.ones((ACC_ROWS - LANES, width), BF16)
        return jnp.concatenate([v_ref[p * LANES:(p + 1) * LANES, pl.ds(k0, width)], ones], axis=0)

    def softmax_pv(s_t, v_t, m_prev, acc_prev, s_max=None):
        if s_max is None:
            s_max = jnp.max(s_t, axis=0, keepdims=True)
        m_new = jnp.maximum(m_prev, s_max)
        alpha = jnp.exp2(m_prev - m_new)
        p_t = jnp.exp2(s_t - m_new).astype(BF16)
        return m_new, alpha * acc_prev + jnp.dot(v_t, p_t, preferred_element_type=F32)

    def write_tile(q_rows, acc_fin, m_fin):
        top = lax.broadcasted_iota(I32, (LANES, tq), 0) < HEAD_DIM
        for p in range(n_pairs):
            even, odd = acc_fin[2 * p], acc_fin[2 * p + 1]
            for g in range(group):
                cols = slice(g * tq, (g + 1) * tq)
                num = jnp.where(top, even[:LANES, cols], odd[:LANES, cols])
                den = jnp.where(top, even[LANES:LANES + 1, cols], odd[LANES:LANES + 1, cols])
                out_cols = slice((p * group + g) * LANES, (p * group + g + 1) * LANES)
                o_ref[q_rows, out_cols] = (num / den).T.astype(o_ref.dtype)
        if want_lse:
            assert group == 1 and n_chain <= LSE_ROWS
            rows = [m_fin[c] * LN2 + jnp.log(acc_fin[c][LANES:LANES + 1, :]) for c in range(n_chain)]
            lse_ref[:, q_rows] = jnp.concatenate(rows + [jnp.zeros((LSE_ROWS - n_chain, tq), F32)], axis=0)

    if band_r is None:
        qm_ref, acc_ref, s_even, s_odd = scratch
        n_blocks = n_keys // tk

        def start_tile(u):
            q_masked = masked_queries(tile_rows[u])
            for c in range(n_chain):
                qm_ref[c] = q_masked[c]
                acc_ref[u % 2, c] = acc0

        def scores(kt, c, s_ref):
            s_t = lax.dot_general(kt, qm_ref[c], _NT, preferred_element_type=F32)
            s_ref[c] = s_t
            return jnp.max(s_t, axis=0, keepdims=True)

        def step(kb_next, s_next, kb, s_ref, m_prev, s_max, acc):
            k0 = pl.multiple_of(kb * tk, tk)
            m_next, max_next = [], []
            for p in range(n_pairs):
                v_t = values_t(p, k0, tk)
                if kb_next is not None:
                    next0 = kb_next * tk if isinstance(kb_next, int) else pl.multiple_of(kb_next * tk, tk)
                    kt = k_ref[pl.ds(next0, tk), p * LANES:(p + 1) * LANES]
                for par in range(2):
                    c = 2 * p + par
                    if kb_next is not None:
                        max_next.append(scores(kt, c, s_next))
                    m_new, acc[c] = softmax_pv(s_ref[c], v_t, m_prev[c], acc[c], s_max[c])
                    m_next.append(m_new)
            return m_next, max_next

        start_tile(0)
        s_max = []
        for p in range(n_pairs):
            kt = k_ref[0:tk, p * LANES:(p + 1) * LANES]
            s_max += [scores(kt, 2 * p + par, s_even) for par in range(2)]
        for u, q_rows in enumerate(tile_rows):
            acc = acc_ref.at[u % 2]

            def body(j, carry, acc=acc):
                m, mx = list(carry[:n_chain]), list(carry[n_chain:])
                m, mx = step(2 * j + 1, s_odd, 2 * j, s_even, m, mx, acc)
                m, mx = step(2 * j + 2, s_even, 2 * j + 1, s_odd, m, mx, acc)
                return tuple(m + mx)

            carry = lax.fori_loop(0, n_blocks // 2 - 1, body, tuple(m0 + s_max))
            m_fin, s_max = step(n_blocks - 1, s_odd, n_blocks - 2, s_even,
                                list(carry[:n_chain]), list(carry[n_chain:]), acc)
            if u + 1 < len(tile_rows):
                start_tile(u + 1)
                m_fin, s_max = step(0, s_even, n_blocks - 1, s_odd, m_fin, s_max, acc)
            else:
                m_fin, _ = step(None, None, n_blocks - 1, s_odd, m_fin, s_max, acc)
            write_tile(q_rows, [acc[c] for c in range(n_chain)], m_fin)
    else:
        halo = -(-band_r // LANES) * LANES
        width = min(tq + 2 * halo, n_keys)
        ws = pl.multiple_of(jnp.clip(i * tq - halo, 0, n_keys - width), LANES)
        kpos = ws + lax.broadcasted_iota(I32, (width, 1), 0)
        col = lax.broadcasted_iota(I32, (1, tq), 1)
        qpos = i * tq + jnp.concatenate([col] * group, axis=1)
        mask = jnp.abs(kpos - qpos) <= band_r
        q_masked = masked_queries(tile_rows[0])
        s_all = []
        for p in range(n_pairs):
            kt = k_ref[pl.ds(ws, width), p * LANES:(p + 1) * LANES]
            s_all += [lax.dot_general(kt, q_masked[2 * p + par], _NT, preferred_element_type=F32)
                      for par in range(2)]
        m_fin, acc_fin = [], []
        for c in range(n_chain):
            m_new, acc_new = softmax_pv(jnp.where(mask, s_all[c], NEG_INF), values_t(c // 2, ws, width),
                                        m0[c], acc0)
            m_fin.append(m_new)
            acc_fin.append(acc_new)
        write_tile(tile_rows[0], acc_fin, m_fin)


def _attention(q, k, v_t, *, batch, n_keys, residues, group, n_pairs, band_r, sink=None, want_lse=False):
    dense = band_r is None
    sub_q = DENSE_TQ if dense else min(BAND_TQ, BAND_COLS // group, n_keys)
    n_sub = max(1, min(DENSE_SUBTILES if dense else BAND_SUBTILES, n_keys // sub_q))
    tq = sub_q * n_sub
    wq = n_pairs * group * LANES
    wk = n_pairs * LANES
    q3 = q.reshape(batch, n_keys, residues * wq)
    k3 = k.reshape(batch, n_keys, residues * wk)
    v4 = v_t.reshape(batch, wk, n_keys, residues).transpose(0, 3, 1, 2)

    in_specs = []
    args = []
    if sink is not None:
        in_specs.append(pl.BlockSpec(memory_space=pltpu.SMEM))
        args.append(sink)
    in_specs += [
        pl.BlockSpec((None, tq, wq), lambda b, r, i: (b, i, r)),
        pl.BlockSpec((None, n_keys, wk), lambda b, r, i: (b, 0, r)),
        pl.BlockSpec((None, None, wk, n_keys), lambda b, r, i: (b, r, 0, 0)),
    ]
    args += [q3, k3, v4]
    out_specs = [pl.BlockSpec((None, tq, wq), lambda b, r, i: (b, i, r))]
    out_shape = [jax.ShapeDtypeStruct((batch, n_keys, residues * wq), BF16)]
    if want_lse:
        out_specs.append(pl.BlockSpec((None, None, LSE_ROWS, tq), lambda b, r, i: (b, r, 0, i)))
        out_shape.append(jax.ShapeDtypeStruct((batch, residues, LSE_ROWS, n_keys), F32))
    m_cols = group * sub_q
    scratch = []
    if dense:
        scratch = [pltpu.VMEM((2 * n_pairs, m_cols, LANES), BF16),
                   pltpu.VMEM((2, 2 * n_pairs, ACC_ROWS, m_cols), F32)]
        scratch += [pltpu.VMEM((2 * n_pairs, DENSE_TK, m_cols), F32) for _ in range(2)]
    outs = pl.pallas_call(
        functools.partial(_attn_kernel, group=group, n_pairs=n_pairs, tq=tq, n_sub=n_sub, tk=DENSE_TK,
                          n_keys=n_keys, band_r=band_r, has_sink=sink is not None, want_lse=want_lse),
        grid=(batch, residues, n_keys // tq),
        in_specs=in_specs, out_specs=out_specs, out_shape=out_shape,
        scratch_shapes=scratch,
        compiler_params=_cparams(("parallel", "parallel", "arbitrary")),
        name="attention_dense" if dense else "attention_band",
    )(*args)
    result = [outs[0].reshape(batch * n_keys, residues * wq)]
    if want_lse:
        result.append(outs[1].transpose(2, 0, 3, 1).reshape(LSE_ROWS, batch * n_keys * residues))
    return result


def _outproj_kernel(*refs, merge_dils):
    refs = list(refs)
    n_merge = len(merge_dils)
    n_scratch = sum(d > 1 for d in merge_dils)
    if n_merge:
        o_parts = refs[:n_merge]
        lse_parts = refs[n_merge:2 * n_merge]
        expand_ref = refs[2 * n_merge]
        scratch = refs[len(refs) - n_scratch:]
        refs = refs[2 * n_merge + 1:len(refs) - n_scratch]
    o_rest, x_ref, w_ref, g_ref, wrh_ref, wrl_ref, xo_ref, h_ref, aff_ref = refs
    tm = x_ref.shape[0]
    if n_merge:
        ordered = []
        free_scratch = list(scratch)
        for o_ref, dil in zip(o_parts, merge_dils):
            if dil > 1:
                rows_ref = free_scratch.pop(0)
                wd = o_ref.shape[1] // dil
                for r in range(dil):
                    for half in range(wd // LANES):
                        col = r * wd + half * LANES
                        rows_ref[half, pl.ds(r, tm // dil, stride=dil), :] = o_ref[:, col:col + LANES].astype(F32)
                ordered.append(lambda rows, ref=rows_ref: jnp.concatenate(
                    [ref[half, rows, :] for half in range(ref.shape[0])], axis=1))
            else:
                ordered.append(lambda rows, ref=o_ref: ref[rows, :].astype(F32))
        o_parts = ordered
    halves = [slice(0, tm // 2), slice(tm // 2, tm)]
    accs = []
    for rows in halves:
        acc = x_ref[rows, :]
        k0 = 0
        if n_merge:
            lses = [r[:, rows] for r in lse_parts]
            m = functools.reduce(jnp.maximum, lses)
            ws = [jnp.exp(l - m) for l in lses]
            inv = 1.0 / sum(ws)
            num = 0.0
            for wgt, r in zip(ws, o_parts):
                wn = wgt * inv
                hi = wn.astype(BF16)
                split = jnp.concatenate([hi, (wn - hi.astype(F32)).astype(BF16)], axis=0)
                full = lax.dot_general(split, expand_ref[...], (((0,), (0,)), ((), ())),
                                       preferred_element_type=F32)
                num = num + full * r(rows)
            oa = num.astype(BF16)
            k0 = oa.shape[1]
            acc = acc + jnp.dot(oa, w_ref[:k0, :], preferred_element_type=F32)
        accs.append(acc + jnp.dot(o_rest[rows, :], w_ref[k0:, :], preferred_element_type=F32))
    for rows, acc in zip(halves, accs):
        xo_ref[rows, :] = acc
        ms = jnp.mean(acc * acc, axis=1, keepdims=True)
        h = (acc * lax.rsqrt(ms + EPS)) * g_ref[...]
        h_hi = h.astype(BF16)
        h_ref[rows, :] = h_hi
        h_lo = (h - h_hi.astype(F32)).astype(BF16)
        w_both = jnp.concatenate([wrh_ref[...], wrl_ref[...]], axis=0)
        both = lax.dot_general(w_both, h_hi, _NT, preferred_element_type=F32)
        logits = (both[:N_EXPERTS] + both[N_EXPERTS:]
                  + lax.dot_general(wrh_ref[...], h_lo, _NT, preferred_element_type=F32))
        mx = jnp.max(logits, axis=0, keepdims=True)
        e = jnp.exp(logits - mx)
        aff_ref[:, rows] = e / jnp.sum(e, axis=0, keepdims=True)


def _out_projection(o_merge, lse_merge, o_rest, x, w_out, norm_g, wr_hi, wr_lo):
    t_tokens = x.shape[0]
    tm = OUT_TM
    n_merge = len(o_merge)
    row = lambda i: (i, 0)
    fixed = lambda i: (0, 0)
    merge_args = list(o_merge) + list(lse_merge)
    merge_dils = tuple(t_tokens // o.shape[0] for o in o_merge)
    in_specs = ([pl.BlockSpec((tm // dil, o.shape[1]), row) for o, dil in zip(o_merge, merge_dils)]
                + [pl.BlockSpec((LSE_ROWS, tm), lambda i: (0, i)) for _ in lse_merge])
    if n_merge:
        merged_width = o_merge[0].shape[1] // merge_dils[0]
        head = np.arange(merged_width) // HEAD_DIM
        expand = (np.arange(2 * LSE_ROWS)[:, None] % LSE_ROWS == head[None, :]).astype(np.float32)
        merge_args.append(jnp.asarray(expand, dtype=BF16))
        in_specs.append(pl.BlockSpec(expand.shape, fixed))
    in_specs = (in_specs
                + [pl.BlockSpec((tm, o_rest.shape[1]), row),
                   pl.BlockSpec((tm, D_MODEL), row),
                   pl.BlockSpec(w_out.shape, fixed),
                   pl.BlockSpec((1, D_MODEL), fixed),
                   pl.BlockSpec(wr_hi.shape, fixed),
                   pl.BlockSpec(wr_lo.shape, fixed)])
    out_specs = [pl.BlockSpec((tm, D_MODEL), row),
                 pl.BlockSpec((tm, D_MODEL), row),
                 pl.BlockSpec((N_EXPERTS, tm), lambda i: (0, i))]
    out_shape = [jax.ShapeDtypeStruct((t_tokens, D_MODEL), F32),
                 jax.ShapeDtypeStruct((t_tokens, D_MODEL), BF16),
                 jax.ShapeDtypeStruct((N_EXPERTS, t_tokens), F32)]
    return pl.pallas_call(
        functools.partial(_outproj_kernel, merge_dils=merge_dils),
        grid=(t_tokens // tm,),
        in_specs=in_specs, out_specs=out_specs, out_shape=out_shape,
        scratch_shapes=[pltpu.VMEM((merged_width // LANES, tm, LANES), F32) for dil in merge_dils if dil > 1],
        compiler_params=_cparams(("parallel",)),
        name="out_projection",
    )(*merge_args, o_rest, x, w_out, norm_g, wr_hi, wr_lo)


def _select_kernel(aff_ref, ind_ref, tri_ref, sel_ref, off_ref, *, cap):
    aff = aff_ref[...]
    n_tok = aff.shape[1]
    bits = pltpu.bitcast(aff, I32)

    def count(pred):
        return jnp.sum(jnp.where(pred, 1.0, 0.0), axis=1, keepdims=True)

    def value_step(j, thr):
        cand = thr | lax.shift_left(jnp.int32(1), 30 - j)
        return jnp.where(count(bits >= cand) >= cap, cand, thr)

    thr = lax.fori_loop(0, 31, value_step, jnp.zeros((N_EXPERTS, 1), I32))
    gt = bits > thr
    eq = bits == thr
    need = cap - count(gt)
    idx = lax.broadcasted_iota(I32, aff.shape, 1)
    idx_bits = int(math.log2(n_tok))

    def index_step(j, bound):
        cand = bound | lax.shift_left(jnp.int32(1), idx_bits - j)
        return jnp.where(count(eq & (idx < cand)) <= need, cand, bound)

    bound = lax.fori_loop(0, idx_bits + 1, index_step, jnp.zeros((N_EXPERTS, 1), I32))
    sel = jnp.where(gt | (eq & (idx < bound)), 1.0, 0.0).astype(BF16)
    sel_ref[...] = sel
    counts = jnp.dot(sel, ind_ref[...], preferred_element_type=F32)
    offs = jnp.dot(counts.astype(BF16), tri_ref[...], preferred_element_type=F32)
    off_ref[...] = offs.astype(I32)


def _strict_upper(n):
    return jnp.asarray(np.triu(np.ones((n, n), np.float32), k=1), dtype=BF16)


def _select(aff_t, cap):
    n_tok = aff_t.shape[1]
    ind = np.zeros((n_tok, LANES), np.float32)
    ind[np.arange(n_tok), np.arange(n_tok) // MOE_TB] = 1.0
    return pl.pallas_call(
        functools.partial(_select_kernel, cap=cap),
        out_shape=[jax.ShapeDtypeStruct((N_EXPERTS, n_tok), BF16),
                   jax.ShapeDtypeStruct((N_EXPERTS, LANES), I32)],
        compiler_params=pltpu.CompilerParams(vmem_limit_bytes=VMEM_LIMIT),
        name="select",
    )(aff_t, jnp.asarray(ind, dtype=BF16), _strict_upper(LANES))


SLOT_ALIGN = 16
ALIGN_BITS = 4
CHUNK_BITS = 6
NOT_SELECTED = -1e6


class _BlockLayout:
    def __init__(self, off_ref, b):
        self.start = [off_ref[e, b] for e in range(N_EXPERTS)]
        self.end = [off_ref[e, b + 1] for e in range(N_EXPERTS)]
        def floor_tile(v):
            return lax.shift_left(lax.shift_right_logical(v, ALIGN_BITS), ALIGN_BITS)

        self.base = [floor_tile(s) for s in self.start]
        self.shift = [s - a for s, a in zip(self.start, self.base)]
        span = [en - a for en, a in zip(self.end, self.base)]
        self.n_chunks = functools.reduce(
            jnp.maximum, [lax.shift_right_logical(sp, CHUNK_BITS) for sp in span]) + 1
        tail = [floor_tile(sp) for sp in span]
        self.tail_chunk = [lax.shift_right_logical(t, CHUNK_BITS) for t in tail]
        self.tail_row = [pl.multiple_of(t & (MOE_W - 1), SLOT_ALIGN) for t in tail]

    def window_row(self, e, chunk):
        return pl.multiple_of(self.base[e] + chunk * MOE_W, SLOT_ALIGN)


def _slot_onehots(pos, weight, chunk):
    n_tok = pos.shape[1]
    slot = (lax.broadcasted_iota(I32, (MOE_W, n_tok), 0) + chunk * MOE_W).astype(F32)
    parts = []
    for e in range(N_EXPERTS):
        hit = pos[e:e + 1, :] == slot
        val = 1.0 if weight is None else weight[e:e + 1, :]
        parts.append(jnp.where(hit, val, 0.0).astype(BF16))
    return jnp.concatenate(parts, axis=0)


def _window_positions(sel, tri_ref, layout):
    rank = jnp.dot(sel, tri_ref[...], preferred_element_type=F32)
    shift = jnp.concatenate([jnp.full((1, 1), s, I32) for s in layout.shift], axis=0).astype(F32)
    return jnp.where(sel > 0, rank + shift, NOT_SELECTED)


def _dispatch_kernel(off_ref, sel_ref, h_ref, tri_ref, xe_ref, *scratch):
    nb = pl.num_programs(0) * MOE_BLOCKS_PER_STEP
    for part in range(MOE_BLOCKS_PER_STEP):
        tokens = slice(part * MOE_TB, (part + 1) * MOE_TB)
        _dispatch_block(pl.program_id(0) * MOE_BLOCKS_PER_STEP + part, nb, part % 2, off_ref,
                        sel_ref[:, tokens], h_ref[tokens, :], tri_ref, xe_ref, *scratch)


def _dispatch_block(b, nb, slot, off_ref, sel, h, tri_ref, xe_ref, stage_ref, extra_ref, tail_ref, zero_ref, sem,
                    sem_extra):
    layout = _BlockLayout(off_ref, b)
    pos = _window_positions(sel, tri_ref, layout)

    def rows_for(chunk):
        return jnp.dot(_slot_onehots(pos, None, chunk), h, preferred_element_type=F32)

    def copy(src, e, row0, s):
        return pltpu.make_async_copy(src.at[e], xe_ref.at[e, pl.ds(row0, MOE_W)], s)

    @pl.when(b == 0)
    def _():
        tail_ref[...] = jnp.zeros(tail_ref.shape, BF16)
        zero_ref[...] = jnp.zeros(zero_ref.shape, BF16)
        pad = zero_ref.shape[0]
        fills = [pltpu.make_async_copy(zero_ref, xe_ref.at[e, pl.ds(xe_ref.shape[1] - pad, pad)], sem_extra.at[0])
                 for e in range(N_EXPERTS)]
        for f in fills:
            f.start()
        for f in fills:
            f.wait()

    rows0 = rows_for(0)

    for e in range(N_EXPERTS):
        r0 = e * MOE_W
        first = rows0[r0:r0 + SLOT_ALIGN] + tail_ref[e].astype(F32)
        stage_ref[slot, e, :SLOT_ALIGN] = first.astype(BF16)
        stage_ref[slot, e, SLOT_ALIGN:] = rows0[r0 + SLOT_ALIGN:r0 + MOE_W].astype(BF16)
    for e in range(N_EXPERTS):
        tile = stage_ref[slot, e, pl.ds(layout.tail_row[e], SLOT_ALIGN), :]
        tail_ref[e] = jnp.where(layout.tail_chunk[e] == 0, tile, tail_ref[e])

    @pl.when(b > 0)
    def _():
        for e in range(N_EXPERTS):
            copy(stage_ref.at[1 - slot], e, 0, sem.at[1 - slot]).wait()

    for e in range(N_EXPERTS):
        copy(stage_ref.at[slot], e, layout.window_row(e, 0), sem.at[slot]).start()

    def overflow(chunk, carry):
        more = rows_for(chunk).astype(BF16)
        for e in range(N_EXPERTS):
            extra_ref[e] = more[e * MOE_W:(e + 1) * MOE_W]
        for e in range(N_EXPERTS):
            copy(extra_ref, e, layout.window_row(e, chunk), sem_extra.at[0]).start()
        for e in range(N_EXPERTS):
            @pl.when(layout.tail_chunk[e] == chunk)
            def _():
                tail_ref[e] = extra_ref[e, pl.ds(layout.tail_row[e], SLOT_ALIGN), :]
        for e in range(N_EXPERTS):
            copy(extra_ref, e, 0, sem_extra.at[0]).wait()
        return carry

    lax.fori_loop(1, layout.n_chunks, overflow, 0)

    @pl.when(b == nb - 1)
    def _():
        for e in range(N_EXPERTS):
            copy(stage_ref.at[slot], e, 0, sem.at[slot]).wait()


def _slot_rows(cap):
    pad = ((SLOT_ALIGN + MOE_TB) // MOE_W + 1) * MOE_W
    return cap + -(-pad // FFN_TM) * FFN_TM


def _dispatch(offs, sel, h, cap):
    n_tok = h.shape[0]
    step_tokens = MOE_BLOCKS_PER_STEP * MOE_TB
    rows = _slot_rows(cap)
    grid_spec = pltpu.PrefetchScalarGridSpec(
        num_scalar_prefetch=1,
        grid=(n_tok // step_tokens,),
        in_specs=[pl.BlockSpec((N_EXPERTS, step_tokens), lambda s, off: (0, s)),
                  pl.BlockSpec((step_tokens, D_MODEL), lambda s, off: (s, 0)),
                  pl.BlockSpec((MOE_TB, MOE_TB), lambda s, off: (0, 0))],
        out_specs=pl.BlockSpec(memory_space=pl.ANY),
        scratch_shapes=[pltpu.VMEM((2, N_EXPERTS, MOE_W, D_MODEL), BF16),
                        pltpu.VMEM((N_EXPERTS, MOE_W, D_MODEL), BF16),
                        pltpu.VMEM((N_EXPERTS, SLOT_ALIGN, D_MODEL), BF16),
                        pltpu.VMEM((rows - cap, D_MODEL), BF16),
                        pltpu.SemaphoreType.DMA((2,)),
                        pltpu.SemaphoreType.DMA((1,))],
    )
    return pl.pallas_call(
        _dispatch_kernel,
        grid_spec=grid_spec,
        out_shape=jax.ShapeDtypeStruct((N_EXPERTS, rows, D_MODEL), BF16),
        compiler_params=_cparams(("arbitrary",)),
        name="dispatch",
    )(offs, sel, h, _strict_upper(MOE_TB))


def _ffn_kernel(*refs, first_step, n_tiles):
    n_groups = len(n_tiles)
    xe_refs = refs[:n_groups]
    wg_ref, wu_ref, wd_ref = refs[n_groups:n_groups + 3]
    ye_refs = refs[n_groups + 3:2 * n_groups + 3]
    wg_bf, wu_bf, wd_bf = refs[2 * n_groups + 3:]
    j = pl.program_id(1)

    @pl.when(j == 0)
    def _():
        wg_bf[...] = wg_ref[...].astype(BF16)
        wu_bf[...] = wu_ref[...].astype(BF16)
        wd_bf[...] = wd_ref[...].astype(BF16)

    for g in range(n_groups):
        @pl.when((j >= first_step[g]) & (j < first_step[g] + n_tiles[g]))
        def _():
            x = xe_refs[g][...]
            gate = jnp.dot(x, wg_bf[...], preferred_element_type=F32)
            up = jnp.dot(x, wu_bf[...], preferred_element_type=F32)
            hid = (gate * jax.nn.sigmoid(gate) * up).astype(BF16)
            ye_refs[g][...] = jnp.dot(hid, wd_bf[...], preferred_element_type=F32).astype(BF16)


def _expert_ffn(xes, w_gate, w_up, w_down, layer, caps):
    n_groups = len(xes)
    n_tiles = [cap // FFN_TM for cap in caps]
    first_step = [sum(n_tiles[:g]) for g in range(n_groups)]

    def weight_index(e, j):
        return (layer, jnp.minimum(e + (j >= 1).astype(I32), N_EXPERTS - 1), 0, 0)

    wspec = lambda shape: pl.BlockSpec((None, None) + shape, weight_index)

    def tile_spec(g):
        return pl.BlockSpec((None, FFN_TM, D_MODEL),
                            lambda e, j: (e, jnp.clip(j - first_step[g], 0, n_tiles[g] - 1), 0))

    return pl.pallas_call(
        functools.partial(_ffn_kernel, first_step=tuple(first_step), n_tiles=tuple(n_tiles)),
        grid=(N_EXPERTS, sum(n_tiles)),
        in_specs=([tile_spec(g) for g in range(n_groups)]
                  + [wspec((D_MODEL, EXPERT_FF)), wspec((D_MODEL, EXPERT_FF)), wspec((EXPERT_FF, D_MODEL))]),
        out_specs=[tile_spec(g) for g in range(n_groups)],
        out_shape=[jax.ShapeDtypeStruct(xe.shape, BF16) for xe in xes],
        input_output_aliases={g: g for g in range(n_groups)},
        scratch_shapes=[pltpu.VMEM((D_MODEL, EXPERT_FF), BF16), pltpu.VMEM((D_MODEL, EXPERT_FF), BF16),
                        pltpu.VMEM((EXPERT_FF, D_MODEL), BF16)],
        compiler_params=_cparams(("arbitrary", "arbitrary")),
        name="expert_ffn",
    )(*xes, w_gate, w_up, w_down)


def _combine_kernel(off_ref, sel_ref, aff_ref, tri_ref, x_ref, ye_ref, out_ref, *scratch, rows):
    nb = pl.num_programs(0) * MOE_BLOCKS_PER_STEP
    for part in range(MOE_BLOCKS_PER_STEP):
        tokens = slice(part * MOE_TB, (part + 1) * MOE_TB)
        _combine_block(pl.program_id(0) * MOE_BLOCKS_PER_STEP + part, nb, part % 2, tokens, off_ref,
                       sel_ref[:, tokens], aff_ref[:, tokens], tri_ref, x_ref, ye_ref, out_ref, *scratch,
                       rows=rows)


def _combine_block(b, nb, slot, tokens, off_ref, sel, aff, tri_ref, x_ref, ye_ref, out_ref, win_ref, extra_ref,
                   sem, sem_extra, *, rows):
    def copy(dst, e, row0, s):
        return pltpu.make_async_copy(ye_ref.at[e, pl.ds(row0, MOE_W)], dst.at[e], s)

    def fetch(blk, dst_slot):
        ahead = _BlockLayout(off_ref, blk)
        for e in range(N_EXPERTS):
            copy(win_ref.at[dst_slot], e, ahead.window_row(e, 0), sem.at[dst_slot]).start()

    @pl.when(b == 0)
    def _():
        fetch(0, 0)

    @pl.when(b + 1 < nb)
    def _():
        fetch(b + 1, 1 - slot)

    layout = _BlockLayout(off_ref, b)
    pos = _window_positions(sel, tri_ref, layout)
    tn = (((0,), (0,)), ((), ()))

    def contribution(chunk, window):
        gates = _slot_onehots(pos, aff, chunk)
        vals = window.reshape(N_EXPERTS * MOE_W, window.shape[-1])
        return lax.dot_general(gates, vals, tn, preferred_element_type=F32)

    for e in range(N_EXPERTS):
        copy(win_ref.at[slot], e, 0, sem.at[slot]).wait()
    out_ref[tokens, :] = x_ref[tokens, :] + contribution(0, win_ref[slot])

    def overflow(chunk, carry):
        for e in range(N_EXPERTS):
            row0 = pl.multiple_of(jnp.minimum(layout.window_row(e, chunk), rows - MOE_W), SLOT_ALIGN)
            copy(extra_ref, e, row0, sem_extra.at[0]).start()
        for e in range(N_EXPERTS):
            copy(extra_ref, e, 0, sem_extra.at[0]).wait()
        out_ref[tokens, :] += contribution(chunk, extra_ref[...])
        return carry

    lax.fori_loop(1, layout.n_chunks, overflow, 0)


def _combine(offs, sel, aff_t, x, ye):
    n_tok = x.shape[0]
    step_tokens = MOE_BLOCKS_PER_STEP * MOE_TB
    rows = ye.shape[1]
    grid_spec = pltpu.PrefetchScalarGridSpec(
        num_scalar_prefetch=1,
        grid=(n_tok // step_tokens,),
        in_specs=[pl.BlockSpec((N_EXPERTS, step_tokens), lambda s, off: (0, s)),
                  pl.BlockSpec((N_EXPERTS, step_tokens), lambda s, off: (0, s)),
                  pl.BlockSpec((MOE_TB, MOE_TB), lambda s, off: (0, 0)),
                  pl.BlockSpec((step_tokens, D_MODEL), lambda s, off: (s, 0)),
                  pl.BlockSpec(memory_space=pl.ANY)],
        out_specs=pl.BlockSpec((step_tokens, D_MODEL), lambda s, off: (s, 0)),
        scratch_shapes=[pltpu.VMEM((2, N_EXPERTS, MOE_W, D_MODEL), BF16),
                        pltpu.VMEM((N_EXPERTS, MOE_W, D_MODEL), BF16),
                        pltpu.SemaphoreType.DMA((2,)),
                        pltpu.SemaphoreType.DMA((1,))],
    )
    return pl.pallas_call(
        functools.partial(_combine_kernel, rows=rows),
        grid_spec=grid_spec,
        out_shape=jax.ShapeDtypeStruct((n_tok, D_MODEL), F32),
        compiler_params=_cparams(("arbitrary",)),
        name="combine",
    )(offs, sel, aff_t, _strict_upper(MOE_TB), x, ye)


def _moe(routed, w_gate, w_up, w_down, layer):
    plans = []
    for x, h, aff_t in routed:
        cap = CAPACITY_FACTOR * x.shape[0] // N_EXPERTS
        sel, offs = _select(aff_t, cap)
        plans.append((cap, sel, offs, _dispatch(offs, sel, h, cap)))
    yes = _expert_ffn([p[3] for p in plans], w_gate, w_up, w_down, layer, [p[0] for p in plans])
    return [_combine(offs, sel, aff_t, x, ye)
            for (x, _, aff_t), (_, sel, offs, _), ye in zip(routed, plans, yes)]


def _row(v):
    return v.reshape(1, -1).astype(F32)


def _router_split(w_router):
    wt = w_router.T.astype(F32)
    hi = wt.astype(BF16)
    lo = (wt - hi.astype(F32)).astype(BF16)
    return hi, lo


def _prep_ab(norm_g, w_in, qn_a, kn_a, qn_b, kn_b, w_out, norm_ffn, w_router):
    scale = HEAD_DIM ** -0.5 * LOG2E
    perm_b = _pair_layout_perm(B_KV_HEADS, B_Q_HEADS // B_KV_HEADS)
    a3 = 3 * A_WIDTH
    bq = B_Q_HEADS * HEAD_DIM
    w_cols = np.concatenate([np.arange(a3), a3 + perm_b, np.arange(a3 + bq, w_in.shape[1])])
    gains = jnp.concatenate([
        jnp.repeat(qn_a, A_HEADS, axis=0).reshape(-1) * scale,
        jnp.repeat(kn_a, A_HEADS, axis=0).reshape(-1),
        jnp.ones((A_WIDTH,), F32),
        jnp.tile(qn_b, B_Q_HEADS) * scale,
        jnp.tile(kn_b, B_KV_HEADS),
        jnp.ones((B_KV_HEADS * HEAD_DIM,), F32)])
    a_out = A_HEADS * HEAD_DIM
    w_o = w_out[np.concatenate([np.arange(a_out), a_out + perm_b])].astype(BF16)
    return (_row(norm_g), w_in[:, w_cols].astype(BF16), _row(gains), w_o, _row(norm_ffn)) + _router_split(w_router)


def _layer_ab(x, batch, seq, tabs, prep):
    norm_g, w, gains, w_o, norm_ffn, wr_hi, wr_lo = prep
    tab_a, tab_b = tabs
    bq = B_Q_HEADS * HEAD_DIM
    n_a = A_GROUPS
    assert A_HEADS * HEAD_DIM == MXU_COLS
    dils = [dil for _, dil in A_PATTERNS]
    plan = ([("qk", g, 0, 0, ROT_DIM // 2, dils[g]) for g in range(n_a)]
            + [("qk", n_a + g, 0, 0, ROT_DIM // 2, dils[g]) for g in range(n_a)]
            + [("v", 2 * n_a + g, 0, 0, 0, 1) for g in range(n_a)]
            + [("qk", 3 * n_a, c * MXU_COLS, 1, HEAD_DIM // 4, 1) for c in range(bq // MXU_COLS)]
            + [("qk", 3 * n_a + 1, 0, 1, HEAD_DIM // 4, 1), ("v", 3 * n_a + 2, 0, 0, 0, 1)])
    assert B_KV_HEADS * HEAD_DIM == MXU_COLS
    a_widths = tuple(MXU_COLS if dil == 1 else ("dilated", dil) for dil in dils)
    outs = _projection(
        x, norm_g, w, gains, (tab_a, tab_b), plan,
        a_widths * 2 + (None,) * n_a + (bq, MXU_COLS, None), seq)
    qa, ka, va = outs[:n_a], outs[n_a:2 * n_a], outs[2 * n_a:3 * n_a]
    qb, kb, vb = outs[3 * n_a:]
    o_parts, lse_parts = [], []
    for gi, (window, dil) in enumerate(A_PATTERNS):
        o, lse = _attention(qa[gi], ka[gi], va[gi], batch=batch, n_keys=seq // dil, residues=dil, group=1,
                            n_pairs=A_HEADS // 2, band_r=window // (2 * dil), want_lse=True)
        o_parts.append(o)
        lse_parts.append(lse)
    (ob,) = _attention(qb, kb, vb, batch=batch, n_keys=seq, residues=1,
                       group=B_Q_HEADS // B_KV_HEADS, n_pairs=B_KV_HEADS // 2, band_r=None)
    return _out_projection(o_parts, lse_parts, ob, x, w_o, norm_ffn, wr_hi, wr_lo)


def _prep_c(norm_g, w_in, qn, kn, sink, w_out, norm_ffn, w_router):
    scale = HEAD_DIM ** -0.5 * LOG2E
    perm = _pair_layout_perm(C_KV_HEADS, C_Q_HEADS // C_KV_HEADS)
    cq = C_Q_HEADS * HEAD_DIM
    w_cols = np.concatenate([perm, np.arange(cq, w_in.shape[1])])
    gains = jnp.concatenate([jnp.tile(qn, C_Q_HEADS) * scale, jnp.tile(kn, C_KV_HEADS),
                             jnp.ones((C_KV_HEADS * HEAD_DIM,), F32)])
    return (_row(norm_g), w_in[:, w_cols].astype(BF16), _row(gains), sink.astype(F32),
            w_out[perm].astype(BF16), _row(norm_ffn)) + _router_split(w_router)


def _layer_c(x, batch, seq, tabs, prep):
    norm_g, w, gains, sink, w_o, norm_ffn, wr_hi, wr_lo = prep
    tab_a, _ = tabs
    cq = C_Q_HEADS * HEAD_DIM
    plan = ([("qk", 0, c * MXU_COLS, 0, ROT_DIM // 2, 1) for c in range(cq // MXU_COLS)]
            + [("qk", 1, 0, 0, ROT_DIM // 2, 1), ("v", 2, 0, 0, 0, 1)])
    assert C_KV_HEADS * HEAD_DIM == MXU_COLS
    q, k, v = _projection(x, norm_g, w, gains, (tab_a,), plan, (cq, MXU_COLS, None), seq)
    (o,) = _attention(q, k, v, batch=batch, n_keys=seq, residues=1,
                      group=C_Q_HEADS // C_KV_HEADS, n_pairs=C_KV_HEADS // 2, band_r=C_RADIUS, sink=sink)
    return _out_projection([], [], o, x, w_o, norm_ffn, wr_hi, wr_lo)


def _encode(xs, params):
    (norm_mix, norm_ffn, w_in_ab, qn_a, kn_a, qn_b, kn_b, w_out_ab, w_in_c, qn_c, kn_c, sink_c, w_out_c,
     w_router, w_gate, w_up, w_down) = params
    shapes = [x.shape[:2] for x in xs]
    tabs = [_rope_tables(seq) for _, seq in shapes]
    xts = [x.reshape(batch * seq, D_MODEL) for x, (batch, seq) in zip(xs, shapes)]
    for layer in range(norm_mix.shape[0]):
        j = layer // 2
        if layer % 2 == 0:
            prep = _prep_ab(norm_mix[layer], w_in_ab[j], qn_a[j], kn_a[j], qn_b[j], kn_b[j], w_out_ab[j],
                            norm_ffn[layer], w_router[layer])
            mixer = _layer_ab
        else:
            prep = _prep_c(norm_mix[layer], w_in_c[j], qn_c[j], kn_c[j], sink_c[j], w_out_c[j],
                           norm_ffn[layer], w_router[layer])
            mixer = _layer_c
        routed = [mixer(xt, batch, seq, tab, prep) for xt, (batch, seq), tab in zip(xts, shapes, tabs)]
        xts = _moe(routed, w_gate, w_up, w_down, layer)
    return tuple(xt.reshape(batch, seq, D_MODEL) for xt, (batch, seq) in zip(xts, shapes))


def kernel(x_prompt, x_sample, norm_mix, norm_ffn, w_in_ab, qn_a, kn_a, qn_b, kn_b, w_out_ab, w_in_c, qn_c,
           kn_c, sink_c, w_out_c, w_router, w_gate, w_up, w_down):
    params = (norm_mix, norm_ffn, w_in_ab, qn_a, kn_a, qn_b, kn_b, w_out_ab, w_in_c, qn_c, kn_c, sink_c,
              w_out_c, w_router, w_gate, w_up, w_down)
    return _encode((x_prompt, x_sample), params)
```

```python
import functools
import math

import jax
import jax.numpy as jnp
import numpy as np
from jax import lax
from jax.experimental import pallas as pl
from jax.experimental.pallas import tpu as pltpu

F32 = jnp.float32
BF16 = jnp.bfloat16
I32 = jnp.int32

D_MODEL = 1024
HEAD_DIM = 64
LANES = 128
MXU_COLS = 256
GRID_W = 64
ROT_DIM = HEAD_DIM // 4
ROPE_THETA = 500000.0
AXIAL_THETA = 10000.0
A_PATTERNS = ((128, 1), (512, 4), (2048, 16))
A_HEADS = 4
A_GROUPS = len(A_PATTERNS)
A_WIDTH = A_GROUPS * A_HEADS * HEAD_DIM
B_Q_HEADS = 12
B_KV_HEADS = 4
C_Q_HEADS = 16
C_KV_HEADS = 4
C_RADIUS = 128
N_EXPERTS = 16
EXPERT_FF = 1024
CAPACITY_FACTOR = 2
NEG_INF = -1e30
EPS = 1e-6

VMEM_LIMIT = 56 * 1024 * 1024

PROJ_TM = 512
OUT_TM = 1024
DENSE_TQ = 256
BAND_TQ = 256
BAND_COLS = 1024
BAND_SUBTILES = 4
DENSE_SUBTILES = 4
ACC_ROWS = LANES + 16
LSE_ROWS = 8
DENSE_TK = 512
MOE_TB = 256
MOE_W = 64
FFN_TM = 512


def _cparams(sem):
    return pltpu.CompilerParams(dimension_semantics=sem, vmem_limit_bytes=VMEM_LIMIT)


def _head_block_diag():
    idx = np.arange(MXU_COLS) // HEAD_DIM
    return jnp.asarray((idx[:, None] == idx[None, :]).astype(np.float32), dtype=BF16)


def _rope_tables(max_len):
    pos = jnp.arange(max_len, dtype=F32)
    j = np.arange(LANES) % HEAD_DIM

    def angles(p, dim, theta):
        exps = jnp.arange(0, dim, 2, dtype=F32) / dim
        inv = jnp.power(jnp.float32(theta), -exps)
        return p[:, None] * inv[None, :]

    half = ROT_DIM // 2
    ang = angles(pos, ROT_DIM, ROPE_THETA)
    cos, sin = jnp.cos(ang), jnp.sin(ang)
    fa = np.where(j < half, j, np.where(j < ROT_DIM, j - half, 0))
    cos_l, sin_l = cos[:, fa], sin[:, fa]
    lo = jnp.asarray(j < half)[None, :]
    hi = jnp.asarray((j >= half) & (j < ROT_DIM))[None, :]
    tab_a = jnp.stack([jnp.where(lo | hi, cos_l, 1.0),
                       jnp.where(lo, -sin_l, 0.0),
                       jnp.where(hi, sin_l, 0.0)])
    hb = HEAD_DIM // 2
    qb = hb // 2
    t = jnp.arange(max_len)
    ang_r = angles((t // GRID_W).astype(F32), hb, AXIAL_THETA)
    ang_c = angles((t % GRID_W).astype(F32), hb, AXIAL_THETA)
    fb = j % qb
    is_col = jnp.asarray(j >= hb)[None, :]
    ang_l = jnp.where(is_col, ang_c[:, fb], ang_r[:, fb])
    cos_b, sin_b = jnp.cos(ang_l), jnp.sin(ang_l)
    first = jnp.asarray((j % hb) < qb)[None, :]
    tab_b = jnp.stack([cos_b, jnp.where(first, -sin_b, 0.0), jnp.where(first, 0.0, sin_b)])
    return tab_a.astype(F32), tab_b.astype(F32)


def _pair_layout_perm(n_kv, group):
    cols = []
    for p in range(n_kv // 2):
        for g in range(group):
            for par in range(2):
                h = (2 * p + par) * group + g
                cols.extend(range(h * HEAD_DIM, (h + 1) * HEAD_DIM))
    return np.asarray(cols, dtype=np.int32)


def _proj_kernel(x_ref, g_ref, w_ref, s_ref, gain_ref, *rest, plan, n_tabs):
    tab_refs = rest[:n_tabs]
    out_refs = rest[n_tabs:-1]
    rows_ref = rest[-1]
    x = x_ref[...]
    tm = x.shape[0]
    ms = jnp.mean(x * x, axis=1, keepdims=True)
    xn = ((x * lax.rsqrt(ms + EPS)) * g_ref[...]).astype(BF16)

    def project(c):
        return jnp.dot(xn, w_ref[:, c * MXU_COLS:(c + 1) * MXU_COLS], preferred_element_type=F32)

    ahead = project(0)
    for c, (kind, out_i, out_col, tab_i, shift, dil) in enumerate(plan):
        acc = ahead
        if c + 1 < len(plan):
            ahead = project(c + 1)
        o_ref = out_refs[out_i]
        if kind == "v":
            o_ref[...] = acc.T.astype(BF16)
            continue
        ss = jnp.dot((acc * acc).astype(BF16), s_ref[...], preferred_element_type=F32)
        y = (acc * lax.rsqrt(ss * (1.0 / HEAD_DIM) + EPS)) * gain_ref[:, c * MXU_COLS:(c + 1) * MXU_COLS]
        tab = tab_refs[tab_i]
        t0 = jnp.concatenate([tab[0], tab[0]], axis=1)
        t1 = jnp.concatenate([tab[1], tab[1]], axis=1)
        t2 = jnp.concatenate([tab[2], tab[2]], axis=1)
        y = y * t0 + pltpu.roll(y, MXU_COLS - shift, 1) * t1 + pltpu.roll(y, shift, 1) * t2
        if dil == 1:
            o_ref[:, out_col:out_col + MXU_COLS] = y.astype(BF16)
        else:
            for half in range(MXU_COLS // LANES):
                rows_ref[half] = y[:, half * LANES:(half + 1) * LANES]
            for r in range(dil):
                for half in range(MXU_COLS // LANES):
                    piece = rows_ref[half, pl.ds(r, tm // dil, stride=dil), :]
                    col = r * MXU_COLS + half * LANES
                    o_ref[:, col:col + LANES] = piece.astype(BF16)


def _projection(x, norm_g, w, gains, tabs, plan, out_widths, seq_len):
    t_tokens = x.shape[0]
    tm = PROJ_TM
    n_in = w.shape[1]
    blocks_per_seq = seq_len // tm
    in_specs = [
        pl.BlockSpec((tm, D_MODEL), lambda i: (i, 0)),
        pl.BlockSpec((1, D_MODEL), lambda i: (0, 0)),
        pl.BlockSpec((D_MODEL, n_in), lambda i: (0, 0)),
        pl.BlockSpec((MXU_COLS, MXU_COLS), lambda i: (0, 0)),
        pl.BlockSpec((1, n_in), lambda i: (0, 0)),
    ] + [pl.BlockSpec((3, tm, LANES), lambda i: (0, i % blocks_per_seq, 0)) for _ in tabs]
    out_specs, out_shape = [], []
    for wd in out_widths:
        if wd is None:
            out_specs.append(pl.BlockSpec((None, MXU_COLS, tm),
                                          lambda i: (i // blocks_per_seq, 0, i % blocks_per_seq)))
            out_shape.append(jax.ShapeDtypeStruct((t_tokens // seq_len, MXU_COLS, seq_len), BF16))
        elif isinstance(wd, tuple):
            dil = wd[1]
            out_specs.append(pl.BlockSpec((tm // dil, dil * MXU_COLS), lambda i: (i, 0)))
            out_shape.append(jax.ShapeDtypeStruct((t_tokens // dil, dil * MXU_COLS), BF16))
        else:
            out_specs.append(pl.BlockSpec((tm, wd), lambda i: (i, 0)))
            out_shape.append(jax.ShapeDtypeStruct((t_tokens, wd), BF16))
    return pl.pallas_call(
        functools.partial(_proj_kernel, plan=tuple(plan), n_tabs=len(tabs)),
        grid=(t_tokens // tm,),
        in_specs=in_specs, out_specs=out_specs, out_shape=out_shape,
        scratch_shapes=[pltpu.VMEM((MXU_COLS // LANES, tm, LANES), F32)],
        compiler_params=_cparams(("parallel",)),
        name="projection",
    )(x, norm_g, w, _head_block_diag(), gains, *tabs)


LOG2E = math.log2(math.e)
LN2 = math.log(2.0)
_NT = (((1,), (1,)), ((), ()))


def _attn_kernel(*refs, group, n_pairs, tq, n_sub, tk, n_keys, band_r, has_sink, want_lse):
    refs = list(refs)
    sink_ref = refs.pop(0) if has_sink else None
    q_ref, k_ref, v_ref, o_ref = refs[:4]
    refs = refs[4:]
    lse_ref = refs.pop(0) if want_lse else None
    scratch = refs
    n_chain = 2 * n_pairs
    lane = lax.broadcasted_iota(I32, (1, LANES), 1)
    low = lane < HEAD_DIM
    half_mask = [jnp.where(low, 1.0, 0.0).astype(BF16), jnp.where(low, 0.0, 1.0).astype(BF16)]
    sub_q = tq // n_sub
    tile_rows = [slice(u * sub_q, (u + 1) * sub_q) for u in range(n_sub)]
    common = dict(group=group, n_pairs=n_pairs, tq=sub_q, tk=tk, n_keys=n_keys, band_r=band_r)
    if band_r is None:
        _attn_tiles(sink_ref, q_ref, k_ref, v_ref, o_ref, lse_ref, scratch, half_mask, tile_rows, None, **common)
    else:
        for u in range(n_sub):
            _attn_tiles(sink_ref, q_ref, k_ref, v_ref, o_ref, lse_ref, scratch, half_mask, tile_rows[u:u + 1],
                        pl.program_id(2) * n_sub + u, **common)


def _attn_tiles(sink_ref, q_ref, k_ref, v_ref, o_ref, lse_ref, scratch, half_mask, tile_rows, i, *,
                group, n_pairs, tq, tk, n_keys, band_r):
    has_sink = sink_ref is not None
    want_lse = lse_ref is not None
    n_chain = 2 * n_pairs
    m_cols = group * tq

    def masked_queries(q_rows):
        q_masked = []
        for p in range(n_pairs):
            q_p = jnp.concatenate(
                [q_ref[q_rows, (p * group + g) * LANES:(p * group + g + 1) * LANES] for g in range(group)],
                axis=0)
            q_masked += [q_p * half_mask[par] for par in range(2)]
        return q_masked

    sub = lax.broadcasted_iota(I32, (ACC_ROWS, m_cols), 0)
    if has_sink:
        acc0 = jnp.where(sub >= LANES, 1.0, 0.0).astype(F32)
        m0 = []
        for c in range(n_chain):
            m0.append(jnp.concatenate(
                [jnp.full((1, tq), sink_ref[c * group + g] * LOG2E, F32) for g in range(group)], axis=1))
    else:
        acc0 = jnp.zeros((ACC_ROWS, m_cols), F32)
        m0 = [jnp.full((1, m_cols), NEG_INF, F32) for _ in range(n_chain)]

    def values_t(p, k0, width):
        ones = jnp.ones((ACC_ROWS - LANES, width), BF16)
        return jnp.concatenate([v_ref[p * LANES:(p + 1) * LANES, pl.ds(k0, width)], ones], axis=0)

    def softmax_pv(s_t, v_t, m_prev, acc_prev, s_max=None):
        if s_max is None:
            s_max = jnp.max(s_t, axis=0, keepdims=True)
        m_new = jnp.maximum(m_prev, s_max)
        alpha = jnp.exp2(m_prev - m_new)
        p_t = jnp.exp2(s_t - m_new).astype(BF16)
        return m_new, alpha * acc_prev + jnp.dot(v_t, p_t, preferred_element_type=F32)

    def write_tile(q_rows, acc_fin, m_fin):
        top = lax.broadcasted_iota(I32, (LANES, tq), 0) < HEAD_DIM
        for p in range(n_pairs):
            even, odd = acc_fin[2 * p], acc_fin[2 * p + 1]
            for g in range(group):
                cols = slice(g * tq, (g + 1) * tq)
                num = jnp.where(top, even[:LANES, cols], odd[:LANES, cols])
                den = jnp.where(top, even[LANES:LANES + 1, cols], odd[LANES:LANES + 1, cols])
                out_cols = slice((p * group + g) * LANES, (p * group + g + 1) * LANES)
                o_ref[q_rows, out_cols] = (num / den).T.astype(o_ref.dtype)
        if want_lse:
            assert group == 1 and n_chain <= LSE_ROWS
            rows = [m_fin[c] * LN2 + jnp.log(acc_fin[c][LANES:LANES + 1, :]) for c in range(n_chain)]
            lse_ref[:, q_rows] = jnp.concatenate(rows + [jnp.zeros((LSE_ROWS - n_chain, tq), F32)], axis=0)

    if band_r is None:
        qm_ref, acc_ref, s_even, s_odd = scratch
        n_blocks = n_keys // tk

        def start_tile(u):
            q_masked = masked_queries(tile_rows[u])
            for c in range(n_chain):
                qm_ref[c] = q_masked[c]
                acc_ref[u % 2, c] = acc0

        def scores(kt, c, s_ref):
            s_t = lax.dot_general(kt, qm_ref[c], _NT, preferred_element_type=F32)
            s_ref[c] = s_t
            return jnp.max(s_t, axis=0, keepdims=True)

        def step(kb_next, s_next, kb, s_ref, m_prev, s_max, acc):
            k0 = pl.multiple_of(kb * tk, tk)
            m_next, max_next = [], []
            for p in range(n_pairs):
                v_t = values_t(p, k0, tk)
                if kb_next is not None:
                    next0 = kb_next * tk if isinstance(kb_next, int) else pl.multiple_of(kb_next * tk, tk)
                    kt = k_ref[pl.ds(next0, tk), p * LANES:(p + 1) * LANES]
                for par in range(2):
                    c = 2 * p + par
                    if kb_next is not None:
                        max_next.append(scores(kt, c, s_next))
                    m_new, acc[c] = softmax_pv(s_ref[c], v_t, m_prev[c], acc[c], s_max[c])
                    m_next.append(m_new)
            return m_next, max_next

        start_tile(0)
        s_max = []
        for p in range(n_pairs):
            kt = k_ref[0:tk, p * LANES:(p + 1) * LANES]
            s_max += [scores(kt, 2 * p + par, s_even) for par in range(2)]
        for u, q_rows in enumerate(tile_rows):
            acc = acc_ref.at[u % 2]

            def body(j, carry, acc=acc):
                m, mx = list(carry[:n_chain]), list(carry[n_chain:])
                m, mx = step(2 * j + 1, s_odd, 2 * j, s_even, m, mx, acc)
                m, mx = step(2 * j + 2, s_even, 2 * j + 1, s_odd, m, mx, acc)
                return tuple(m + mx)

            carry = lax.fori_loop(0, n_blocks // 2 - 1, body, tuple(m0 + s_max))
            m_fin, s_max = step(n_blocks - 1, s_odd, n_blocks - 2, s_even,
                                list(carry[:n_chain]), list(carry[n_chain:]), acc)
            if u + 1 < len(tile_rows):
                start_tile(u + 1)
                m_fin, s_max = step(0, s_even, n_blocks - 1, s_odd, m_fin, s_max, acc)
            else:
                m_fin, _ = step(None, None, n_blocks - 1, s_odd, m_fin, s_max, acc)
            write_tile(q_rows, [acc[c] for c in range(n_chain)], m_fin)
    else:
        halo = -(-band_r // LANES) * LANES
        width = min(tq + 2 * halo, n_keys)
        ws = pl.multiple_of(jnp.clip(i * tq - halo, 0, n_keys - width), LANES)
        kpos = ws + lax.broadcasted_iota(I32, (width, 1), 0)
        col = lax.broadcasted_iota(I32, (1, tq), 1)
        qpos = i * tq + jnp.concatenate([col] * group, axis=1)
        mask = jnp.abs(kpos - qpos) <= band_r
        q_masked = masked_queries(tile_rows[0])
        s_all = []
        for p in range(n_pairs):
            kt = k_ref[pl.ds(ws, width), p * LANES:(p + 1) * LANES]
            s_all += [lax.dot_general(kt, q_masked[2 * p + par], _NT, preferred_element_type=F32)
                      for par in range(2)]
        m_fin, acc_fin = [], []
        for c in range(n_chain):
            m_new, acc_new = softmax_pv(jnp.where(mask, s_all[c], NEG_INF), values_t(c // 2, ws, width),
                                        m0[c], acc0)
            m_fin.append(m_new)
            acc_fin.append(acc_new)
        write_tile(tile_rows[0], acc_fin, m_fin)


def _attention(q, k, v_t, *, batch, n_keys, residues, group, n_pairs, band_r, sink=None, want_lse=False):
    dense = band_r is None
    sub_q = DENSE_TQ if dense else min(BAND_TQ, BAND_COLS // group, n_keys)
    n_sub = max(1, min(DENSE_SUBTILES if dense else BAND_SUBTILES, n_keys // sub_q))
    tq = sub_q * n_sub
    wq = n_pairs * group * LANES
    wk = n_pairs * LANES
    q3 = q.reshape(batch, n_keys, residues * wq)
    k3 = k.reshape(batch, n_keys, residues * wk)
    v4 = v_t.reshape(batch, wk, n_keys, residues).transpose(0, 3, 1, 2)

    in_specs = []
    args = []
    if sink is not None:
        in_specs.append(pl.BlockSpec(memory_space=pltpu.SMEM))
        args.append(sink)
    in_specs += [
        pl.BlockSpec((None, tq, wq), lambda b, r, i: (b, i, r)),
        pl.BlockSpec((None, n_keys, wk), lambda b, r, i: (b, 0, r)),
        pl.BlockSpec((None, None, wk, n_keys), lambda b, r, i: (b, r, 0, 0)),
    ]
    args += [q3, k3, v4]
    out_specs = [pl.BlockSpec((None, tq, wq), lambda b, r, i: (b, i, r))]
    out_shape = [jax.ShapeDtypeStruct((batch, n_keys, residues * wq), BF16)]
    if want_lse:
        out_specs.append(pl.BlockSpec((None, None, LSE_ROWS, tq), lambda b, r, i: (b, r, 0, i)))
        out_shape.append(jax.ShapeDtypeStruct((batch, residues, LSE_ROWS, n_keys), F32))
    m_cols = group * sub_q
    scratch = []
    if dense:
        scratch = [pltpu.VMEM((2 * n_pairs, m_cols, LANES), BF16),
                   pltpu.VMEM((2, 2 * n_pairs, ACC_ROWS, m_cols), F32)]
        scratch += [pltpu.VMEM((2 * n_pairs, DENSE_TK, m_cols), F32) for _ in range(2)]
    outs = pl.pallas_call(
        functools.partial(_attn_kernel, group=group, n_pairs=n_pairs, tq=tq, n_sub=n_sub, tk=DENSE_TK,
                          n_keys=n_keys, band_r=band_r, has_sink=sink is not None, want_lse=want_lse),
        grid=(batch, residues, n_keys // tq),
        in_specs=in_specs, out_specs=out_specs, out_shape=out_shape,
        scratch_shapes=scratch,
        compiler_params=_cparams(("parallel", "parallel", "arbitrary")),
        name="attention_dense" if dense else "attention_band",
    )(*args)
    result = [outs[0].reshape(batch * n_keys, residues * wq)]
    if want_lse:
        result.append(outs[1].transpose(2, 0, 3, 1).reshape(LSE_ROWS, batch * n_keys * residues))
    return result


def _outproj_kernel(*refs, merge_dils):
    refs = list(refs)
    n_merge = len(merge_dils)
    n_scratch = sum(d > 1 for d in merge_dils)
    if n_merge:
        o_parts = refs[:n_merge]
        lse_parts = refs[n_merge:2 * n_merge]
        expand_ref = refs[2 * n_merge]
        scratch = refs[len(refs) - n_scratch:]
        refs = refs[2 * n_merge + 1:len(refs) - n_scratch]
    o_rest, x_ref, w_ref, g_ref, wrh_ref, wrl_ref, xo_ref, h_ref, aff_ref = refs
    tm = x_ref.shape[0]
    if n_merge:
        ordered = []
        free_scratch = list(scratch)
        for o_ref, dil in zip(o_parts, merge_dils):
            if dil > 1:
                rows_ref = free_scratch.pop(0)
                wd = o_ref.shape[1] // dil
                for r in range(dil):
                    for half in range(wd // LANES):
                        col = r * wd + half * LANES
                        rows_ref[half, pl.ds(r, tm // dil, stride=dil), :] = o_ref[:, col:col + LANES].astype(F32)
                ordered.append(lambda rows, ref=rows_ref: jnp.concatenate(
                    [ref[half, rows, :] for half in range(ref.shape[0])], axis=1))
            else:
                ordered.append(lambda rows, ref=o_ref: ref[rows, :].astype(F32))
        o_parts = ordered
    halves = [slice(0, tm // 2), slice(tm // 2, tm)]
    accs = []
    for rows in halves:
        acc = x_ref[rows, :]
        k0 = 0
        if n_merge:
            lses = [r[:, rows] for r in lse_parts]
            m = functools.reduce(jnp.maximum, lses)
            ws = [jnp.exp(l - m) for l in lses]
            inv = 1.0 / sum(ws)
            num = 0.0
            for wgt, r in zip(ws, o_parts):
                wn = wgt * inv
                hi = wn.astype(BF16)
                split = jnp.concatenate([hi, (wn - hi.astype(F32)).astype(BF16)], axis=0)
                full = lax.dot_general(split, expand_ref[...], (((0,), (0,)), ((), ())),
                                       preferred_element_type=F32)
                num = num + full * r(rows)
            oa = num.astype(BF16)
            k0 = oa.shape[1]
            acc = acc + jnp.dot(oa, w_ref[:k0, :], preferred_element_type=F32)
        accs.append(acc + jnp.dot(o_rest[rows, :], w_ref[k0:, :], preferred_element_type=F32))
    for rows, acc in zip(halves, accs):
        xo_ref[rows, :] = acc
        ms = jnp.mean(acc * acc, axis=1, keepdims=True)
        h = (acc * lax.rsqrt(ms + EPS)) * g_ref[...]
        h_hi = h.astype(BF16)
        h_ref[rows, :] = h_hi
        h_lo = (h - h_hi.astype(F32)).astype(BF16)
        w_both = jnp.concatenate([wrh_ref[...], wrl_ref[...]], axis=0)
        both = lax.dot_general(w_both, h_hi, _NT, preferred_element_type=F32)
        logits = (both[:N_EXPERTS] + both[N_EXPERTS:]
                  + lax.dot_general(wrh_ref[...], h_lo, _NT, preferred_element_type=F32))
        mx = jnp.max(logits, axis=0, keepdims=True)
        e = jnp.exp(logits - mx)
        aff_ref[:, rows] = e / jnp.sum(e, axis=0, keepdims=True)


def _out_projection(o_merge, lse_merge, o_rest, x, w_out, norm_g, wr_hi, wr_lo):
    t_tokens = x.shape[0]
    tm = OUT_TM
    n_merge = len(o_merge)
    row = lambda i: (i, 0)
    fixed = lambda i: (0, 0)
    merge_args = list(o_merge) + list(lse_merge)
    merge_dils = tuple(t_tokens // o.shape[0] for o in o_merge)
    in_specs = ([pl.BlockSpec((tm // dil, o.shape[1]), row) for o, dil in zip(o_merge, merge_dils)]
                + [pl.BlockSpec((LSE_ROWS, tm), lambda i: (0, i)) for _ in lse_merge])
    if n_merge:
        merged_width = o_merge[0].shape[1] // merge_dils[0]
        head = np.arange(merged_width) // HEAD_DIM
        expand = (np.arange(2 * LSE_ROWS)[:, None] % LSE_ROWS == head[None, :]).astype(np.float32)
        merge_args.append(jnp.asarray(expand, dtype=BF16))
        in_specs.append(pl.BlockSpec(expand.shape, fixed))
    in_specs = (in_specs
                + [pl.BlockSpec((tm, o_rest.shape[1]), row),
                   pl.BlockSpec((tm, D_MODEL), row),
                   pl.BlockSpec(w_out.shape, fixed),
                   pl.BlockSpec((1, D_MODEL), fixed),
                   pl.BlockSpec(wr_hi.shape, fixed),
                   pl.BlockSpec(wr_lo.shape, fixed)])
    out_specs = [pl.BlockSpec((tm, D_MODEL), row),
                 pl.BlockSpec((tm, D_MODEL), row),
                 pl.BlockSpec((N_EXPERTS, tm), lambda i: (0, i))]
    out_shape = [jax.ShapeDtypeStruct((t_tokens, D_MODEL), F32),
                 jax.ShapeDtypeStruct((t_tokens, D_MODEL), BF16),
                 jax.ShapeDtypeStruct((N_EXPERTS, t_tokens), F32)]
    return pl.pallas_call(
        functools.partial(_outproj_kernel, merge_dils=merge_dils),
        grid=(t_tokens // tm,),
        in_specs=in_specs, out_specs=out_specs, out_shape=out_shape,
        scratch_shapes=[pltpu.VMEM((merged_width // LANES, tm, LANES), F32) for dil in merge_dils if dil > 1],
        compiler_params=_cparams(("parallel",)),
        name="out_projection",
    )(*merge_args, o_rest, x, w_out, norm_g, wr_hi, wr_lo)


def _select_kernel(aff_ref, ind_ref, tri_ref, sel_ref, off_ref, *, cap):
    aff = aff_ref[...]
    n_tok = aff.shape[1]
    bits = pltpu.bitcast(aff, I32)

    def count(pred):
        return jnp.sum(jnp.where(pred, 1.0, 0.0), axis=1, keepdims=True)

    def value_step(j, thr):
        cand = thr | lax.shift_left(jnp.int32(1), 30 - j)
        return jnp.where(count(bits >= cand) >= cap, cand, thr)

    thr = lax.fori_loop(0, 31, value_step, jnp.zeros((N_EXPERTS, 1), I32))
    gt = bits > thr
    eq = bits == thr
    need = cap - count(gt)
    idx = lax.broadcasted_iota(I32, aff.shape, 1)
    idx_bits = int(math.log2(n_tok))

    def index_step(j, bound):
        cand = bound | lax.shift_left(jnp.int32(1), idx_bits - j)
        return jnp.where(count(eq & (idx < cand)) <= need, cand, bound)

    bound = lax.fori_loop(0, idx_bits + 1, index_step, jnp.zeros((N_EXPERTS, 1), I32))
    sel = jnp.where(gt | (eq & (idx < bound)), 1.0, 0.0).astype(BF16)
    sel_ref[...] = sel
    counts = jnp.dot(sel, ind_ref[...], preferred_element_type=F32)
    offs = jnp.dot(counts.astype(BF16), tri_ref[...], preferred_element_type=F32)
    off_ref[...] = offs.astype(I32)


def _strict_upper(n):
    return jnp.asarray(np.triu(np.ones((n, n), np.float32), k=1), dtype=BF16)


def _select(aff_t, cap):
    n_tok = aff_t.shape[1]
    ind = np.zeros((n_tok, LANES), np.float32)
    ind[np.arange(n_tok), np.arange(n_tok) // MOE_TB] = 1.0
    return pl.pallas_call(
        functools.partial(_select_kernel, cap=cap),
        out_shape=[jax.ShapeDtypeStruct((N_EXPERTS, n_tok), BF16),
                   jax.ShapeDtypeStruct((N_EXPERTS, LANES), I32)],
        compiler_params=pltpu.CompilerParams(vmem_limit_bytes=VMEM_LIMIT),
        name="select",
    )(aff_t, jnp.asarray(ind, dtype=BF16), _strict_upper(LANES))


SLOT_ALIGN = 16
ALIGN_BITS = 4
CHUNK_BITS = 6
NOT_SELECTED = -1e6


class _BlockLayout:
    def __init__(self, off_ref, b):
        self.start = [off_ref[e, b] for e in range(N_EXPERTS)]
        self.end = [off_ref[e, b + 1] for e in range(N_EXPERTS)]
        def floor_tile(v):
            return lax.shift_left(lax.shift_right_logical(v, ALIGN_BITS), ALIGN_BITS)

        self.base = [floor_tile(s) for s in self.start]
        self.shift = [s - a for s, a in zip(self.start, self.base)]
        span = [en - a for en, a in zip(self.end, self.base)]
        self.n_chunks = functools.reduce(
            jnp.maximum, [lax.shift_right_logical(sp, CHUNK_BITS) for sp in span]) + 1
        tail = [floor_tile(sp) for sp in span]
        self.tail_chunk = [lax.shift_right_logical(t, CHUNK_BITS) for t in tail]
        self.tail_row = [pl.multiple_of(t & (MOE_W - 1), SLOT_ALIGN) for t in tail]

    def window_row(self, e, chunk):
        return pl.multiple_of(self.base[e] + chunk * MOE_W, SLOT_ALIGN)


def _slot_onehots(pos, weight, chunk):
    n_tok = pos.shape[1]
    slot = (lax.broadcasted_iota(I32, (MOE_W, n_tok), 0) + chunk * MOE_W).astype(F32)
    parts = []
    for e in range(N_EXPERTS):
        hit = pos[e:e + 1, :] == slot
        val = 1.0 if weight is None else weight[e:e + 1, :]
        parts.append(jnp.where(hit, val, 0.0).astype(BF16))
    return jnp.concatenate(parts, axis=0)


def _window_positions(sel_ref, tri_ref, layout):
    sel = sel_ref[...]
    rank = jnp.dot(sel, tri_ref[...], preferred_element_type=F32)
    shift = jnp.concatenate([jnp.full((1, 1), s, I32) for s in layout.shift], axis=0).astype(F32)
    return jnp.where(sel > 0, rank + shift, NOT_SELECTED)


def _dispatch_kernel(off_ref, sel_ref, h_ref, tri_ref, xe_ref, stage_ref, extra_ref, tail_ref, zero_ref, sem,
                     sem_extra):
    b = pl.program_id(0)
    nb = pl.num_programs(0)
    slot = b % 2
    layout = _BlockLayout(off_ref, b)
    pos = _window_positions(sel_ref, tri_ref, layout)
    h = h_ref[...]

    def rows_for(chunk):
        return jnp.dot(_slot_onehots(pos, None, chunk), h, preferred_element_type=F32)

    def copy(src, e, row0, s):
        return pltpu.make_async_copy(src.at[e], xe_ref.at[e, pl.ds(row0, MOE_W)], s)

    @pl.when(b == 0)
    def _():
        tail_ref[...] = jnp.zeros(tail_ref.shape, BF16)
        zero_ref[...] = jnp.zeros(zero_ref.shape, BF16)
        pad = zero_ref.shape[0]
        fills = [pltpu.make_async_copy(zero_ref, xe_ref.at[e, pl.ds(xe_ref.shape[1] - pad, pad)], sem_extra.at[0])
                 for e in range(N_EXPERTS)]
        for f in fills:
            f.start()
        for f in fills:
            f.wait()

    rows0 = rows_for(0)

    for e in range(N_EXPERTS):
        r0 = e * MOE_W
        first = rows0[r0:r0 + SLOT_ALIGN] + tail_ref[e].astype(F32)
        stage_ref[slot, e, :SLOT_ALIGN] = first.astype(BF16)
        stage_ref[slot, e, SLOT_ALIGN:] = rows0[r0 + SLOT_ALIGN:r0 + MOE_W].astype(BF16)
    for e in range(N_EXPERTS):
        tile = stage_ref[slot, e, pl.ds(layout.tail_row[e], SLOT_ALIGN), :]
        tail_ref[e] = jnp.where(layout.tail_chunk[e] == 0, tile, tail_ref[e])

    @pl.when(b > 0)
    def _():
        for e in range(N_EXPERTS):
            copy(stage_ref.at[1 - slot], e, 0, sem.at[1 - slot]).wait()

    for e in range(N_EXPERTS):
        copy(stage_ref.at[slot], e, layout.window_row(e, 0), sem.at[slot]).start()

    def overflow(chunk, carry):
        more = rows_for(chunk).astype(BF16)
        for e in range(N_EXPERTS):
            extra_ref[e] = more[e * MOE_W:(e + 1) * MOE_W]
        for e in range(N_EXPERTS):
            copy(extra_ref, e, layout.window_row(e, chunk), sem_extra.at[0]).start()
        for e in range(N_EXPERTS):
            @pl.when(layout.tail_chunk[e] == chunk)
            def _():
                tail_ref[e] = extra_ref[e, pl.ds(layout.tail_row[e], SLOT_ALIGN), :]
        for e in range(N_EXPERTS):
            copy(extra_ref, e, 0, sem_extra.at[0]).wait()
        return carry

    lax.fori_loop(1, layout.n_chunks, overflow, 0)

    @pl.when(b == nb - 1)
    def _():
        for e in range(N_EXPERTS):
            copy(stage_ref.at[slot], e, 0, sem.at[slot]).wait()


def _slot_rows(cap):
    pad = ((SLOT_ALIGN + MOE_TB) // MOE_W + 1) * MOE_W
    return cap + -(-pad // FFN_TM) * FFN_TM


def _dispatch(offs, sel, h, cap):
    n_tok = h.shape[0]
    nb = n_tok // MOE_TB
    rows = _slot_rows(cap)
    grid_spec = pltpu.PrefetchScalarGridSpec(
        num_scalar_prefetch=1,
        grid=(nb,),
        in_specs=[pl.BlockSpec((N_EXPERTS, MOE_TB), lambda b, off: (0, b)),
                  pl.BlockSpec((MOE_TB, D_MODEL), lambda b, off: (b, 0)),
                  pl.BlockSpec((MOE_TB, MOE_TB), lambda b, off: (0, 0))],
        out_specs=pl.BlockSpec(memory_space=pl.ANY),
        scratch_shapes=[pltpu.VMEM((2, N_EXPERTS, MOE_W, D_MODEL), BF16),
                        pltpu.VMEM((N_EXPERTS, MOE_W, D_MODEL), BF16),
                        pltpu.VMEM((N_EXPERTS, SLOT_ALIGN, D_MODEL), BF16),
                        pltpu.VMEM((rows - cap, D_MODEL), BF16),
                        pltpu.SemaphoreType.DMA((2,)),
                        pltpu.SemaphoreType.DMA((1,))],
    )
    return pl.pallas_call(
        _dispatch_kernel,
        grid_spec=grid_spec,
        out_shape=jax.ShapeDtypeStruct((N_EXPERTS, rows, D_MODEL), BF16),
        compiler_params=_cparams(("arbitrary",)),
        name="dispatch",
    )(offs, sel, h, _strict_upper(MOE_TB))


def _ffn_kernel(*refs, first_step, n_tiles):
    n_groups = len(n_tiles)
    xe_refs = refs[:n_groups]
    wg_ref, wu_ref, wd_ref = refs[n_groups:n_groups + 3]
    ye_refs = refs[n_groups + 3:2 * n_groups + 3]
    wg_bf, wu_bf, wd_bf = refs[2 * n_groups + 3:]
    j = pl.program_id(1)

    @pl.when(j == 0)
    def _():
        wg_bf[...] = wg_ref[...].astype(BF16)
        wu_bf[...] = wu_ref[...].astype(BF16)
        wd_bf[...] = wd_ref[...].astype(BF16)

    for g in range(n_groups):
        @pl.when((j >= first_step[g]) & (j < first_step[g] + n_tiles[g]))
        def _():
            x = xe_refs[g][...]
            gate = jnp.dot(x, wg_bf[...], preferred_element_type=F32)
            up = jnp.dot(x, wu_bf[...], preferred_element_type=F32)
            hid = (gate * jax.nn.sigmoid(gate) * up).astype(BF16)
            ye_refs[g][...] = jnp.dot(hid, wd_bf[...], preferred_element_type=F32).astype(BF16)


def _expert_ffn(xes, w_gate, w_up, w_down, layer, caps):
    n_groups = len(xes)
    n_tiles = [cap // FFN_TM for cap in caps]
    first_step = [sum(n_tiles[:g]) for g in range(n_groups)]

    def weight_index(e, j):
        return (layer, jnp.minimum(e + (j >= 1).astype(I32), N_EXPERTS - 1), 0, 0)

    wspec = lambda shape: pl.BlockSpec((None, None) + shape, weight_index)

    def tile_spec(g):
        return pl.BlockSpec((None, FFN_TM, D_MODEL),
                            lambda e, j: (e, jnp.clip(j - first_step[g], 0, n_tiles[g] - 1), 0))

    return pl.pallas_call(
        functools.partial(_ffn_kernel, first_step=tuple(first_step), n_tiles=tuple(n_tiles)),
        grid=(N_EXPERTS, sum(n_tiles)),
        in_specs=([tile_spec(g) for g in range(n_groups)]
                  + [wspec((D_MODEL, EXPERT_FF)), wspec((D_MODEL, EXPERT_FF)), wspec((EXPERT_FF, D_MODEL))]),
        out_specs=[tile_spec(g) for g in range(n_groups)],
        out_shape=[jax.ShapeDtypeStruct(xe.shape, BF16) for xe in xes],
        input_output_aliases={g: g for g in range(n_groups)},
        scratch_shapes=[pltpu.VMEM((D_MODEL, EXPERT_FF), BF16), pltpu.VMEM((D_MODEL, EXPERT_FF), BF16),
                        pltpu.VMEM((EXPERT_FF, D_MODEL), BF16)],
        compiler_params=_cparams(("arbitrary", "arbitrary")),
        name="expert_ffn",
    )(*xes, w_gate, w_up, w_down)


def _combine_kernel(off_ref, sel_ref, aff_ref, tri_ref, x_ref, ye_ref, out_ref, win_ref, extra_ref, sem, sem_extra,
                    *, rows):
    b = pl.program_id(0)
    nb = pl.num_programs(0)
    slot = b % 2

    def copy(dst, e, row0, s):
        return pltpu.make_async_copy(ye_ref.at[e, pl.ds(row0, MOE_W)], dst.at[e], s)

    def fetch(blk, dst_slot):
        ahead = _BlockLayout(off_ref, blk)
        for e in range(N_EXPERTS):
            copy(win_ref.at[dst_slot], e, ahead.window_row(e, 0), sem.at[dst_slot]).start()

    @pl.when(b == 0)
    def _():
        fetch(0, 0)

    @pl.when(b + 1 < nb)
    def _():
        fetch(b + 1, 1 - slot)

    layout = _BlockLayout(off_ref, b)
    pos = _window_positions(sel_ref, tri_ref, layout)
    aff = aff_ref[...]
    tn = (((0,), (0,)), ((), ()))

    def contribution(chunk, window):
        gates = _slot_onehots(pos, aff, chunk)
        vals = window.reshape(N_EXPERTS * MOE_W, window.shape[-1])
        return lax.dot_general(gates, vals, tn, preferred_element_type=F32)

    for e in range(N_EXPERTS):
        copy(win_ref.at[slot], e, 0, sem.at[slot]).wait()
    out_ref[...] = x_ref[...] + contribution(0, win_ref[slot])

    def overflow(chunk, carry):
        for e in range(N_EXPERTS):
            row0 = pl.multiple_of(jnp.minimum(layout.window_row(e, chunk), rows - MOE_W), SLOT_ALIGN)
            copy(extra_ref, e, row0, sem_extra.at[0]).start()
        for e in range(N_EXPERTS):
            copy(extra_ref, e, 0, sem_extra.at[0]).wait()
        out_ref[...] += contribution(chunk, extra_ref[...])
        return carry

    lax.fori_loop(1, layout.n_chunks, overflow, 0)


def _combine(offs, sel, aff_t, x, ye):
    n_tok = x.shape[0]
    nb = n_tok // MOE_TB
    rows = ye.shape[1]
    grid_spec = pltpu.PrefetchScalarGridSpec(
        num_scalar_prefetch=1,
        grid=(nb,),
        in_specs=[pl.BlockSpec((N_EXPERTS, MOE_TB), lambda b, off: (0, b)),
                  pl.BlockSpec((N_EXPERTS, MOE_TB), lambda b, off: (0, b)),
                  pl.BlockSpec((MOE_TB, MOE_TB), lambda b, off: (0, 0)),
                  pl.BlockSpec((MOE_TB, D_MODEL), lambda b, off: (b, 0)),
                  pl.BlockSpec(memory_space=pl.ANY)],
        out_specs=pl.BlockSpec((MOE_TB, D_MODEL), lambda b, off: (b, 0)),
        scratch_shapes=[pltpu.VMEM((2, N_EXPERTS, MOE_W, D_MODEL), BF16),
                        pltpu.VMEM((N_EXPERTS, MOE_W, D_MODEL), BF16),
                        pltpu.SemaphoreType.DMA((2,)),
                        pltpu.SemaphoreType.DMA((1,))],
    )
    return pl.pallas_call(
        functools.partial(_combine_kernel, rows=rows),
        grid_spec=grid_spec,
        out_shape=jax.ShapeDtypeStruct((n_tok, D_MODEL), F32),
        compiler_params=_cparams(("arbitrary",)),
        name="combine",
    )(offs, sel, aff_t, _strict_upper(MOE_TB), x, ye)


def _moe(routed, w_gate, w_up, w_down, layer):
    plans = []
    for x, h, aff_t in routed:
        cap = CAPACITY_FACTOR * x.shape[0] // N_EXPERTS
        sel, offs = _select(aff_t, cap)
        plans.append((cap, sel, offs, _dispatch(offs, sel, h, cap)))
    yes = _expert_ffn([p[3] for p in plans], w_gate, w_up, w_down, layer, [p[0] for p in plans])
    return [_combine(offs, sel, aff_t, x, ye)
            for (x, _, aff_t), (_, sel, offs, _), ye in zip(routed, plans, yes)]


def _row(v):
    return v.reshape(1, -1).astype(F32)


def _router_split(w_router):
    wt = w_router.T.astype(F32)
    hi = wt.astype(BF16)
    lo = (wt - hi.astype(F32)).astype(BF16)
    return hi, lo


def _prep_ab(norm_g, w_in, qn_a, kn_a, qn_b, kn_b, w_out, norm_ffn, w_router):
    scale = HEAD_DIM ** -0.5 * LOG2E
    perm_b = _pair_layout_perm(B_KV_HEADS, B_Q_HEADS // B_KV_HEADS)
    a3 = 3 * A_WIDTH
    bq = B_Q_HEADS * HEAD_DIM
    w_cols = np.concatenate([np.arange(a3), a3 + perm_b, np.arange(a3 + bq, w_in.shape[1])])
    gains = jnp.concatenate([
        jnp.repeat(qn_a, A_HEADS, axis=0).reshape(-1) * scale,
        jnp.repeat(kn_a, A_HEADS, axis=0).reshape(-1),
        jnp.ones((A_WIDTH,), F32),
        jnp.tile(qn_b, B_Q_HEADS) * scale,
        jnp.tile(kn_b, B_KV_HEADS),
        jnp.ones((B_KV_HEADS * HEAD_DIM,), F32)])
    a_out = A_HEADS * HEAD_DIM
    w_o = w_out[np.concatenate([np.arange(a_out), a_out + perm_b])].astype(BF16)
    return (_row(norm_g), w_in[:, w_cols].astype(BF16), _row(gains), w_o, _row(norm_ffn)) + _router_split(w_router)


def _layer_ab(x, batch, seq, tabs, prep):
    norm_g, w, gains, w_o, norm_ffn, wr_hi, wr_lo = prep
    tab_a, tab_b = tabs
    bq = B_Q_HEADS * HEAD_DIM
    n_a = A_GROUPS
    assert A_HEADS * HEAD_DIM == MXU_COLS
    dils = [dil for _, dil in A_PATTERNS]
    plan = ([("qk", g, 0, 0, ROT_DIM // 2, dils[g]) for g in range(n_a)]
            + [("qk", n_a + g, 0, 0, ROT_DIM // 2, dils[g]) for g in range(n_a)]
            + [("v", 2 * n_a + g, 0, 0, 0, 1) for g in range(n_a)]
            + [("qk", 3 * n_a, c * MXU_COLS, 1, HEAD_DIM // 4, 1) for c in range(bq // MXU_COLS)]
            + [("qk", 3 * n_a + 1, 0, 1, HEAD_DIM // 4, 1), ("v", 3 * n_a + 2, 0, 0, 0, 1)])
    assert B_KV_HEADS * HEAD_DIM == MXU_COLS
    a_widths = tuple(MXU_COLS if dil == 1 else ("dilated", dil) for dil in dils)
    outs = _projection(
        x, norm_g, w, gains, (tab_a, tab_b), plan,
        a_widths * 2 + (None,) * n_a + (bq, MXU_COLS, None), seq)
    qa, ka, va = outs[:n_a], outs[n_a:2 * n_a], outs[2 * n_a:3 * n_a]
    qb, kb, vb = outs[3 * n_a:]
    o_parts, lse_parts = [], []
    for gi, (window, dil) in enumerate(A_PATTERNS):
        o, lse = _attention(qa[gi], ka[gi], va[gi], batch=batch, n_keys=seq // dil, residues=dil, group=1,
                            n_pairs=A_HEADS // 2, band_r=window // (2 * dil), want_lse=True)
        o_parts.append(o)
        lse_parts.append(lse)
    (ob,) = _attention(qb, kb, vb, batch=batch, n_keys=seq, residues=1,
                       group=B_Q_HEADS // B_KV_HEADS, n_pairs=B_KV_HEADS // 2, band_r=None)
    return _out_projection(o_parts, lse_parts, ob, x, w_o, norm_ffn, wr_hi, wr_lo)


def _prep_c(norm_g, w_in, qn, kn, sink, w_out, norm_ffn, w_router):
    scale = HEAD_DIM ** -0.5 * LOG2E
    perm = _pair_layout_perm(C_KV_HEADS, C_Q_HEADS // C_KV_HEADS)
    cq = C_Q_HEADS * HEAD_DIM
    w_cols = np.concatenate([perm, np.arange(cq, w_in.shape[1])])
    gains = jnp.concatenate([jnp.tile(qn, C_Q_HEADS) * scale, jnp.tile(kn, C_KV_HEADS),
                             jnp.ones((C_KV_HEADS * HEAD_DIM,), F32)])
    return (_row(norm_g), w_in[:, w_cols].astype(BF16), _row(gains), sink.astype(F32),
            w_out[perm].astype(BF16), _row(norm_ffn)) + _router_split(w_router)


def _layer_c(x, batch, seq, tabs, prep):
    norm_g, w, gains, sink, w_o, norm_ffn, wr_hi, wr_lo = prep
    tab_a, _ = tabs
    cq = C_Q_HEADS * HEAD_DIM
    plan = ([("qk", 0, c * MXU_COLS, 0, ROT_DIM // 2, 1) for c in range(cq // MXU_COLS)]
            + [("qk", 1, 0, 0, ROT_DIM // 2, 1), ("v", 2, 0, 0, 0, 1)])
    assert C_KV_HEADS * HEAD_DIM == MXU_COLS
    q, k, v = _projection(x, norm_g, w, gains, (tab_a,), plan, (cq, MXU_COLS, None), seq)
    (o,) = _attention(q, k, v, batch=batch, n_keys=seq, residues=1,
                      group=C_Q_HEADS // C_KV_HEADS, n_pairs=C_KV_HEADS // 2, band_r=C_RADIUS, sink=sink)
    return _out_projection([], [], o, x, w_o, norm_ffn, wr_hi, wr_lo)


def _encode(xs, params):
    (norm_mix, norm_ffn, w_in_ab, qn_a, kn_a, qn_b, kn_b, w_out_ab, w_in_c, qn_c, kn_c, sink_c, w_out_c,
     w_router, w_gate, w_up, w_down) = params
    shapes = [x.shape[:2] for x in xs]
    tabs = [_rope_tables(seq) for _, seq in shapes]
    xts = [x.reshape(batch * seq, D_MODEL) for x, (batch, seq) in zip(xs, shapes)]
    for layer in range(norm_mix.shape[0]):
        j = layer // 2
        if layer % 2 == 0:
            prep = _prep_ab(norm_mix[layer], w_in_ab[j], qn_a[j], kn_a[j], qn_b[j], kn_b[j], w_out_ab[j],
                            norm_ffn[layer], w_router[layer])
            mixer = _layer_ab
        else:
            prep = _prep_c(norm_mix[layer], w_in_c[j], qn_c[j], kn_c[j], sink_c[j], w_out_c[j],
                           norm_ffn[layer], w_router[layer])
            mixer = _layer_c
        routed = [mixer(xt, batch, seq, tab, prep) for xt, (batch, seq), tab in zip(xts, shapes, tabs)]
        xts = _moe(routed, w_gate, w_up, w_down, layer)
    return tuple(xt.reshape(batch, seq, D_MODEL) for xt, (batch, seq) in zip(xts, shapes))


def kernel(x_prompt, x_sample, norm_mix, norm_ffn, w_in_ab, qn_a, kn_a, qn_b, kn_b, w_out_ab, w_in_c, qn_c,
           kn_c, sink_c, w_out_c, w_router, w_gate, w_up, w_down):
    params = (norm_mix, norm_ffn, w_in_ab, qn_a, kn_a, qn_b, kn_b, w_out_ab, w_in_c, qn_c, kn_c, sink_c,
              w_out_c, w_router, w_gate, w_up, w_down)
    return _encode((x_prompt, x_sample), params)
```

```python
import functools
import math

import jax
import jax.numpy as jnp
import numpy as np
from jax import lax
from jax.experimental import pallas as pl
from jax.experimental.pallas import tpu as pltpu

F32 = jnp.float32
BF16 = jnp.bfloat16
I32 = jnp.int32

D_MODEL = 1024
HEAD_DIM = 64
LANES = 128
MXU_COLS = 256
GRID_W = 64
ROT_DIM = HEAD_DIM // 4
ROPE_THETA = 500000.0
AXIAL_THETA = 10000.0
A_PATTERNS = ((128, 1), (512, 4), (2048, 16))
A_HEADS = 4
A_GROUPS = len(A_PATTERNS)
A_WIDTH = A_GROUPS * A_HEADS * HEAD_DIM
B_Q_HEADS = 12
B_KV_HEADS = 4
C_Q_HEADS = 16
C_KV_HEADS = 4
C_RADIUS = 128
N_EXPERTS = 16
EXPERT_FF = 1024
CAPACITY_FACTOR = 2
NEG_INF = -1e30
EPS = 1e-6

VMEM_LIMIT = 56 * 1024 * 1024

PROJ_TM = 512
OUT_TM = 1024
DENSE_TQ = 256
BAND_TQ = 256
BAND_COLS = 1024
BAND_SUBTILES = 4
DENSE_SUBTILES = 2
ACC_ROWS = LANES + 16
LSE_ROWS = 8
DENSE_TK = 512
MOE_TB = 256
MOE_W = 64
FFN_TM = 512


def _cparams(sem):
    return pltpu.CompilerParams(dimension_semantics=sem, vmem_limit_bytes=VMEM_LIMIT)


def _head_block_diag():
    idx = np.arange(MXU_COLS) // HEAD_DIM
    return jnp.asarray((idx[:, None] == idx[None, :]).astype(np.float32), dtype=BF16)


def _rope_tables(max_len):
    pos = jnp.arange(max_len, dtype=F32)
    j = np.arange(LANES) % HEAD_DIM

    def angles(p, dim, theta):
        exps = jnp.arange(0, dim, 2, dtype=F32) / dim
        inv = jnp.power(jnp.float32(theta), -exps)
        return p[:, None] * inv[None, :]

    half = ROT_DIM // 2
    ang = angles(pos, ROT_DIM, ROPE_THETA)
    cos, sin = jnp.cos(ang), jnp.sin(ang)
    fa = np.where(j < half, j, np.where(j < ROT_DIM, j - half, 0))
    cos_l, sin_l = cos[:, fa], sin[:, fa]
    lo = jnp.asarray(j < half)[None, :]
    hi = jnp.asarray((j >= half) & (j < ROT_DIM))[None, :]
    tab_a = jnp.stack([jnp.where(lo | hi, cos_l, 1.0),
                       jnp.where(lo, -sin_l, 0.0),
                       jnp.where(hi, sin_l, 0.0)])
    hb = HEAD_DIM // 2
    qb = hb // 2
    t = jnp.arange(max_len)
    ang_r = angles((t // GRID_W).astype(F32), hb, AXIAL_THETA)
    ang_c = angles((t % GRID_W).astype(F32), hb, AXIAL_THETA)
    fb = j % qb
    is_col = jnp.asarray(j >= hb)[None, :]
    ang_l = jnp.where(is_col, ang_c[:, fb], ang_r[:, fb])
    cos_b, sin_b = jnp.cos(ang_l), jnp.sin(ang_l)
    first = jnp.asarray((j % hb) < qb)[None, :]
    tab_b = jnp.stack([cos_b, jnp.where(first, -sin_b, 0.0), jnp.where(first, 0.0, sin_b)])
    return tab_a.astype(F32), tab_b.astype(F32)


def _pair_layout_perm(n_kv, group):
    cols = []
    for p in range(n_kv // 2):
        for g in range(group):
            for par in range(2):
                h = (2 * p + par) * group + g
                cols.extend(range(h * HEAD_DIM, (h + 1) * HEAD_DIM))
    return np.asarray(cols, dtype=np.int32)


def _proj_kernel(x_ref, g_ref, w_ref, s_ref, gain_ref, *rest, plan, n_tabs):
    tab_refs = rest[:n_tabs]
    out_refs = rest[n_tabs:-1]
    rows_ref = rest[-1]
    x = x_ref[...]
    tm = x.shape[0]
    ms = jnp.mean(x * x, axis=1, keepdims=True)
    xn = ((x * lax.rsqrt(ms + EPS)) * g_ref[...]).astype(BF16)

    def project(c):
        return jnp.dot(xn, w_ref[:, c * MXU_COLS:(c + 1) * MXU_COLS], preferred_element_type=F32)

    ahead = project(0)
    for c, (kind, out_i, out_col, tab_i, shift, dil) in enumerate(plan):
        acc = ahead
        if c + 1 < len(plan):
            ahead = project(c + 1)
        o_ref = out_refs[out_i]
        if kind == "v":
            o_ref[...] = acc.T.astype(BF16)
            continue
        ss = jnp.dot((acc * acc).astype(BF16), s_ref[...], preferred_element_type=F32)
        y = (acc * lax.rsqrt(ss * (1.0 / HEAD_DIM) + EPS)) * gain_ref[:, c * MXU_COLS:(c + 1) * MXU_COLS]
        tab = tab_refs[tab_i]
        t0 = jnp.concatenate([tab[0], tab[0]], axis=1)
        t1 = jnp.concatenate([tab[1], tab[1]], axis=1)
        t2 = jnp.concatenate([tab[2], tab[2]], axis=1)
        y = y * t0 + pltpu.roll(y, MXU_COLS - shift, 1) * t1 + pltpu.roll(y, shift, 1) * t2
        if dil == 1:
            o_ref[:, out_col:out_col + MXU_COLS] = y.astype(BF16)
        else:
            for half in range(MXU_COLS // LANES):
                rows_ref[half] = y[:, half * LANES:(half + 1) * LANES]
            for r in range(dil):
                for half in range(MXU_COLS // LANES):
                    piece = rows_ref[half, pl.ds(r, tm // dil, stride=dil), :]
                    col = r * MXU_COLS + half * LANES
                    o_ref[:, col:col + LANES] = piece.astype(BF16)


def _projection(x, norm_g, w, gains, tabs, plan, out_widths, seq_len):
    t_tokens = x.shape[0]
    tm = PROJ_TM
    n_in = w.shape[1]
    blocks_per_seq = seq_len // tm
    in_specs = [
        pl.BlockSpec((tm, D_MODEL), lambda i: (i, 0)),
        pl.BlockSpec((1, D_MODEL), lambda i: (0, 0)),
        pl.BlockSpec((D_MODEL, n_in), lambda i: (0, 0)),
        pl.BlockSpec((MXU_COLS, MXU_COLS), lambda i: (0, 0)),
        pl.BlockSpec((1, n_in), lambda i: (0, 0)),
    ] + [pl.BlockSpec((3, tm, LANES), lambda i: (0, i % blocks_per_seq, 0)) for _ in tabs]
    out_specs, out_shape = [], []
    for wd in out_widths:
        if wd is None:
            out_specs.append(pl.BlockSpec((None, MXU_COLS, tm),
                                          lambda i: (i // blocks_per_seq, 0, i % blocks_per_seq)))
            out_shape.append(jax.ShapeDtypeStruct((t_tokens // seq_len, MXU_COLS, seq_len), BF16))
        elif isinstance(wd, tuple):
            dil = wd[1]
            out_specs.append(pl.BlockSpec((tm // dil, dil * MXU_COLS), lambda i: (i, 0)))
            out_shape.append(jax.ShapeDtypeStruct((t_tokens // dil, dil * MXU_COLS), BF16))
        else:
            out_specs.append(pl.BlockSpec((tm, wd), lambda i: (i, 0)))
            out_shape.append(jax.ShapeDtypeStruct((t_tokens, wd), BF16))
    return pl.pallas_call(
        functools.partial(_proj_kernel, plan=tuple(plan), n_tabs=len(tabs)),
        grid=(t_tokens // tm,),
        in_specs=in_specs, out_specs=out_specs, out_shape=out_shape,
        scratch_shapes=[pltpu.VMEM((MXU_COLS // LANES, tm, LANES), F32)],
        compiler_params=_cparams(("parallel",)),
        name="projection",
    )(x, norm_g, w, _head_block_diag(), gains, *tabs)


LOG2E = math.log2(math.e)
LN2 = math.log(2.0)
_NT = (((1,), (1,)), ((), ()))


def _attn_kernel(*refs, group, n_pairs, tq, n_sub, tk, n_keys, band_r, has_sink, want_lse):
    refs = list(refs)
    sink_ref = refs.pop(0) if has_sink else None
    q_ref, k_ref, v_ref, o_ref = refs[:4]
    refs = refs[4:]
    lse_ref = refs.pop(0) if want_lse else None
    scratch = refs
    n_chain = 2 * n_pairs
    lane = lax.broadcasted_iota(I32, (1, LANES), 1)
    low = lane < HEAD_DIM
    half_mask = [jnp.where(low, 1.0, 0.0).astype(BF16), jnp.where(low, 0.0, 1.0).astype(BF16)]
    sub_q = tq // n_sub
    tile_rows = [slice(u * sub_q, (u + 1) * sub_q) for u in range(n_sub)]
    common = dict(group=group, n_pairs=n_pairs, tq=sub_q, tk=tk, n_keys=n_keys, band_r=band_r)
    if band_r is None:
        _attn_tiles(sink_ref, q_ref, k_ref, v_ref, o_ref, lse_ref, scratch, half_mask, tile_rows, None, **common)
    else:
        for u in range(n_sub):
            _attn_tiles(sink_ref, q_ref, k_ref, v_ref, o_ref, lse_ref, scratch, half_mask, tile_rows[u:u + 1],
                        pl.program_id(2) * n_sub + u, **common)


def _attn_tiles(sink_ref, q_ref, k_ref, v_ref, o_ref, lse_ref, scratch, half_mask, tile_rows, i, *,
                group, n_pairs, tq, tk, n_keys, band_r):
    has_sink = sink_ref is not None
    want_lse = lse_ref is not None
    n_chain = 2 * n_pairs
    m_cols = group * tq

    def masked_queries(q_rows):
        q_masked = []
        for p in range(n_pairs):
            q_p = jnp.concatenate(
                [q_ref[q_rows, (p * group + g) * LANES:(p * group + g + 1) * LANES] for g in range(group)],
                axis=0)
            q_masked += [q_p * half_mask[par] for par in range(2)]
        return q_masked

    sub = lax.broadcasted_iota(I32, (ACC_ROWS, m_cols), 0)
    if has_sink:
        acc0 = jnp.where(sub >= LANES, 1.0, 0.0).astype(F32)
        m0 = []
        for c in range(n_chain):
            m0.append(jnp.concatenate(
                [jnp.full((1, tq), sink_ref[c * group + g] * LOG2E, F32) for g in range(group)], axis=1))
    else:
        acc0 = jnp.zeros((ACC_ROWS, m_cols), F32)
        m0 = [jnp.full((1, m_cols), NEG_INF, F32) for _ in range(n_chain)]

    def values_t(p, k0, width):
        ones = jnp.ones((ACC_ROWS - LANES, width), BF16)
        return jnp.concatenate([v_ref[p * LANES:(p + 1) * LANES, pl.ds(k0, width)], ones], axis=0)

    def softmax_pv(s_t, v_t, m_prev, acc_prev, s_max=None):
        if s_max is None:
            s_max = jnp.max(s_t, axis=0, keepdims=True)
        m_new = jnp.maximum(m_prev, s_max)
        alpha = jnp.exp2(m_prev - m_new)
        p_t = jnp.exp2(s_t - m_new).astype(BF16)
        return m_new, alpha * acc_prev + jnp.dot(v_t, p_t, preferred_element_type=F32)

    def write_tile(q_rows, acc_fin, m_fin):
        top = lax.broadcasted_iota(I32, (LANES, tq), 0) < HEAD_DIM
        for p in range(n_pairs):
            even, odd = acc_fin[2 * p], acc_fin[2 * p + 1]
            for g in range(group):
                cols = slice(g * tq, (g + 1) * tq)
                num = jnp.where(top, even[:LANES, cols], odd[:LANES, cols])
                den = jnp.where(top, even[LANES:LANES + 1, cols], odd[LANES:LANES + 1, cols])
                out_cols = slice((p * group + g) * LANES, (p * group + g + 1) * LANES)
                o_ref[q_rows, out_cols] = (num / den).T.astype(o_ref.dtype)
        if want_lse:
            assert group == 1 and n_chain <= LSE_ROWS
            rows = [m_fin[c] * LN2 + jnp.log(acc_fin[c][LANES:LANES + 1, :]) for c in range(n_chain)]
            lse_ref[:, q_rows] = jnp.concatenate(rows + [jnp.zeros((LSE_ROWS - n_chain, tq), F32)], axis=0)

    if band_r is None:
        qm_ref, acc_ref, s_even, s_odd = scratch
        n_blocks = n_keys // tk

        def start_tile(u):
            q_masked = masked_queries(tile_rows[u])
            for c in range(n_chain):
                qm_ref[c] = q_masked[c]
                acc_ref[u % 2, c] = acc0

        def scores(kt, c, s_ref):
            s_t = lax.dot_general(kt, qm_ref[c], _NT, preferred_element_type=F32)
            s_ref[c] = s_t
            return jnp.max(s_t, axis=0, keepdims=True)

        def step(kb_next, s_next, kb, s_ref, m_prev, s_max, acc):
            k0 = pl.multiple_of(kb * tk, tk)
            m_next, max_next = [], []
            for p in range(n_pairs):
                v_t = values_t(p, k0, tk)
                if kb_next is not None:
                    next0 = kb_next * tk if isinstance(kb_next, int) else pl.multiple_of(kb_next * tk, tk)
                    kt = k_ref[pl.ds(next0, tk), p * LANES:(p + 1) * LANES]
                for par in range(2):
                    c = 2 * p + par
                    if kb_next is not None:
                        max_next.append(scores(kt, c, s_next))
                    m_new, acc[c] = softmax_pv(s_ref[c], v_t, m_prev[c], acc[c], s_max[c])
                    m_next.append(m_new)
            return m_next, max_next

        start_tile(0)
        s_max = []
        for p in range(n_pairs):
            kt = k_ref[0:tk, p * LANES:(p + 1) * LANES]
            s_max += [scores(kt, 2 * p + par, s_even) for par in range(2)]
        for u, q_rows in enumerate(tile_rows):
            acc = acc_ref.at[u % 2]

            def body(j, carry, acc=acc):
                m, mx = list(carry[:n_chain]), list(carry[n_chain:])
                m, mx = step(2 * j + 1, s_odd, 2 * j, s_even, m, mx, acc)
                m, mx = step(2 * j + 2, s_even, 2 * j + 1, s_odd, m, mx, acc)
                return tuple(m + mx)

            carry = lax.fori_loop(0, n_blocks // 2 - 1, body, tuple(m0 + s_max))
            m_fin, s_max = step(n_blocks - 1, s_odd, n_blocks - 2, s_even,
                                list(carry[:n_chain]), list(carry[n_chain:]), acc)
            if u + 1 < len(tile_rows):
                start_tile(u + 1)
                m_fin, s_max = step(0, s_even, n_blocks - 1, s_odd, m_fin, s_max, acc)
            else:
                m_fin, _ = step(None, None, n_blocks - 1, s_odd, m_fin, s_max, acc)
            write_tile(q_rows, [acc[c] for c in range(n_chain)], m_fin)
    else:
        halo = -(-band_r // LANES) * LANES
        width = min(tq + 2 * halo, n_keys)
        ws = pl.multiple_of(jnp.clip(i * tq - halo, 0, n_keys - width), LANES)
        kpos = ws + lax.broadcasted_iota(I32, (width, 1), 0)
        col = lax.broadcasted_iota(I32, (1, tq), 1)
        qpos = i * tq + jnp.concatenate([col] * group, axis=1)
        mask = jnp.abs(kpos - qpos) <= band_r
        q_masked = masked_queries(tile_rows[0])
        s_all = []
        for p in range(n_pairs):
            kt = k_ref[pl.ds(ws, width), p * LANES:(p + 1) * LANES]
            s_all += [lax.dot_general(kt, q_masked[2 * p + par], _NT, preferred_element_type=F32)
                      for par in range(2)]
        m_fin, acc_fin = [], []
        for c in range(n_chain):
            m_new, acc_new = softmax_pv(jnp.where(mask, s_all[c], NEG_INF), values_t(c // 2, ws, width),
                                        m0[c], acc0)
            m_fin.append(m_new)
            acc_fin.append(acc_new)
        write_tile(tile_rows[0], acc_fin, m_fin)


def _attention(q, k, v_t, *, batch, n_keys, residues, group, n_pairs, band_r, sink=None, want_lse=False):
    dense = band_r is None
    sub_q = DENSE_TQ if dense else min(BAND_TQ, BAND_COLS // group, n_keys)
    n_sub = max(1, min(DENSE_SUBTILES if dense else BAND_SUBTILES, n_keys // sub_q))
    tq = sub_q * n_sub
    wq = n_pairs * group * LANES
    wk = n_pairs * LANES
    q3 = q.reshape(batch, n_keys, residues * wq)
    k3 = k.reshape(batch, n_keys, residues * wk)
    v4 = v_t.reshape(batch, wk, n_keys, residues).transpose(0, 3, 1, 2)

    in_specs = []
    args = []
    if sink is not None:
        in_specs.append(pl.BlockSpec(memory_space=pltpu.SMEM))
        args.append(sink)
    in_specs += [
        pl.BlockSpec((None, tq, wq), lambda b, r, i: (b, i, r)),
        pl.BlockSpec((None, n_keys, wk), lambda b, r, i: (b, 0, r)),
        pl.BlockSpec((None, None, wk, n_keys), lambda b, r, i: (b, r, 0, 0)),
    ]
    args += [q3, k3, v4]
    out_specs = [pl.BlockSpec((None, tq, wq), lambda b, r, i: (b, i, r))]
    out_shape = [jax.ShapeDtypeStruct((batch, n_keys, residues * wq), BF16)]
    if want_lse:
        out_specs.append(pl.BlockSpec((None, None, LSE_ROWS, tq), lambda b, r, i: (b, r, 0, i)))
        out_shape.append(jax.ShapeDtypeStruct((batch, residues, LSE_ROWS, n_keys), F32))
    m_cols = group * sub_q
    scratch = []
    if dense:
        scratch = [pltpu.VMEM((2 * n_pairs, m_cols, LANES), BF16),
                   pltpu.VMEM((2, 2 * n_pairs, ACC_ROWS, m_cols), F32)]
        scratch += [pltpu.VMEM((2 * n_pairs, DENSE_TK, m_cols), F32) for _ in range(2)]
    outs = pl.pallas_call(
        functools.partial(_attn_kernel, group=group, n_pairs=n_pairs, tq=tq, n_sub=n_sub, tk=DENSE_TK,
                          n_keys=n_keys, band_r=band_r, has_sink=sink is not None, want_lse=want_lse),
        grid=(batch, residues, n_keys // tq),
        in_specs=in_specs, out_specs=out_specs, out_shape=out_shape,
        scratch_shapes=scratch,
        compiler_params=_cparams(("parallel", "parallel", "arbitrary")),
        name="attention_dense" if dense else "attention_band",
    )(*args)
    result = [outs[0].reshape(batch * n_keys, residues * wq)]
    if want_lse:
        result.append(outs[1].transpose(2, 0, 3, 1).reshape(LSE_ROWS, batch * n_keys * residues))
    return result


def _outproj_kernel(*refs, merge_dils):
    refs = list(refs)
    n_merge = len(merge_dils)
    n_scratch = sum(d > 1 for d in merge_dils)
    if n_merge:
        o_parts = refs[:n_merge]
        lse_parts = refs[n_merge:2 * n_merge]
        expand_ref = refs[2 * n_merge]
        scratch = refs[len(refs) - n_scratch:]
        refs = refs[2 * n_merge + 1:len(refs) - n_scratch]
    o_rest, x_ref, w_ref, g_ref, wrh_ref, wrl_ref, xo_ref, h_ref, aff_ref = refs
    tm = x_ref.shape[0]
    if n_merge:
        ordered = []
        free_scratch = list(scratch)
        for o_ref, dil in zip(o_parts, merge_dils):
            if dil > 1:
                rows_ref = free_scratch.pop(0)
                wd = o_ref.shape[1] // dil
                for r in range(dil):
                    for half in range(wd // LANES):
                        col = r * wd + half * LANES
                        rows_ref[half, pl.ds(r, tm // dil, stride=dil), :] = o_ref[:, col:col + LANES].astype(F32)
                ordered.append(lambda rows, ref=rows_ref: jnp.concatenate(
                    [ref[half, rows, :] for half in range(ref.shape[0])], axis=1))
            else:
                ordered.append(lambda rows, ref=o_ref: ref[rows, :].astype(F32))
        o_parts = ordered
    halves = [slice(0, tm // 2), slice(tm // 2, tm)]
    accs = []
    for rows in halves:
        acc = x_ref[rows, :]
        k0 = 0
        if n_merge:
            lses = [r[:, rows] for r in lse_parts]
            m = functools.reduce(jnp.maximum, lses)
            ws = [jnp.exp(l - m) for l in lses]
            inv = 1.0 / sum(ws)
            num = 0.0
            for wgt, r in zip(ws, o_parts):
                wn = wgt * inv
                hi = wn.astype(BF16)
                split = jnp.concatenate([hi, (wn - hi.astype(F32)).astype(BF16)], axis=0)
                full = lax.dot_general(split, expand_ref[...], (((0,), (0,)), ((), ())),
                                       preferred_element_type=F32)
                num = num + full * r(rows)
            oa = num.astype(BF16)
            k0 = oa.shape[1]
            acc = acc + jnp.dot(oa, w_ref[:k0, :], preferred_element_type=F32)
        accs.append(acc + jnp.dot(o_rest[rows, :], w_ref[k0:, :], preferred_element_type=F32))
    for rows, acc in zip(halves, accs):
        xo_ref[rows, :] = acc
        ms = jnp.mean(acc * acc, axis=1, keepdims=True)
        h = (acc * lax.rsqrt(ms + EPS)) * g_ref[...]
        h_hi = h.astype(BF16)
        h_ref[rows, :] = h_hi
        h_lo = (h - h_hi.astype(F32)).astype(BF16)
        w_both = jnp.concatenate([wrh_ref[...], wrl_ref[...]], axis=0)
        both = lax.dot_general(w_both, h_hi, _NT, preferred_element_type=F32)
        logits = (both[:N_EXPERTS] + both[N_EXPERTS:]
                  + lax.dot_general(wrh_ref[...], h_lo, _NT, preferred_element_type=F32))
        mx = jnp.max(logits, axis=0, keepdims=True)
        e = jnp.exp(logits - mx)
        aff_ref[:, rows] = e / jnp.sum(e, axis=0, keepdims=True)


def _out_projection(o_merge, lse_merge, o_rest, x, w_out, norm_g, wr_hi, wr_lo):
    t_tokens = x.shape[0]
    tm = OUT_TM
    n_merge = len(o_merge)
    row = lambda i: (i, 0)
    fixed = lambda i: (0, 0)
    merge_args = list(o_merge) + list(lse_merge)
    merge_dils = tuple(t_tokens // o.shape[0] for o in o_merge)
    in_specs = ([pl.BlockSpec((tm // dil, o.shape[1]), row) for o, dil in zip(o_merge, merge_dils)]
                + [pl.BlockSpec((LSE_ROWS, tm), lambda i: (0, i)) for _ in lse_merge])
    if n_merge:
        merged_width = o_merge[0].shape[1] // merge_dils[0]
        head = np.arange(merged_width) // HEAD_DIM
        expand = (np.arange(2 * LSE_ROWS)[:, None] % LSE_ROWS == head[None, :]).astype(np.float32)
        merge_args.append(jnp.asarray(expand, dtype=BF16))
        in_specs.append(pl.BlockSpec(expand.shape, fixed))
    in_specs = (in_specs
                + [pl.BlockSpec((tm, o_rest.shape[1]), row),
                   pl.BlockSpec((tm, D_MODEL), row),
                   pl.BlockSpec(w_out.shape, fixed),
                   pl.BlockSpec((1, D_MODEL), fixed),
                   pl.BlockSpec(wr_hi.shape, fixed),
                   pl.BlockSpec(wr_lo.shape, fixed)])
    out_specs = [pl.BlockSpec((tm, D_MODEL), row),
                 pl.BlockSpec((tm, D_MODEL), row),
                 pl.BlockSpec((N_EXPERTS, tm), lambda i: (0, i))]
    out_shape = [jax.ShapeDtypeStruct((t_tokens, D_MODEL), F32),
                 jax.ShapeDtypeStruct((t_tokens, D_MODEL), BF16),
                 jax.ShapeDtypeStruct((N_EXPERTS, t_tokens), F32)]
    return pl.pallas_call(
        functools.partial(_outproj_kernel, merge_dils=merge_dils),
        grid=(t_tokens // tm,),
        in_specs=in_specs, out_specs=out_specs, out_shape=out_shape,
        scratch_shapes=[pltpu.VMEM((merged_width // LANES, tm, LANES), F32) for dil in merge_dils if dil > 1],
        compiler_params=_cparams(("parallel",)),
        name="out_projection",
    )(*merge_args, o_rest, x, w_out, norm_g, wr_hi, wr_lo)


def _select_kernel(aff_ref, ind_ref, tri_ref, sel_ref, off_ref, *, cap):
    aff = aff_ref[...]
    n_tok = aff.shape[1]
    bits = pltpu.bitcast(aff, I32)

    def count(pred):
        return jnp.sum(jnp.where(pred, 1.0, 0.0), axis=1, keepdims=True)

    def value_step(j, thr):
        cand = thr | lax.shift_left(jnp.int32(1), 30 - j)
        return jnp.where(count(bits >= cand) >= cap, cand, thr)

    thr = lax.fori_loop(0, 31, value_step, jnp.zeros((N_EXPERTS, 1), I32))
    gt = bits > thr
    eq = bits == thr
    need = cap - count(gt)
    idx = lax.broadcasted_iota(I32, aff.shape, 1)
    idx_bits = int(math.log2(n_tok))

    def index_step(j, bound):
        cand = bound | lax.shift_left(jnp.int32(1), idx_bits - j)
        return jnp.where(count(eq & (idx < cand)) <= need, cand, bound)

    bound = lax.fori_loop(0, idx_bits + 1, index_step, jnp.zeros((N_EXPERTS, 1), I32))
    sel = jnp.where(gt | (eq & (idx < bound)), 1.0, 0.0).astype(BF16)
    sel_ref[...] = sel
    counts = jnp.dot(sel, ind_ref[...], preferred_element_type=F32)
    offs = jnp.dot(counts.astype(BF16), tri_ref[...], preferred_element_type=F32)
    off_ref[...] = offs.astype(I32)


def _strict_upper(n):
    return jnp.asarray(np.triu(np.ones((n, n), np.float32), k=1), dtype=BF16)


def _select(aff_t, cap):
    n_tok = aff_t.shape[1]
    ind = np.zeros((n_tok, LANES), np.float32)
    ind[np.arange(n_tok), np.arange(n_tok) // MOE_TB] = 1.0
    return pl.pallas_call(
        functools.partial(_select_kernel, cap=cap),
        out_shape=[jax.ShapeDtypeStruct((N_EXPERTS, n_tok), BF16),
                   jax.ShapeDtypeStruct((N_EXPERTS, LANES), I32)],
        compiler_params=pltpu.CompilerParams(vmem_limit_bytes=VMEM_LIMIT),
        name="select",
    )(aff_t, jnp.asarray(ind, dtype=BF16), _strict_upper(LANES))


SLOT_ALIGN = 16
ALIGN_BITS = 4
CHUNK_BITS = 6
NOT_SELECTED = -1e6


class _BlockLayout:
    def __init__(self, off_ref, b):
        self.start = [off_ref[e, b] for e in range(N_EXPERTS)]
        self.end = [off_ref[e, b + 1] for e in range(N_EXPERTS)]
        def floor_tile(v):
            return lax.shift_left(lax.shift_right_logical(v, ALIGN_BITS), ALIGN_BITS)

        self.base = [floor_tile(s) for s in self.start]
        self.shift = [s - a for s, a in zip(self.start, self.base)]
        span = [en - a for en, a in zip(self.end, self.base)]
        self.n_chunks = functools.reduce(
            jnp.maximum, [lax.shift_right_logical(sp, CHUNK_BITS) for sp in span]) + 1
        tail = [floor_tile(sp) for sp in span]
        self.tail_chunk = [lax.shift_right_logical(t, CHUNK_BITS) for t in tail]
        self.tail_row = [pl.multiple_of(t & (MOE_W - 1), SLOT_ALIGN) for t in tail]

    def window_row(self, e, chunk):
        return pl.multiple_of(self.base[e] + chunk * MOE_W, SLOT_ALIGN)


def _slot_onehots(pos, weight, chunk):
    n_tok = pos.shape[1]
    slot = (lax.broadcasted_iota(I32, (MOE_W, n_tok), 0) + chunk * MOE_W).astype(F32)
    parts = []
    for e in range(N_EXPERTS):
        hit = pos[e:e + 1, :] == slot
        val = 1.0 if weight is None else weight[e:e + 1, :]
        parts.append(jnp.where(hit, val, 0.0).astype(BF16))
    return jnp.concatenate(parts, axis=0)


def _window_positions(sel_ref, tri_ref, layout):
    sel = sel_ref[...]
    rank = jnp.dot(sel, tri_ref[...], preferred_element_type=F32)
    shift = jnp.concatenate([jnp.full((1, 1), s, I32) for s in layout.shift], axis=0).astype(F32)
    return jnp.where(sel > 0, rank + shift, NOT_SELECTED)


def _dispatch_kernel(off_ref, sel_ref, h_ref, tri_ref, xe_ref, stage_ref, extra_ref, tail_ref, zero_ref, sem,
                     sem_extra):
    b = pl.program_id(0)
    nb = pl.num_programs(0)
    slot = b % 2
    layout = _BlockLayout(off_ref, b)
    pos = _window_positions(sel_ref, tri_ref, layout)
    h = h_ref[...]

    def rows_for(chunk):
        return jnp.dot(_slot_onehots(pos, None, chunk), h, preferred_element_type=F32)

    def copy(src, e, row0, s):
        return pltpu.make_async_copy(src.at[e], xe_ref.at[e, pl.ds(row0, MOE_W)], s)

    @pl.when(b == 0)
    def _():
        tail_ref[...] = jnp.zeros(tail_ref.shape, BF16)
        zero_ref[...] = jnp.zeros(zero_ref.shape, BF16)
        pad = zero_ref.shape[0]
        fills = [pltpu.make_async_copy(zero_ref, xe_ref.at[e, pl.ds(xe_ref.shape[1] - pad, pad)], sem_extra.at[0])
                 for e in range(N_EXPERTS)]
        for f in fills:
            f.start()
        for f in fills:
            f.wait()

    rows0 = rows_for(0)

    for e in range(N_EXPERTS):
        r0 = e * MOE_W
        first = rows0[r0:r0 + SLOT_ALIGN] + tail_ref[e].astype(F32)
        stage_ref[slot, e, :SLOT_ALIGN] = first.astype(BF16)
        stage_ref[slot, e, SLOT_ALIGN:] = rows0[r0 + SLOT_ALIGN:r0 + MOE_W].astype(BF16)
    for e in range(N_EXPERTS):
        tile = stage_ref[slot, e, pl.ds(layout.tail_row[e], SLOT_ALIGN), :]
        tail_ref[e] = jnp.where(layout.tail_chunk[e] == 0, tile, tail_ref[e])

    @pl.when(b > 0)
    def _():
        for e in range(N_EXPERTS):
            copy(stage_ref.at[1 - slot], e, 0, sem.at[1 - slot]).wait()

    for e in range(N_EXPERTS):
        copy(stage_ref.at[slot], e, layout.window_row(e, 0), sem.at[slot]).start(priority=e % 2)

    def overflow(chunk, carry):
        more = rows_for(chunk).astype(BF16)
        for e in range(N_EXPERTS):
            extra_ref[e] = more[e * MOE_W:(e + 1) * MOE_W]
        for e in range(N_EXPERTS):
            copy(extra_ref, e, layout.window_row(e, chunk), sem_extra.at[0]).start()
        for e in range(N_EXPERTS):
            @pl.when(layout.tail_chunk[e] == chunk)
            def _():
                tail_ref[e] = extra_ref[e, pl.ds(layout.tail_row[e], SLOT_ALIGN), :]
        for e in range(N_EXPERTS):
            copy(extra_ref, e, 0, sem_extra.at[0]).wait()
        return carry

    lax.fori_loop(1, layout.n_chunks, overflow, 0)

    @pl.when(b == nb - 1)
    def _():
        for e in range(N_EXPERTS):
            copy(stage_ref.at[slot], e, 0, sem.at[slot]).wait()


def _slot_rows(cap):
    pad = ((SLOT_ALIGN + MOE_TB) // MOE_W + 1) * MOE_W
    return cap + -(-pad // FFN_TM) * FFN_TM


def _dispatch(offs, sel, h, cap):
    n_tok = h.shape[0]
    nb = n_tok // MOE_TB
    rows = _slot_rows(cap)
    grid_spec = pltpu.PrefetchScalarGridSpec(
        num_scalar_prefetch=1,
        grid=(nb,),
        in_specs=[pl.BlockSpec((N_EXPERTS, MOE_TB), lambda b, off: (0, b)),
                  pl.BlockSpec((MOE_TB, D_MODEL), lambda b, off: (b, 0)),
                  pl.BlockSpec((MOE_TB, MOE_TB), lambda b, off: (0, 0))],
        out_specs=pl.BlockSpec(memory_space=pl.ANY),
        scratch_shapes=[pltpu.VMEM((2, N_EXPERTS, MOE_W, D_MODEL), BF16),
                        pltpu.VMEM((N_EXPERTS, MOE_W, D_MODEL), BF16),
                        pltpu.VMEM((N_EXPERTS, SLOT_ALIGN, D_MODEL), BF16),
                        pltpu.VMEM((rows - cap, D_MODEL), BF16),
                        pltpu.SemaphoreType.DMA((2,)),
                        pltpu.SemaphoreType.DMA((1,))],
    )
    return pl.pallas_call(
        _dispatch_kernel,
        grid_spec=grid_spec,
        out_shape=jax.ShapeDtypeStruct((N_EXPERTS, rows, D_MODEL), BF16),
        compiler_params=_cparams(("arbitrary",)),
        name="dispatch",
    )(offs, sel, h, _strict_upper(MOE_TB))


def _ffn_kernel(*refs, first_step, n_tiles):
    n_groups = len(n_tiles)
    xe_refs = refs[:n_groups]
    wg_ref, wu_ref, wd_ref = refs[n_groups:n_groups + 3]
    ye_refs = refs[n_groups + 3:2 * n_groups + 3]
    wg_bf, wu_bf, wd_bf = refs[2 * n_groups + 3:]
    j = pl.program_id(1)

    @pl.when(j == 0)
    def _():
        wg_bf[...] = wg_ref[...].astype(BF16)
        wu_bf[...] = wu_ref[...].astype(BF16)
        wd_bf[...] = wd_ref[...].astype(BF16)

    for g in range(n_groups):
        @pl.when((j >= first_step[g]) & (j < first_step[g] + n_tiles[g]))
        def _():
            x = xe_refs[g][...]
            gate = jnp.dot(x, wg_bf[...], preferred_element_type=F32)
            up = jnp.dot(x, wu_bf[...], preferred_element_type=F32)
            hid = (gate * jax.nn.sigmoid(gate) * up).astype(BF16)
            ye_refs[g][...] = jnp.dot(hid, wd_bf[...], preferred_element_type=F32).astype(BF16)


def _expert_ffn(xes, w_gate, w_up, w_down, layer, caps):
    n_groups = len(xes)
    n_tiles = [cap // FFN_TM for cap in caps]
    first_step = [sum(n_tiles[:g]) for g in range(n_groups)]

    def weight_index(e, j):
        return (layer, jnp.minimum(e + (j >= 1).astype(I32), N_EXPERTS - 1), 0, 0)

    wspec = lambda shape: pl.BlockSpec((None, None) + shape, weight_index)

    def tile_spec(g):
        return pl.BlockSpec((None, FFN_TM, D_MODEL),
                            lambda e, j: (e, jnp.clip(j - first_step[g], 0, n_tiles[g] - 1), 0))

    return pl.pallas_call(
        functools.partial(_ffn_kernel, first_step=tuple(first_step), n_tiles=tuple(n_tiles)),
        grid=(N_EXPERTS, sum(n_tiles)),
        in_specs=([tile_spec(g) for g in range(n_groups)]
                  + [wspec((D_MODEL, EXPERT_FF)), wspec((D_MODEL, EXPERT_FF)), wspec((EXPERT_FF, D_MODEL))]),
        out_specs=[tile_spec(g) for g in range(n_groups)],
        out_shape=[jax.ShapeDtypeStruct(xe.shape, BF16) for xe in xes],
        input_output_aliases={g: g for g in range(n_groups)},
        scratch_shapes=[pltpu.VMEM((D_MODEL, EXPERT_FF), BF16), pltpu.VMEM((D_MODEL, EXPERT_FF), BF16),
                        pltpu.VMEM((EXPERT_FF, D_MODEL), BF16)],
        compiler_params=_cparams(("arbitrary", "arbitrary")),
        name="expert_ffn",
    )(*xes, w_gate, w_up, w_down)


def _combine_kernel(off_ref, sel_ref, aff_ref, tri_ref, x_ref, ye_ref, out_ref, win_ref, extra_ref, sem, sem_extra,
                    *, rows):
    b = pl.program_id(0)
    nb = pl.num_programs(0)
    slot = b % 2

    def copy(dst, e, row0, s):
        return pltpu.make_async_copy(ye_ref.at[e, pl.ds(row0, MOE_W)], dst.at[e], s)

    def fetch(blk, dst_slot):
        ahead = _BlockLayout(off_ref, blk)
        for e in range(N_EXPERTS):
            copy(win_ref.at[dst_slot], e, ahead.window_row(e, 0), sem.at[dst_slot]).start(priority=e % 2)

    @pl.when(b == 0)
    def _():
        fetch(0, 0)

    @pl.when(b + 1 < nb)
    def _():
        fetch(b + 1, 1 - slot)

    layout = _BlockLayout(off_ref, b)
    pos = _window_positions(sel_ref, tri_ref, layout)
    aff = aff_ref[...]
    tn = (((0,), (0,)), ((), ()))

    def contribution(chunk, window):
        gates = _slot_onehots(pos, aff, chunk)
        vals = window.reshape(N_EXPERTS * MOE_W, window.shape[-1])
        return lax.dot_general(gates, vals, tn, preferred_element_type=F32)

    for e in range(N_EXPERTS):
        copy(win_ref.at[slot], e, 0, sem.at[slot]).wait()
    out_ref[...] = x_ref[...] + contribution(0, win_ref[slot])

    def overflow(chunk, carry):
        for e in range(N_EXPERTS):
            row0 = pl.multiple_of(jnp.minimum(layout.window_row(e, chunk), rows - MOE_W), SLOT_ALIGN)
            copy(extra_ref, e, row0, sem_extra.at[0]).start()
        for e in range(N_EXPERTS):
            copy(extra_ref, e, 0, sem_extra.at[0]).wait()
        out_ref[...] += contribution(chunk, extra_ref[...])
        return carry

    lax.fori_loop(1, layout.n_chunks, overflow, 0)


def _combine(offs, sel, aff_t, x, ye):
    n_tok = x.shape[0]
    nb = n_tok // MOE_TB
    rows = ye.shape[1]
    grid_spec = pltpu.PrefetchScalarGridSpec(
        num_scalar_prefetch=1,
        grid=(nb,),
        in_specs=[pl.BlockSpec((N_EXPERTS, MOE_TB), lambda b, off: (0, b)),
                  pl.BlockSpec((N_EXPERTS, MOE_TB), lambda b, off: (0, b)),
                  pl.BlockSpec((MOE_TB, MOE_TB), lambda b, off: (0, 0)),
                  pl.BlockSpec((MOE_TB, D_MODEL), lambda b, off: (b, 0)),
                  pl.BlockSpec(memory_space=pl.ANY)],
        out_specs=pl.BlockSpec((MOE_TB, D_MODEL), lambda b, off: (b, 0)),
        scratch_shapes=[pltpu.VMEM((2, N_EXPERTS, MOE_W, D_MODEL), BF16),
                        pltpu.VMEM((N_EXPERTS, MOE_W, D_MODEL), BF16),
                        pltpu.SemaphoreType.DMA((2,)),
                        pltpu.SemaphoreType.DMA((1,))],
    )
    return pl.pallas_call(
        functools.partial(_combine_kernel, rows=rows),
        grid_spec=grid_spec,
        out_shape=jax.ShapeDtypeStruct((n_tok, D_MODEL), F32),
        compiler_params=_cparams(("arbitrary",)),
        name="combine",
    )(offs, sel, aff_t, _strict_upper(MOE_TB), x, ye)


def _moe(routed, w_gate, w_up, w_down, layer):
    plans = []
    for x, h, aff_t in routed:
        cap = CAPACITY_FACTOR * x.shape[0] // N_EXPERTS
        sel, offs = _select(aff_t, cap)
        plans.append((cap, sel, offs, _dispatch(offs, sel, h, cap)))
    yes = _expert_ffn([p[3] for p in plans], w_gate, w_up, w_down, layer, [p[0] for p in plans])
    return [_combine(offs, sel, aff_t, x, ye)
            for (x, _, aff_t), (_, sel, offs, _), ye in zip(routed, plans, yes)]


def _row(v):
    return v.reshape(1, -1).astype(F32)


def _router_split(w_router):
    wt = w_router.T.astype(F32)
    hi = wt.astype(BF16)
    lo = (wt - hi.astype(F32)).astype(BF16)
    return hi, lo


def _prep_ab(norm_g, w_in, qn_a, kn_a, qn_b, kn_b, w_out, norm_ffn, w_router):
    scale = HEAD_DIM ** -0.5 * LOG2E
    perm_b = _pair_layout_perm(B_KV_HEADS, B_Q_HEADS // B_KV_HEADS)
    a3 = 3 * A_WIDTH
    bq = B_Q_HEADS * HEAD_DIM
    w_cols = np.concatenate([np.arange(a3), a3 + perm_b, np.arange(a3 + bq, w_in.shape[1])])
    gains = jnp.concatenate([
        jnp.repeat(qn_a, A_HEADS, axis=0).reshape(-1) * scale,
        jnp.repeat(kn_a, A_HEADS, axis=0).reshape(-1),
        jnp.ones((A_WIDTH,), F32),
        jnp.tile(qn_b, B_Q_HEADS) * scale,
        jnp.tile(kn_b, B_KV_HEADS),
        jnp.ones((B_KV_HEADS * HEAD_DIM,), F32)])
    a_out = A_HEADS * HEAD_DIM
    w_o = w_out[np.concatenate([np.arange(a_out), a_out + perm_b])].astype(BF16)
    return (_row(norm_g), w_in[:, w_cols].astype(BF16), _row(gains), w_o, _row(norm_ffn)) + _router_split(w_router)


def _layer_ab(x, batch, seq, tabs, prep):
    norm_g, w, gains, w_o, norm_ffn, wr_hi, wr_lo = prep
    tab_a, tab_b = tabs
    bq = B_Q_HEADS * HEAD_DIM
    n_a = A_GROUPS
    assert A_HEADS * HEAD_DIM == MXU_COLS
    dils = [dil for _, dil in A_PATTERNS]
    plan = ([("qk", g, 0, 0, ROT_DIM // 2, dils[g]) for g in range(n_a)]
            + [("qk", n_a + g, 0, 0, ROT_DIM // 2, dils[g]) for g in range(n_a)]
            + [("v", 2 * n_a + g, 0, 0, 0, 1) for g in range(n_a)]
            + [("qk", 3 * n_a, c * MXU_COLS, 1, HEAD_DIM // 4, 1) for c in range(bq // MXU_COLS)]
            + [("qk", 3 * n_a + 1, 0, 1, HEAD_DIM // 4, 1), ("v", 3 * n_a + 2, 0, 0, 0, 1)])
    assert B_KV_HEADS * HEAD_DIM == MXU_COLS
    a_widths = tuple(MXU_COLS if dil == 1 else ("dilated", dil) for dil in dils)
    outs = _projection(
        x, norm_g, w, gains, (tab_a, tab_b), plan,
        a_widths * 2 + (None,) * n_a + (bq, MXU_COLS, None), seq)
    qa, ka, va = outs[:n_a], outs[n_a:2 * n_a], outs[2 * n_a:3 * n_a]
    qb, kb, vb = outs[3 * n_a:]
    o_parts, lse_parts = [], []
    for gi, (window, dil) in enumerate(A_PATTERNS):
        o, lse = _attention(qa[gi], ka[gi], va[gi], batch=batch, n_keys=seq // dil, residues=dil, group=1,
                            n_pairs=A_HEADS // 2, band_r=window // (2 * dil), want_lse=True)
        o_parts.append(o)
        lse_parts.append(lse)
    (ob,) = _attention(qb, kb, vb, batch=batch, n_keys=seq, residues=1,
                       group=B_Q_HEADS // B_KV_HEADS, n_pairs=B_KV_HEADS // 2, band_r=None)
    return _out_projection(o_parts, lse_parts, ob, x, w_o, norm_ffn, wr_hi, wr_lo)


def _prep_c(norm_g, w_in, qn, kn, sink, w_out, norm_ffn, w_router):
    scale = HEAD_DIM ** -0.5 * LOG2E
    perm = _pair_layout_perm(C_KV_HEADS, C_Q_HEADS // C_KV_HEADS)
    cq = C_Q_HEADS * HEAD_DIM
    w_cols = np.concatenate([perm, np.arange(cq, w_in.shape[1])])
    gains = jnp.concatenate([jnp.tile(qn, C_Q_HEADS) * scale, jnp.tile(kn, C_KV_HEADS),
                             jnp.ones((C_KV_HEADS * HEAD_DIM,), F32)])
    return (_row(norm_g), w_in[:, w_cols].astype(BF16), _row(gains), sink.astype(F32),
            w_out[perm].astype(BF16), _row(norm_ffn)) + _router_split(w_router)


def _layer_c(x, batch, seq, tabs, prep):
    norm_g, w, gains, sink, w_o, norm_ffn, wr_hi, wr_lo = prep
    tab_a, _ = tabs
    cq = C_Q_HEADS * HEAD_DIM
    plan = ([("qk", 0, c * MXU_COLS, 0, ROT_DIM // 2, 1) for c in range(cq // MXU_COLS)]
            + [("qk", 1, 0, 0, ROT_DIM // 2, 1), ("v", 2, 0, 0, 0, 1)])
    assert C_KV_HEADS * HEAD_DIM == MXU_COLS
    q, k, v = _projection(x, norm_g, w, gains, (tab_a,), plan, (cq, MXU_COLS, None), seq)
    (o,) = _attention(q, k, v, batch=batch, n_keys=seq, residues=1,
                      group=C_Q_HEADS // C_KV_HEADS, n_pairs=C_KV_HEADS // 2, band_r=C_RADIUS, sink=sink)
    return _out_projection([], [], o, x, w_o, norm_ffn, wr_hi, wr_lo)


def _encode(xs, params):
    (norm_mix, norm_ffn, w_in_ab, qn_a, kn_a, qn_b, kn_b, w_out_ab, w_in_c, qn_c, kn_c, sink_c, w_out_c,
     w_router, w_gate, w_up, w_down) = params
    shapes = [x.shape[:2] for x in xs]
    tabs = [_rope_tables(seq) for _, seq in shapes]
    xts = [x.reshape(batch * seq, D_MODEL) for x, (batch, seq) in zip(xs, shapes)]
    for layer in range(norm_mix.shape[0]):
        j = layer // 2
        if layer % 2 == 0:
            prep = _prep_ab(norm_mix[layer], w_in_ab[j], qn_a[j], kn_a[j], qn_b[j], kn_b[j], w_out_ab[j],
                            norm_ffn[layer], w_router[layer])
            mixer = _layer_ab
        else:
            prep = _prep_c(norm_mix[layer], w_in_c[j], qn_c[j], kn_c[j], sink_c[j], w_out_c[j],
                           norm_ffn[layer], w_router[layer])
            mixer = _layer_c
        routed = [mixer(xt, batch, seq, tab, prep) for xt, (batch, seq), tab in zip(xts, shapes, tabs)]
        xts = _moe(routed, w_gate, w_up, w_down, layer)
    return tuple(xt.reshape(batch, seq, D_MODEL) for xt, (batch, seq) in zip(xts, shapes))


def kernel(x_prompt, x_sample, norm_mix, norm_ffn, w_in_ab, qn_a, kn_a, qn_b, kn_b, w_out_ab, w_in_c, qn_c,
           kn_c, sink_c, w_out_c, w_router, w_gate, w_up, w_down):
    params = (norm_mix, norm_ffn, w_in_ab, qn_a, kn_a, qn_b, kn_b, w_out_ab, w_in_c, qn_c, kn_c, sink_c,
              w_out_c, w_router, w_gate, w_up, w_down)
    return _encode((x_prompt, x_sample), params)
```
